```python
import jax, jax.numpy as jnp
from jax import lax
import numpy as np

D_MODEL = 2048
BATCH = 2
SEQ = 16384
DEPTH = 4
DEC_BATCH = 16
DEC_SEQ = 32
PAST_LEN = 2048

CHUNK = 64
N_MIXERS = 3
HEAD_DIM = 64
NORM_EPS = 1e-6
RW_HEADS = D_MODEL // HEAD_DIM
RW_DECAY_LORA = 96
RW_A_LORA = 96
RW_GATE_LORA = 256
RW_COLS = 3 * D_MODEL + RW_DECAY_LORA + RW_A_LORA + RW_GATE_LORA
RW_GN_EPS = HEAD_DIM * 1e-5
ATT_Q_HEADS = D_MODEL // HEAD_DIM
ATT_KV_HEADS = 4
ATT_GROUP = ATT_Q_HEADS // ATT_KV_HEADS
Q_COLS = ATT_Q_HEADS * HEAD_DIM
KV_COLS = ATT_KV_HEADS * HEAD_DIM
WINDOW = 128
WIN_CHUNKS = WINDOW // CHUNK
BAND = (WIN_CHUNKS + 1) * CHUNK
B_COLS = Q_COLS + 2 * KV_COLS
IDX_HEADS = 8
IDX_DIM = 64
TOPK_MAX = 256
IDX_Q_BLOCK = 128
C_COLS = Q_COLS + 2 * KV_COLS + IDX_HEADS * IDX_DIM + IDX_DIM + IDX_HEADS
MEM_TOKENS = 256
MEM_HEADS = 4
MEM_Q_COLS = MEM_HEADS * HEAD_DIM
MIX_OUT = D_MODEL + MEM_Q_COLS
D_FF = 5632
CONV_W = 3
N_A = (DEPTH + 2) // 3
N_B = (DEPTH + 1) // 3
N_C = DEPTH // 3

kernel_name = 'hybrid_stream_encoder_step'


def rmsnorm(x, g, eps=NORM_EPS):
    xf = x.astype(jnp.float32)
    y = xf * lax.rsqrt(jnp.mean(xf * xf, axis=-1, keepdims=True) + eps)
    return (y * g.astype(jnp.float32)).astype(x.dtype)


def rwkv7_mix(z, shift_prev, s0, mu, w0, w2, a0, a2, g2, k_k, k_a, r_k, ln_w, ln_b):
    B, T, _ = z.shape
    D = D_MODEL
    zc = jnp.concatenate([shift_prev[:, None].astype(z.dtype), z], axis=1)
    zs = z + (zc[:, :-1] - z) * mu
    o1 = 3 * D
    o2 = o1 + RW_DECAY_LORA
    o3 = o2 + RW_A_LORA
    r, k, v = zs[..., :D], zs[..., D:2 * D], zs[..., 2 * D:3 * D]
    zw, za, zg = zs[..., o1:o2], zs[..., o2:o3], zs[..., o3:]
    w_log = -jax.nn.softplus(-(w0 + jnp.tanh(zw) @ w2).astype(jnp.float32)) - 0.5
    decay = jnp.exp(-jnp.exp(w_log))
    a = jax.nn.sigmoid(a0 + za @ a2)
    g = jax.nn.sigmoid(zg) @ g2
    heads = lambda t: t.astype(jnp.float32).reshape(B, T, RW_HEADS, HEAD_DIM)
    kk = heads(k * k_k)
    kk = kk / jnp.maximum(jnp.sqrt(jnp.sum(kk * kk, axis=-1, keepdims=True)), 1e-12)
    k = k * (1.0 + (a - 1.0) * k_a)
    r_, k_, v_, a_, w_ = heads(r), heads(k), heads(v), heads(a), heads(decay)
    tm = lambda t: jnp.swapaxes(t, 0, 1)

    def step(S, inp):
        rt, wt, kt, vt, at, bt = inp
        sa = jnp.einsum('bhij,bhj->bhi', S, at)
        S = S * wt[:, :, None, :] + sa[..., None] * bt[:, :, None, :] + vt[..., None] * kt[:, :, None, :]
        return S, jnp.einsum('bhij,bhj->bhi', S, rt)

    s_fin, y = lax.scan(step, s0.astype(jnp.float32),
                        (tm(r_), tm(w_), tm(k_), tm(v_), tm(-kk), tm(kk * a_)))
    y = tm(y)
    mean = jnp.mean(y, axis=-1, keepdims=True)
    var = jnp.mean(jnp.square(y - mean), axis=-1, keepdims=True)
    y = ((y - mean) * lax.rsqrt(var + RW_GN_EPS)).reshape(B, T, D) * ln_w.astype(jnp.float32) + ln_b.astype(jnp.float32)
    bonus = jnp.sum(r_ * k_ * r_k.astype(jnp.float32), axis=-1, keepdims=True) * v_
    y = (y + bonus.reshape(B, T, D)) * g.astype(jnp.float32)
    return y.astype(z.dtype), zc[:, -1], s_fin.astype(s0.dtype)


def split_qkv(z, q_norm, k_norm):
    B, T = z.shape[:2]
    q = rmsnorm(z[..., :Q_COLS].reshape(B, T, ATT_KV_HEADS, ATT_GROUP, HEAD_DIM), q_norm)
    k = rmsnorm(z[..., Q_COLS:Q_COLS + KV_COLS].reshape(B, T, ATT_KV_HEADS, HEAD_DIM), k_norm)
    v = z[..., Q_COLS + KV_COLS:Q_COLS + 2 * KV_COLS].reshape(B, T, ATT_KV_HEADS, HEAD_DIM)
    return q, k, v


def sink_attend(q, k, v, valid, sink):
    s = jnp.einsum('bqhgd,bkhd->bhgqk', q, k).astype(jnp.float32) * HEAD_DIM ** -0.5
    s = jnp.where(valid, s, -jnp.inf)
    sk = sink.astype(jnp.float32).reshape(ATT_KV_HEADS, ATT_GROUP)[None, :, :, None]
    m = jnp.maximum(jnp.max(s, axis=-1), sk)
    p = jnp.exp(s - m[..., None])
    den = jnp.sum(p, axis=-1) + jnp.exp(sk - m)
    return jnp.einsum('bhgqk,bkhd->bqhgd', (p / den[..., None]).astype(v.dtype), v)


def swa_prompt(q, k, v, sink):
    B, T = k.shape[:2]
    nc = T // CHUNK
    pad = jnp.zeros((B, WINDOW) + k.shape[2:], k.dtype)

    def bands(t):
        tc = jnp.concatenate([pad, t], axis=1).reshape((B, nc + WIN_CHUNKS, CHUNK) + t.shape[2:])
        return jnp.concatenate([tc[:, j:j + nc] for j in range(WIN_CHUNKS + 1)], axis=2)

    kpos = (jnp.arange(nc)[:, None] - WIN_CHUNKS) * CHUNK + jnp.arange(BAND)[None, :]
    valid = kpos >= 0
    qc = q.reshape((B, nc, CHUNK) + q.shape[2:])
    mv = lambda t: jnp.moveaxis(t, 1, 0)
    o = lax.map(lambda a: sink_attend(a[0], a[1], a[2], a[3], sink), (mv(qc), mv(bands(k)), mv(bands(v)), valid))
    return jnp.moveaxis(o, 0, 1).reshape(B, T, Q_COLS)


def split_indexer(z, ki_norm):
    B, T = z.shape[:2]
    o = Q_COLS + 2 * KV_COLS
    qi = z[..., o:o + IDX_HEADS * IDX_DIM].reshape(B, T, IDX_HEADS, IDX_DIM)
    o = o + IDX_HEADS * IDX_DIM
    ki = rmsnorm(z[..., o:o + IDX_DIM], ki_norm)
    wi = z[..., o + IDX_DIM:o + IDX_DIM + IDX_HEADS]
    return qi, ki, wi


def index_scores(qi, wi, ki):
    s = jax.nn.relu(jnp.einsum('bqhd,bsd->bqhs', qi, ki).astype(jnp.float32) * IDX_DIM ** -0.5)
    return jnp.einsum('bqhs,bqh->bqs', s, wi.astype(jnp.float32) * IDX_HEADS ** -0.5)


def gathered_attend(q, k, v, idx, valid):
    gather = jax.vmap(lambda kb, ib: kb[ib])
    ks, vs = gather(k, idx), gather(v, idx)
    s = jnp.einsum('bqhgd,bqkhd->bqhgk', q, ks).astype(jnp.float32) * HEAD_DIM ** -0.5
    s = jnp.where(valid[:, :, None, None, :], s, -jnp.inf)
    p = jax.nn.softmax(s, axis=-1)
    return jnp.einsum('bqhgk,bqkhd->bqhgd', p.astype(v.dtype), vs)


def dsa_prompt(q, k, v, qi, wi, ki, k_sel):
    B, T = k.shape[:2]
    nb = T // IDX_Q_BLOCK
    kpos = jnp.arange(T)
    blocks = lambda t: jnp.moveaxis(t.reshape((B, nb, IDX_Q_BLOCK) + t.shape[2:]), 1, 0)

    def one_block(args):
        qb, qib, wib, start = args
        sc = index_scores(qib, wib, ki)
        qpos = start + jnp.arange(IDX_Q_BLOCK)
        adm = kpos[None, :] < ((qpos // CHUNK + 1) * CHUNK)[:, None]
        vals, idx = lax.top_k(jnp.where(adm, sc, -jnp.inf), k_sel)
        return gathered_attend(qb, k, v, idx, jnp.isfinite(vals))

    o = lax.map(one_block, (blocks(q), blocks(qi), blocks(wi), jnp.arange(nb) * IDX_Q_BLOCK))
    return jnp.moveaxis(o, 0, 1).reshape(B, T, Q_COLS)


def mem_kv(mem, g, w_kv, k_norm):
    B, M, _ = mem.shape
    kv = rmsnorm(mem, g) @ w_kv
    km = rmsnorm(kv[..., :MEM_Q_COLS].reshape(B, M, MEM_HEADS, HEAD_DIM), k_norm)
    vm = kv[..., MEM_Q_COLS:].reshape(B, M, MEM_HEADS, HEAD_DIM)
    return km, vm


def merge_out(mix, z, km, vm, q_norm, w_out):
    B, T = z.shape[:2]
    qm = rmsnorm(z[..., -MEM_Q_COLS:].reshape(B, T, MEM_HEADS, HEAD_DIM), q_norm)
    s = jnp.einsum('bqhd,bkhd->bhqk', qm, km).astype(jnp.float32) * HEAD_DIM ** -0.5
    p = jax.nn.softmax(s, axis=-1)
    mo = jnp.einsum('bhqk,bkhd->bqhd', p.astype(vm.dtype), vm).reshape(B, T, MEM_Q_COLS)
    return jnp.concatenate([mix, mo], axis=-1) @ w_out


def conv_ffn(x, g, w_up, conv_w, w_down, prev):
    T = x.shape[1]
    u = rmsnorm(x, g) @ w_up
    uc = jnp.concatenate([prev.astype(u.dtype), u], axis=1)
    c = uc[:, 0:T] * conv_w[0]
    for j in range(1, CONV_W):
        c = c + uc[:, j:j + T] * conv_w[j]
    y = (jax.nn.silu(c[..., :D_FF]) * c[..., D_FF:]) @ w_down
    return y, uc[:, T:]


def setup_inputs(seed: int = 0) -> dict:
    key = jax.random.key(seed)
    ks = iter(jax.random.split(key, 80))
    f32 = jnp.float32

    def nrm(shape, scale=1.0):
        return jax.random.normal(next(ks), shape, f32) * scale

    def gain(shape):
        return 1.0 + nrm(shape, 0.05)

    win_rows = min(WINDOW, PAST_LEN)
    D = D_MODEL
    inp = {}
    inp['x_prompt'] = nrm((BATCH, SEQ, D))
    inp['x_sample'] = nrm((DEC_BATCH, DEC_SEQ, D))
    inp['state_rwkv_wkv'] = nrm((N_A, DEC_BATCH, RW_HEADS, HEAD_DIM, HEAD_DIM), 0.3)
    inp['state_rwkv_shift'] = nrm((N_A, DEC_BATCH, RW_COLS))
    inp['cache_swa_k'] = nrm((N_B, DEC_BATCH, win_rows, ATT_KV_HEADS, HEAD_DIM))
    inp['cache_swa_v'] = nrm((N_B, DEC_BATCH, win_rows, ATT_KV_HEADS, HEAD_DIM))
    inp['cache_dsa_k'] = nrm((N_C, DEC_BATCH, PAST_LEN, ATT_KV_HEADS, HEAD_DIM))
    inp['cache_dsa_v'] = nrm((N_C, DEC_BATCH, PAST_LEN, ATT_KV_HEADS, HEAD_DIM))
    inp['cache_dsa_idx_k'] = nrm((N_C, DEC_BATCH, PAST_LEN, IDX_DIM))
    inp['cache_mem_k'] = nrm((DEPTH, DEC_BATCH, MEM_TOKENS, MEM_HEADS, HEAD_DIM))
    inp['cache_mem_v'] = nrm((DEPTH, DEC_BATCH, MEM_TOKENS, MEM_HEADS, HEAD_DIM))
    inp['state_ffn_conv'] = nrm((DEPTH, DEC_BATCH, CONV_W - 1, 2 * D_FF))
    inp['mem_prompt'] = nrm((BATCH, MEM_TOKENS, D))
    inp['attn_norm'] = gain((DEPTH, D))
    inp['ffn_norm'] = gain((DEPTH, D))
    inp['mem_norm'] = gain((DEPTH, D))
    inp['mem_w_kv'] = nrm((DEPTH, D, 2 * MEM_Q_COLS), D ** -0.5)
    inp['mem_q_norm'] = gain((DEPTH, HEAD_DIM))
    inp['mem_k_norm'] = gain((DEPTH, HEAD_DIM))
    inp['a_w_in'] = nrm((N_A, D, RW_COLS + MEM_Q_COLS), D ** -0.5)
    inp['a_mu'] = jax.random.uniform(next(ks), (N_A, RW_COLS), f32)
    inp['a_w0'] = jax.random.uniform(next(ks), (N_A, D), f32, -6.0, -1.0)
    inp['a_w2'] = nrm((N_A, RW_DECAY_LORA, D), 0.5 * RW_DECAY_LORA ** -0.5)
    inp['a_a0'] = nrm((N_A, D), 0.1)
    inp['a_a2'] = nrm((N_A, RW_A_LORA, D), 0.5 * RW_A_LORA ** -0.5)
    inp['a_g2'] = nrm((N_A, RW_GATE_LORA, D), RW_GATE_LORA ** -0.5)
    inp['a_k_k'] = 0.85 + nrm((N_A, D), 0.05)
    inp['a_k_a'] = 1.0 + nrm((N_A, D), 0.05)
    inp['a_r_k'] = nrm((N_A, RW_HEADS, HEAD_DIM), 0.1)
    inp['a_ln_w'] = gain((N_A, D))
    inp['a_ln_b'] = nrm((N_A, D), 0.02)
    inp['a_w_out'] = nrm((N_A, MIX_OUT, D), MIX_OUT ** -0.5)
    inp['b_w_in'] = nrm((N_B, D, B_COLS + MEM_Q_COLS), D ** -0.5)
    inp['b_q_norm'] = gain((N_B, HEAD_DIM))
    inp['b_k_norm'] = gain((N_B, HEAD_DIM))
    inp['b_sink'] = nrm((N_B, ATT_Q_HEADS), 0.5)
    inp['b_w_out'] = nrm((N_B, MIX_OUT, D), MIX_OUT ** -0.5)
    inp['c_w_in'] = nrm((N_C, D, C_COLS + MEM_Q_COLS), D ** -0.5)
    inp['c_q_norm'] = gain((N_C, HEAD_DIM))
    inp['c_k_norm'] = gain((N_C, HEAD_DIM))
    inp['c_idx_k_norm'] = gain((N_C, IDX_DIM))
    inp['c_w_out'] = nrm((N_C, MIX_OUT, D), MIX_OUT ** -0.5)
    inp['ffn_w_up'] = nrm((DEPTH, D, 2 * D_FF), D ** -0.5)
    inp['ffn_conv'] = nrm((DEPTH, CONV_W, 2 * D_FF), 0.5)
    inp['ffn_w_down'] = nrm((DEPTH, D_FF, D), D_FF ** -0.5)
    return inp


def reference(x_prompt, x_sample, state_rwkv_wkv, state_rwkv_shift, cache_swa_k, cache_swa_v,
              cache_dsa_k, cache_dsa_v, cache_dsa_idx_k, cache_mem_k, cache_mem_v, state_ffn_conv,
              mem_prompt, attn_norm, ffn_norm, mem_norm, mem_w_kv, mem_q_norm, mem_k_norm,
              a_w_in, a_mu, a_w0, a_w2, a_a0, a_a2, a_g2, a_k_k, a_k_a, a_r_k, a_ln_w, a_ln_b, a_w_out,
              b_w_in, b_q_norm, b_k_norm, b_sink, b_w_out,
              c_w_in, c_q_norm, c_k_norm, c_idx_k_norm, c_w_out,
              ffn_w_up, ffn_conv, ffn_w_down):
    xp, xs = x_prompt, x_sample
    Bp, T = xp.shape[:2]
    Bd, S = xs.shape[:2]
    win_rows = cache_swa_k.shape[2]
    k_sel_p = min(TOPK_MAX, T // 4)
    k_sel_s = min(TOPK_MAX, (cache_dsa_k.shape[2] + S) // 4)
    p_rw_wkv, p_rw_sh, p_sw_k, p_sw_v, p_ds_k, p_ds_v, p_ds_i, p_mk, p_mv, p_cv = [], [], [], [], [], [], [], [], [], []
    s_rw_wkv, s_rw_sh, s_sw_k, s_sw_v, s_ds_k, s_ds_v, s_ds_i, s_cv = [], [], [], [], [], [], [], []
    for i in range(DEPTH):
        kind, j = i % N_MIXERS, i // N_MIXERS
        hp, hs = rmsnorm(xp, attn_norm[i]), rmsnorm(xs, attn_norm[i])
        if kind == 0:
            zp, zs = hp @ a_w_in[j], hs @ a_w_in[j]
            rw = (a_mu[j], a_w0[j], a_w2[j], a_a0[j], a_a2[j], a_g2[j], a_k_k[j], a_k_a[j], a_r_k[j], a_ln_w[j], a_ln_b[j])
            mp, shp, stp = rwkv7_mix(zp[..., :RW_COLS], jnp.zeros((Bp, RW_COLS), xp.dtype),
                                     jnp.zeros((Bp, RW_HEADS, HEAD_DIM, HEAD_DIM), xp.dtype), *rw)
            ms, shs, sts = rwkv7_mix(zs[..., :RW_COLS], state_rwkv_shift[j], state_rwkv_wkv[j], *rw)
            p_rw_sh.append(shp)
            p_rw_wkv.append(stp)
            s_rw_sh.append(shs)
            s_rw_wkv.append(sts)
            w_out = a_w_out[j]
        elif kind == 1:
            zp, zs = hp @ b_w_in[j], hs @ b_w_in[j]
            qp, kp, vp = split_qkv(zp, b_q_norm[j], b_k_norm[j])
            mp = swa_prompt(qp, kp, vp, b_sink[j])
            p_sw_k.append(kp[:, -win_rows:])
            p_sw_v.append(vp[:, -win_rows:])
            qs, ks_, vs_ = split_qkv(zs, b_q_norm[j], b_k_norm[j])
            k_all = jnp.concatenate([cache_swa_k[j].astype(ks_.dtype), ks_], axis=1)
            v_all = jnp.concatenate([cache_swa_v[j].astype(vs_.dtype), vs_], axis=1)
            ms = sink_attend(qs, k_all, v_all, jnp.ones((k_all.shape[1],), bool), b_sink[j]).reshape(Bd, S, Q_COLS)
            s_sw_k.append(k_all[:, -win_rows:])
            s_sw_v.append(v_all[:, -win_rows:])
            w_out = b_w_out[j]
        else:
            zp, zs = hp @ c_w_in[j], hs @ c_w_in[j]
            qp, kp, vp = split_qkv(zp, c_q_norm[j], c_k_norm[j])
            qip, kip, wip = split_indexer(zp, c_idx_k_norm[j])
            mp = dsa_prompt(qp, kp, vp, qip, wip, kip, k_sel_p)
            p_ds_k.append(kp)
            p_ds_v.append(vp)
            p_ds_i.append(kip)
            qs, ks_, vs_ = split_qkv(zs, c_q_norm[j], c_k_norm[j])
            qis, kis, wis = split_indexer(zs, c_idx_k_norm[j])
            k_all = jnp.concatenate([cache_dsa_k[j].astype(ks_.dtype), ks_], axis=1)
            v_all = jnp.concatenate([cache_dsa_v[j].astype(vs_.dtype), vs_], axis=1)
            ki_all = jnp.concatenate([cache_dsa_idx_k[j].astype(kis.dtype), kis], axis=1)
            _, idx = lax.top_k(index_scores(qis, wis, ki_all), k_sel_s)
            ms = gathered_attend(qs, k_all, v_all, idx, jnp.ones(idx.shape, bool)).reshape(Bd, S, Q_COLS)
            s_ds_k.append(ks_)
            s_ds_v.append(vs_)
            s_ds_i.append(kis)
            w_out = c_w_out[j]
        km_p, vm_p = mem_kv(mem_prompt, mem_norm[i], mem_w_kv[i], mem_k_norm[i])
        p_mk.append(km_p)
        p_mv.append(vm_p)
        xp = xp + merge_out(mp, zp, km_p, vm_p, mem_q_norm[i], w_out)
        xs = xs + merge_out(ms, zs, cache_mem_k[i].astype(xs.dtype), cache_mem_v[i].astype(xs.dtype), mem_q_norm[i], w_out)
        fp, cp = conv_ffn(xp, ffn_norm[i], ffn_w_up[i], ffn_conv[i], ffn_w_down[i],
                          jnp.zeros((Bp, CONV_W - 1, 2 * D_FF), xp.dtype))
        fs, cs = conv_ffn(xs, ffn_norm[i], ffn_w_up[i], ffn_conv[i], ffn_w_down[i], state_ffn_conv[i])
        p_cv.append(cp)
        s_cv.append(cs)
        xp = xp + fp
        xs = xs + fs
    st = jnp.stack
    return (xp, xs,
            st(p_rw_wkv), st(p_rw_sh), st(p_sw_k), st(p_sw_v), st(p_ds_k), st(p_ds_v), st(p_ds_i),
            st(p_mk), st(p_mv), st(p_cv),
            st(s_rw_wkv), st(s_rw_sh), st(s_sw_k), st(s_sw_v), st(s_ds_k), st(s_ds_v), st(s_ds_i), st(s_cv))
```

```python
import functools

import jax
import jax.numpy as jnp
from jax import lax
from jax.experimental import pallas as pl
from jax.experimental.pallas import tpu as pltpu

F32 = jnp.float32
BF16 = jnp.bfloat16

HEAD_DIM = 64
CHUNK = 64
NORM_EPS = 1e-6
RW_GN_EPS = HEAD_DIM * 1e-5
ATT_KV_HEADS = 4
WINDOW = 128
IDX_HEADS = 8
IDX_DIM = 64
TOPK_MAX = 256
MEM_HEADS = 4
CONV_W = 3

LANE = 128
VMEM_LIMIT = 52 * 1024 * 1024
NEG_BIG = -1e30
INT_MIN = -2147483648


def _cparams(sem, vmem=VMEM_LIMIT):
    return pltpu.CompilerParams(dimension_semantics=sem, vmem_limit_bytes=vmem)


def _split3(a):
    a1 = a.astype(BF16)
    r1 = a - a1.astype(F32)
    a2 = r1.astype(BF16)
    r2 = r1 - a2.astype(F32)
    return a1, a2, r2.astype(BF16)


def _dot_exact_rhs(a, e):
    a1, a2, a3 = _split3(a)
    d = lambda x: jnp.dot(x, e, preferred_element_type=F32)
    return (d(a3) + d(a2)) + d(a1)


def _dot_exact_lhs(e, a):
    a1, a2, a3 = _split3(a)
    d = lambda x: jnp.dot(e, x, preferred_element_type=F32)
    return (d(a3) + d(a2)) + d(a1)


def _head_block_matrix(width, value):
    r = lax.broadcasted_iota(jnp.int32, (width, width), 0) // HEAD_DIM
    c = lax.broadcasted_iota(jnp.int32, (width, width), 1) // HEAD_DIM
    return jnp.where(r == c, value, 0.0).astype(BF16)


def _dot_hi(a, b, dims):
    return lax.dot_general(a, b, (dims, ((), ())), precision=lax.Precision.HIGHEST,
                           preferred_element_type=F32)


def _pick_tile(n, cap):
    best = None
    for t in range(LANE, min(n, cap) + 1, LANE):
        if n % t == 0:
            best = t
    assert best is not None, n
    return best


def _mm_norm_body(x_ref, g_ref, w_ref, o_ref, xn_ref):
    @pl.when(pl.program_id(1) == 0)
    def _():
        x = x_ref[...]
        ms = jnp.mean(x * x, axis=-1, keepdims=True)
        xn_ref[...] = ((x * lax.rsqrt(ms + NORM_EPS)) * g_ref[...]).astype(BF16)

    o_ref[...] = jnp.dot(xn_ref[...], w_ref[...], preferred_element_type=F32)


def _mm_norm(x, gain, w):
    m, k = x.shape
    n = w.shape[1]
    tm = min(m, 512)
    tn = _pick_tile(n, 1536)
    return pl.pallas_call(
        _mm_norm_body,
        out_shape=jax.ShapeDtypeStruct((m, n), F32),
        grid=(m // tm, n // tn),
        in_specs=[pl.BlockSpec((tm, k), lambda i, j: (i, 0)),
                  pl.BlockSpec((1, k), lambda i, j: (0, 0)),
                  pl.BlockSpec((k, tn), lambda i, j: (0, j))],
        out_specs=pl.BlockSpec((tm, tn), lambda i, j: (i, j)),
        scratch_shapes=[pltpu.VMEM((tm, k), BF16)],
        compiler_params=_cparams(("parallel", "arbitrary")),
        name="mm_norm",
    )(x, gain.reshape(1, k), w)


def _mm_res_body(*refs, n_lhs):
    lhs = refs[:n_lhs]
    ws = refs[n_lhs:2 * n_lhs]
    r_ref, o_ref = refs[2 * n_lhs], refs[2 * n_lhs + 1]
    acc = jnp.dot(lhs[0][...], ws[0][...], preferred_element_type=F32)
    for a, w in zip(lhs[1:], ws[1:]):
        acc = acc + jnp.dot(a[...], w[...], preferred_element_type=F32)
    o_ref[...] = r_ref[...] + acc


def _mm_res(lhs_list, w_list, res):
    m, n = res.shape
    ktot = sum(a.shape[1] for a in lhs_list)
    tm = min(m, 512)
    tn = _pick_tile(n, 1024 if ktot <= 3072 else 512)
    n_lhs = len(lhs_list)
    in_specs = [pl.BlockSpec((tm, a.shape[1]), lambda i, j: (i, 0)) for a in lhs_list]
    in_specs += [pl.BlockSpec((w.shape[0], tn), lambda i, j: (0, j)) for w in w_list]
    in_specs += [pl.BlockSpec((tm, tn), lambda i, j: (i, j))]
    return pl.pallas_call(
        functools.partial(_mm_res_body, n_lhs=n_lhs),
        out_shape=jax.ShapeDtypeStruct((m, n), F32),
        grid=(m // tm, n // tn),
        in_specs=in_specs,
        out_specs=pl.BlockSpec((tm, tn), lambda i, j: (i, j)),
        compiler_params=_cparams(("parallel", "arbitrary")),
        name="mm_res",
    )(*lhs_list, *w_list, res)


def _headnorm_body(x_ref, g_ref, o_ref, *, width):
    avg = _head_block_matrix(LANE, 1.0 / HEAD_DIM)
    for c in range(width // LANE):
        x = x_ref[:, c * LANE:(c + 1) * LANE]
        ms = _dot_exact_rhs(x * x, avg)
        o_ref[:, c * LANE:(c + 1) * LANE] = (x * lax.rsqrt(ms + NORM_EPS)) * g_ref[:, c * LANE:(c + 1) * LANE]


def _headnorm(x, col_block, width, gain_row):
    m = x.shape[0]
    tm = min(m, 1024)
    return pl.pallas_call(
        functools.partial(_headnorm_body, width=width),
        out_shape=jax.ShapeDtypeStruct((m, width), F32),
        grid=(m // tm,),
        in_specs=[pl.BlockSpec((tm, width), lambda i: (i, col_block)),
                  pl.BlockSpec((1, width), lambda i: (0, 0))],
        out_specs=pl.BlockSpec((tm, width), lambda i: (i, 0)),
        compiler_params=_cparams(("parallel",)),
        name="headnorm",
    )(x, gain_row)


def _conv_gate_body(ua_ref, ub_ref, pa_ref, pb_ref, wa_ref, wb_ref, o_ref, ca_ref, cb_ref, *, tm):
    @pl.when(pl.program_id(2) == 0)
    def _():
        ca_ref[...] = pa_ref[0]
        cb_ref[...] = pb_ref[0]

    row = lax.broadcasted_iota(jnp.int32, ua_ref.shape[1:], 0)

    def conv(u_ref, c_ref, w_ref):
        u = u_ref[0]
        c0, c1 = c_ref[0:1, :], c_ref[1:2, :]
        u1 = jnp.where(row == 0, c1, pltpu.roll(u, 1, axis=0))
        u2 = jnp.where(row == 0, c0, jnp.where(row == 1, c1, pltpu.roll(u, 2, axis=0)))
        c_ref[...] = u[tm - 2:tm, :]
        return (u2 * w_ref[0:1, :] + u1 * w_ref[1:2, :]) + u * w_ref[2:3, :]

    a = conv(ua_ref, ca_ref, wa_ref)
    b = conv(ub_ref, cb_ref, wb_ref)
    o_ref[0] = ((a * jax.nn.sigmoid(a)) * b).astype(o_ref.dtype)


def _conv_gate(u, prev, conv_w):
    b, t, f2 = u.shape
    f = f2 // 2
    tm = min(t, 512)
    tc = _pick_tile(f, 512)
    ncb = f // tc
    return pl.pallas_call(
        functools.partial(_conv_gate_body, tm=tm),
        out_shape=jax.ShapeDtypeStruct((b, t, f), BF16),
        grid=(b, ncb, t // tm),
        in_specs=[pl.BlockSpec((1, tm, tc), lambda bi, c, ti: (bi, ti, c)),
                  pl.BlockSpec((1, tm, tc), lambda bi, c, ti: (bi, ti, ncb + c)),
                  pl.BlockSpec((1, CONV_W - 1, tc), lambda bi, c, ti: (bi, 0, c)),
                  pl.BlockSpec((1, CONV_W - 1, tc), lambda bi, c, ti: (bi, 0, ncb + c)),
                  pl.BlockSpec((CONV_W, tc), lambda bi, c, ti: (0, c)),
                  pl.BlockSpec((CONV_W, tc), lambda bi, c, ti: (0, ncb + c))],
        out_specs=pl.BlockSpec((1, tm, tc), lambda bi, c, ti: (bi, ti, c)),
        scratch_shapes=[pltpu.VMEM((CONV_W - 1, tc), F32), pltpu.VMEM((CONV_W - 1, tc), F32)],
        compiler_params=_cparams(("parallel", "parallel", "arbitrary")),
        name="conv_gate",
    )(u, u, prev, prev, conv_w, conv_w)


def _flash_body(*refs, mode, tq, kb, wq, group, n_valid, nk, causal):
    q_ref, k_ref, v_ref, qg_ref, sink_ref = refs[:5]
    pos = 5
    if mode == "dsa":
        sc_ref, tau_ref, jb_ref = refs[pos:pos + 3]
        pos += 3
    o_ref = refs[pos]
    qn_scr, acc_scr, m_scr, l_scr, kab_scr, vab_scr, bias_scr = refs[pos + 1:]

    i = pl.program_id(1)
    j = pl.program_id(2)
    npairs = wq // LANE
    pairs_per_unit = group // 2 if group > 1 else 1
    n_units = npairs // pairs_per_unit
    lane = lax.broadcasted_iota(jnp.int32, (1, LANE), 1)
    lo_half = lane < HEAD_DIM

    @pl.when(j == 0)
    def _():
        avg = _head_block_matrix(LANE, 1.0 / HEAD_DIM)
        for p in range(npairs):
            x = q_ref[0, :, p * LANE:(p + 1) * LANE]
            ms = _dot_exact_rhs(x * x, avg)
            qn = ((x * lax.rsqrt(ms + NORM_EPS)) * qg_ref[...]) * (HEAD_DIM ** -0.5)
            qn_scr[p] = qn.astype(BF16)
        acc_scr[...] = jnp.zeros(acc_scr.shape, F32)
        l_scr[...] = jnp.zeros(l_scr.shape, F32)
        m_scr[...] = jnp.full(m_scr.shape, NEG_BIG, F32)

    if mode == "band":
        kblk = i - WINDOW // kb + j
        active = kblk >= 0
    elif mode == "dsa" and causal:
        kblk = j
        active = j * kb < (i + 1) * tq
    else:
        kblk = j
        active = None

    def step():
        for u in range(n_units):
            if group > 1:
                tile_idx, half = u // 2, u % 2
            else:
                tile_idx, half = u, None
            for src, dst in ((k_ref, kab_scr), (v_ref, vab_scr)):
                tile = src[0, :, tile_idx * LANE:(tile_idx + 1) * LANE]
                if half is None:
                    a_part = jnp.where(lo_half, tile, 0.0)
                    b_part = jnp.where(lo_half, 0.0, tile)
                elif half == 0:
                    a_part = jnp.where(lo_half, tile, 0.0)
                    b_part = pltpu.roll(a_part, HEAD_DIM, axis=1)
                else:
                    b_part = jnp.where(lo_half, 0.0, tile)
                    a_part = pltpu.roll(b_part, HEAD_DIM, axis=1)
                dst[u, 0:kb, :] = a_part.astype(BF16)
                dst[u, kb:2 * kb, :] = b_part.astype(BF16)

        kidx = kblk * kb + lax.broadcasted_iota(jnp.int32, (tq, kb), 1)
        qpos = i * tq + lax.broadcasted_iota(jnp.int32, (tq, kb), 0)
        if mode == "band":
            qchunk = qpos // CHUNK
            sel = (kidx >= (qchunk - WINDOW // CHUNK) * CHUNK) & (kidx < (qchunk + 1) * CHUNK) & (kidx >= 0)
        elif mode == "dsa":
            x = sc_ref[0]
            tau = tau_ref[0][:, 0:1]
            jb = jb_ref[0][:, 0:1]
            adm = kidx < ((qpos // CHUNK + 1) * CHUNK if causal else n_valid)
            sel = adm & ((x > tau) | ((x == tau) & (kidx <= jb)))
        else:
            sel = kidx < n_valid
        bias_scr[...] = jnp.where(sel, 0.0, NEG_BIG)

        def pair_body(p, carry):
            u = p // pairs_per_unit
            s = lax.dot_general(qn_scr[p], kab_scr[u], (((1,), (1,)), ((), ())),
                                preferred_element_type=F32)
            bias = bias_scr[...]
            m_old = m_scr[p]
            s0 = s[:, 0:kb] + bias
            s1 = s[:, kb:2 * kb] + bias
            mn0 = jnp.maximum(m_old[:, 0:1], jnp.max(s0, axis=1, keepdims=True))
            mn1 = jnp.maximum(m_old[:, HEAD_DIM:HEAD_DIM + 1], jnp.max(s1, axis=1, keepdims=True))
            p0 = jnp.exp(s0 - mn0)
            p1 = jnp.exp(s1 - mn1)
            l0 = jnp.sum(p0, axis=1, keepdims=True)
            l1 = jnp.sum(p1, axis=1, keepdims=True)
            pcat = jnp.concatenate([p0, p1], axis=1).astype(BF16)
            pv = jnp.dot(pcat, vab_scr[u], preferred_element_type=F32)
            mn = jnp.where(lo_half, mn0, mn1)
            alpha = jnp.exp(m_old - mn)
            acc_scr[p] = acc_scr[p] * alpha + pv
            l_scr[p] = l_scr[p] * alpha + jnp.where(lo_half, l0, l1)
            m_scr[p] = mn
            return carry

        lax.fori_loop(0, npairs, pair_body, 0)

    if active is None:
        step()
    else:
        pl.when(active)(step)

    @pl.when(j == nk - 1)
    def _():
        for p in range(npairs):
            den = l_scr[p] + jnp.exp(sink_ref[:, p * LANE:(p + 1) * LANE] - m_scr[p])
            o_ref[0, :, p * LANE:(p + 1) * LANE] = (acc_scr[p] / den).astype(o_ref.dtype)


def _flash(q_arr, q_cb, wq, k_arr, k_cb, v_arr, v_cb, q_gain, sink_row, *, mode, tq, kb, group,
           n_valid=None, causal=False, scores=None, tau=None, jb=None):
    b, t_q = q_arr.shape[0], q_arr.shape[1]
    t_k = k_arr.shape[1]
    wk = ATT_KV_HEADS * HEAD_DIM
    nq = t_q // tq
    if mode == "band":
        assert tq == kb == WINDOW
        nk = 2
        kmap = lambda i, j: jnp.maximum(i - 1 + j, 0)
    elif mode == "dsa" and causal:
        nk = t_k // kb
        kmap = lambda i, j: jnp.minimum(j, ((i + 1) * tq - 1) // kb)
    else:
        nk = t_k // kb
        kmap = lambda i, j: j
    npairs = wq // LANE
    n_units = npairs // (group // 2) if group > 1 else npairs
    in_specs = [pl.BlockSpec((1, tq, wq), lambda bi, i, j: (bi, i, q_cb)),
                pl.BlockSpec((1, kb, wk), lambda bi, i, j: (bi, kmap(i, j), k_cb)),
                pl.BlockSpec((1, kb, wk), lambda bi, i, j: (bi, kmap(i, j), v_cb)),
                pl.BlockSpec((1, LANE), lambda bi, i, j: (0, 0)),
                pl.BlockSpec((1, wq), lambda bi, i, j: (0, 0))]
    args = [q_arr, k_arr, v_arr, q_gain, sink_row]
    if mode == "dsa":
        in_specs += [pl.BlockSpec((1, tq, kb), lambda bi, i, j: (bi, i, kmap(i, j))),
                     pl.BlockSpec((1, tq, LANE), lambda bi, i, j: (bi, i, 0)),
                     pl.BlockSpec((1, tq, LANE), lambda bi, i, j: (bi, i, 0))]
        args += [scores, tau, jb]
    body = functools.partial(_flash_body, mode=mode, tq=tq, kb=kb, wq=wq, group=group,
                             n_valid=n_valid, nk=nk, causal=causal)
    return pl.pallas_call(
        body,
        out_shape=jax.ShapeDtypeStruct((b, t_q, wq), BF16),
        grid=(b, nq, nk),
        in_specs=in_specs,
        out_specs=pl.BlockSpec((1, tq, wq), lambda bi, i, j: (bi, i, 0)),
        scratch_shapes=[pltpu.VMEM((npairs, tq, LANE), BF16),
                        pltpu.VMEM((npairs, tq, LANE), F32),
                        pltpu.VMEM((npairs, tq, LANE), F32),
                        pltpu.VMEM((npairs, tq, LANE), F32),
                        pltpu.VMEM((n_units, 2 * kb, LANE), BF16),
                        pltpu.VMEM((n_units, 2 * kb, LANE), BF16),
                        pltpu.VMEM((tq, kb), F32)],
        compiler_params=_cparams(("parallel", "parallel", "arbitrary")),
        name="flash_" + mode,
    )(*args)


def _dsa_select_body(qi_ref, kw_ref, ki_ref, sc_ref, tau_ref, jb_ref, sc_scr, *, tq, kb, nk, k_sel,
                     n_valid, causal, idx_bits):
    i = pl.program_id(1)
    j = pl.program_id(2)
    qpos = i * tq + lax.broadcasted_iota(jnp.int32, (tq, 1), 0)
    if causal:
        n_adm = (qpos // CHUNK + 1) * CHUNK
        active = j * kb < (i + 1) * tq
        n_blocks = ((i + 1) * tq + kb - 1) // kb
    else:
        n_adm = jnp.full((tq, 1), n_valid, jnp.int32)
        active = None
        n_blocks = nk

    def compute():
        ka = ki_ref[0]
        kab = jnp.concatenate([ka, pltpu.roll(ka, HEAD_DIM, axis=1)], axis=0).astype(BF16)
        kw = kw_ref[0]
        acc = jnp.zeros((tq, kb), F32)
        for p in range(IDX_HEADS // 2):
            qp = qi_ref[0, :, p * LANE:(p + 1) * LANE].astype(BF16)
            s = lax.dot_general(qp, kab, (((1,), (1,)), ((), ())), preferred_element_type=F32)
            for h in range(2):
                col = HEAD_DIM + 2 * p + h
                w = (kw[:, col:col + 1] * IDX_HEADS ** -0.5) * IDX_DIM ** -0.5
                acc = acc + jnp.maximum(s[:, h * kb:(h + 1) * kb], 0.0) * w
        acc = jnp.where(acc == 0.0, 0.0, acc)
        kidx = j * kb + lax.broadcasted_iota(jnp.int32, (tq, kb), 1)
        x = jnp.where(kidx < n_adm, acc, -jnp.inf)
        sc_scr[j] = x
        sc_ref[0] = x

    if active is None:
        compute()
    else:
        pl.when(active)(compute)

        @pl.when(jnp.logical_not(active))
        def _():
            sc_ref[0] = jnp.full((tq, kb), -jnp.inf, F32)

    @pl.when(j == nk - 1)
    def _():
        kf = float(k_sel)

        def count(pred):
            def blk(jj, acc):
                h = jnp.where(pred(sc_scr[jj], jj), 1.0, 0.0)
                for a in range(kb // LANE):
                    acc = acc + h[:, a * LANE:(a + 1) * LANE]
                return acc
            acc = lax.fori_loop(0, n_blocks, blk, jnp.zeros((tq, LANE), F32))
            return jnp.sum(acc, axis=1, keepdims=True)

        def key_to_float(tu):
            cs = tu ^ INT_MIN
            fb = jnp.where(cs >= 0, cs, cs ^ 0x7FFFFFFF)
            return lax.bitcast_convert_type(fb, F32)

        def bit_body(it, tu):
            cand = tu | jnp.left_shift(jnp.int32(1), 31 - it)
            thr = key_to_float(cand)
            c = count(lambda x, jj: x >= thr)
            return jnp.where(c >= kf, cand, tu)

        tu = lax.fori_loop(0, 32, bit_body, jnp.zeros((tq, 1), jnp.int32))
        full = n_adm >= k_sel
        tau = jnp.where(full, key_to_float(tu), -jnp.inf)
        c_gt = count(lambda x, jj: x > tau)
        c_ge = count(lambda x, jj: x >= tau)
        need = kf - c_gt
        tau_ref[0] = jnp.broadcast_to(tau, (tq, LANE))
        jb_ref[0] = jnp.full((tq, LANE), 2 ** 30, jnp.int32)
        surplus = jnp.max(jnp.where(full, (c_ge - c_gt) - need, 0.0))

        @pl.when(surplus > 0.5)
        def _():
            def tie_body(it, cut):
                cand = cut | jnp.left_shift(jnp.int32(1), idx_bits - 1 - it)

                def pred(x, jj):
                    kidx = jj * kb + lax.broadcasted_iota(jnp.int32, (tq, kb), 1)
                    return (x == tau) & (kidx < cand)
                c = count(pred)
                return jnp.where(c < need, cand, cut)

            cut = lax.fori_loop(0, idx_bits, tie_body, jnp.zeros((tq, 1), jnp.int32))
            jb_ref[0] = jnp.broadcast_to(cut, (tq, LANE))


def _dsa_select(z, qi_cb, kw_cb, ki_n, *, tq, kb, k_sel, n_valid, causal):
    b, t_q = z.shape[0], z.shape[1]
    t_k = ki_n.shape[1]
    nq, nk = t_q // tq, t_k // kb
    kmap = (lambda i, j: jnp.minimum(j, ((i + 1) * tq - 1) // kb)) if causal else (lambda i, j: j)
    body = functools.partial(_dsa_select_body, tq=tq, kb=kb, nk=nk, k_sel=k_sel, n_valid=n_valid,
                             causal=causal, idx_bits=max(1, (t_k - 1).bit_length()))
    return pl.pallas_call(
        body,
        out_shape=(jax.ShapeDtypeStruct((b, t_q, t_k), F32),
                   jax.ShapeDtypeStruct((b, t_q, LANE), F32),
                   jax.ShapeDtypeStruct((b, t_q, LANE), jnp.int32)),
        grid=(b, nq, nk),
        in_specs=[pl.BlockSpec((1, tq, IDX_HEADS * IDX_DIM), lambda bi, i, j: (bi, i, qi_cb)),
                  pl.BlockSpec((1, tq, LANE), lambda bi, i, j: (bi, i, kw_cb)),
                  pl.BlockSpec((1, kb, LANE), lambda bi, i, j: (bi, kmap(i, j), 0))],
        out_specs=(pl.BlockSpec((1, tq, kb), lambda bi, i, j: (bi, i, j)),
                   pl.BlockSpec((1, tq, LANE), lambda bi, i, j: (bi, i, 0)),
                   pl.BlockSpec((1, tq, LANE), lambda bi, i, j: (bi, i, 0))),
        scratch_shapes=[pltpu.VMEM((nk, tq, kb), F32)],
        compiler_params=_cparams(("parallel", "parallel", "arbitrary")),
        name="dsa_select",
    )(z, z, ki_n)


def _rwkv_body(zr_ref, zk_ref, zv_ref, zw_ref, za_ref, zg_ref,
               sr_ref, sk_ref, sv_ref, sw_ref, sa_ref, sg_ref,
               mr_ref, mk_ref, mv_ref, mw_ref, ma_ref, mg_ref,
               vec_ref, w2_ref, a2_ref, g2_ref, s0_ref,
               mix_ref, sfin_ref,
               s_scr, prev_scr, prevg_scr, buf_scr, y_scr, *, tc, lc, nt):
    t = pl.program_id(2)

    @pl.when(t == 0)
    def _():
        s_scr[...] = s0_ref[0, 0]
        prev_scr[0:1, :] = sr_ref[0]
        prev_scr[1:2, :] = sk_ref[0]
        prev_scr[2:3, :] = sv_ref[0]
        prev_scr[3:4, :] = sw_ref[0]
        prev_scr[4:5, :] = sa_ref[0]
        prevg_scr[0:1, :] = sg_ref[0]

    row = lax.broadcasted_iota(jnp.int32, (tc, 1), 0)

    def shifted(z_ref, prow, mu_ref):
        z = z_ref[0]
        zp = jnp.where(row == 0, prow, pltpu.roll(z, 1, axis=0))
        return z + (zp - z) * mu_ref[...], z[tc - 1:tc, :]

    r, last_r = shifted(zr_ref, prev_scr[0:1, :], mr_ref)
    k, last_k = shifted(zk_ref, prev_scr[1:2, :], mk_ref)
    v, last_v = shifted(zv_ref, prev_scr[2:3, :], mv_ref)
    zw, last_w = shifted(zw_ref, prev_scr[3:4, :], mw_ref)
    za, last_a = shifted(za_ref, prev_scr[4:5, :], ma_ref)
    zg, last_g = shifted(zg_ref, prevg_scr[0:1, :], mg_ref)
    prev_scr[0:1, :] = last_r
    prev_scr[1:2, :] = last_k
    prev_scr[2:3, :] = last_v
    prev_scr[3:4, :] = last_w
    prev_scr[4:5, :] = last_a
    prevg_scr[0:1, :] = last_g

    w0, a0 = vec_ref[0:1, :], vec_ref[1:2, :]
    k_k, k_a, r_k = vec_ref[2:3, :], vec_ref[3:4, :], vec_ref[4:5, :]
    ln_w, ln_b = vec_ref[5:6, :], vec_ref[6:7, :]

    ones_blk = _head_block_matrix(LANE, 1.0)
    avg_blk = _head_block_matrix(LANE, 1.0 / HEAD_DIM)

    xw = w0 + jnp.dot(jnp.tanh(zw).astype(BF16), w2_ref[...], preferred_element_type=F32)
    nx = -xw
    softplus = jnp.maximum(nx, 0.0) + jnp.log1p(jnp.exp(-jnp.abs(nx)))
    w_log = -softplus - 0.5
    lw = -jnp.exp(w_log)
    a = jax.nn.sigmoid(a0 + jnp.dot(za.astype(BF16), a2_ref[...], preferred_element_type=F32))
    g = jnp.dot(jax.nn.sigmoid(zg).astype(BF16), g2_ref[...], preferred_element_type=F32)
    kk = k * k_k
    kk = kk / jnp.maximum(jnp.sqrt(_dot_exact_rhs(kk * kk, ones_blk)), 1e-12)
    k2 = k * (1.0 + (a - 1.0) * k_a)

    buf_scr[0] = r
    buf_scr[1] = lw
    buf_scr[2] = k2
    buf_scr[3] = v
    buf_scr[4] = -kk
    buf_scr[5] = kk * a

    lane = lax.broadcasted_iota(jnp.int32, (1, LANE), 1)
    m0 = lane < HEAD_DIM
    rr = lax.broadcasted_iota(jnp.int32, (lc, 2 * lc), 0)
    cc = lax.broadcasted_iota(jnp.int32, (lc, 2 * lc), 1)
    strict0 = (cc < lc) & (cc < rr)
    strict1 = (cc >= lc) & (cc - lc < rr)
    incl = jnp.where(cc < lc, cc, cc - lc) <= rr
    tri = (lax.broadcasted_iota(jnp.int32, (lc, lc), 1)
           <= lax.broadcasted_iota(jnp.int32, (lc, lc), 0)).astype(BF16)
    nsteps = lc.bit_length() - 1
    nt_dims = ((1,), (1,))
    nn_dims = ((1,), (0,))
    tn_dims = ((0,), (0,))

    def chunk(c, carry):
        rows = pl.ds(pl.multiple_of(c * lc, lc), lc)
        rc, lwc, kc, vc = buf_scr[0, rows, :], buf_scr[1, rows, :], buf_scr[2, rows, :], buf_scr[3, rows, :]
        ac, bc = buf_scr[4, rows, :], buf_scr[5, rows, :]
        s_prev = s_scr[...]
        cs = _dot_exact_lhs(tri, lwc)
        p_in = jnp.exp(-cs)
        at = ac * jnp.exp(cs - lwc)
        bt = bc * p_in
        kt = kc * p_in
        rt = rc * jnp.exp(cs)
        p_last = jnp.exp(cs[lc - 1:lc, :])
        bk_s = jnp.concatenate([jnp.where(m0, bt, 0.0), jnp.where(m0, 0.0, bt),
                                jnp.where(m0, kt, 0.0), jnp.where(m0, 0.0, kt)], axis=0)
        gram = _dot_hi(jnp.concatenate([at, rt], axis=0), bk_s, nt_dims)
        g_ab, g_ak = gram[0:lc, 0:2 * lc], gram[0:lc, 2 * lc:4 * lc]
        g_rb, g_rk = gram[lc:2 * lc, 0:2 * lc], gram[lc:2 * lc, 2 * lc:4 * lc]
        t_m = jnp.concatenate([jnp.where(strict0, g_ab, 0.0), jnp.where(strict1, g_ab, 0.0)], axis=0)
        w_ak = jnp.concatenate([jnp.where(strict0, g_ak, 0.0), jnp.where(strict1, g_ak, 0.0)], axis=0)
        v_s = jnp.concatenate([jnp.where(m0, vc, 0.0), jnp.where(m0, 0.0, vc)], axis=0)
        x0 = _dot_hi(at, s_prev, nt_dims)
        u_s = jnp.concatenate([x0, x0], axis=0) + _dot_hi(w_ak, v_s, nn_dims)
        t_p = t_m
        for step in range(nsteps):
            u_s = u_s + _dot_hi(t_p, u_s, nn_dims)
            if step < nsteps - 1:
                t_p = _dot_hi(t_p, t_p, nn_dims)
        u_sm = jnp.concatenate([jnp.where(m0, u_s[0:lc], 0.0), jnp.where(m0, 0.0, u_s[lc:2 * lc])], axis=0)
        y = (_dot_hi(rt, s_prev, nt_dims)
             + _dot_hi(jnp.where(incl, g_rb, 0.0), u_sm, nn_dims)
             + _dot_hi(jnp.where(incl, g_rk, 0.0), v_s, nn_dims))
        y_scr[rows, :] = y
        ds = _dot_hi(jnp.concatenate([u_sm, v_s], axis=0), bk_s, tn_dims)
        s_scr[...] = (s_prev + ds) * p_last
        return carry

    lax.fori_loop(0, tc // lc, chunk, 0)

    y = y_scr[...]
    mean = _dot_exact_rhs(y, avg_blk)
    d = y - mean
    var = _dot_exact_rhs(d * d, avg_blk)
    yn = (d * lax.rsqrt(var + RW_GN_EPS)) * ln_w + ln_b
    bonus = _dot_exact_rhs((r * k2) * r_k, ones_blk) * v
    mix_ref[0] = ((yn + bonus) * g).astype(mix_ref.dtype)

    @pl.when(t == nt - 1)
    def _():
        sfin_ref[0, 0] = s_scr[...]


def _rwkv(z, shift_prev, s0_pairs, mu, vecs, w2, a2, g2, d_model):
    b, t = z.shape[0], z.shape[1]
    npairs = d_model // LANE
    tc = min(t, 512)
    lc = min(CHUNK, t)
    nt = t // tc
    cb_w, cb_a, cb_g = 3 * npairs, 3 * npairs + 1, (3 * npairs + 2) // 2

    def zspec(width, cbf):
        return pl.BlockSpec((1, tc, width), lambda bi, p, ti: (bi, ti, cbf(p)))

    def sspec(width, cbf):
        return pl.BlockSpec((1, 1, width), lambda bi, p, ti: (bi, 0, cbf(p)))

    def mspec(width, cbf):
        return pl.BlockSpec((1, width), lambda bi, p, ti: (0, cbf(p)))

    cbfs = [(LANE, lambda p: p), (LANE, lambda p: npairs + p), (LANE, lambda p: 2 * npairs + p),
            (LANE, lambda p: cb_w), (LANE, lambda p: cb_a), (2 * LANE, lambda p: cb_g)]
    in_specs = ([zspec(w, f) for w, f in cbfs] + [sspec(w, f) for w, f in cbfs] + [mspec(w, f) for w, f in cbfs]
                + [pl.BlockSpec((8, LANE), lambda bi, p, ti: (0, p)),
                   pl.BlockSpec((LANE, LANE), lambda bi, p, ti: (0, p)),
                   pl.BlockSpec((LANE, LANE), lambda bi, p, ti: (0, p)),
                   pl.BlockSpec((2 * LANE, LANE), lambda bi, p, ti: (0, p)),
                   pl.BlockSpec((1, 1, LANE, LANE), lambda bi, p, ti: (bi, p, 0, 0))])
    return pl.pallas_call(
        functools.partial(_rwkv_body, tc=tc, lc=lc, nt=nt),
        out_shape=(jax.ShapeDtypeStruct((b, t, d_model), BF16),
                   jax.ShapeDtypeStruct((b, npairs, LANE, LANE), F32)),
        grid=(b, npairs, nt),
        in_specs=in_specs,
        out_specs=(pl.BlockSpec((1, tc, LANE), lambda bi, p, ti: (bi, ti, p)),
                   pl.BlockSpec((1, 1, LANE, LANE), lambda bi, p, ti: (bi, p, 0, 0))),
        scratch_shapes=[pltpu.VMEM((LANE, LANE), F32),
                        pltpu.VMEM((8, LANE), F32),
                        pltpu.VMEM((8, 2 * LANE), F32),
                        pltpu.VMEM((6, tc, LANE), F32),
                        pltpu.VMEM((tc, LANE), F32)],
        compiler_params=_cparams(("parallel", "parallel", "arbitrary")),
        name="rwkv7",
    )(*([z] * 6), *([shift_prev] * 6), *([mu] * 6), vecs, w2, a2, g2, s0_pairs)


def _pad_cols(x, segments):
    parts = []
    for start, width, padded in segments:
        seg = x[..., start:start + width]
        if padded > width:
            seg = jnp.concatenate([seg, jnp.zeros(seg.shape[:-1] + (padded - width,), seg.dtype)], axis=-1)
        parts.append(seg)
    return jnp.concatenate(parts, axis=-1)


def _pad_rows(x, padded):
    return jnp.concatenate([x, jnp.zeros((padded - x.shape[0],) + x.shape[1:], x.dtype)], axis=0)


def _pairs_from_heads(s):
    b, h = s.shape[0], s.shape[1]
    s = s.reshape(b, h // 2, 2, HEAD_DIM, HEAD_DIM)
    z = jnp.zeros_like(s[:, :, 0])
    top = jnp.concatenate([s[:, :, 0], z], axis=-1)
    bot = jnp.concatenate([z, s[:, :, 1]], axis=-1)
    return jnp.concatenate([top, bot], axis=-2)


def _heads_from_pairs(sp):
    b, npairs = sp.shape[0], sp.shape[1]
    s = jnp.stack([sp[:, :, :HEAD_DIM, :HEAD_DIM], sp[:, :, HEAD_DIM:, HEAD_DIM:]], axis=2)
    return s.reshape(b, 2 * npairs, HEAD_DIM, HEAD_DIM)


def _tile_gain(g, width):
    return jnp.tile(g.astype(F32), width // HEAD_DIM).reshape(1, width)


def _mem_attend(z, memq_cb, km, vm, q_gain):
    t_q = z.shape[1]
    wq = MEM_HEADS * HEAD_DIM
    tq = min(t_q, 512)
    no_sink = jnp.full((1, wq), -jnp.inf, F32)
    return _flash(z, memq_cb, wq, km, 0, vm, 0, _tile_gain(q_gain, LANE), no_sink,
                  mode="all", tq=tq, kb=km.shape[1], group=1, n_valid=km.shape[1])


def _conv_ffn(x2d, b, t, gain, w_up, conv_w, w_down, prev):
    u = _mm_norm(x2d, gain, w_up)
    u3 = u.reshape(b, t, -1)
    act = _conv_gate(u3, prev, conv_w)
    y = _mm_res([act.reshape(b * t, -1)], [w_down], x2d)
    return y, u3[:, t - (CONV_W - 1):]


def kernel(x_prompt, x_sample, state_rwkv_wkv, state_rwkv_shift, cache_swa_k, cache_swa_v, cache_dsa_k, cache_dsa_v, cache_dsa_idx_k, cache_mem_k, cache_mem_v, state_ffn_conv, mem_prompt, attn_norm, ffn_norm, mem_norm, mem_w_kv, mem_q_norm, mem_k_norm, a_w_in, a_mu, a_w0, a_w2, a_a0, a_a2, a_g2, a_k_k, a_k_a, a_r_k, a_ln_w, a_ln_b, a_w_out, b_w_in, b_q_norm, b_k_norm, b_sink, b_w_out, c_w_in, c_q_norm, c_k_norm, c_idx_k_norm, c_w_out, ffn_w_up, ffn_conv, ffn_w_down):
    bp, t, d = x_prompt.shape
    bd, s_len = x_sample.shape[:2]
    depth = attn_norm.shape[0]
    win_rows = cache_swa_k.shape[2]
    past = cache_dsa_k.shape[2] if cache_dsa_k.shape[0] else 0
    d_ff = ffn_w_down.shape[1]
    mem_tokens = mem_prompt.shape[1]
    q_cols = d
    kv_cols = ATT_KV_HEADS * HEAD_DIM
    memq_cols = MEM_HEADS * HEAD_DIM
    att_group = (d // HEAD_DIM) // ATT_KV_HEADS
    dec_lora = a_w2.shape[1]
    a_lora = a_a2.shape[1]
    g_lora = a_g2.shape[1]
    rw_cols = 3 * d + dec_lora + a_lora + g_lora
    k_sel_p = min(TOPK_MAX, t // 4)
    k_sel_s = min(TOPK_MAX, (past + s_len) // 4)
    assert g_lora == 2 * LANE and dec_lora <= LANE and a_lora <= LANE

    xp = x_prompt.reshape(bp * t, d)
    xs = x_sample.reshape(bd * s_len, d)

    rw_segments = [(0, 3 * d, 3 * d), (3 * d, dec_lora, LANE), (3 * d + dec_lora, a_lora, LANE),
                   (3 * d + dec_lora + a_lora, g_lora, g_lora)]
    rw_padded = 3 * d + 2 * LANE + g_lora
    o_qi = q_cols + 2 * kv_cols
    o_ki = o_qi + IDX_HEADS * IDX_DIM
    c_cols = o_ki + IDX_DIM + IDX_HEADS

    outs = {k: [] for k in ("p_rw_wkv", "p_rw_sh", "p_sw_k", "p_sw_v", "p_ds_k", "p_ds_v", "p_ds_i", "p_mk",
                            "p_mv", "p_cv", "s_rw_wkv", "s_rw_sh", "s_sw_k", "s_sw_v", "s_ds_k", "s_ds_v",
                            "s_ds_i", "s_cv")}

    def unpad_rw(row):
        return jnp.concatenate([row[..., :3 * d], row[..., 3 * d:3 * d + dec_lora],
                                row[..., 3 * d + LANE:3 * d + LANE + a_lora],
                                row[..., 3 * d + 2 * LANE:3 * d + 2 * LANE + g_lora]], axis=-1)

    for i in range(depth):
        kind, j = i % 3, i // 3
        if kind == 0:
            w_in = jnp.concatenate([_pad_cols(a_w_in[j], rw_segments), a_w_in[j][:, rw_cols:]], axis=1).astype(BF16)
            memq_cb = rw_padded // memq_cols
            zp = _mm_norm(xp, attn_norm[i], w_in).reshape(bp, t, -1)
            zs = _mm_norm(xs, attn_norm[i], w_in).reshape(bd, s_len, -1)
            mu = _pad_cols(a_mu[j].reshape(1, -1), rw_segments)
            mu = jnp.concatenate([mu, jnp.zeros((1, memq_cols), F32)], axis=1)
            vecs = jnp.stack([a_w0[j], a_a0[j], a_k_k[j], a_k_a[j], a_r_k[j].reshape(-1), a_ln_w[j], a_ln_b[j],
                              jnp.zeros((d,), F32)], axis=0)
            w2 = _pad_rows(a_w2[j], LANE).astype(BF16)
            a2 = _pad_rows(a_a2[j], LANE).astype(BF16)
            g2 = a_g2[j].astype(BF16)
            sh_p = jnp.zeros((bp, 1, zp.shape[-1]), F32)
            st_p = jnp.zeros((bp, d // LANE, LANE, LANE), F32)
            sh_s = _pad_cols(state_rwkv_shift[j], rw_segments)
            sh_s = jnp.concatenate([sh_s, jnp.zeros((bd, memq_cols), F32)], axis=1).reshape(bd, 1, -1)
            st_s = _pairs_from_heads(state_rwkv_wkv[j])
            mp, stp = _rwkv(zp, sh_p, st_p, mu, vecs, w2, a2, g2, d)
            ms, sts = _rwkv(zs, sh_s, st_s, mu, vecs, w2, a2, g2, d)
            outs["p_rw_sh"].append(unpad_rw(zp[:, -1]))
            outs["p_rw_wkv"].append(_heads_from_pairs(stp))
            outs["s_rw_sh"].append(unpad_rw(zs[:, -1]))
            outs["s_rw_wkv"].append(_heads_from_pairs(sts))
            w_out = a_w_out[j]
        elif kind == 1:
            w_in = b_w_in[j].astype(BF16)
            memq_cb = (q_cols + 2 * kv_cols) // memq_cols
            k_cb, v_cb = q_cols // kv_cols, q_cols // kv_cols + 1
            zp = _mm_norm(xp, attn_norm[i], w_in).reshape(bp, t, -1)
            zs = _mm_norm(xs, attn_norm[i], w_in).reshape(bd, s_len, -1)
            kgain = _tile_gain(b_k_norm[j], kv_cols)
            qgain = _tile_gain(b_q_norm[j], LANE)
            sink = jnp.repeat(b_sink[j].astype(F32), HEAD_DIM).reshape(1, q_cols)
            knp = _headnorm(zp.reshape(bp * t, -1), k_cb, kv_cols, kgain).reshape(bp, t, kv_cols)
            mp = _flash(zp, 0, q_cols, knp, 0, zp, v_cb, qgain, sink, mode="band", tq=WINDOW, kb=WINDOW,
                        group=att_group)
            outs["p_sw_k"].append(knp[:, t - win_rows:].reshape(bp, win_rows, ATT_KV_HEADS, HEAD_DIM))
            outs["p_sw_v"].append(zp[:, t - win_rows:, q_cols + kv_cols:q_cols + 2 * kv_cols]
                                  .reshape(bp, win_rows, ATT_KV_HEADS, HEAD_DIM))
            kns = _headnorm(zs.reshape(bd * s_len, -1), k_cb, kv_cols, kgain).reshape(bd, s_len, kv_cols)
            vs_new = zs[:, :, q_cols + kv_cols:q_cols + 2 * kv_cols]
            k_all = jnp.concatenate([cache_swa_k[j].reshape(bd, win_rows, kv_cols), kns], axis=1)
            v_all = jnp.concatenate([cache_swa_v[j].reshape(bd, win_rows, kv_cols), vs_new], axis=1)
            n_keys = win_rows + s_len
            n_pad = -(-n_keys // LANE) * LANE
            pad = jnp.zeros((bd, n_pad - n_keys, kv_cols), F32)
            ms = _flash(zs, 0, q_cols, jnp.concatenate([k_all, pad], axis=1), 0,
                        jnp.concatenate([v_all, pad], axis=1), 0, qgain, sink, mode="all", tq=s_len, kb=n_pad,
                        group=att_group, n_valid=n_keys)
            outs["s_sw_k"].append(k_all[:, n_keys - win_rows:].reshape(bd, win_rows, ATT_KV_HEADS, HEAD_DIM))
            outs["s_sw_v"].append(v_all[:, n_keys - win_rows:].reshape(bd, win_rows, ATT_KV_HEADS, HEAD_DIM))
            w_out = b_w_out[j]
        else:
            wc = c_w_in[j]
            w_in = jnp.concatenate([wc[:, :o_ki], wc[:, c_cols:],
                                    _pad_cols(wc, [(o_ki, IDX_DIM + IDX_HEADS, LANE)])], axis=1).astype(BF16)
            memq_cb = o_ki // memq_cols
            kw_cb = (o_ki + memq_cols) // LANE
            k_cb, v_cb = q_cols // kv_cols, q_cols // kv_cols + 1
            qi_cb = o_qi // (IDX_HEADS * IDX_DIM)
            zp = _mm_norm(xp, attn_norm[i], w_in).reshape(bp, t, -1)
            zs = _mm_norm(xs, attn_norm[i], w_in).reshape(bd, s_len, -1)
            kgain = _tile_gain(c_k_norm[j], kv_cols)
            qgain = _tile_gain(c_q_norm[j], LANE)
            igain = jnp.concatenate([c_idx_k_norm[j].astype(F32), jnp.zeros((LANE - IDX_DIM,), F32)]).reshape(1, LANE)
            no_sink = jnp.full((1, q_cols), -jnp.inf, F32)
            knp = _headnorm(zp.reshape(bp * t, -1), k_cb, kv_cols, kgain).reshape(bp, t, kv_cols)
            kip = _headnorm(zp.reshape(bp * t, -1), kw_cb, LANE, igain).reshape(bp, t, LANE)
            tq = min(t, 256)
            kb = min(t, 512)
            sc, tau, cut = _dsa_select(zp, qi_cb, kw_cb, kip, tq=tq, kb=kb, k_sel=k_sel_p, n_valid=t, causal=True)
            mp = _flash(zp, 0, q_cols, knp, 0, zp, v_cb, qgain, no_sink, mode="dsa", tq=tq, kb=kb, group=att_group,
                        n_valid=t, causal=True, scores=sc, tau=tau, jb=cut)
            outs["p_ds_k"].append(knp.reshape(bp, t, ATT_KV_HEADS, HEAD_DIM))
            outs["p_ds_v"].append(zp[:, :, q_cols + kv_cols:q_cols + 2 * kv_cols].reshape(bp, t, ATT_KV_HEADS, HEAD_DIM))
            outs["p_ds_i"].append(kip[:, :, :IDX_DIM])
            kns = _headnorm(zs.reshape(bd * s_len, -1), k_cb, kv_cols, kgain).reshape(bd, s_len, kv_cols)
            kis = _headnorm(zs.reshape(bd * s_len, -1), kw_cb, LANE, igain).reshape(bd, s_len, LANE)
            vs_new = zs[:, :, q_cols + kv_cols:q_cols + 2 * kv_cols]
            n_keys = past + s_len
            n_pad = -(-n_keys // LANE) * LANE
            zpad = lambda w: jnp.zeros((bd, n_pad - n_keys, w), F32)
            k_all = jnp.concatenate([cache_dsa_k[j].reshape(bd, past, kv_cols), kns, zpad(kv_cols)], axis=1)
            v_all = jnp.concatenate([cache_dsa_v[j].reshape(bd, past, kv_cols), vs_new, zpad(kv_cols)], axis=1)
            ki_cache = jnp.concatenate([cache_dsa_idx_k[j], jnp.zeros((bd, past, LANE - IDX_DIM), F32)], axis=-1)
            ki_all = jnp.concatenate([ki_cache, kis, zpad(LANE)], axis=1)
            sc, tau, cut = _dsa_select(zs, qi_cb, kw_cb, ki_all, tq=s_len, kb=n_pad, k_sel=k_sel_s, n_valid=n_keys,
                                       causal=False)
            ms = _flash(zs, 0, q_cols, k_all, 0, v_all, 0, qgain, no_sink, mode="dsa", tq=s_len, kb=n_pad,
                        group=att_group, n_valid=n_keys, causal=False, scores=sc, tau=tau, jb=cut)
            outs["s_ds_k"].append(kns.reshape(bd, s_len, ATT_KV_HEADS, HEAD_DIM))
            outs["s_ds_v"].append(vs_new.reshape(bd, s_len, ATT_KV_HEADS, HEAD_DIM))
            outs["s_ds_i"].append(kis[:, :, :IDX_DIM])
            w_out = c_w_out[j]

        kv_mem = _mm_norm(mem_prompt.reshape(bp * mem_tokens, d), mem_norm[i], mem_w_kv[i].astype(BF16))
        km_p = _headnorm(kv_mem, 0, memq_cols, _tile_gain(mem_k_norm[i], memq_cols)).reshape(bp, mem_tokens, memq_cols)
        vm_p = kv_mem[:, memq_cols:].reshape(bp, mem_tokens, memq_cols)
        outs["p_mk"].append(km_p.reshape(bp, mem_tokens, MEM_HEADS, HEAD_DIM))
        outs["p_mv"].append(vm_p.reshape(bp, mem_tokens, MEM_HEADS, HEAD_DIM))
        mo_p = _mem_attend(zp, memq_cb, km_p, vm_p, mem_q_norm[i])
        mo_s = _mem_attend(zs, memq_cb, cache_mem_k[i].reshape(bd, mem_tokens, memq_cols),
                           cache_mem_v[i].reshape(bd, mem_tokens, memq_cols), mem_q_norm[i])
        w_mix, w_mem = w_out[:d].astype(BF16), w_out[d:].astype(BF16)
        xp = _mm_res([mp.reshape(bp * t, d), mo_p.reshape(bp * t, memq_cols)], [w_mix, w_mem], xp)
        xs = _mm_res([ms.reshape(bd * s_len, d), mo_s.reshape(bd * s_len, memq_cols)], [w_mix, w_mem], xs)

        w_up, w_down = ffn_w_up[i].astype(BF16), ffn_w_down[i].astype(BF16)
        xp, cp = _conv_ffn(xp, bp, t, ffn_norm[i], w_up, ffn_conv[i], w_down,
                           jnp.zeros((bp, CONV_W - 1, 2 * d_ff), F32))
        xs, cs = _conv_ffn(xs, bd, s_len, ffn_norm[i], w_up, ffn_conv[i], w_down, state_ffn_conv[i])
        outs["p_cv"].append(cp)
        outs["s_cv"].append(cs)

    st = jnp.stack
    order = ("p_rw_wkv", "p_rw_sh", "p_sw_k", "p_sw_v", "p_ds_k", "p_ds_v", "p_ds_i", "p_mk", "p_mv", "p_cv",
             "s_rw_wkv", "s_rw_sh", "s_sw_k", "s_sw_v", "s_ds_k", "s_ds_v", "s_ds_i", "s_cv")
    return (xp.reshape(bp, t, d), xs.reshape(bd, s_len, d)) + tuple(st(outs[k]) for k in order)
```

```python
import functools

import jax
import jax.numpy as jnp
from jax import lax
from jax.experimental import pallas as pl
from jax.experimental.pallas import tpu as pltpu

F32 = jnp.float32
BF16 = jnp.bfloat16

HEAD_DIM = 64
CHUNK = 64
NORM_EPS = 1e-6
RW_GN_EPS = HEAD_DIM * 1e-5
ATT_KV_HEADS = 4
WINDOW = 128
IDX_HEADS = 8
IDX_DIM = 64
TOPK_MAX = 256
MEM_HEADS = 4
CONV_W = 3

LANE = 128
VMEM_LIMIT = 52 * 1024 * 1024
NEG_BIG = -(2.0 ** 100)
INT_MIN = -2147483648
DSA_TQ = 256
DSA_KB = 512


def _cparams(sem, vmem=VMEM_LIMIT):
    return pltpu.CompilerParams(dimension_semantics=sem, vmem_limit_bytes=vmem)


def _split3(a):
    a1 = a.astype(BF16)
    r1 = a - a1.astype(F32)
    a2 = r1.astype(BF16)
    r2 = r1 - a2.astype(F32)
    return a1, a2, r2.astype(BF16)


def _dot_exact_rhs(a, e):
    a1, a2, a3 = _split3(a)
    d = lambda x: jnp.dot(x, e, preferred_element_type=F32)
    return (d(a3) + d(a2)) + d(a1)


def _dot_exact_lhs(e, a):
    a1, a2, a3 = _split3(a)
    d = lambda x: jnp.dot(e, x, preferred_element_type=F32)
    return (d(a3) + d(a2)) + d(a1)


def _head_block_matrix(width, value):
    r = lax.broadcasted_iota(jnp.int32, (width, width), 0) // HEAD_DIM
    c = lax.broadcasted_iota(jnp.int32, (width, width), 1) // HEAD_DIM
    return jnp.where(r == c, value, 0.0).astype(BF16)


def _dot_rw(a, b, dims):
    return lax.dot_general(a.astype(BF16), b.astype(BF16), (dims, ((), ())), preferred_element_type=F32)


def _pick_tile(n, cap):
    best = None
    for t in range(LANE, min(n, cap) + 1, LANE):
        if n % t == 0:
            best = t
    assert best is not None, n
    return best


def _mm_norm_body(x_ref, g_ref, w_ref, o_ref, xn_ref):
    @pl.when(pl.program_id(1) == 0)
    def _():
        x = x_ref[...]
        ms = jnp.mean(x * x, axis=-1, keepdims=True)
        xn_ref[...] = ((x * lax.rsqrt(ms + NORM_EPS)) * g_ref[...]).astype(BF16)

    o_ref[...] = jnp.dot(xn_ref[...], w_ref[...], preferred_element_type=F32)


def _mm_norm(x, gain, w):
    m, k = x.shape
    n = w.shape[1]
    tm = min(m, 512)
    tn = _pick_tile(n, 1536)
    return pl.pallas_call(
        _mm_norm_body,
        out_shape=jax.ShapeDtypeStruct((m, n), F32),
        grid=(m // tm, n // tn),
        in_specs=[pl.BlockSpec((tm, k), lambda i, j: (i, 0)),
                  pl.BlockSpec((1, k), lambda i, j: (0, 0)),
                  pl.BlockSpec((k, tn), lambda i, j: (0, j))],
        out_specs=pl.BlockSpec((tm, tn), lambda i, j: (i, j)),
        scratch_shapes=[pltpu.VMEM((tm, k), BF16)],
        compiler_params=_cparams(("parallel", "arbitrary")),
        name="mm_norm",
    )(x, gain.reshape(1, k), w)


def _mm_res_body(*refs, n_lhs):
    lhs = refs[:n_lhs]
    ws = refs[n_lhs:2 * n_lhs]
    r_ref, o_ref = refs[2 * n_lhs], refs[2 * n_lhs + 1]
    acc = jnp.dot(lhs[0][...], ws[0][...], preferred_element_type=F32)
    for a, w in zip(lhs[1:], ws[1:]):
        acc = acc + jnp.dot(a[...], w[...], preferred_element_type=F32)
    o_ref[...] = r_ref[...] + acc


def _mm_res(lhs_list, w_list, res):
    m, n = res.shape
    ktot = sum(a.shape[1] for a in lhs_list)
    tm = min(m, 512)
    tn = _pick_tile(n, 1024 if ktot <= 3072 else 512)
    n_lhs = len(lhs_list)
    in_specs = [pl.BlockSpec((tm, a.shape[1]), lambda i, j: (i, 0)) for a in lhs_list]
    in_specs += [pl.BlockSpec((w.shape[0], tn), lambda i, j: (0, j)) for w in w_list]
    in_specs += [pl.BlockSpec((tm, tn), lambda i, j: (i, j))]
    return pl.pallas_call(
        functools.partial(_mm_res_body, n_lhs=n_lhs),
        out_shape=jax.ShapeDtypeStruct((m, n), F32),
        grid=(m // tm, n // tn),
        in_specs=in_specs,
        out_specs=pl.BlockSpec((tm, tn), lambda i, j: (i, j)),
        compiler_params=_cparams(("parallel", "arbitrary")),
        name="mm_res",
    )(*lhs_list, *w_list, res)


def _headnorm_body(x_ref, g_ref, o_ref, *, width):
    avg = _head_block_matrix(LANE, 1.0 / HEAD_DIM)
    for c in range(width // LANE):
        x = x_ref[:, c * LANE:(c + 1) * LANE]
        ms = _dot_exact_rhs(x * x, avg)
        o_ref[:, c * LANE:(c + 1) * LANE] = (x * lax.rsqrt(ms + NORM_EPS)) * g_ref[:, c * LANE:(c + 1) * LANE]


def _headnorm(x, col_block, width, gain_row):
    m = x.shape[0]
    tm = min(m, 1024)
    return pl.pallas_call(
        functools.partial(_headnorm_body, width=width),
        out_shape=jax.ShapeDtypeStruct((m, width), F32),
        grid=(m // tm,),
        in_specs=[pl.BlockSpec((tm, width), lambda i: (i, col_block)),
                  pl.BlockSpec((1, width), lambda i: (0, 0))],
        out_specs=pl.BlockSpec((tm, width), lambda i: (i, 0)),
        compiler_params=_cparams(("parallel",)),
        name="headnorm",
    )(x, gain_row)


def _up_conv_body(x_ref, g_ref, wa_ref, wb_ref, pa_ref, pb_ref, cwa_ref, cwb_ref, o_ref, la_ref, lb_ref,
                  xn_ref, ca_ref, cb_ref, *, tm, tiles_per_batch):
    i = pl.program_id(0)
    j = pl.program_id(1)

    @pl.when(j == 0)
    def _():
        x = x_ref[...]
        ms = jnp.mean(x * x, axis=-1, keepdims=True)
        xn_ref[...] = ((x * lax.rsqrt(ms + NORM_EPS)) * g_ref[...]).astype(BF16)

    @pl.when(i % tiles_per_batch == 0)
    def _():
        ca_ref[j] = pa_ref[0]
        cb_ref[j] = pb_ref[0]

    xn = xn_ref[...]
    row = lax.broadcasted_iota(jnp.int32, o_ref.shape, 0)

    def conv(w_ref, c_ref, cw_ref, last_ref):
        u = jnp.dot(xn, w_ref[...], preferred_element_type=F32)
        car = c_ref[j]
        c0, c1 = car[0:1, :], car[1:2, :]
        u1 = jnp.where(row == 0, c1, pltpu.roll(u, 1, axis=0))
        u2 = jnp.where(row == 0, c0, jnp.where(row == 1, c1, pltpu.roll(u, 2, axis=0)))
        c_ref[j] = u[tm - 2:tm, :]
        last_ref[0] = u[tm - 2:tm, :]
        return (u2 * cw_ref[0:1, :] + u1 * cw_ref[1:2, :]) + u * cw_ref[2:3, :]

    a = conv(wa_ref, ca_ref, cwa_ref, la_ref)
    b = conv(wb_ref, cb_ref, cwb_ref, lb_ref)
    o_ref[...] = ((a * jax.nn.sigmoid(a)) * b).astype(o_ref.dtype)


def _up_conv(x, gain, w_up, conv_w, prev, t):
    m, k = x.shape
    f = w_up.shape[1] // 2
    b = m // t
    tm = min(t, 512)
    tn = _pick_tile(f, 512)
    nj = f // tn
    tpb = t // tm
    act, la, lb = pl.pallas_call(
        functools.partial(_up_conv_body, tm=tm, tiles_per_batch=tpb),
        out_shape=(jax.ShapeDtypeStruct((m, f), BF16),
                   jax.ShapeDtypeStruct((b, CONV_W - 1, f), F32),
                   jax.ShapeDtypeStruct((b, CONV_W - 1, f), F32)),
        grid=(m // tm, nj),
        in_specs=[pl.BlockSpec((tm, k), lambda i, j: (i, 0)),
                  pl.BlockSpec((1, k), lambda i, j: (0, 0)),
                  pl.BlockSpec((k, tn), lambda i, j: (0, j)),
                  pl.BlockSpec((k, tn), lambda i, j: (0, nj + j)),
                  pl.BlockSpec((1, CONV_W - 1, tn), lambda i, j: (i // tpb, 0, j)),
                  pl.BlockSpec((1, CONV_W - 1, tn), lambda i, j: (i // tpb, 0, nj + j)),
                  pl.BlockSpec((CONV_W, tn), lambda i, j: (0, j)),
                  pl.BlockSpec((CONV_W, tn), lambda i, j: (0, nj + j))],
        out_specs=(pl.BlockSpec((tm, tn), lambda i, j: (i, j)),
                   pl.BlockSpec((1, CONV_W - 1, tn), lambda i, j: (i // tpb, 0, j)),
                   pl.BlockSpec((1, CONV_W - 1, tn), lambda i, j: (i // tpb, 0, j))),
        scratch_shapes=[pltpu.VMEM((tm, k), BF16),
                        pltpu.VMEM((nj, CONV_W - 1, tn), F32),
                        pltpu.VMEM((nj, CONV_W - 1, tn), F32)],
        compiler_params=_cparams(("arbitrary", "arbitrary")),
        name="up_conv_gate",
    )(x, gain.reshape(1, k), w_up, w_up, prev, prev, conv_w, conv_w)
    return act, jnp.concatenate([la, lb], axis=-1)


def _flash_body(*refs, mode, tq, kb, wq, group, n_valid, nk, causal):
    q_ref, k_ref, v_ref, qg_ref, sink_ref = refs[:5]
    pos = 5
    if mode == "dsa":
        sc_ref, tau_ref, jb_ref = refs[pos:pos + 3]
        pos += 3
    o_ref = refs[pos]
    qn_scr, acc_scr, m_scr, l_scr, kab_scr, vab_scr, bias_scr = refs[pos + 1:]

    i = pl.program_id(1)
    j = pl.program_id(2)
    npairs = wq // LANE
    pairs_per_unit = group // 2 if group > 1 else 1
    n_units = npairs // pairs_per_unit
    lane = lax.broadcasted_iota(jnp.int32, (1, LANE), 1)
    lo_half = lane < HEAD_DIM

    @pl.when(j == 0)
    def _():
        avg = _head_block_matrix(LANE, 1.0 / HEAD_DIM)
        for p in range(npairs):
            x = q_ref[0, :, p * LANE:(p + 1) * LANE]
            ms = _dot_exact_rhs(x * x, avg)
            qn = ((x * lax.rsqrt(ms + NORM_EPS)) * qg_ref[...]) * (HEAD_DIM ** -0.5)
            qn_scr[p] = qn.astype(BF16)
        acc_scr[...] = jnp.zeros(acc_scr.shape, F32)
        l_scr[...] = jnp.zeros(l_scr.shape, F32)
        m_scr[...] = jnp.full(m_scr.shape, NEG_BIG, F32)
        ones_lo = jnp.broadcast_to(jnp.where(lo_half, 1.0, 0.0), (kb, LANE)).astype(BF16)
        ones_hi = jnp.broadcast_to(jnp.where(lo_half, 0.0, 1.0), (kb, LANE)).astype(BF16)
        for u in range(n_units):
            vab_scr[u, 0:kb, LANE:2 * LANE] = ones_lo
            vab_scr[u, kb:2 * kb, LANE:2 * LANE] = ones_hi

    if mode == "band":
        kblk = i - WINDOW // kb + j
        active = kblk >= 0
    elif mode == "dsa" and causal:
        kblk = j
        active = j * kb < (i + 1) * tq
    else:
        kblk = j
        active = None

    def step():
        for u in range(n_units):
            if group > 1:
                tile_idx, half = u // 2, u % 2
            else:
                tile_idx, half = u, None
            for src, dst in ((k_ref, kab_scr), (v_ref, vab_scr)):
                tile = src[0, :, tile_idx * LANE:(tile_idx + 1) * LANE]
                if half is None:
                    a_part = jnp.where(lo_half, tile, 0.0)
                    b_part = jnp.where(lo_half, 0.0, tile)
                elif half == 0:
                    a_part = jnp.where(lo_half, tile, 0.0)
                    b_part = pltpu.roll(a_part, HEAD_DIM, axis=1)
                else:
                    b_part = jnp.where(lo_half, 0.0, tile)
                    a_part = pltpu.roll(b_part, HEAD_DIM, axis=1)
                dst[u, 0:kb, 0:LANE] = a_part.astype(BF16)
                dst[u, kb:2 * kb, 0:LANE] = b_part.astype(BF16)

        kidx = kblk * kb + lax.broadcasted_iota(jnp.int32, (tq, kb), 1)
        qpos = i * tq + lax.broadcasted_iota(jnp.int32, (tq, kb), 0)
        if mode == "band":
            qchunk = qpos // CHUNK
            sel = (kidx >= (qchunk - WINDOW // CHUNK) * CHUNK) & (kidx < (qchunk + 1) * CHUNK) & (kidx >= 0)
        elif mode == "dsa":
            x = sc_ref[0]
            tau = tau_ref[0][:, 0:1]
            jb = jb_ref[0][:, 0:1]
            adm = kidx < ((qpos // CHUNK + 1) * CHUNK if causal else n_valid)
            sel = adm & ((x > tau) | ((x == tau) & (kidx <= jb)))
        else:
            sel = kidx < n_valid
        bias_scr[...] = jnp.where(sel, 0.0, NEG_BIG).astype(BF16)

        def pair_body(p, carry):
            u = p // pairs_per_unit
            s = lax.dot_general(qn_scr[p], kab_scr[u], (((1,), (1,)), ((), ())),
                                preferred_element_type=F32)
            bias = bias_scr[...]
            m_old = m_scr[p]
            s0 = s[:, 0:kb].astype(BF16) + bias
            s1 = s[:, kb:2 * kb].astype(BF16) + bias
            mn0 = jnp.maximum(m_old[:, 0:1], jnp.max(s0, axis=1, keepdims=True).astype(F32))
            mn1 = jnp.maximum(m_old[:, HEAD_DIM:HEAD_DIM + 1], jnp.max(s1, axis=1, keepdims=True).astype(F32))
            p0 = jnp.exp(s0 - mn0.astype(BF16))
            p1 = jnp.exp(s1 - mn1.astype(BF16))
            pcat = jnp.concatenate([p0, p1], axis=1)
            pv = jnp.dot(pcat, vab_scr[u], preferred_element_type=F32)
            mn = jnp.where(lo_half, mn0, mn1)
            alpha = jnp.exp(m_old - mn)
            acc_scr[p] = acc_scr[p] * alpha + pv[:, 0:LANE]
            l_scr[p] = l_scr[p] * alpha + pv[:, LANE:2 * LANE]
            m_scr[p] = mn
            return carry

        lax.fori_loop(0, npairs, pair_body, 0, unroll=min(npairs, 8))

    if active is None:
        step()
    else:
        pl.when(active)(step)

    @pl.when(j == nk - 1)
    def _():
        for p in range(npairs):
            den = l_scr[p] + jnp.exp(sink_ref[:, p * LANE:(p + 1) * LANE] - m_scr[p])
            o_ref[0, :, p * LANE:(p + 1) * LANE] = (acc_scr[p] / den).astype(o_ref.dtype)


def _flash(q_arr, q_cb, wq, k_arr, k_cb, v_arr, v_cb, q_gain, sink_row, *, mode, tq, kb, group,
           n_valid=None, causal=False, scores=None, tau=None, jb=None):
    b, t_q = q_arr.shape[0], q_arr.shape[1]
    t_k = k_arr.shape[1]
    wk = ATT_KV_HEADS * HEAD_DIM
    nq = t_q // tq
    if mode == "band":
        assert tq == kb == WINDOW
        nk = 2
        kmap = lambda i, j: jnp.maximum(i - 1 + j, 0)
    elif mode == "dsa" and causal:
        nk = t_k // kb
        kmap = lambda i, j: jnp.minimum(j, ((i + 1) * tq - 1) // kb)
    else:
        nk = t_k // kb
        kmap = lambda i, j: j
    npairs = wq // LANE
    n_units = npairs // (group // 2) if group > 1 else npairs
    in_specs = [pl.BlockSpec((1, tq, wq), lambda bi, i, j: (bi, i, q_cb)),
                pl.BlockSpec((1, kb, wk), lambda bi, i, j: (bi, kmap(i, j), k_cb)),
                pl.BlockSpec((1, kb, wk), lambda bi, i, j: (bi, kmap(i, j), v_cb)),
                pl.BlockSpec((1, LANE), lambda bi, i, j: (0, 0)),
                pl.BlockSpec((1, wq), lambda bi, i, j: (0, 0))]
    args = [q_arr, k_arr, v_arr, q_gain, sink_row]
    if mode == "dsa":
        in_specs += [pl.BlockSpec((1, tq, kb), lambda bi, i, j: (bi, i, kmap(i, j))),
                     pl.BlockSpec((1, tq, LANE), lambda bi, i, j: (bi, i, 0)),
                     pl.BlockSpec((1, tq, LANE), lambda bi, i, j: (bi, i, 0))]
        args += [scores, tau, jb]
    body = functools.partial(_flash_body, mode=mode, tq=tq, kb=kb, wq=wq, group=group,
                             n_valid=n_valid, nk=nk, causal=causal)
    return pl.pallas_call(
        body,
        out_shape=jax.ShapeDtypeStruct((b, t_q, wq), BF16),
        grid=(b, nq, nk),
        in_specs=in_specs,
        out_specs=pl.BlockSpec((1, tq, wq), lambda bi, i, j: (bi, i, 0)),
        scratch_shapes=[pltpu.VMEM((npairs, tq, LANE), BF16),
                        pltpu.VMEM((npairs, tq, LANE), F32),
                        pltpu.VMEM((npairs, tq, LANE), F32),
                        pltpu.VMEM((npairs, tq, LANE), F32),
                        pltpu.VMEM((n_units, 2 * kb, LANE), BF16),
                        pltpu.VMEM((n_units, 2 * kb, 2 * LANE), BF16),
                        pltpu.VMEM((tq, kb), BF16)],
        compiler_params=_cparams(("parallel", "parallel", "arbitrary")),
        name="flash_" + mode,
    )(*args)


def _dsa_select_body(qi_ref, kw_ref, ki_ref, sc_ref, tau_ref, jb_ref, sc_scr, *, tq, kb, nk, k_sel,
                     n_valid, causal, idx_bits):
    i = pl.program_id(1)
    j = pl.program_id(2)
    qpos = i * tq + lax.broadcasted_iota(jnp.int32, (tq, 1), 0)
    if causal:
        n_adm = (qpos // CHUNK + 1) * CHUNK
        active = j * kb < (i + 1) * tq
        n_blocks = ((i + 1) * tq + kb - 1) // kb
    else:
        n_adm = jnp.full((tq, 1), n_valid, jnp.int32)
        active = None
        n_blocks = nk

    def compute():
        ka = ki_ref[0]
        kab = jnp.concatenate([ka, pltpu.roll(ka, HEAD_DIM, axis=1)], axis=0).astype(BF16)
        kw = kw_ref[0]
        acc = jnp.zeros((tq, kb), F32)
        for p in range(IDX_HEADS // 2):
            qp = qi_ref[0, :, p * LANE:(p + 1) * LANE].astype(BF16)
            s = lax.dot_general(qp, kab, (((1,), (1,)), ((), ())), preferred_element_type=F32)
            for h in range(2):
                col = HEAD_DIM + 2 * p + h
                w = (kw[:, col:col + 1] * IDX_HEADS ** -0.5) * IDX_DIM ** -0.5
                acc = acc + jnp.maximum(s[:, h * kb:(h + 1) * kb], 0.0) * w
        acc = jnp.where(acc == 0.0, 0.0, acc)
        kidx = j * kb + lax.broadcasted_iota(jnp.int32, (tq, kb), 1)
        x = jnp.where(kidx < n_adm, acc, -jnp.inf)
        sc_scr[j] = x
        sc_ref[0] = x

    if active is None:
        compute()
    else:
        pl.when(active)(compute)

        @pl.when(jnp.logical_not(active))
        def _():
            sc_ref[0] = jnp.full((tq, kb), -jnp.inf, F32)

    @pl.when(j == nk - 1)
    def _():
        kf = float(k_sel)

        def count(pred):
            def blk(jj, acc):
                h = jnp.where(pred(sc_scr[jj], jj), 1.0, 0.0)
                for a in range(kb // LANE):
                    acc = acc + h[:, a * LANE:(a + 1) * LANE]
                return acc
            acc = lax.fori_loop(0, n_blocks, blk, jnp.zeros((tq, LANE), F32))
            return jnp.sum(acc, axis=1, keepdims=True)

        def key_to_float(tu):
            cs = tu ^ INT_MIN
            fb = jnp.where(cs >= 0, cs, cs ^ 0x7FFFFFFF)
            return lax.bitcast_convert_type(fb, F32)

        def bit_body(it, tu):
            cand = tu | jnp.left_shift(jnp.int32(1), 31 - it)
            thr = key_to_float(cand)
            c = count(lambda x, jj: x >= thr)
            return jnp.where(c >= kf, cand, tu)

        tu = lax.fori_loop(0, 32, bit_body, jnp.zeros((tq, 1), jnp.int32))
        full = n_adm >= k_sel
        tau = jnp.where(full, key_to_float(tu), -jnp.inf)
        c_gt = count(lambda x, jj: x > tau)
        c_ge = count(lambda x, jj: x >= tau)
        need = kf - c_gt
        tau_ref[0] = jnp.broadcast_to(tau, (tq, LANE))
        jb_ref[0] = jnp.full((tq, LANE), 2 ** 30, jnp.int32)
        surplus = jnp.max(jnp.where(full, (c_ge - c_gt) - need, 0.0))

        @pl.when(surplus > 0.5)
        def _():
            def tie_body(it, cut):
                cand = cut | jnp.left_shift(jnp.int32(1), idx_bits - 1 - it)

                def pred(x, jj):
                    kidx = jj * kb + lax.broadcasted_iota(jnp.int32, (tq, kb), 1)
                    return (x == tau) & (kidx < cand)
                c = count(pred)
                return jnp.where(c < need, cand, cut)

            cut = lax.fori_loop(0, idx_bits, tie_body, jnp.zeros((tq, 1), jnp.int32))
            jb_ref[0] = jnp.broadcast_to(cut, (tq, LANE))


def _dsa_select(z, qi_cb, kw_cb, ki_n, *, tq, kb, k_sel, n_valid, causal):
    b, t_q = z.shape[0], z.shape[1]
    t_k = ki_n.shape[1]
    nq, nk = t_q // tq, t_k // kb
    kmap = (lambda i, j: jnp.minimum(j, ((i + 1) * tq - 1) // kb)) if causal else (lambda i, j: j)
    body = functools.partial(_dsa_select_body, tq=tq, kb=kb, nk=nk, k_sel=k_sel, n_valid=n_valid,
                             causal=causal, idx_bits=max(1, (t_k - 1).bit_length()))
    return pl.pallas_call(
        body,
        out_shape=(jax.ShapeDtypeStruct((b, t_q, t_k), F32),
                   jax.ShapeDtypeStruct((b, t_q, LANE), F32),
                   jax.ShapeDtypeStruct((b, t_q, LANE), jnp.int32)),
        grid=(b, nq, nk),
        in_specs=[pl.BlockSpec((1, tq, IDX_HEADS * IDX_DIM), lambda bi, i, j: (bi, i, qi_cb)),
                  pl.BlockSpec((1, tq, LANE), lambda bi, i, j: (bi, i, kw_cb)),
                  pl.BlockSpec((1, kb, LANE), lambda bi, i, j: (bi, kmap(i, j), 0))],
        out_specs=(pl.BlockSpec((1, tq, kb), lambda bi, i, j: (bi, i, j)),
                   pl.BlockSpec((1, tq, LANE), lambda bi, i, j: (bi, i, 0)),
                   pl.BlockSpec((1, tq, LANE), lambda bi, i, j: (bi, i, 0))),
        scratch_shapes=[pltpu.VMEM((nk, tq, kb), F32)],
        compiler_params=_cparams(("parallel", "parallel", "arbitrary")),
        name="dsa_select",
    )(z, z, ki_n)


def _rwkv_body(zr_ref, zk_ref, zv_ref, zw_ref, za_ref, zg_ref,
               sr_ref, sk_ref, sv_ref, sw_ref, sa_ref, sg_ref,
               mr_ref, mk_ref, mv_ref, mw_ref, ma_ref, mg_ref,
               vec_ref, w2_ref, a2_ref, g2_ref, s0_ref,
               mix_ref, sfin_ref,
               s_scr, prev_scr, prevg_scr, y_scr, *, tc, lc, nt):
    t = pl.program_id(2)

    @pl.when(t == 0)
    def _():
        s_scr[...] = s0_ref[0, 0]
        prev_scr[0:1, :] = sr_ref[0]
        prev_scr[1:2, :] = sk_ref[0]
        prev_scr[2:3, :] = sv_ref[0]
        prev_scr[3:4, :] = sw_ref[0]
        prev_scr[4:5, :] = sa_ref[0]
        prevg_scr[0:1, :] = sg_ref[0]

    row = lax.broadcasted_iota(jnp.int32, (tc, 1), 0)

    def shifted(z_ref, prow, mu_ref):
        z = z_ref[0]
        zp = jnp.where(row == 0, prow, pltpu.roll(z, 1, axis=0))
        return z + (zp - z) * mu_ref[...], z[tc - 1:tc, :]

    r, last_r = shifted(zr_ref, prev_scr[0:1, :], mr_ref)
    k, last_k = shifted(zk_ref, prev_scr[1:2, :], mk_ref)
    v, last_v = shifted(zv_ref, prev_scr[2:3, :], mv_ref)
    zw, last_w = shifted(zw_ref, prev_scr[3:4, :], mw_ref)
    za, last_a = shifted(za_ref, prev_scr[4:5, :], ma_ref)
    zg, last_g = shifted(zg_ref, prevg_scr[0:1, :], mg_ref)
    prev_scr[0:1, :] = last_r
    prev_scr[1:2, :] = last_k
    prev_scr[2:3, :] = last_v
    prev_scr[3:4, :] = last_w
    prev_scr[4:5, :] = last_a
    prevg_scr[0:1, :] = last_g

    w0, a0 = vec_ref[0:1, :], vec_ref[1:2, :]
    k_k, k_a, r_k = vec_ref[2:3, :], vec_ref[3:4, :], vec_ref[4:5, :]
    ln_w, ln_b = vec_ref[5:6, :], vec_ref[6:7, :]

    ones_blk = _head_block_matrix(LANE, 1.0)
    avg_blk = _head_block_matrix(LANE, 1.0 / HEAD_DIM)

    xw = w0 + jnp.dot(jnp.tanh(zw).astype(BF16), w2_ref[...], preferred_element_type=F32)
    nx = -xw
    softplus = jnp.maximum(nx, 0.0) + jnp.log1p(jnp.exp(-jnp.abs(nx)))
    w_log = -softplus - 0.5
    lw = -jnp.exp(w_log)
    a = jax.nn.sigmoid(a0 + jnp.dot(za.astype(BF16), a2_ref[...], preferred_element_type=F32))
    g = jnp.dot(jax.nn.sigmoid(zg).astype(BF16), g2_ref[...], preferred_element_type=F32)
    kk = k * k_k
    kk = kk / jnp.maximum(jnp.sqrt(_dot_exact_rhs(kk * kk, ones_blk)), 1e-12)
    k2 = k * (1.0 + (a - 1.0) * k_a)

    a_step = -kk
    b_step = kk * a

    lane = lax.broadcasted_iota(jnp.int32, (1, LANE), 1)
    m0 = lane < HEAD_DIM
    rr = lax.broadcasted_iota(jnp.int32, (lc, 2 * lc), 0)
    cc = lax.broadcasted_iota(jnp.int32, (lc, 2 * lc), 1)
    incl = jnp.where(cc < lc, cc, cc - lc) <= rr
    strict = (lax.broadcasted_iota(jnp.int32, (2 * lc, 2 * lc), 1)
              < lax.broadcasted_iota(jnp.int32, (2 * lc, 2 * lc), 0))
    tri = (lax.broadcasted_iota(jnp.int32, (lc, lc), 1)
           <= lax.broadcasted_iota(jnp.int32, (lc, lc), 0)).astype(BF16)
    nsteps = lc.bit_length() - 1
    nt_dims = ((1,), (1,))
    nn_dims = ((1,), (0,))
    tn_dims = ((0,), (0,))

    eye = (lax.broadcasted_iota(jnp.int32, (2 * lc, 2 * lc), 0)
           == lax.broadcasted_iota(jnp.int32, (2 * lc, 2 * lc), 1)).astype(F32)

    chunks = range(tc // lc)
    st = []
    for c in chunks:
        rows = slice(c * lc, (c + 1) * lc)
        lwc = lw[rows]
        cs = _dot_exact_lhs(tri, lwc)
        p_in = jnp.exp(-cs)
        at = a_step[rows] * jnp.exp(cs - lwc)
        bt = b_step[rows] * p_in
        kt = k2[rows] * p_in
        rt = r[rows] * jnp.exp(cs)
        vc = v[rows]
        stack = lambda x: jnp.concatenate([jnp.where(m0, x, 0.0), jnp.where(m0, 0.0, x)], axis=0).astype(BF16)
        st.append(dict(ar=jnp.concatenate([stack(at), rt.astype(BF16)], axis=0),
                       bk=jnp.concatenate([stack(bt), stack(kt)], axis=0),
                       v_s=stack(vc), p_last=jnp.exp(cs[lc - 1:lc, :])))
    for d in st:
        gram = _dot_rw(d["ar"], d["bk"], nt_dims)
        d["t_p"] = jnp.where(strict, gram[0:2 * lc, 0:2 * lc], 0.0)
        d["w_ak"] = jnp.where(strict, gram[0:2 * lc, 2 * lc:4 * lc], 0.0).astype(BF16)
        d["w_rb"] = jnp.where(incl, gram[2 * lc:3 * lc, 0:2 * lc], 0.0).astype(BF16)
        d["w_rk"] = jnp.where(incl, gram[2 * lc:3 * lc, 2 * lc:4 * lc], 0.0).astype(BF16)
        d["minv"] = eye + d["t_p"]
    for _ in range(nsteps - 1):
        for d in st:
            d["t_p"] = _dot_rw(d["t_p"], d["t_p"], nn_dims)
        for d in st:
            d["minv"] = d["minv"] + _dot_rw(d["minv"], d["t_p"], nn_dims)
    for d in st:
        d["minv"] = d["minv"].astype(BF16)
        d["wv"] = _dot_rw(d["w_ak"], d["v_s"], nn_dims)
        d["y_c"] = _dot_rw(d["w_rk"], d["v_s"], nn_dims)
    for d in st:
        d["ma"] = _dot_rw(d["minv"], d["ar"][0:2 * lc], nn_dims)
        d["mwv"] = _dot_rw(d["minv"], d["wv"], nn_dims)
    for d in st:
        d["g"] = _dot_rw(d["ma"], d["bk"][0:2 * lc], tn_dims).astype(BF16)
        d["d"] = _dot_rw(jnp.concatenate([d["mwv"].astype(BF16), d["v_s"]], axis=0), d["bk"], tn_dims)
    s_cur = s_scr[...]
    for d in st:
        d["s0"] = s_cur.astype(BF16)
        s_cur = ((s_cur + _dot_rw(d["s0"], d["g"], nn_dims)) + d["d"]) * d["p_last"]
    s_scr[...] = s_cur
    for d in st:
        d["xr"] = _dot_rw(d["ar"], d["s0"], nt_dims)
    for d in st:
        d["u"] = (_dot_rw(d["minv"], d["xr"][0:2 * lc], nn_dims) + d["mwv"]).astype(BF16)
    for c, d in zip(chunks, st):
        y_scr[c * lc:(c + 1) * lc, :] = (d["xr"][2 * lc:3 * lc] + _dot_rw(d["w_rb"], d["u"], nn_dims)) + d["y_c"]

    y = y_scr[...]
    mean = _dot_exact_rhs(y, avg_blk)
    d = y - mean
    var = _dot_exact_rhs(d * d, avg_blk)
    yn = (d * lax.rsqrt(var + RW_GN_EPS)) * ln_w + ln_b
    bonus = _dot_exact_rhs((r * k2) * r_k, ones_blk) * v
    mix_ref[0] = ((yn + bonus) * g).astype(mix_ref.dtype)

    @pl.when(t == nt - 1)
    def _():
        sfin_ref[0, 0] = s_scr[...]


def _rwkv(z, shift_prev, s0_pairs, mu, vecs, w2, a2, g2, d_model):
    b, t = z.shape[0], z.shape[1]
    npairs = d_model // LANE
    tc = min(t, 512)
    lc = min(CHUNK, t)
    nt = t // tc
    cb_w, cb_a, cb_g = 3 * npairs, 3 * npairs + 1, (3 * npairs + 2) // 2

    def zspec(width, cbf):
        return pl.BlockSpec((1, tc, width), lambda bi, p, ti: (bi, ti, cbf(p)))

    def sspec(width, cbf):
        return pl.BlockSpec((1, 1, width), lambda bi, p, ti: (bi, 0, cbf(p)))

    def mspec(width, cbf):
        return pl.BlockSpec((1, width), lambda bi, p, ti: (0, cbf(p)))

    cbfs = [(LANE, lambda p: p), (LANE, lambda p: npairs + p), (LANE, lambda p: 2 * npairs + p),
            (LANE, lambda p: cb_w), (LANE, lambda p: cb_a), (2 * LANE, lambda p: cb_g)]
    in_specs = ([zspec(w, f) for w, f in cbfs] + [sspec(w, f) for w, f in cbfs] + [mspec(w, f) for w, f in cbfs]
                + [pl.BlockSpec((8, LANE), lambda bi, p, ti: (0, p)),
                   pl.BlockSpec((LANE, LANE), lambda bi, p, ti: (0, p)),
                   pl.BlockSpec((LANE, LANE), lambda bi, p, ti: (0, p)),
                   pl.BlockSpec((2 * LANE, LANE), lambda bi, p, ti: (0, p)),
                   pl.BlockSpec((1, 1, LANE, LANE), lambda bi, p, ti: (bi, p, 0, 0))])
    return pl.pallas_call(
        functools.partial(_rwkv_body, tc=tc, lc=lc, nt=nt),
        out_shape=(jax.ShapeDtypeStruct((b, t, d_model), BF16),
                   jax.ShapeDtypeStruct((b, npairs, LANE, LANE), F32)),
        grid=(b, npairs, nt),
        in_specs=in_specs,
        out_specs=(pl.BlockSpec((1, tc, LANE), lambda bi, p, ti: (bi, ti, p)),
                   pl.BlockSpec((1, 1, LANE, LANE), lambda bi, p, ti: (bi, p, 0, 0))),
        scratch_shapes=[pltpu.VMEM((LANE, LANE), F32),
                        pltpu.VMEM((8, LANE), F32),
                        pltpu.VMEM((8, 2 * LANE), F32),
                        pltpu.VMEM((tc, LANE), F32)],
        compiler_params=_cparams(("parallel", "parallel", "arbitrary")),
        name="rwkv7",
    )(*([z] * 6), *([shift_prev] * 6), *([mu] * 6), vecs, w2, a2, g2, s0_pairs)


def _pad_cols(x, segments):
    parts = []
    for start, width, padded in segments:
        seg = x[..., start:start + width]
        if padded > width:
            seg = jnp.concatenate([seg, jnp.zeros(seg.shape[:-1] + (padded - width,), seg.dtype)], axis=-1)
        parts.append(seg)
    return jnp.concatenate(parts, axis=-1)


def _pad_rows(x, padded):
    return jnp.concatenate([x, jnp.zeros((padded - x.shape[0],) + x.shape[1:], x.dtype)], axis=0)


def _pairs_from_heads(s):
    b, h = s.shape[0], s.shape[1]
    s = s.reshape(b, h // 2, 2, HEAD_DIM, HEAD_DIM)
    z = jnp.zeros_like(s[:, :, 0])
    top = jnp.concatenate([s[:, :, 0], z], axis=-1)
    bot = jnp.concatenate([z, s[:, :, 1]], axis=-1)
    return jnp.concatenate([top, bot], axis=-2)


def _heads_from_pairs(sp):
    b, npairs = sp.shape[0], sp.shape[1]
    s = jnp.stack([sp[:, :, :HEAD_DIM, :HEAD_DIM], sp[:, :, HEAD_DIM:, HEAD_DIM:]], axis=2)
    return s.reshape(b, 2 * npairs, HEAD_DIM, HEAD_DIM)


def _tile_gain(g, width):
    return jnp.tile(g.astype(F32), width // HEAD_DIM).reshape(1, width)


def _mem_attend(z, memq_cb, km, vm, q_gain):
    t_q = z.shape[1]
    wq = MEM_HEADS * HEAD_DIM
    tq = min(t_q, 512)
    no_sink = jnp.full((1, wq), -jnp.inf, F32)
    return _flash(z, memq_cb, wq, km, 0, vm, 0, _tile_gain(q_gain, LANE), no_sink,
                  mode="all", tq=tq, kb=km.shape[1], group=1, n_valid=km.shape[1])


def _conv_ffn(x2d, b, t, gain, w_up, conv_w, w_down, prev):
    act, u_last = _up_conv(x2d, gain, w_up, conv_w, prev, t)
    return _mm_res([act], [w_down], x2d), u_last


def kernel(x_prompt, x_sample, state_rwkv_wkv, state_rwkv_shift, cache_swa_k, cache_swa_v, cache_dsa_k, cache_dsa_v, cache_dsa_idx_k, cache_mem_k, cache_mem_v, state_ffn_conv, mem_prompt, attn_norm, ffn_norm, mem_norm, mem_w_kv, mem_q_norm, mem_k_norm, a_w_in, a_mu, a_w0, a_w2, a_a0, a_a2, a_g2, a_k_k, a_k_a, a_r_k, a_ln_w, a_ln_b, a_w_out, b_w_in, b_q_norm, b_k_norm, b_sink, b_w_out, c_w_in, c_q_norm, c_k_norm, c_idx_k_norm, c_w_out, ffn_w_up, ffn_conv, ffn_w_down):
    bp, t, d = x_prompt.shape
    bd, s_len = x_sample.shape[:2]
    depth = attn_norm.shape[0]
    win_rows = cache_swa_k.shape[2]
    past = cache_dsa_k.shape[2] if cache_dsa_k.shape[0] else 0
    d_ff = ffn_w_down.shape[1]
    mem_tokens = mem_prompt.shape[1]
    q_cols = d
    kv_cols = ATT_KV_HEADS * HEAD_DIM
    memq_cols = MEM_HEADS * HEAD_DIM
    att_group = (d // HEAD_DIM) // ATT_KV_HEADS
    dec_lora = a_w2.shape[1]
    a_lora = a_a2.shape[1]
    g_lora = a_g2.shape[1]
    rw_cols = 3 * d + dec_lora + a_lora + g_lora
    k_sel_p = min(TOPK_MAX, t // 4)
    k_sel_s = min(TOPK_MAX, (past + s_len) // 4)
    assert g_lora == 2 * LANE and dec_lora <= LANE and a_lora <= LANE

    xp = x_prompt.reshape(bp * t, d)
    xs = x_sample.reshape(bd * s_len, d)

    rw_segments = [(0, 3 * d, 3 * d), (3 * d, dec_lora, LANE), (3 * d + dec_lora, a_lora, LANE),
                   (3 * d + dec_lora + a_lora, g_lora, g_lora)]
    rw_padded = 3 * d + 2 * LANE + g_lora
    o_qi = q_cols + 2 * kv_cols
    o_ki = o_qi + IDX_HEADS * IDX_DIM
    c_cols = o_ki + IDX_DIM + IDX_HEADS

    outs = {k: [] for k in ("p_rw_wkv", "p_rw_sh", "p_sw_k", "p_sw_v", "p_ds_k", "p_ds_v", "p_ds_i", "p_mk",
                            "p_mv", "p_cv", "s_rw_wkv", "s_rw_sh", "s_sw_k", "s_sw_v", "s_ds_k", "s_ds_v",
                            "s_ds_i", "s_cv")}

    def unpad_rw(row):
        return jnp.concatenate([row[..., :3 * d], row[..., 3 * d:3 * d + dec_lora],
                                row[..., 3 * d + LANE:3 * d + LANE + a_lora],
                                row[..., 3 * d + 2 * LANE:3 * d + 2 * LANE + g_lora]], axis=-1)

    for i in range(depth):
        kind, j = i % 3, i // 3
        if kind == 0:
            w_in = jnp.concatenate([_pad_cols(a_w_in[j], rw_segments), a_w_in[j][:, rw_cols:]], axis=1).astype(BF16)
            memq_cb = rw_padded // memq_cols
            zp = _mm_norm(xp, attn_norm[i], w_in).reshape(bp, t, -1)
            zs = _mm_norm(xs, attn_norm[i], w_in).reshape(bd, s_len, -1)
            mu = _pad_cols(a_mu[j].reshape(1, -1), rw_segments)
            mu = jnp.concatenate([mu, jnp.zeros((1, memq_cols), F32)], axis=1)
            vecs = jnp.stack([a_w0[j], a_a0[j], a_k_k[j], a_k_a[j], a_r_k[j].reshape(-1), a_ln_w[j], a_ln_b[j],
                              jnp.zeros((d,), F32)], axis=0)
            w2 = _pad_rows(a_w2[j], LANE).astype(BF16)
            a2 = _pad_rows(a_a2[j], LANE).astype(BF16)
            g2 = a_g2[j].astype(BF16)
            sh_p = jnp.zeros((bp, 1, zp.shape[-1]), F32)
            st_p = jnp.zeros((bp, d // LANE, LANE, LANE), F32)
            sh_s = _pad_cols(state_rwkv_shift[j], rw_segments)
            sh_s = jnp.concatenate([sh_s, jnp.zeros((bd, memq_cols), F32)], axis=1).reshape(bd, 1, -1)
            st_s = _pairs_from_heads(state_rwkv_wkv[j])
            mp, stp = _rwkv(zp, sh_p, st_p, mu, vecs, w2, a2, g2, d)
            ms, sts = _rwkv(zs, sh_s, st_s, mu, vecs, w2, a2, g2, d)
            outs["p_rw_sh"].append(unpad_rw(zp[:, -1]))
            outs["p_rw_wkv"].append(_heads_from_pairs(stp))
            outs["s_rw_sh"].append(unpad_rw(zs[:, -1]))
            outs["s_rw_wkv"].append(_heads_from_pairs(sts))
            w_out = a_w_out[j]
        elif kind == 1:
            w_in = b_w_in[j].astype(BF16)
            memq_cb = (q_cols + 2 * kv_cols) // memq_cols
            k_cb, v_cb = q_cols // kv_cols, q_cols // kv_cols + 1
            zp = _mm_norm(xp, attn_norm[i], w_in).reshape(bp, t, -1)
            zs = _mm_norm(xs, attn_norm[i], w_in).reshape(bd, s_len, -1)
            kgain = _tile_gain(b_k_norm[j], kv_cols)
            qgain = _tile_gain(b_q_norm[j], LANE)
            sink = jnp.repeat(b_sink[j].astype(F32), HEAD_DIM).reshape(1, q_cols)
            knp = _headnorm(zp.reshape(bp * t, -1), k_cb, kv_cols, kgain).reshape(bp, t, kv_cols)
            mp = _flash(zp, 0, q_cols, knp, 0, zp, v_cb, qgain, sink, mode="band", tq=WINDOW, kb=WINDOW,
                        group=att_group)
            outs["p_sw_k"].append(knp[:, t - win_rows:].reshape(bp, win_rows, ATT_KV_HEADS, HEAD_DIM))
            outs["p_sw_v"].append(zp[:, t - win_rows:, q_cols + kv_cols:q_cols + 2 * kv_cols]
                                  .reshape(bp, win_rows, ATT_KV_HEADS, HEAD_DIM))
            kns = _headnorm(zs.reshape(bd * s_len, -1), k_cb, kv_cols, kgain).reshape(bd, s_len, kv_cols)
            vs_new = zs[:, :, q_cols + kv_cols:q_cols + 2 * kv_cols]
            k_all = jnp.concatenate([cache_swa_k[j].reshape(bd, win_rows, kv_cols), kns], axis=1)
            v_all = jnp.concatenate([cache_swa_v[j].reshape(bd, win_rows, kv_cols), vs_new], axis=1)
            n_keys = win_rows + s_len
            n_pad = -(-n_keys // LANE) * LANE
            pad = jnp.zeros((bd, n_pad - n_keys, kv_cols), F32)
            ms = _flash(zs, 0, q_cols, jnp.concatenate([k_all, pad], axis=1), 0,
                        jnp.concatenate([v_all, pad], axis=1), 0, qgain, sink, mode="all", tq=s_len, kb=n_pad,
                        group=att_group, n_valid=n_keys)
            outs["s_sw_k"].append(k_all[:, n_keys - win_rows:].reshape(bd, win_rows, ATT_KV_HEADS, HEAD_DIM))
            outs["s_sw_v"].append(v_all[:, n_keys - win_rows:].reshape(bd, win_rows, ATT_KV_HEADS, HEAD_DIM))
            w_out = b_w_out[j]
        else:
            wc = c_w_in[j]
            w_in = jnp.concatenate([wc[:, :o_ki], wc[:, c_cols:],
                                    _pad_cols(wc, [(o_ki, IDX_DIM + IDX_HEADS, LANE)])], axis=1).astype(BF16)
            memq_cb = o_ki // memq_cols
            kw_cb = (o_ki + memq_cols) // LANE
            k_cb, v_cb = q_cols // kv_cols, q_cols // kv_cols + 1
            qi_cb = o_qi // (IDX_HEADS * IDX_DIM)
            zp = _mm_norm(xp, attn_norm[i], w_in).reshape(bp, t, -1)
            zs = _mm_norm(xs, attn_norm[i], w_in).reshape(bd, s_len, -1)
            kgain = _tile_gain(c_k_norm[j], kv_cols)
            qgain = _tile_gain(c_q_norm[j], LANE)
            igain = jnp.concatenate([c_idx_k_norm[j].astype(F32), jnp.zeros((LANE - IDX_DIM,), F32)]).reshape(1, LANE)
            no_sink = jnp.full((1, q_cols), -jnp.inf, F32)
            knp = _headnorm(zp.reshape(bp * t, -1), k_cb, kv_cols, kgain).reshape(bp, t, kv_cols)
            kip = _headnorm(zp.reshape(bp * t, -1), kw_cb, LANE, igain).reshape(bp, t, LANE)
            tq = min(t, DSA_TQ)
            kb = min(t, DSA_KB)
            sc, tau, cut = _dsa_select(zp, qi_cb, kw_cb, kip, tq=tq, kb=kb, k_sel=k_sel_p, n_valid=t, causal=True)
            mp = _flash(zp, 0, q_cols, knp, 0, zp, v_cb, qgain, no_sink, mode="dsa", tq=tq, kb=kb, group=att_group,
                        n_valid=t, causal=True, scores=sc, tau=tau, jb=cut)
            outs["p_ds_k"].append(knp.reshape(bp, t, ATT_KV_HEADS, HEAD_DIM))
            outs["p_ds_v"].append(zp[:, :, q_cols + kv_cols:q_cols + 2 * kv_cols].reshape(bp, t, ATT_KV_HEADS, HEAD_DIM))
            outs["p_ds_i"].append(kip[:, :, :IDX_DIM])
            kns = _headnorm(zs.reshape(bd * s_len, -1), k_cb, kv_cols, kgain).reshape(bd, s_len, kv_cols)
            kis = _headnorm(zs.reshape(bd * s_len, -1), kw_cb, LANE, igain).reshape(bd, s_len, LANE)
            vs_new = zs[:, :, q_cols + kv_cols:q_cols + 2 * kv_cols]
            n_keys = past + s_len
            n_pad = -(-n_keys // LANE) * LANE
            zpad = lambda w: jnp.zeros((bd, n_pad - n_keys, w), F32)
            k_all = jnp.concatenate([cache_dsa_k[j].reshape(bd, past, kv_cols), kns, zpad(kv_cols)], axis=1)
            v_all = jnp.concatenate([cache_dsa_v[j].reshape(bd, past, kv_cols), vs_new, zpad(kv_cols)], axis=1)
            ki_cache = jnp.concatenate([cache_dsa_idx_k[j], jnp.zeros((bd, past, LANE - IDX_DIM), F32)], axis=-1)
            ki_all = jnp.concatenate([ki_cache, kis, zpad(LANE)], axis=1)
            sc, tau, cut = _dsa_select(zs, qi_cb, kw_cb, ki_all, tq=s_len, kb=n_pad, k_sel=k_sel_s, n_valid=n_keys,
                                       causal=False)
            ms = _flash(zs, 0, q_cols, k_all, 0, v_all, 0, qgain, no_sink, mode="dsa", tq=s_len, kb=n_pad,
                        group=att_group, n_valid=n_keys, causal=False, scores=sc, tau=tau, jb=cut)
            outs["s_ds_k"].append(kns.reshape(bd, s_len, ATT_KV_HEADS, HEAD_DIM))
            outs["s_ds_v"].append(vs_new.reshape(bd, s_len, ATT_KV_HEADS, HEAD_DIM))
            outs["s_ds_i"].append(kis[:, :, :IDX_DIM])
            w_out = c_w_out[j]

        kv_mem = _mm_norm(mem_prompt.reshape(bp * mem_tokens, d), mem_norm[i], mem_w_kv[i].astype(BF16))
        km_p = _headnorm(kv_mem, 0, memq_cols, _tile_gain(mem_k_norm[i], memq_cols)).reshape(bp, mem_tokens, memq_cols)
        vm_p = kv_mem[:, memq_cols:].reshape(bp, mem_tokens, memq_cols)
        outs["p_mk"].append(km_p.reshape(bp, mem_tokens, MEM_HEADS, HEAD_DIM))
        outs["p_mv"].append(vm_p.reshape(bp, mem_tokens, MEM_HEADS, HEAD_DIM))
        mo_p = _mem_attend(zp, memq_cb, km_p, vm_p, mem_q_norm[i])
        mo_s = _mem_attend(zs, memq_cb, cache_mem_k[i].reshape(bd, mem_tokens, memq_cols),
                           cache_mem_v[i].reshape(bd, mem_tokens, memq_cols), mem_q_norm[i])
        w_mix, w_mem = w_out[:d].astype(BF16), w_out[d:].astype(BF16)
        xp = _mm_res([mp.reshape(bp * t, d), mo_p.reshape(bp * t, memq_cols)], [w_mix, w_mem], xp)
        xs = _mm_res([ms.reshape(bd * s_len, d), mo_s.reshape(bd * s_len, memq_cols)], [w_mix, w_mem], xs)

        w_up, w_down = ffn_w_up[i].astype(BF16), ffn_w_down[i].astype(BF16)
        xp, cp = _conv_ffn(xp, bp, t, ffn_norm[i], w_up, ffn_conv[i], w_down,
                           jnp.zeros((bp, CONV_W - 1, 2 * d_ff), F32))
        xs, cs = _conv_ffn(xs, bd, s_len, ffn_norm[i], w_up, ffn_conv[i], w_down, state_ffn_conv[i])
        outs["p_cv"].append(cp)
        outs["s_cv"].append(cs)

    st = jnp.stack
    order = ("p_rw_wkv", "p_rw_sh", "p_sw_k", "p_sw_v", "p_ds_k", "p_ds_v", "p_ds_i", "p_mk", "p_mv", "p_cv",
             "s_rw_wkv", "s_rw_sh", "s_sw_k", "s_sw_v", "s_ds_k", "s_ds_v", "s_ds_i", "s_cv")
    return (xp.reshape(bp, t, d), xs.reshape(bd, s_len, d)) + tuple(st(outs[k]) for k in order)
```

```python
import functools

import jax
import jax.numpy as jnp
from jax import lax
from jax.experimental import pallas as pl
from jax.experimental.pallas import tpu as pltpu

F32 = jnp.float32
BF16 = jnp.bfloat16

HEAD_DIM = 64
CHUNK = 64
NORM_EPS = 1e-6
RW_GN_EPS = HEAD_DIM * 1e-5
ATT_KV_HEADS = 4
WINDOW = 128
IDX_HEADS = 8
IDX_DIM = 64
TOPK_MAX = 256
MEM_HEADS = 4
CONV_W = 3

LANE = 128
VMEM_LIMIT = 52 * 1024 * 1024
NEG_BIG = -(2.0 ** 100)
INT_MIN = -2147483648
DSA_TQ = 256
DSA_KB = 512


def _cparams(sem, vmem=VMEM_LIMIT):
    return pltpu.CompilerParams(dimension_semantics=sem, vmem_limit_bytes=vmem)


def _split3(a):
    a1 = a.astype(BF16)
    r1 = a - a1.astype(F32)
    a2 = r1.astype(BF16)
    r2 = r1 - a2.astype(F32)
    return a1, a2, r2.astype(BF16)


def _dot_exact_rhs(a, e):
    a1, a2, a3 = _split3(a)
    d = lambda x: jnp.dot(x, e, preferred_element_type=F32)
    return (d(a3) + d(a2)) + d(a1)


def _dot_exact_lhs(e, a):
    a1, a2, a3 = _split3(a)
    d = lambda x: jnp.dot(e, x, preferred_element_type=F32)
    return (d(a3) + d(a2)) + d(a1)


def _head_block_matrix(width, value):
    r = lax.broadcasted_iota(jnp.int32, (width, width), 0) // HEAD_DIM
    c = lax.broadcasted_iota(jnp.int32, (width, width), 1) // HEAD_DIM
    return jnp.where(r == c, value, 0.0).astype(BF16)


def _dot_rw(a, b, dims):
    return lax.dot_general(a.astype(BF16), b.astype(BF16), (dims, ((), ())), preferred_element_type=F32)


def _pick_tile(n, cap):
    best = None
    for t in range(LANE, min(n, cap) + 1, LANE):
        if n % t == 0:
            best = t
    assert best is not None, n
    return best


def _mm_norm_body(x_ref, g_ref, w_ref, o_ref, xn_ref):
    @pl.when(pl.program_id(1) == 0)
    def _():
        x = x_ref[...]
        ms = jnp.mean(x * x, axis=-1, keepdims=True)
        xn_ref[...] = ((x * lax.rsqrt(ms + NORM_EPS)) * g_ref[...]).astype(BF16)

    o_ref[...] = jnp.dot(xn_ref[...], w_ref[...], preferred_element_type=F32)


def _mm_norm(x, gain, w):
    m, k = x.shape
    n = w.shape[1]
    tm = min(m, 512)
    tn = _pick_tile(n, 1536)
    return pl.pallas_call(
        _mm_norm_body,
        out_shape=jax.ShapeDtypeStruct((m, n), F32),
        grid=(m // tm, n // tn),
        in_specs=[pl.BlockSpec((tm, k), lambda i, j: (i, 0)),
                  pl.BlockSpec((1, k), lambda i, j: (0, 0)),
                  pl.BlockSpec((k, tn), lambda i, j: (0, j))],
        out_specs=pl.BlockSpec((tm, tn), lambda i, j: (i, j)),
        scratch_shapes=[pltpu.VMEM((tm, k), BF16)],
        compiler_params=_cparams(("parallel", "arbitrary")),
        name="mm_norm",
    )(x, gain.reshape(1, k), w)


def _mm_res_body(*refs, n_lhs):
    lhs = refs[:n_lhs]
    ws = refs[n_lhs:2 * n_lhs]
    r_ref, o_ref = refs[2 * n_lhs], refs[2 * n_lhs + 1]
    acc = jnp.dot(lhs[0][...], ws[0][...], preferred_element_type=F32)
    for a, w in zip(lhs[1:], ws[1:]):
        acc = acc + jnp.dot(a[...], w[...], preferred_element_type=F32)
    o_ref[...] = r_ref[...] + acc


def _mm_res(lhs_list, w_list, res):
    m, n = res.shape
    ktot = sum(a.shape[1] for a in lhs_list)
    tm = min(m, 512)
    tn = _pick_tile(n, 1024 if ktot <= 3072 else 512)
    n_lhs = len(lhs_list)
    in_specs = [pl.BlockSpec((tm, a.shape[1]), lambda i, j: (i, 0)) for a in lhs_list]
    in_specs += [pl.BlockSpec((w.shape[0], tn), lambda i, j: (0, j)) for w in w_list]
    in_specs += [pl.BlockSpec((tm, tn), lambda i, j: (i, j))]
    return pl.pallas_call(
        functools.partial(_mm_res_body, n_lhs=n_lhs),
        out_shape=jax.ShapeDtypeStruct((m, n), F32),
        grid=(m // tm, n // tn),
        in_specs=in_specs,
        out_specs=pl.BlockSpec((tm, tn), lambda i, j: (i, j)),
        compiler_params=_cparams(("parallel", "arbitrary")),
        name="mm_res",
    )(*lhs_list, *w_list, res)


def _headnorm_body(x_ref, g_ref, o_ref, *, width):
    avg = _head_block_matrix(LANE, 1.0 / HEAD_DIM)
    for c in range(width // LANE):
        x = x_ref[:, c * LANE:(c + 1) * LANE]
        ms = _dot_exact_rhs(x * x, avg)
        o_ref[:, c * LANE:(c + 1) * LANE] = (x * lax.rsqrt(ms + NORM_EPS)) * g_ref[:, c * LANE:(c + 1) * LANE]


def _headnorm(x, col_block, width, gain_row):
    m = x.shape[0]
    tm = min(m, 1024)
    return pl.pallas_call(
        functools.partial(_headnorm_body, width=width),
        out_shape=jax.ShapeDtypeStruct((m, width), F32),
        grid=(m // tm,),
        in_specs=[pl.BlockSpec((tm, width), lambda i: (i, col_block)),
                  pl.BlockSpec((1, width), lambda i: (0, 0))],
        out_specs=pl.BlockSpec((tm, width), lambda i: (i, 0)),
        compiler_params=_cparams(("parallel",)),
        name="headnorm",
    )(x, gain_row)


def _up_conv_body(x_ref, g_ref, wa_ref, wb_ref, pa_ref, pb_ref, cwa_ref, cwb_ref, o_ref, la_ref, lb_ref,
                  xn_ref, ca_ref, cb_ref, *, tm, tiles_per_batch):
    i = pl.program_id(0)
    j = pl.program_id(1)

    @pl.when(j == 0)
    def _():
        x = x_ref[...]
        ms = jnp.mean(x * x, axis=-1, keepdims=True)
        xn_ref[...] = ((x * lax.rsqrt(ms + NORM_EPS)) * g_ref[...]).astype(BF16)

    @pl.when(i % tiles_per_batch == 0)
    def _():
        ca_ref[j] = pa_ref[0]
        cb_ref[j] = pb_ref[0]

    xn = xn_ref[...]
    row = lax.broadcasted_iota(jnp.int32, o_ref.shape, 0)

    def conv(w_ref, c_ref, cw_ref, last_ref):
        u = jnp.dot(xn, w_ref[...], preferred_element_type=F32)
        car = c_ref[j]
        c0, c1 = car[0:1, :], car[1:2, :]
        u1 = jnp.where(row == 0, c1, pltpu.roll(u, 1, axis=0))
        u2 = jnp.where(row == 0, c0, jnp.where(row == 1, c1, pltpu.roll(u, 2, axis=0)))
        c_ref[j] = u[tm - 2:tm, :]
        last_ref[0] = u[tm - 2:tm, :]
        return (u2 * cw_ref[0:1, :] + u1 * cw_ref[1:2, :]) + u * cw_ref[2:3, :]

    a = conv(wa_ref, ca_ref, cwa_ref, la_ref)
    b = conv(wb_ref, cb_ref, cwb_ref, lb_ref)
    o_ref[...] = ((a * jax.nn.sigmoid(a)) * b).astype(o_ref.dtype)


def _up_conv(x, gain, w_up, conv_w, prev, t):
    m, k = x.shape
    f = w_up.shape[1] // 2
    b = m // t
    tm = min(t, 512)
    tn = _pick_tile(f, 512)
    nj = f // tn
    tpb = t // tm
    act, la, lb = pl.pallas_call(
        functools.partial(_up_conv_body, tm=tm, tiles_per_batch=tpb),
        out_shape=(jax.ShapeDtypeStruct((m, f), BF16),
                   jax.ShapeDtypeStruct((b, CONV_W - 1, f), F32),
                   jax.ShapeDtypeStruct((b, CONV_W - 1, f), F32)),
        grid=(m // tm, nj),
        in_specs=[pl.BlockSpec((tm, k), lambda i, j: (i, 0)),
                  pl.BlockSpec((1, k), lambda i, j: (0, 0)),
                  pl.BlockSpec((k, tn), lambda i, j: (0, j)),
                  pl.BlockSpec((k, tn), lambda i, j: (0, nj + j)),
                  pl.BlockSpec((1, CONV_W - 1, tn), lambda i, j: (i // tpb, 0, j)),
                  pl.BlockSpec((1, CONV_W - 1, tn), lambda i, j: (i // tpb, 0, nj + j)),
                  pl.BlockSpec((CONV_W, tn), lambda i, j: (0, j)),
                  pl.BlockSpec((CONV_W, tn), lambda i, j: (0, nj + j))],
        out_specs=(pl.BlockSpec((tm, tn), lambda i, j: (i, j)),
                   pl.BlockSpec((1, CONV_W - 1, tn), lambda i, j: (i // tpb, 0, j)),
                   pl.BlockSpec((1, CONV_W - 1, tn), lambda i, j: (i // tpb, 0, j))),
        scratch_shapes=[pltpu.VMEM((tm, k), BF16),
                        pltpu.VMEM((nj, CONV_W - 1, tn), F32),
                        pltpu.VMEM((nj, CONV_W - 1, tn), F32)],
        compiler_params=_cparams(("arbitrary", "arbitrary")),
        name="up_conv_gate",
    )(x, gain.reshape(1, k), w_up, w_up, prev, prev, conv_w, conv_w)
    return act, jnp.concatenate([la, lb], axis=-1)


def _flash_body(*refs, mode, tq, kb, wq, group, n_valid, nk, causal):
    q_ref, k_ref, v_ref, qg_ref, sink_ref = refs[:5]
    pos = 5
    if mode == "dsa":
        sc_ref, tau_ref, jb_ref = refs[pos:pos + 3]
        pos += 3
    o_ref = refs[pos]
    qn_scr, acc_scr, m_scr, l_scr, kab_scr, vab_scr, bias_scr = refs[pos + 1:]

    i = pl.program_id(1)
    j = pl.program_id(2)
    npairs = wq // LANE
    pairs_per_unit = group // 2 if group > 1 else 1
    n_units = npairs // pairs_per_unit
    lane = lax.broadcasted_iota(jnp.int32, (1, LANE), 1)
    lo_half = lane < HEAD_DIM

    @pl.when(j == 0)
    def _():
        avg = _head_block_matrix(LANE, 1.0 / HEAD_DIM)
        for p in range(npairs):
            x = q_ref[0, :, p * LANE:(p + 1) * LANE]
            ms = _dot_exact_rhs(x * x, avg)
            qn = ((x * lax.rsqrt(ms + NORM_EPS)) * qg_ref[...]) * (HEAD_DIM ** -0.5)
            qn_scr[p] = qn.astype(BF16)
        acc_scr[...] = jnp.zeros(acc_scr.shape, F32)
        l_scr[...] = jnp.zeros(l_scr.shape, F32)
        m_scr[...] = jnp.full(m_scr.shape, NEG_BIG, F32)
        ones_lo = jnp.broadcast_to(jnp.where(lo_half, 1.0, 0.0), (kb, LANE)).astype(BF16)
        ones_hi = jnp.broadcast_to(jnp.where(lo_half, 0.0, 1.0), (kb, LANE)).astype(BF16)
        for u in range(n_units):
            vab_scr[u, 0:kb, LANE:2 * LANE] = ones_lo
            vab_scr[u, kb:2 * kb, LANE:2 * LANE] = ones_hi

    if mode == "band":
        kblk = i - WINDOW // kb + j
        active = kblk >= 0
    elif mode == "dsa" and causal:
        kblk = j
        active = j * kb < (i + 1) * tq
    else:
        kblk = j
        active = None

    def step():
        for u in range(n_units):
            if group > 1:
                tile_idx, half = u // 2, u % 2
            else:
                tile_idx, half = u, None
            for src, dst in ((k_ref, kab_scr), (v_ref, vab_scr)):
                tile = src[0, :, tile_idx * LANE:(tile_idx + 1) * LANE]
                if half is None:
                    a_part = jnp.where(lo_half, tile, 0.0)
                    b_part = jnp.where(lo_half, 0.0, tile)
                elif half == 0:
                    a_part = jnp.where(lo_half, tile, 0.0)
                    b_part = pltpu.roll(a_part, HEAD_DIM, axis=1)
                else:
                    b_part = jnp.where(lo_half, 0.0, tile)
                    a_part = pltpu.roll(b_part, HEAD_DIM, axis=1)
                dst[u, 0:kb, 0:LANE] = a_part.astype(BF16)
                dst[u, kb:2 * kb, 0:LANE] = b_part.astype(BF16)

        kidx = kblk * kb + lax.broadcasted_iota(jnp.int32, (tq, kb), 1)
        qpos = i * tq + lax.broadcasted_iota(jnp.int32, (tq, kb), 0)
        if mode == "band":
            qchunk = qpos // CHUNK
            sel = (kidx >= (qchunk - WINDOW // CHUNK) * CHUNK) & (kidx < (qchunk + 1) * CHUNK) & (kidx >= 0)
        elif mode == "dsa":
            x = sc_ref[0]
            tau = tau_ref[0][:, 0:1]
            jb = jb_ref[0][:, 0:1]
            adm = kidx < ((qpos // CHUNK + 1) * CHUNK if causal else n_valid)
            sel = adm & ((x > tau) | ((x == tau) & (kidx <= jb)))
        else:
            sel = kidx < n_valid
        bias_scr[...] = jnp.where(sel, 0.0, NEG_BIG).astype(BF16)

        def pair_body(p, carry):
            u = p // pairs_per_unit
            s = lax.dot_general(qn_scr[p], kab_scr[u], (((1,), (1,)), ((), ())),
                                preferred_element_type=F32)
            bias = bias_scr[...]
            m_old = m_scr[p]
            s0 = s[:, 0:kb].astype(BF16) + bias
            s1 = s[:, kb:2 * kb].astype(BF16) + bias
            mn0 = jnp.maximum(m_old[:, 0:1], jnp.max(s0, axis=1, keepdims=True).astype(F32))
            mn1 = jnp.maximum(m_old[:, HEAD_DIM:HEAD_DIM + 1], jnp.max(s1, axis=1, keepdims=True).astype(F32))
            p0 = jnp.exp(s0 - mn0.astype(BF16))
            p1 = jnp.exp(s1 - mn1.astype(BF16))
            pcat = jnp.concatenate([p0, p1], axis=1)
            pv = jnp.dot(pcat, vab_scr[u], preferred_element_type=F32)
            mn = jnp.where(lo_half, mn0, mn1)
            alpha = jnp.exp(m_old - mn)
            acc_scr[p] = acc_scr[p] * alpha + pv[:, 0:LANE]
            l_scr[p] = l_scr[p] * alpha + pv[:, LANE:2 * LANE]
            m_scr[p] = mn
            return carry

        lax.fori_loop(0, npairs, pair_body, 0, unroll=True)

    if active is None:
        step()
    else:
        pl.when(active)(step)

    @pl.when(j == nk - 1)
    def _():
        for p in range(npairs):
            den = l_scr[p] + jnp.exp(sink_ref[:, p * LANE:(p + 1) * LANE] - m_scr[p])
            o_ref[0, :, p * LANE:(p + 1) * LANE] = (acc_scr[p] / den).astype(o_ref.dtype)


def _flash(q_arr, q_cb, wq, k_arr, k_cb, v_arr, v_cb, q_gain, sink_row, *, mode, tq, kb, group,
           n_valid=None, causal=False, scores=None, tau=None, jb=None):
    b, t_q = q_arr.shape[0], q_arr.shape[1]
    t_k = k_arr.shape[1]
    wk = ATT_KV_HEADS * HEAD_DIM
    nq = t_q // tq
    if mode == "band":
        assert tq == kb == WINDOW
        nk = 2
        kmap = lambda i, j: jnp.maximum(i - 1 + j, 0)
    elif mode == "dsa" and causal:
        nk = t_k // kb
        kmap = lambda i, j: jnp.minimum(j, ((i + 1) * tq - 1) // kb)
    else:
        nk = t_k // kb
        kmap = lambda i, j: j
    npairs = wq // LANE
    n_units = npairs // (group // 2) if group > 1 else npairs
    in_specs = [pl.BlockSpec((1, tq, wq), lambda bi, i, j: (bi, i, q_cb)),
                pl.BlockSpec((1, kb, wk), lambda bi, i, j: (bi, kmap(i, j), k_cb)),
                pl.BlockSpec((1, kb, wk), lambda bi, i, j: (bi, kmap(i, j), v_cb)),
                pl.BlockSpec((1, LANE), lambda bi, i, j: (0, 0)),
                pl.BlockSpec((1, wq), lambda bi, i, j: (0, 0))]
    args = [q_arr, k_arr, v_arr, q_gain, sink_row]
    if mode == "dsa":
        in_specs += [pl.BlockSpec((1, tq, kb), lambda bi, i, j: (bi, i, kmap(i, j))),
                     pl.BlockSpec((1, tq, LANE), lambda bi, i, j: (bi, i, 0)),
                     pl.BlockSpec((1, tq, LANE), lambda bi, i, j: (bi, i, 0))]
        args += [scores, tau, jb]
    body = functools.partial(_flash_body, mode=mode, tq=tq, kb=kb, wq=wq, group=group,
                             n_valid=n_valid, nk=nk, causal=causal)
    return pl.pallas_call(
        body,
        out_shape=jax.ShapeDtypeStruct((b, t_q, wq), BF16),
        grid=(b, nq, nk),
        in_specs=in_specs,
        out_specs=pl.BlockSpec((1, tq, wq), lambda bi, i, j: (bi, i, 0)),
        scratch_shapes=[pltpu.VMEM((npairs, tq, LANE), BF16),
                        pltpu.VMEM((npairs, tq, LANE), F32),
                        pltpu.VMEM((npairs, tq, LANE), F32),
                        pltpu.VMEM((npairs, tq, LANE), F32),
                        pltpu.VMEM((n_units, 2 * kb, LANE), BF16),
                        pltpu.VMEM((n_units, 2 * kb, 2 * LANE), BF16),
                        pltpu.VMEM((tq, kb), BF16)],
        compiler_params=_cparams(("parallel", "parallel", "arbitrary")),
        name="flash_" + mode,
    )(*args)


def _dsa_select_body(qi_ref, kw_ref, ki_ref, sc_ref, tau_ref, jb_ref, sc_scr, *, tq, kb, nk, k_sel,
                     n_valid, causal, idx_bits):
    i = pl.program_id(1)
    j = pl.program_id(2)
    qpos = i * tq + lax.broadcasted_iota(jnp.int32, (tq, 1), 0)
    if causal:
        n_adm = (qpos // CHUNK + 1) * CHUNK
        active = j * kb < (i + 1) * tq
        n_blocks = ((i + 1) * tq + kb - 1) // kb
    else:
        n_adm = jnp.full((tq, 1), n_valid, jnp.int32)
        active = None
        n_blocks = nk

    def compute():
        ka = ki_ref[0]
        kab = jnp.concatenate([ka, pltpu.roll(ka, HEAD_DIM, axis=1)], axis=0).astype(BF16)
        kw = kw_ref[0]
        acc = jnp.zeros((tq, kb), F32)
        for p in range(IDX_HEADS // 2):
            qp = qi_ref[0, :, p * LANE:(p + 1) * LANE].astype(BF16)
            s = lax.dot_general(qp, kab, (((1,), (1,)), ((), ())), preferred_element_type=F32)
            for h in range(2):
                col = HEAD_DIM + 2 * p + h
                w = (kw[:, col:col + 1] * IDX_HEADS ** -0.5) * IDX_DIM ** -0.5
                acc = acc + jnp.maximum(s[:, h * kb:(h + 1) * kb], 0.0) * w
        acc = jnp.where(acc == 0.0, 0.0, acc)
        kidx = j * kb + lax.broadcasted_iota(jnp.int32, (tq, kb), 1)
        x = jnp.where(kidx < n_adm, acc, -jnp.inf)
        sc_scr[j] = x
        sc_ref[0] = x

    if active is None:
        compute()
    else:
        pl.when(active)(compute)

        @pl.when(jnp.logical_not(active))
        def _():
            sc_ref[0] = jnp.full((tq, kb), -jnp.inf, F32)

    @pl.when(j == nk - 1)
    def _():
        kf = float(k_sel)

        rsz = min(tq, LANE)

        def count(make_pred):
            outs = []
            for r0 in range(0, tq, rsz):
                pred = make_pred(lambda col, r0=r0: jnp.broadcast_to(col[r0:r0 + rsz], (rsz, LANE)))

                def blk(jj, acc, r0=r0, pred=pred):
                    for a in range(kb // LANE):
                        x = sc_scr[jj, r0:r0 + rsz, a * LANE:(a + 1) * LANE]
                        acc = acc + jnp.where(pred(x, jj * kb + a * LANE), 1.0, 0.0)
                    return acc
                acc = lax.fori_loop(0, n_blocks, blk, jnp.zeros((rsz, LANE), F32))
                outs.append(jnp.sum(acc, axis=1, keepdims=True))
            return jnp.concatenate(outs, axis=0)

        def key_to_float(tu):
            cs = tu ^ INT_MIN
            fb = jnp.where(cs >= 0, cs, cs ^ 0x7FFFFFFF)
            return lax.bitcast_convert_type(fb, F32)

        def bit_body(it, tu):
            cand = tu | jnp.left_shift(jnp.int32(1), 31 - it)
            thr = key_to_float(cand)
            c = count(lambda widen: (lambda x, base, t=widen(thr): x >= t))
            return jnp.where(c >= kf, cand, tu)

        tu = lax.fori_loop(0, 32, bit_body, jnp.zeros((tq, 1), jnp.int32))
        full = n_adm >= k_sel
        tau = jnp.where(full, key_to_float(tu), -jnp.inf)
        c_gt = count(lambda widen: (lambda x, base, t=widen(tau): x > t))
        c_ge = count(lambda widen: (lambda x, base, t=widen(tau): x >= t))
        need = kf - c_gt
        tau_ref[0] = jnp.broadcast_to(tau, (tq, LANE))
        jb_ref[0] = jnp.full((tq, LANE), 2 ** 30, jnp.int32)
        surplus = jnp.max(jnp.where(full, (c_ge - c_gt) - need, 0.0))

        @pl.when(surplus > 0.5)
        def _():
            def tie_body(it, cut):
                cand = cut | jnp.left_shift(jnp.int32(1), idx_bits - 1 - it)

                def make_pred(widen):
                    t, cnd = widen(tau), widen(cand)
                    lane_idx = lax.broadcasted_iota(jnp.int32, (rsz, LANE), 1)
                    return lambda x, base: (x == t) & (base + lane_idx < cnd)
                c = count(make_pred)
                return jnp.where(c < need, cand, cut)

            cut = lax.fori_loop(0, idx_bits, tie_body, jnp.zeros((tq, 1), jnp.int32))
            jb_ref[0] = jnp.broadcast_to(cut, (tq, LANE))


def _dsa_select(z, qi_cb, kw_cb, ki_n, *, tq, kb, k_sel, n_valid, causal):
    b, t_q = z.shape[0], z.shape[1]
    t_k = ki_n.shape[1]
    nq, nk = t_q // tq, t_k // kb
    kmap = (lambda i, j: jnp.minimum(j, ((i + 1) * tq - 1) // kb)) if causal else (lambda i, j: j)
    body = functools.partial(_dsa_select_body, tq=tq, kb=kb, nk=nk, k_sel=k_sel, n_valid=n_valid,
                             causal=causal, idx_bits=max(1, (t_k - 1).bit_length()))
    return pl.pallas_call(
        body,
        out_shape=(jax.ShapeDtypeStruct((b, t_q, t_k), F32),
                   jax.ShapeDtypeStruct((b, t_q, LANE), F32),
                   jax.ShapeDtypeStruct((b, t_q, LANE), jnp.int32)),
        grid=(b, nq, nk),
        in_specs=[pl.BlockSpec((1, tq, IDX_HEADS * IDX_DIM), lambda bi, i, j: (bi, i, qi_cb)),
                  pl.BlockSpec((1, tq, LANE), lambda bi, i, j: (bi, i, kw_cb)),
                  pl.BlockSpec((1, kb, LANE), lambda bi, i, j: (bi, kmap(i, j), 0))],
        out_specs=(pl.BlockSpec((1, tq, kb), lambda bi, i, j: (bi, i, j)),
                   pl.BlockSpec((1, tq, LANE), lambda bi, i, j: (bi, i, 0)),
                   pl.BlockSpec((1, tq, LANE), lambda bi, i, j: (bi, i, 0))),
        scratch_shapes=[pltpu.VMEM((nk, tq, kb), F32)],
        compiler_params=_cparams(("parallel", "parallel", "arbitrary")),
        name="dsa_select",
    )(z, z, ki_n)


def _rwkv_body(zr_ref, zk_ref, zv_ref, zw_ref, za_ref, zg_ref,
               sr_ref, sk_ref, sv_ref, sw_ref, sa_ref, sg_ref,
               mr_ref, mk_ref, mv_ref, mw_ref, ma_ref, mg_ref,
               vec_ref, w2_ref, a2_ref, g2_ref, s0_ref,
               mix_ref, sfin_ref,
               s_scr, prev_scr, prevl_scr, prevg_scr, y_scr, *, tc, lc, nt, pp):
    t = pl.program_id(2)

    @pl.when(t == 0)
    def _():
        s_scr[...] = s0_ref[0]
        prev_scr[0:1, :] = sr_ref[0]
        prev_scr[1:2, :] = sk_ref[0]
        prev_scr[2:3, :] = sv_ref[0]
        prevl_scr[0:1, :] = sw_ref[0]
        prevl_scr[1:2, :] = sa_ref[0]
        prevg_scr[0:1, :] = sg_ref[0]

    row = lax.broadcasted_iota(jnp.int32, (tc, 1), 0)

    def shifted(z_ref, prow, mu_ref):
        z = z_ref[0]
        zp = jnp.where(row == 0, prow, pltpu.roll(z, 1, axis=0))
        return z + (zp - z) * mu_ref[...], z[tc - 1:tc, :]

    r, last_r = shifted(zr_ref, prev_scr[0:1, :], mr_ref)
    k, last_k = shifted(zk_ref, prev_scr[1:2, :], mk_ref)
    v, last_v = shifted(zv_ref, prev_scr[2:3, :], mv_ref)
    zw, last_w = shifted(zw_ref, prevl_scr[0:1, :], mw_ref)
    za, last_a = shifted(za_ref, prevl_scr[1:2, :], ma_ref)
    zg, last_g = shifted(zg_ref, prevg_scr[0:1, :], mg_ref)
    prev_scr[0:1, :] = last_r
    prev_scr[1:2, :] = last_k
    prev_scr[2:3, :] = last_v
    prevl_scr[0:1, :] = last_w
    prevl_scr[1:2, :] = last_a
    prevg_scr[0:1, :] = last_g

    w0, a0 = vec_ref[0:1, :], vec_ref[1:2, :]
    k_k, k_a, r_k = vec_ref[2:3, :], vec_ref[3:4, :], vec_ref[4:5, :]
    ln_w, ln_b = vec_ref[5:6, :], vec_ref[6:7, :]

    ones_blk = _head_block_matrix(LANE, 1.0)
    avg_blk = _head_block_matrix(LANE, 1.0 / HEAD_DIM)

    def per_head(x, blk):
        return jnp.concatenate([_dot_exact_rhs(x[:, i * LANE:(i + 1) * LANE], blk) for i in range(pp)], axis=1)

    xw = w0 + jnp.dot(jnp.tanh(zw).astype(BF16), w2_ref[...], preferred_element_type=F32)
    nx = -xw
    softplus = jnp.maximum(nx, 0.0) + jnp.log1p(jnp.exp(-jnp.abs(nx)))
    w_log = -softplus - 0.5
    lw = -jnp.exp(w_log)
    a = jax.nn.sigmoid(a0 + jnp.dot(za.astype(BF16), a2_ref[...], preferred_element_type=F32))
    g = jnp.dot(jax.nn.sigmoid(zg).astype(BF16), g2_ref[...], preferred_element_type=F32)
    kk = k * k_k
    kk = kk / jnp.maximum(jnp.sqrt(per_head(kk * kk, ones_blk)), 1e-12)
    k2 = k * (1.0 + (a - 1.0) * k_a)

    a_step = -kk
    b_step = kk * a

    lane = lax.broadcasted_iota(jnp.int32, (1, LANE), 1)
    m0 = lane < HEAD_DIM
    rr = lax.broadcasted_iota(jnp.int32, (lc, 2 * lc), 0)
    cc = lax.broadcasted_iota(jnp.int32, (lc, 2 * lc), 1)
    incl = jnp.where(cc < lc, cc, cc - lc) <= rr
    strict = (lax.broadcasted_iota(jnp.int32, (2 * lc, 2 * lc), 1)
              < lax.broadcasted_iota(jnp.int32, (2 * lc, 2 * lc), 0))
    tri = (lax.broadcasted_iota(jnp.int32, (lc, lc), 1)
           <= lax.broadcasted_iota(jnp.int32, (lc, lc), 0)).astype(BF16)
    nsteps = lc.bit_length() - 1
    nt_dims = ((1,), (1,))
    nn_dims = ((1,), (0,))
    tn_dims = ((0,), (0,))

    eye = (lax.broadcasted_iota(jnp.int32, (2 * lc, 2 * lc), 0)
           == lax.broadcasted_iota(jnp.int32, (2 * lc, 2 * lc), 1)).astype(F32)

    chunks = range(tc // lc)
    st = []
    for c in chunks:
        rows = slice(c * lc, (c + 1) * lc)
        lwc = lw[rows]
        cs = _dot_exact_lhs(tri, lwc)
        p_in = jnp.exp(-cs)
        at = a_step[rows] * jnp.exp(cs - lwc)
        bt = b_step[rows] * p_in
        kt = k2[rows] * p_in
        rt = r[rows] * jnp.exp(cs)
        vc = v[rows]
        p_last = jnp.exp(cs[lc - 1:lc, :])
        for pi in range(pp):
            cols = slice(pi * LANE, (pi + 1) * LANE)
            stack = lambda x: jnp.concatenate([jnp.where(m0, x[:, cols], 0.0),
                                               jnp.where(m0, 0.0, x[:, cols])], axis=0).astype(BF16)
            st.append(dict(c=c, pi=pi,
                           ar=jnp.concatenate([stack(at), rt[:, cols].astype(BF16)], axis=0),
                           bk=jnp.concatenate([stack(bt), stack(kt)], axis=0),
                           v_s=stack(vc), p_last=p_last[:, cols]))
    for d in st:
        gram = _dot_rw(d["ar"], d["bk"], nt_dims)
        d["t_p"] = jnp.where(strict, gram[0:2 * lc, 0:2 * lc], 0.0)
        d["w_ak"] = jnp.where(strict, gram[0:2 * lc, 2 * lc:4 * lc], 0.0).astype(BF16)
        d["w_rb"] = jnp.where(incl, gram[2 * lc:3 * lc, 0:2 * lc], 0.0).astype(BF16)
        d["w_rk"] = jnp.where(incl, gram[2 * lc:3 * lc, 2 * lc:4 * lc], 0.0).astype(BF16)
        d["minv"] = eye + d["t_p"]
    for _ in range(nsteps - 1):
        for d in st:
            d["t_p"] = _dot_rw(d["t_p"], d["t_p"], nn_dims)
        for d in st:
            d["minv"] = d["minv"] + _dot_rw(d["minv"], d["t_p"], nn_dims)
    for d in st:
        d["minv"] = d["minv"].astype(BF16)
        d["wv"] = _dot_rw(d["w_ak"], d["v_s"], nn_dims)
        d["y_c"] = _dot_rw(d["w_rk"], d["v_s"], nn_dims)
    for d in st:
        d["ma"] = _dot_rw(d["minv"], d["ar"][0:2 * lc], nn_dims)
        d["mwv"] = _dot_rw(d["minv"], d["wv"], nn_dims)
    for d in st:
        d["g"] = _dot_rw(d["ma"], d["bk"][0:2 * lc], tn_dims).astype(BF16)
        d["d"] = _dot_rw(jnp.concatenate([d["mwv"].astype(BF16), d["v_s"]], axis=0), d["bk"], tn_dims)
    s_cur = [s_scr[pi] for pi in range(pp)]
    for d in st:
        s_in = s_cur[d["pi"]]
        d["s0"] = s_in.astype(BF16)
        s_cur[d["pi"]] = ((s_in + _dot_rw(d["s0"], d["g"], nn_dims)) + d["d"]) * d["p_last"]
    for pi in range(pp):
        s_scr[pi] = s_cur[pi]
    for d in st:
        d["xr"] = _dot_rw(d["ar"], d["s0"], nt_dims)
    for d in st:
        d["u"] = (_dot_rw(d["minv"], d["xr"][0:2 * lc], nn_dims) + d["mwv"]).astype(BF16)
    for d in st:
        c, pi = d["c"], d["pi"]
        y_scr[c * lc:(c + 1) * lc, pi * LANE:(pi + 1) * LANE] = (
            (d["xr"][2 * lc:3 * lc] + _dot_rw(d["w_rb"], d["u"], nn_dims)) + d["y_c"])

    y = y_scr[...]
    mean = per_head(y, avg_blk)
    dev = y - mean
    var = per_head(dev * dev, avg_blk)
    yn = (dev * lax.rsqrt(var + RW_GN_EPS)) * ln_w + ln_b
    bonus = per_head((r * k2) * r_k, ones_blk) * v
    mix_ref[0] = ((yn + bonus) * g).astype(mix_ref.dtype)

    @pl.when(t == nt - 1)
    def _():
        sfin_ref[0] = s_scr[...]


def _rwkv(z, shift_prev, s0_pairs, mu, vecs, w2, a2, g2, d_model):
    b, t = z.shape[0], z.shape[1]
    npairs = d_model // LANE
    tc = min(t, 512)
    lc = min(CHUNK, t)
    nt = t // tc
    pp = 2 if tc // lc >= 4 else 8
    wp = pp * LANE
    ngroups = npairs // pp
    cb_w, cb_a, cb_g = 3 * npairs, 3 * npairs + 1, (3 * npairs + 2) // 2

    def zspec(width, cbf):
        return pl.BlockSpec((1, tc, width), lambda bi, p, ti: (bi, ti, cbf(p)))

    def sspec(width, cbf):
        return pl.BlockSpec((1, 1, width), lambda bi, p, ti: (bi, 0, cbf(p)))

    def mspec(width, cbf):
        return pl.BlockSpec((1, width), lambda bi, p, ti: (0, cbf(p)))

    cbfs = [(wp, lambda p: p), (wp, lambda p: ngroups + p), (wp, lambda p: 2 * ngroups + p),
            (LANE, lambda p: cb_w), (LANE, lambda p: cb_a), (2 * LANE, lambda p: cb_g)]
    in_specs = ([zspec(w, f) for w, f in cbfs] + [sspec(w, f) for w, f in cbfs] + [mspec(w, f) for w, f in cbfs]
                + [pl.BlockSpec((8, wp), lambda bi, p, ti: (0, p)),
                   pl.BlockSpec((LANE, wp), lambda bi, p, ti: (0, p)),
                   pl.BlockSpec((LANE, wp), lambda bi, p, ti: (0, p)),
                   pl.BlockSpec((2 * LANE, wp), lambda bi, p, ti: (0, p)),
                   pl.BlockSpec((1, pp, LANE, LANE), lambda bi, p, ti: (bi, p, 0, 0))])
    return pl.pallas_call(
        functools.partial(_rwkv_body, tc=tc, lc=lc, nt=nt, pp=pp),
        out_shape=(jax.ShapeDtypeStruct((b, t, d_model), BF16),
                   jax.ShapeDtypeStruct((b, npairs, LANE, LANE), F32)),
        grid=(b, ngroups, nt),
        in_specs=in_specs,
        out_specs=(pl.BlockSpec((1, tc, wp), lambda bi, p, ti: (bi, ti, p)),
                   pl.BlockSpec((1, pp, LANE, LANE), lambda bi, p, ti: (bi, p, 0, 0))),
        scratch_shapes=[pltpu.VMEM((pp, LANE, LANE), F32),
                        pltpu.VMEM((8, wp), F32),
                        pltpu.VMEM((8, LANE), F32),
                        pltpu.VMEM((8, 2 * LANE), F32),
                        pltpu.VMEM((tc, wp), F32)],
        compiler_params=_cparams(("parallel", "parallel", "arbitrary")),
        name="rwkv7",
    )(*([z] * 6), *([shift_prev] * 6), *([mu] * 6), vecs, w2, a2, g2, s0_pairs)


def _pad_cols(x, segments):
    parts = []
    for start, width, padded in segments:
        seg = x[..., start:start + width]
        if padded > width:
            seg = jnp.concatenate([seg, jnp.zeros(seg.shape[:-1] + (padded - width,), seg.dtype)], axis=-1)
        parts.append(seg)
    return jnp.concatenate(parts, axis=-1)


def _pad_rows(x, padded):
    return jnp.concatenate([x, jnp.zeros((padded - x.shape[0],) + x.shape[1:], x.dtype)], axis=0)


def _pairs_from_heads(s):
    b, h = s.shape[0], s.shape[1]
    s = s.reshape(b, h // 2, 2, HEAD_DIM, HEAD_DIM)
    z = jnp.zeros_like(s[:, :, 0])
    top = jnp.concatenate([s[:, :, 0], z], axis=-1)
    bot = jnp.concatenate([z, s[:, :, 1]], axis=-1)
    return jnp.concatenate([top, bot], axis=-2)


def _heads_from_pairs(sp):
    b, npairs = sp.shape[0], sp.shape[1]
    s = jnp.stack([sp[:, :, :HEAD_DIM, :HEAD_DIM], sp[:, :, HEAD_DIM:, HEAD_DIM:]], axis=2)
    return s.reshape(b, 2 * npairs, HEAD_DIM, HEAD_DIM)


def _tile_gain(g, width):
    return jnp.tile(g.astype(F32), width // HEAD_DIM).reshape(1, width)


def _mem_attend(z, memq_cb, km, vm, q_gain):
    t_q = z.shape[1]
    wq = MEM_HEADS * HEAD_DIM
    tq = min(t_q, 512)
    no_sink = jnp.full((1, wq), -jnp.inf, F32)
    return _flash(z, memq_cb, wq, km, 0, vm, 0, _tile_gain(q_gain, LANE), no_sink,
                  mode="all", tq=tq, kb=km.shape[1], group=1, n_valid=km.shape[1])


def _conv_ffn(x2d, b, t, gain, w_up, conv_w, w_down, prev):
    act, u_last = _up_conv(x2d, gain, w_up, conv_w, prev, t)
    return _mm_res([act], [w_down], x2d), u_last


def kernel(x_prompt, x_sample, state_rwkv_wkv, state_rwkv_shift, cache_swa_k, cache_swa_v, cache_dsa_k, cache_dsa_v, cache_dsa_idx_k, cache_mem_k, cache_mem_v, state_ffn_conv, mem_prompt, attn_norm, ffn_norm, mem_norm, mem_w_kv, mem_q_norm, mem_k_norm, a_w_in, a_mu, a_w0, a_w2, a_a0, a_a2, a_g2, a_k_k, a_k_a, a_r_k, a_ln_w, a_ln_b, a_w_out, b_w_in, b_q_norm, b_k_norm, b_sink, b_w_out, c_w_in, c_q_norm, c_k_norm, c_idx_k_norm, c_w_out, ffn_w_up, ffn_conv, ffn_w_down):
    bp, t, d = x_prompt.shape
    bd, s_len = x_sample.shape[:2]
    depth = attn_norm.shape[0]
    win_rows = cache_swa_k.shape[2]
    past = cache_dsa_k.shape[2] if cache_dsa_k.shape[0] else 0
    d_ff = ffn_w_down.shape[1]
    mem_tokens = mem_prompt.shape[1]
    q_cols = d
    kv_cols = ATT_KV_HEADS * HEAD_DIM
    memq_cols = MEM_HEADS * HEAD_DIM
    att_group = (d // HEAD_DIM) // ATT_KV_HEADS
    dec_lora = a_w2.shape[1]
    a_lora = a_a2.shape[1]
    g_lora = a_g2.shape[1]
    rw_cols = 3 * d + dec_lora + a_lora + g_lora
    k_sel_p = min(TOPK_MAX, t // 4)
    k_sel_s = min(TOPK_MAX, (past + s_len) // 4)
    assert g_lora == 2 * LANE and dec_lora <= LANE and a_lora <= LANE

    xp = x_prompt.reshape(bp * t, d)
    xs = x_sample.reshape(bd * s_len, d)

    rw_segments = [(0, 3 * d, 3 * d), (3 * d, dec_lora, LANE), (3 * d + dec_lora, a_lora, LANE),
                   (3 * d + dec_lora + a_lora, g_lora, g_lora)]
    rw_padded = 3 * d + 2 * LANE + g_lora
    o_qi = q_cols + 2 * kv_cols
    o_ki = o_qi + IDX_HEADS * IDX_DIM
    c_cols = o_ki + IDX_DIM + IDX_HEADS

    outs = {k: [] for k in ("p_rw_wkv", "p_rw_sh", "p_sw_k", "p_sw_v", "p_ds_k", "p_ds_v", "p_ds_i", "p_mk",
                            "p_mv", "p_cv", "s_rw_wkv", "s_rw_sh", "s_sw_k", "s_sw_v", "s_ds_k", "s_ds_v",
                            "s_ds_i", "s_cv")}

    def unpad_rw(row):
        return jnp.concatenate([row[..., :3 * d], row[..., 3 * d:3 * d + dec_lora],
                                row[..., 3 * d + LANE:3 * d + LANE + a_lora],
                                row[..., 3 * d + 2 * LANE:3 * d + 2 * LANE + g_lora]], axis=-1)

    for i in range(depth):
        kind, j = i % 3, i // 3
        if kind == 0:
            w_in = jnp.concatenate([_pad_cols(a_w_in[j], rw_segments), a_w_in[j][:, rw_cols:]], axis=1).astype(BF16)
            memq_cb = rw_padded // memq_cols
            zp = _mm_norm(xp, attn_norm[i], w_in).reshape(bp, t, -1)
            zs = _mm_norm(xs, attn_norm[i], w_in).reshape(bd, s_len, -1)
            mu = _pad_cols(a_mu[j].reshape(1, -1), rw_segments)
            mu = jnp.concatenate([mu, jnp.zeros((1, memq_cols), F32)], axis=1)
            vecs = jnp.stack([a_w0[j], a_a0[j], a_k_k[j], a_k_a[j], a_r_k[j].reshape(-1), a_ln_w[j], a_ln_b[j],
                              jnp.zeros((d,), F32)], axis=0)
            w2 = _pad_rows(a_w2[j], LANE).astype(BF16)
            a2 = _pad_rows(a_a2[j], LANE).astype(BF16)
            g2 = a_g2[j].astype(BF16)
            sh_p = jnp.zeros((bp, 1, zp.shape[-1]), F32)
            st_p = jnp.zeros((bp, d // LANE, LANE, LANE), F32)
            sh_s = _pad_cols(state_rwkv_shift[j], rw_segments)
            sh_s = jnp.concatenate([sh_s, jnp.zeros((bd, memq_cols), F32)], axis=1).reshape(bd, 1, -1)
            st_s = _pairs_from_heads(state_rwkv_wkv[j])
            mp, stp = _rwkv(zp, sh_p, st_p, mu, vecs, w2, a2, g2, d)
            ms, sts = _rwkv(zs, sh_s, st_s, mu, vecs, w2, a2, g2, d)
            outs["p_rw_sh"].append(unpad_rw(zp[:, -1]))
            outs["p_rw_wkv"].append(_heads_from_pairs(stp))
            outs["s_rw_sh"].append(unpad_rw(zs[:, -1]))
            outs["s_rw_wkv"].append(_heads_from_pairs(sts))
            w_out = a_w_out[j]
        elif kind == 1:
            w_in = b_w_in[j].astype(BF16)
            memq_cb = (q_cols + 2 * kv_cols) // memq_cols
            k_cb, v_cb = q_cols // kv_cols, q_cols // kv_cols + 1
            zp = _mm_norm(xp, attn_norm[i], w_in).reshape(bp, t, -1)
            zs = _mm_norm(xs, attn_norm[i], w_in).reshape(bd, s_len, -1)
            kgain = _tile_gain(b_k_norm[j], kv_cols)
            qgain = _tile_gain(b_q_norm[j], LANE)
            sink = jnp.repeat(b_sink[j].astype(F32), HEAD_DIM).reshape(1, q_cols)
            knp = _headnorm(zp.reshape(bp * t, -1), k_cb, kv_cols, kgain).reshape(bp, t, kv_cols)
            mp = _flash(zp, 0, q_cols, knp, 0, zp, v_cb, qgain, sink, mode="band", tq=WINDOW, kb=WINDOW,
                        group=att_group)
            outs["p_sw_k"].append(knp[:, t - win_rows:].reshape(bp, win_rows, ATT_KV_HEADS, HEAD_DIM))
            outs["p_sw_v"].append(zp[:, t - win_rows:, q_cols + kv_cols:q_cols + 2 * kv_cols]
                                  .reshape(bp, win_rows, ATT_KV_HEADS, HEAD_DIM))
            kns = _headnorm(zs.reshape(bd * s_len, -1), k_cb, kv_cols, kgain).reshape(bd, s_len, kv_cols)
            vs_new = zs[:, :, q_cols + kv_cols:q_cols + 2 * kv_cols]
            k_all = jnp.concatenate([cache_swa_k[j].reshape(bd, win_rows, kv_cols), kns], axis=1)
            v_all = jnp.concatenate([cache_swa_v[j].reshape(bd, win_rows, kv_cols), vs_new], axis=1)
            n_keys = win_rows + s_len
            n_pad = -(-n_keys // LANE) * LANE
            pad = jnp.zeros((bd, n_pad - n_keys, kv_cols), F32)
            ms = _flash(zs, 0, q_cols, jnp.concatenate([k_all, pad], axis=1), 0,
                        jnp.concatenate([v_all, pad], axis=1), 0, qgain, sink, mode="all", tq=s_len, kb=n_pad,
                        group=att_group, n_valid=n_keys)
            outs["s_sw_k"].append(k_all[:, n_keys - win_rows:].reshape(bd, win_rows, ATT_KV_HEADS, HEAD_DIM))
            outs["s_sw_v"].append(v_all[:, n_keys - win_rows:].reshape(bd, win_rows, ATT_KV_HEADS, HEAD_DIM))
            w_out = b_w_out[j]
        else:
            wc = c_w_in[j]
            w_in = jnp.concatenate([wc[:, :o_ki], wc[:, c_cols:],
                                    _pad_cols(wc, [(o_ki, IDX_DIM + IDX_HEADS, LANE)])], axis=1).astype(BF16)
            memq_cb = o_ki // memq_cols
            kw_cb = (o_ki + memq_cols) // LANE
            k_cb, v_cb = q_cols // kv_cols, q_cols // kv_cols + 1
            qi_cb = o_qi // (IDX_HEADS * IDX_DIM)
            zp = _mm_norm(xp, attn_norm[i], w_in).reshape(bp, t, -1)
            zs = _mm_norm(xs, attn_norm[i], w_in).reshape(bd, s_len, -1)
            kgain = _tile_gain(c_k_norm[j], kv_cols)
            qgain = _tile_gain(c_q_norm[j], LANE)
            igain = jnp.concatenate([c_idx_k_norm[j].astype(F32), jnp.zeros((LANE - IDX_DIM,), F32)]).reshape(1, LANE)
            no_sink = jnp.full((1, q_cols), -jnp.inf, F32)
            knp = _headnorm(zp.reshape(bp * t, -1), k_cb, kv_cols, kgain).reshape(bp, t, kv_cols)
            kip = _headnorm(zp.reshape(bp * t, -1), kw_cb, LANE, igain).reshape(bp, t, LANE)
            tq = min(t, DSA_TQ)
            kb = min(t, DSA_KB)
            sc, tau, cut = _dsa_select(zp, qi_cb, kw_cb, kip, tq=tq, kb=kb, k_sel=k_sel_p, n_valid=t, causal=True)
            mp = _flash(zp, 0, q_cols, knp, 0, zp, v_cb, qgain, no_sink, mode="dsa", tq=tq, kb=kb, group=att_group,
                        n_valid=t, causal=True, scores=sc, tau=tau, jb=cut)
            outs["p_ds_k"].append(knp.reshape(bp, t, ATT_KV_HEADS, HEAD_DIM))
            outs["p_ds_v"].append(zp[:, :, q_cols + kv_cols:q_cols + 2 * kv_cols].reshape(bp, t, ATT_KV_HEADS, HEAD_DIM))
            outs["p_ds_i"].append(kip[:, :, :IDX_DIM])
            kns = _headnorm(zs.reshape(bd * s_len, -1), k_cb, kv_cols, kgain).reshape(bd, s_len, kv_cols)
            kis = _headnorm(zs.reshape(bd * s_len, -1), kw_cb, LANE, igain).reshape(bd, s_len, LANE)
            vs_new = zs[:, :, q_cols + kv_cols:q_cols + 2 * kv_cols]
            n_keys = past + s_len
            n_pad = -(-n_keys // LANE) * LANE
            zpad = lambda w: jnp.zeros((bd, n_pad - n_keys, w), F32)
            k_all = jnp.concatenate([cache_dsa_k[j].reshape(bd, past, kv_cols), kns, zpad(kv_cols)], axis=1)
            v_all = jnp.concatenate([cache_dsa_v[j].reshape(bd, past, kv_cols), vs_new, zpad(kv_cols)], axis=1)
            ki_cache = jnp.concatenate([cache_dsa_idx_k[j], jnp.zeros((bd, past, LANE - IDX_DIM), F32)], axis=-1)
            ki_all = jnp.concatenate([ki_cache, kis, zpad(LANE)], axis=1)
            sc, tau, cut = _dsa_select(zs, qi_cb, kw_cb, ki_all, tq=s_len, kb=n_pad, k_sel=k_sel_s, n_valid=n_keys,
                                       causal=False)
            ms = _flash(zs, 0, q_cols, k_all, 0, v_all, 0, qgain, no_sink, mode="dsa", tq=s_len, kb=n_pad,
                        group=att_group, n_valid=n_keys, causal=False, scores=sc, tau=tau, jb=cut)
            outs["s_ds_k"].append(kns.reshape(bd, s_len, ATT_KV_HEADS, HEAD_DIM))
            outs["s_ds_v"].append(vs_new.reshape(bd, s_len, ATT_KV_HEADS, HEAD_DIM))
            outs["s_ds_i"].append(kis[:, :, :IDX_DIM])
            w_out = c_w_out[j]

        kv_mem = _mm_norm(mem_prompt.reshape(bp * mem_tokens, d), mem_norm[i], mem_w_kv[i].astype(BF16))
        km_p = _headnorm(kv_mem, 0, memq_cols, _tile_gain(mem_k_norm[i], memq_cols)).reshape(bp, mem_tokens, memq_cols)
        vm_p = kv_mem[:, memq_cols:].reshape(bp, mem_tokens, memq_cols)
        outs["p_mk"].append(km_p.reshape(bp, mem_tokens, MEM_HEADS, HEAD_DIM))
        outs["p_mv"].append(vm_p.reshape(bp, mem_tokens, MEM_HEADS, HEAD_DIM))
        mo_p = _mem_attend(zp, memq_cb, km_p, vm_p, mem_q_norm[i])
        mo_s = _mem_attend(zs, memq_cb, cache_mem_k[i].reshape(bd, mem_tokens, memq_cols),
                           cache_mem_v[i].reshape(bd, mem_tokens, memq_cols), mem_q_norm[i])
        w_mix, w_mem = w_out[:d].astype(BF16), w_out[d:].astype(BF16)
        xp = _mm_res([mp.reshape(bp * t, d), mo_p.reshape(bp * t, memq_cols)], [w_mix, w_mem], xp)
        xs = _mm_res([ms.reshape(bd * s_len, d), mo_s.reshape(bd * s_len, memq_cols)], [w_mix, w_mem], xs)

        w_up, w_down = ffn_w_up[i].astype(BF16), ffn_w_down[i].astype(BF16)
        xp, cp = _conv_ffn(xp, bp, t, ffn_norm[i], w_up, ffn_conv[i], w_down,
                           jnp.zeros((bp, CONV_W - 1, 2 * d_ff), F32))
        xs, cs = _conv_ffn(xs, bd, s_len, ffn_norm[i], w_up, ffn_conv[i], w_down, state_ffn_conv[i])
        outs["p_cv"].append(cp)
        outs["s_cv"].append(cs)

    st = jnp.stack
    order = ("p_rw_wkv", "p_rw_sh", "p_sw_k", "p_sw_v", "p_ds_k", "p_ds_v", "p_ds_i", "p_mk", "p_mv", "p_cv",
             "s_rw_wkv", "s_rw_sh", "s_sw_k", "s_sw_v", "s_ds_k", "s_ds_v", "s_ds_i", "s_cv")
    return (xp.reshape(bp, t, d), xs.reshape(bd, s_len, d)) + tuple(st(outs[k]) for k in order)
```

```python
import functools

import jax
import jax.numpy as jnp
from jax import lax
from jax.experimental import pallas as pl
from jax.experimental.pallas import tpu as pltpu

F32 = jnp.float32
BF16 = jnp.bfloat16

HEAD_DIM = 64
CHUNK = 64
NORM_EPS = 1e-6
RW_GN_EPS = HEAD_DIM * 1e-5
ATT_KV_HEADS = 4
WINDOW = 128
IDX_HEADS = 8
IDX_DIM = 64
TOPK_MAX = 256
MEM_HEADS = 4
CONV_W = 3

LANE = 128
VMEM_LIMIT = 52 * 1024 * 1024
NEG_BIG = -(2.0 ** 100)
INT_MIN = -2147483648
MM_ROWS = 1024
DSA_TQ = 256
DSA_KB = 512


def _cparams(sem, vmem=VMEM_LIMIT):
    return pltpu.CompilerParams(dimension_semantics=sem, vmem_limit_bytes=vmem)


def _split3(a):
    a1 = a.astype(BF16)
    r1 = a - a1.astype(F32)
    a2 = r1.astype(BF16)
    r2 = r1 - a2.astype(F32)
    return a1, a2, r2.astype(BF16)


def _dot_exact_rhs(a, e):
    a1, a2, a3 = _split3(a)
    d = lambda x: jnp.dot(x, e, preferred_element_type=F32)
    return (d(a3) + d(a2)) + d(a1)


def _dot_exact_lhs(e, a):
    a1, a2, a3 = _split3(a)
    d = lambda x: jnp.dot(e, x, preferred_element_type=F32)
    return (d(a3) + d(a2)) + d(a1)


def _head_block_matrix(width, value):
    r = lax.broadcasted_iota(jnp.int32, (width, width), 0) // HEAD_DIM
    c = lax.broadcasted_iota(jnp.int32, (width, width), 1) // HEAD_DIM
    return jnp.where(r == c, value, 0.0).astype(BF16)


def _dot_rw(a, b, dims):
    return lax.dot_general(a.astype(BF16), b.astype(BF16), (dims, ((), ())), preferred_element_type=F32)


def _pick_tile(n, cap):
    best = None
    for t in range(LANE, min(n, cap) + 1, LANE):
        if n % t == 0:
            best = t
    assert best is not None, n
    return best


def _mm_norm_body(x_ref, g_ref, w_ref, o_ref, xn_ref):
    @pl.when(pl.program_id(1) == 0)
    def _():
        x = x_ref[...]
        ms = jnp.mean(x * x, axis=-1, keepdims=True)
        xn_ref[...] = ((x * lax.rsqrt(ms + NORM_EPS)) * g_ref[...]).astype(BF16)

    o_ref[...] = jnp.dot(xn_ref[...], w_ref[...], preferred_element_type=F32)


def _mm_norm(x, gain, w):
    m, k = x.shape
    n = w.shape[1]
    tm = min(m, MM_ROWS)
    tn = _pick_tile(n, 1536)
    return pl.pallas_call(
        _mm_norm_body,
        out_shape=jax.ShapeDtypeStruct((m, n), F32),
        grid=(m // tm, n // tn),
        in_specs=[pl.BlockSpec((tm, k), lambda i, j: (i, 0)),
                  pl.BlockSpec((1, k), lambda i, j: (0, 0)),
                  pl.BlockSpec((k, tn), lambda i, j: (0, j))],
        out_specs=pl.BlockSpec((tm, tn), lambda i, j: (i, j)),
        scratch_shapes=[pltpu.VMEM((tm, k), BF16)],
        compiler_params=_cparams(("parallel", "arbitrary")),
        name="mm_norm",
    )(x, gain.reshape(1, k), w)


def _mm_res_body(*refs, n_lhs):
    lhs = refs[:n_lhs]
    ws = refs[n_lhs:2 * n_lhs]
    r_ref, o_ref = refs[2 * n_lhs], refs[2 * n_lhs + 1]
    acc = jnp.dot(lhs[0][...], ws[0][...], preferred_element_type=F32)
    for a, w in zip(lhs[1:], ws[1:]):
        acc = acc + jnp.dot(a[...], w[...], preferred_element_type=F32)
    o_ref[...] = r_ref[...] + acc


def _mm_res(lhs_list, w_list, res):
    m, n = res.shape
    ktot = sum(a.shape[1] for a in lhs_list)
    tm = min(m, MM_ROWS)
    tn = _pick_tile(n, 1024 if ktot <= 3072 else 512)
    n_lhs = len(lhs_list)
    in_specs = [pl.BlockSpec((tm, a.shape[1]), lambda i, j: (i, 0)) for a in lhs_list]
    in_specs += [pl.BlockSpec((w.shape[0], tn), lambda i, j: (0, j)) for w in w_list]
    in_specs += [pl.BlockSpec((tm, tn), lambda i, j: (i, j))]
    return pl.pallas_call(
        functools.partial(_mm_res_body, n_lhs=n_lhs),
        out_shape=jax.ShapeDtypeStruct((m, n), F32),
        grid=(m // tm, n // tn),
        in_specs=in_specs,
        out_specs=pl.BlockSpec((tm, tn), lambda i, j: (i, j)),
        compiler_params=_cparams(("parallel", "arbitrary")),
        name="mm_res",
    )(*lhs_list, *w_list, res)


def _headnorm_body(x_ref, g_ref, o_ref, *, width):
    avg = _head_block_matrix(LANE, 1.0 / HEAD_DIM)
    for c in range(width // LANE):
        x = x_ref[:, c * LANE:(c + 1) * LANE]
        ms = _dot_exact_rhs(x * x, avg)
        o_ref[:, c * LANE:(c + 1) * LANE] = (x * lax.rsqrt(ms + NORM_EPS)) * g_ref[:, c * LANE:(c + 1) * LANE]


def _headnorm(x, col_block, width, gain_row):
    m = x.shape[0]
    tm = min(m, 1024)
    return pl.pallas_call(
        functools.partial(_headnorm_body, width=width),
        out_shape=jax.ShapeDtypeStruct((m, width), F32),
        grid=(m // tm,),
        in_specs=[pl.BlockSpec((tm, width), lambda i: (i, col_block)),
                  pl.BlockSpec((1, width), lambda i: (0, 0))],
        out_specs=pl.BlockSpec((tm, width), lambda i: (i, 0)),
        compiler_params=_cparams(("parallel",)),
        name="headnorm",
    )(x, gain_row)


def _up_conv_body(x_ref, g_ref, wa_ref, wb_ref, pa_ref, pb_ref, cwa_ref, cwb_ref, o_ref, la_ref, lb_ref,
                  xn_ref, ca_ref, cb_ref, *, tm, tiles_per_batch, bpt):
    i = pl.program_id(0)
    j = pl.program_id(1)
    rpb = tm // bpt

    @pl.when(j == 0)
    def _():
        x = x_ref[...]
        ms = jnp.mean(x * x, axis=-1, keepdims=True)
        xn_ref[...] = ((x * lax.rsqrt(ms + NORM_EPS)) * g_ref[...]).astype(BF16)

    if bpt == 1:
        @pl.when(i % tiles_per_batch == 0)
        def _():
            ca_ref[j] = pa_ref[0]
            cb_ref[j] = pb_ref[0]

    xn = xn_ref[...]
    row = lax.broadcasted_iota(jnp.int32, o_ref.shape, 0)
    off = row if bpt == 1 else row % rpb
    if bpt > 1:
        pick = (lax.broadcasted_iota(jnp.int32, (tm, bpt), 0) // rpb
                == lax.broadcasted_iota(jnp.int32, (tm, bpt), 1)).astype(BF16)

    def conv(w_ref, p_ref, c_ref, cw_ref, last_ref):
        u = jnp.dot(xn, w_ref[...], preferred_element_type=F32)
        if bpt == 1:
            car = c_ref[j]
            c0, c1 = car[0:1, :], car[1:2, :]
            c_ref[j] = u[tm - 2:tm, :]
            last_ref[0] = u[tm - 2:tm, :]
        else:
            c0 = _dot_exact_lhs(pick, p_ref[:, 0, :])
            c1 = _dot_exact_lhs(pick, p_ref[:, 1, :])
            for bi in range(bpt):
                last_ref[bi] = u[(bi + 1) * rpb - 2:(bi + 1) * rpb, :]
        u1 = jnp.where(off == 0, c1, pltpu.roll(u, 1, axis=0))
        u2 = jnp.where(off == 0, c0, jnp.where(off == 1, c1, pltpu.roll(u, 2, axis=0)))
        return (u2 * cw_ref[0:1, :] + u1 * cw_ref[1:2, :]) + u * cw_ref[2:3, :]

    a = conv(wa_ref, pa_ref, ca_ref, cwa_ref, la_ref)
    b = conv(wb_ref, pb_ref, cb_ref, cwb_ref, lb_ref)
    o_ref[...] = ((a * jax.nn.sigmoid(a)) * b).astype(o_ref.dtype)


def _up_conv(x, gain, w_up, conv_w, prev, t):
    m, k = x.shape
    f = w_up.shape[1] // 2
    b = m // t
    tm = min(m, MM_ROWS)
    bpt = max(1, tm // t)
    assert tm % t == 0 or t % tm == 0
    tn = _pick_tile(f, 512)
    nj = f // tn
    tpb = max(1, t // tm)
    act, la, lb = pl.pallas_call(
        functools.partial(_up_conv_body, tm=tm, tiles_per_batch=tpb, bpt=bpt),
        out_shape=(jax.ShapeDtypeStruct((m, f), BF16),
                   jax.ShapeDtypeStruct((b, CONV_W - 1, f), F32),
                   jax.ShapeDtypeStruct((b, CONV_W - 1, f), F32)),
        grid=(m // tm, nj),
        in_specs=[pl.BlockSpec((tm, k), lambda i, j: (i, 0)),
                  pl.BlockSpec((1, k), lambda i, j: (0, 0)),
                  pl.BlockSpec((k, tn), lambda i, j: (0, j)),
                  pl.BlockSpec((k, tn), lambda i, j: (0, nj + j)),
                  pl.BlockSpec((bpt, CONV_W - 1, tn), lambda i, j: (i // tpb, 0, j)),
                  pl.BlockSpec((bpt, CONV_W - 1, tn), lambda i, j: (i // tpb, 0, nj + j)),
                  pl.BlockSpec((CONV_W, tn), lambda i, j: (0, j)),
                  pl.BlockSpec((CONV_W, tn), lambda i, j: (0, nj + j))],
        out_specs=(pl.BlockSpec((tm, tn), lambda i, j: (i, j)),
                   pl.BlockSpec((bpt, CONV_W - 1, tn), lambda i, j: (i // tpb, 0, j)),
                   pl.BlockSpec((bpt, CONV_W - 1, tn), lambda i, j: (i // tpb, 0, j))),
        scratch_shapes=[pltpu.VMEM((tm, k), BF16),
                        pltpu.VMEM((nj, CONV_W - 1, tn), F32),
                        pltpu.VMEM((nj, CONV_W - 1, tn), F32)],
        compiler_params=_cparams(("arbitrary", "arbitrary")),
        name="up_conv_gate",
    )(x, gain.reshape(1, k), w_up, w_up, prev, prev, conv_w, conv_w)
    return act, jnp.concatenate([la, lb], axis=-1)


def _flash_body(*refs, mode, tq, kb, wq, group, n_valid, nk, causal):
    q_ref, k_ref, v_ref, qg_ref, sink_ref = refs[:5]
    pos = 5
    if mode == "dsa":
        sc_ref, tau_ref, jb_ref = refs[pos:pos + 3]
        pos += 3
    o_ref = refs[pos]
    qn_scr, acc_scr, m_scr, l_scr, bias_scr = refs[pos + 1:]

    i = pl.program_id(1)
    j = pl.program_id(2)
    npairs = wq // LANE
    pairs_per_unit = group // 2 if group > 1 else 1
    lane = lax.broadcasted_iota(jnp.int32, (1, LANE), 1)
    lo_half = lane < HEAD_DIM

    @pl.when(j == 0)
    def _():
        avg = _head_block_matrix(LANE, 1.0 / HEAD_DIM)
        for p in range(npairs):
            x = q_ref[0, :, p * LANE:(p + 1) * LANE]
            ms = _dot_exact_rhs(x * x, avg)
            qn = ((x * lax.rsqrt(ms + NORM_EPS)) * qg_ref[...]) * (HEAD_DIM ** -0.5)
            qn_scr[p] = qn.astype(BF16)
        acc_scr[...] = jnp.zeros(acc_scr.shape, F32)
        l_scr[...] = jnp.zeros(l_scr.shape, F32)
        m_scr[...] = jnp.full(m_scr.shape, NEG_BIG, F32)

    if mode == "band":
        kblk = i - WINDOW // kb + j
        active = kblk >= 0
    elif mode == "dsa" and causal:
        kblk = j
        active = j * kb < (i + 1) * tq
    else:
        kblk = j
        active = None

    def step():
        kidx = kblk * kb + lax.broadcasted_iota(jnp.int32, (tq, kb), 1)
        qpos = i * tq + lax.broadcasted_iota(jnp.int32, (tq, kb), 0)
        if mode == "band":
            qchunk = qpos // CHUNK
            sel = (kidx >= (qchunk - WINDOW // CHUNK) * CHUNK) & (kidx < (qchunk + 1) * CHUNK) & (kidx >= 0)
        elif mode == "dsa":
            x = sc_ref[0]
            tau = tau_ref[0][:, 0:1]
            jb = jb_ref[0][:, 0:1]
            adm = kidx < ((qpos // CHUNK + 1) * CHUNK if causal else n_valid)
            sel = adm & ((x > tau) | ((x == tau) & (kidx <= jb)))
        else:
            sel = kidx < n_valid
        bias_scr[...] = jnp.where(sel, 0.0, NEG_BIG).astype(BF16)

        for p in range(npairs):
            u = p // pairs_per_unit
            s = lax.dot_general(qn_scr[p], k_ref[0, 0, u], (((1,), (1,)), ((), ())),
                                preferred_element_type=F32)
            bias = bias_scr[...]
            m_old = m_scr[p]
            s0 = s[:, 0:kb].astype(BF16) + bias
            s1 = s[:, kb:2 * kb].astype(BF16) + bias
            mn0 = jnp.maximum(m_old[:, 0:1], jnp.max(s0, axis=1, keepdims=True).astype(F32))
            mn1 = jnp.maximum(m_old[:, HEAD_DIM:HEAD_DIM + 1], jnp.max(s1, axis=1, keepdims=True).astype(F32))
            p0 = jnp.exp(s0 - mn0.astype(BF16))
            p1 = jnp.exp(s1 - mn1.astype(BF16))
            pcat = jnp.concatenate([p0, p1], axis=1)
            pv = jnp.dot(pcat, v_ref[0, 0, u], preferred_element_type=F32)
            mn = jnp.where(lo_half, mn0, mn1)
            alpha = jnp.exp(m_old - mn)
            acc_scr[p] = acc_scr[p] * alpha + pv[:, 0:LANE]
            l_scr[p] = l_scr[p] * alpha + pv[:, LANE:2 * LANE]
            m_scr[p] = mn

    if active is None:
        step()
    else:
        pl.when(active)(step)

    @pl.when(j == nk - 1)
    def _():
        for p in range(npairs):
            den = l_scr[p] + jnp.exp(sink_ref[:, p * LANE:(p + 1) * LANE] - m_scr[p])
            o_ref[0, :, p * LANE:(p + 1) * LANE] = (acc_scr[p] / den).astype(o_ref.dtype)


def _pair_operands_body(k_ref, v_ref, kab_ref, vab_ref, *, kb, group, n_units):
    lane = lax.broadcasted_iota(jnp.int32, (1, LANE), 1)
    lo_half = lane < HEAD_DIM
    ones_lo = jnp.broadcast_to(jnp.where(lo_half, 1.0, 0.0), (kb, LANE)).astype(BF16)
    ones_hi = jnp.broadcast_to(jnp.where(lo_half, 0.0, 1.0), (kb, LANE)).astype(BF16)
    for u in range(n_units):
        if group > 1:
            tile_idx, half = u // 2, u % 2
        else:
            tile_idx, half = u, None
        for src, dst in ((k_ref, kab_ref), (v_ref, vab_ref)):
            tile = src[0, :, tile_idx * LANE:(tile_idx + 1) * LANE]
            if half is None:
                a_part = jnp.where(lo_half, tile, 0.0)
                b_part = jnp.where(lo_half, 0.0, tile)
            elif half == 0:
                a_part = jnp.where(lo_half, tile, 0.0)
                b_part = pltpu.roll(a_part, HEAD_DIM, axis=1)
            else:
                b_part = jnp.where(lo_half, 0.0, tile)
                a_part = pltpu.roll(b_part, HEAD_DIM, axis=1)
            dst[0, 0, u, 0:kb, 0:LANE] = a_part.astype(BF16)
            dst[0, 0, u, kb:2 * kb, 0:LANE] = b_part.astype(BF16)
        vab_ref[0, 0, u, 0:kb, LANE:2 * LANE] = ones_lo
        vab_ref[0, 0, u, kb:2 * kb, LANE:2 * LANE] = ones_hi


def _pair_operands(k_arr, k_cb, v_arr, v_cb, kb, group, npairs):
    b, t_k = k_arr.shape[0], k_arr.shape[1]
    wk = ATT_KV_HEADS * HEAD_DIM
    n_units = npairs // (group // 2) if group > 1 else npairs
    nkb = t_k // kb
    return pl.pallas_call(
        functools.partial(_pair_operands_body, kb=kb, group=group, n_units=n_units),
        out_shape=(jax.ShapeDtypeStruct((b, nkb, n_units, 2 * kb, LANE), BF16),
                   jax.ShapeDtypeStruct((b, nkb, n_units, 2 * kb, 2 * LANE), BF16)),
        grid=(b, nkb),
        in_specs=[pl.BlockSpec((1, kb, wk), lambda bi, j: (bi, j, k_cb)),
                  pl.BlockSpec((1, kb, wk), lambda bi, j: (bi, j, v_cb))],
        out_specs=(pl.BlockSpec((1, 1, n_units, 2 * kb, LANE), lambda bi, j: (bi, j, 0, 0, 0)),
                   pl.BlockSpec((1, 1, n_units, 2 * kb, 2 * LANE), lambda bi, j: (bi, j, 0, 0, 0))),
        compiler_params=_cparams(("parallel", "parallel")),
        name="pair_operands",
    )(k_arr, v_arr)


def _flash(q_arr, q_cb, wq, k_arr, k_cb, v_arr, v_cb, q_gain, sink_row, *, mode, tq, kb, group,
           n_valid=None, causal=False, scores=None, tau=None, jb=None):
    b, t_q = q_arr.shape[0], q_arr.shape[1]
    t_k = k_arr.shape[1]
    nq = t_q // tq
    if mode == "band":
        assert tq == kb == WINDOW
        nk = 2
        kmap = lambda i, j: jnp.maximum(i - 1 + j, 0)
    elif mode == "dsa" and causal:
        nk = t_k // kb
        kmap = lambda i, j: jnp.minimum(j, ((i + 1) * tq - 1) // kb)
    else:
        nk = t_k // kb
        kmap = lambda i, j: j
    npairs = wq // LANE
    kab, vab = _pair_operands(k_arr, k_cb, v_arr, v_cb, kb, group, npairs)
    n_units = kab.shape[2]
    in_specs = [pl.BlockSpec((1, tq, wq), lambda bi, i, j: (bi, i, q_cb)),
                pl.BlockSpec((1, 1, n_units, 2 * kb, LANE), lambda bi, i, j: (bi, kmap(i, j), 0, 0, 0)),
                pl.BlockSpec((1, 1, n_units, 2 * kb, 2 * LANE), lambda bi, i, j: (bi, kmap(i, j), 0, 0, 0)),
                pl.BlockSpec((1, LANE), lambda bi, i, j: (0, 0)),
                pl.BlockSpec((1, wq), lambda bi, i, j: (0, 0))]
    args = [q_arr, kab, vab, q_gain, sink_row]
    if mode == "dsa":
        in_specs += [pl.BlockSpec((1, tq, kb), lambda bi, i, j: (bi, i, kmap(i, j))),
                     pl.BlockSpec((1, tq, LANE), lambda bi, i, j: (bi, i, 0)),
                     pl.BlockSpec((1, tq, LANE), lambda bi, i, j: (bi, i, 0))]
        args += [scores, tau, jb]
    body = functools.partial(_flash_body, mode=mode, tq=tq, kb=kb, wq=wq, group=group,
                             n_valid=n_valid, nk=nk, causal=causal)
    return pl.pallas_call(
        body,
        out_shape=jax.ShapeDtypeStruct((b, t_q, wq), BF16),
        grid=(b, nq, nk),
        in_specs=in_specs,
        out_specs=pl.BlockSpec((1, tq, wq), lambda bi, i, j: (bi, i, 0)),
        scratch_shapes=[pltpu.VMEM((npairs, tq, LANE), BF16),
                        pltpu.VMEM((npairs, tq, LANE), F32),
                        pltpu.VMEM((npairs, tq, LANE), F32),
                        pltpu.VMEM((npairs, tq, LANE), F32),
                        pltpu.VMEM((tq, kb), BF16)],
        compiler_params=_cparams(("parallel", "parallel", "arbitrary")),
        name="flash_" + mode,
    )(*args)


def _dsa_select_body(qi_ref, kw_ref, ki_ref, sc_ref, tau_ref, jb_ref, sc_scr, *, tq, kb, nk, k_sel,
                     n_valid, causal, idx_bits):
    i = pl.program_id(1)
    j = pl.program_id(2)
    qpos = i * tq + lax.broadcasted_iota(jnp.int32, (tq, 1), 0)
    if causal:
        n_adm = (qpos // CHUNK + 1) * CHUNK
        active = j * kb < (i + 1) * tq
        n_blocks = ((i + 1) * tq + kb - 1) // kb
    else:
        n_adm = jnp.full((tq, 1), n_valid, jnp.int32)
        active = None
        n_blocks = nk

    def compute():
        ka = ki_ref[0]
        kab = jnp.concatenate([ka, pltpu.roll(ka, HEAD_DIM, axis=1)], axis=0).astype(BF16)
        kw = kw_ref[0]
        acc = jnp.zeros((tq, kb), F32)
        for p in range(IDX_HEADS // 2):
            qp = qi_ref[0, :, p * LANE:(p + 1) * LANE].astype(BF16)
            s = lax.dot_general(qp, kab, (((1,), (1,)), ((), ())), preferred_element_type=F32)
            for h in range(2):
                col = HEAD_DIM + 2 * p + h
                w = (kw[:, col:col + 1] * IDX_HEADS ** -0.5) * IDX_DIM ** -0.5
                acc = acc + jnp.maximum(s[:, h * kb:(h + 1) * kb], 0.0) * w
        acc = jnp.where(acc == 0.0, 0.0, acc)
        kidx = j * kb + lax.broadcasted_iota(jnp.int32, (tq, kb), 1)
        x = jnp.where(kidx < n_adm, acc, -jnp.inf)
        sc_scr[j] = x
        sc_ref[0] = x

    if active is None:
        compute()
    else:
        pl.when(active)(compute)

        @pl.when(jnp.logical_not(active))
        def _():
            sc_ref[0] = jnp.full((tq, kb), -jnp.inf, F32)

    @pl.when(j == nk - 1)
    def _():
        kf = float(k_sel)

        rsz = min(tq, LANE)

        def count(make_pred):
            outs = []
            for r0 in range(0, tq, rsz):
                pred = make_pred(lambda col, r0=r0: jnp.broadcast_to(col[r0:r0 + rsz], (rsz, LANE)))

                def blk(jj, acc, r0=r0, pred=pred):
                    for a in range(kb // LANE):
                        x = sc_scr[jj, r0:r0 + rsz, a * LANE:(a + 1) * LANE]
                        acc = acc + jnp.where(pred(x, jj * kb + a * LANE), 1.0, 0.0)
                    return acc
                acc = lax.fori_loop(0, n_blocks, blk, jnp.zeros((rsz, LANE), F32))
                outs.append(jnp.sum(acc, axis=1, keepdims=True))
            return jnp.concatenate(outs, axis=0)

        def key_to_float(tu):
            cs = tu ^ INT_MIN
            fb = jnp.where(cs >= 0, cs, cs ^ 0x7FFFFFFF)
            return lax.bitcast_convert_type(fb, F32)

        def bit_body(it, tu):
            cand = tu | jnp.left_shift(jnp.int32(1), 31 - it)
            thr = key_to_float(cand)
            c = count(lambda widen: (lambda x, base, t=widen(thr): x >= t))
            return jnp.where(c >= kf, cand, tu)

        tu = lax.fori_loop(0, 32, bit_body, jnp.zeros((tq, 1), jnp.int32))
        full = n_adm >= k_sel
        tau = jnp.where(full, key_to_float(tu), -jnp.inf)
        c_gt = count(lambda widen: (lambda x, base, t=widen(tau): x > t))
        c_ge = count(lambda widen: (lambda x, base, t=widen(tau): x >= t))
        need = kf - c_gt
        tau_ref[0] = jnp.broadcast_to(tau, (tq, LANE))
        jb_ref[0] = jnp.full((tq, LANE), 2 ** 30, jnp.int32)
        surplus = jnp.max(jnp.where(full, (c_ge - c_gt) - need, 0.0))

        @pl.when(surplus > 0.5)
        def _():
            def tie_body(it, cut):
                cand = cut | jnp.left_shift(jnp.int32(1), idx_bits - 1 - it)

                def make_pred(widen):
                    t, cnd = widen(tau), widen(cand)
                    lane_idx = lax.broadcasted_iota(jnp.int32, (rsz, LANE), 1)
                    return lambda x, base: (x == t) & (base + lane_idx < cnd)
                c = count(make_pred)
                return jnp.where(c < need, cand, cut)

            cut = lax.fori_loop(0, idx_bits, tie_body, jnp.zeros((tq, 1), jnp.int32))
            jb_ref[0] = jnp.broadcast_to(cut, (tq, LANE))


def _dsa_select(z, qi_cb, kw_cb, ki_n, *, tq, kb, k_sel, n_valid, causal):
    b, t_q = z.shape[0], z.shape[1]
    t_k = ki_n.shape[1]
    nq, nk = t_q // tq, t_k // kb
    kmap = (lambda i, j: jnp.minimum(j, ((i + 1) * tq - 1) // kb)) if causal else (lambda i, j: j)
    body = functools.partial(_dsa_select_body, tq=tq, kb=kb, nk=nk, k_sel=k_sel, n_valid=n_valid,
                             causal=causal, idx_bits=max(1, (t_k - 1).bit_length()))
    return pl.pallas_call(
        body,
        out_shape=(jax.ShapeDtypeStruct((b, t_q, t_k), F32),
                   jax.ShapeDtypeStruct((b, t_q, LANE), F32),
                   jax.ShapeDtypeStruct((b, t_q, LANE), jnp.int32)),
        grid=(b, nq, nk),
        in_specs=[pl.BlockSpec((1, tq, IDX_HEADS * IDX_DIM), lambda bi, i, j: (bi, i, qi_cb)),
                  pl.BlockSpec((1, tq, LANE), lambda bi, i, j: (bi, i, kw_cb)),
                  pl.BlockSpec((1, kb, LANE), lambda bi, i, j: (bi, kmap(i, j), 0))],
        out_specs=(pl.BlockSpec((1, tq, kb), lambda bi, i, j: (bi, i, j)),
                   pl.BlockSpec((1, tq, LANE), lambda bi, i, j: (bi, i, 0)),
                   pl.BlockSpec((1, tq, LANE), lambda bi, i, j: (bi, i, 0))),
        scratch_shapes=[pltpu.VMEM((nk, tq, kb), F32)],
        compiler_params=_cparams(("parallel", "parallel", "arbitrary")),
        name="dsa_select",
    )(z, z, ki_n)


def _rwkv_body(zr_ref, zk_ref, zv_ref, zw_ref, za_ref, zg_ref,
               sr_ref, sk_ref, sv_ref, sw_ref, sa_ref, sg_ref,
               mr_ref, mk_ref, mv_ref, mw_ref, ma_ref, mg_ref,
               vec_ref, w2_ref, a2_ref, g2_ref, s0_ref,
               mix_ref, sfin_ref,
               s_scr, prev_scr, prevl_scr, prevg_scr, y_scr, *, tc, lc, nt, pp):
    t = pl.program_id(2)

    @pl.when(t == 0)
    def _():
        s_scr[...] = s0_ref[0]
        prev_scr[0:1, :] = sr_ref[0]
        prev_scr[1:2, :] = sk_ref[0]
        prev_scr[2:3, :] = sv_ref[0]
        prevl_scr[0:1, :] = sw_ref[0]
        prevl_scr[1:2, :] = sa_ref[0]
        prevg_scr[0:1, :] = sg_ref[0]

    row = lax.broadcasted_iota(jnp.int32, (tc, 1), 0)

    def shifted(z_ref, prow, mu_ref):
        z = z_ref[0]
        zp = jnp.where(row == 0, prow, pltpu.roll(z, 1, axis=0))
        return z + (zp - z) * mu_ref[...], z[tc - 1:tc, :]

    r, last_r = shifted(zr_ref, prev_scr[0:1, :], mr_ref)
    k, last_k = shifted(zk_ref, prev_scr[1:2, :], mk_ref)
    v, last_v = shifted(zv_ref, prev_scr[2:3, :], mv_ref)
    zw, last_w = shifted(zw_ref, prevl_scr[0:1, :], mw_ref)
    za, last_a = shifted(za_ref, prevl_scr[1:2, :], ma_ref)
    zg, last_g = shifted(zg_ref, prevg_scr[0:1, :], mg_ref)
    prev_scr[0:1, :] = last_r
    prev_scr[1:2, :] = last_k
    prev_scr[2:3, :] = last_v
    prevl_scr[0:1, :] = last_w
    prevl_scr[1:2, :] = last_a
    prevg_scr[0:1, :] = last_g

    w0, a0 = vec_ref[0:1, :], vec_ref[1:2, :]
    k_k, k_a, r_k = vec_ref[2:3, :], vec_ref[3:4, :], vec_ref[4:5, :]
    ln_w, ln_b = vec_ref[5:6, :], vec_ref[6:7, :]

    ones_blk = _head_block_matrix(LANE, 1.0)
    avg_blk = _head_block_matrix(LANE, 1.0 / HEAD_DIM)

    def per_head(x, blk):
        return jnp.concatenate([_dot_exact_rhs(x[:, i * LANE:(i + 1) * LANE], blk) for i in range(pp)], axis=1)

    xw = w0 + jnp.dot(jnp.tanh(zw).astype(BF16), w2_ref[...], preferred_element_type=F32)
    nx = -xw
    softplus = jnp.maximum(nx, 0.0) + jnp.log1p(jnp.exp(-jnp.abs(nx)))
    w_log = -softplus - 0.5
    lw = -jnp.exp(w_log)
    a = jax.nn.sigmoid(a0 + jnp.dot(za.astype(BF16), a2_ref[...], preferred_element_type=F32))
    g = jnp.dot(jax.nn.sigmoid(zg).astype(BF16), g2_ref[...], preferred_element_type=F32)
    kk = k * k_k
    kk = kk / jnp.maximum(jnp.sqrt(per_head(kk * kk, ones_blk)), 1e-12)
    k2 = k * (1.0 + (a - 1.0) * k_a)

    a_step = -kk
    b_step = kk * a

    lane = lax.broadcasted_iota(jnp.int32, (1, LANE), 1)
    m0 = lane < HEAD_DIM
    rr = lax.broadcasted_iota(jnp.int32, (lc, 2 * lc), 0)
    cc = lax.broadcasted_iota(jnp.int32, (lc, 2 * lc), 1)
    incl = jnp.where(cc < lc, cc, cc - lc) <= rr
    strict = (lax.broadcasted_iota(jnp.int32, (2 * lc, 2 * lc), 1)
              < lax.broadcasted_iota(jnp.int32, (2 * lc, 2 * lc), 0))
    tri = (lax.broadcasted_iota(jnp.int32, (lc, lc), 1)
           <= lax.broadcasted_iota(jnp.int32, (lc, lc), 0)).astype(BF16)
    nsteps = lc.bit_length() - 1
    nt_dims = ((1,), (1,))
    nn_dims = ((1,), (0,))
    tn_dims = ((0,), (0,))

    eye = (lax.broadcasted_iota(jnp.int32, (2 * lc, 2 * lc), 0)
           == lax.broadcasted_iota(jnp.int32, (2 * lc, 2 * lc), 1)).astype(F32)

    chunks = range(tc // lc)
    st = []
    for c in chunks:
        rows = slice(c * lc, (c + 1) * lc)
        lwc = lw[rows]
        cs = _dot_exact_lhs(tri, lwc)
        p_in = jnp.exp(-cs)
        at = a_step[rows] * jnp.exp(cs - lwc)
        bt = b_step[rows] * p_in
        kt = k2[rows] * p_in
        rt = r[rows] * jnp.exp(cs)
        vc = v[rows]
        p_last = jnp.exp(cs[lc - 1:lc, :])
        for pi in range(pp):
            cols = slice(pi * LANE, (pi + 1) * LANE)
            stack = lambda x: jnp.concatenate([jnp.where(m0, x[:, cols], 0.0),
                                               jnp.where(m0, 0.0, x[:, cols])], axis=0).astype(BF16)
            st.append(dict(c=c, pi=pi,
                           ar=jnp.concatenate([stack(at), rt[:, cols].astype(BF16)], axis=0),
                           bk=jnp.concatenate([stack(bt), stack(kt)], axis=0),
                           v_s=stack(vc), p_last=p_last[:, cols]))
    for d in st:
        gram = _dot_rw(d["ar"], d["bk"], nt_dims)
        d["t_p"] = jnp.where(strict, gram[0:2 * lc, 0:2 * lc], 0.0)
        d["w_ak"] = jnp.where(strict, gram[0:2 * lc, 2 * lc:4 * lc], 0.0).astype(BF16)
        d["w_rb"] = jnp.where(incl, gram[2 * lc:3 * lc, 0:2 * lc], 0.0).astype(BF16)
        d["w_rk"] = jnp.where(incl, gram[2 * lc:3 * lc, 2 * lc:4 * lc], 0.0).astype(BF16)
        d["minv"] = eye + d["t_p"]
    for _ in range(nsteps - 1):
        for d in st:
            d["t_p"] = _dot_rw(d["t_p"], d["t_p"], nn_dims)
        for d in st:
            d["minv"] = d["minv"] + _dot_rw(d["minv"], d["t_p"], nn_dims)
    for d in st:
        d["minv"] = d["minv"].astype(BF16)
        d["wv"] = _dot_rw(d["w_ak"], d["v_s"], nn_dims)
        d["y_c"] = _dot_rw(d["w_rk"], d["v_s"], nn_dims)
    for d in st:
        d["ma"] = _dot_rw(d["minv"], d["ar"][0:2 * lc], nn_dims)
        d["mwv"] = _dot_rw(d["minv"], d["wv"], nn_dims)
    for d in st:
        d["g"] = _dot_rw(d["ma"], d["bk"][0:2 * lc], tn_dims).astype(BF16)
        d["d"] = _dot_rw(jnp.concatenate([d["mwv"].astype(BF16), d["v_s"]], axis=0), d["bk"], tn_dims)
    s_cur = [s_scr[pi] for pi in range(pp)]
    for d in st:
        s_in = s_cur[d["pi"]]
        d["s0"] = s_in.astype(BF16)
        s_cur[d["pi"]] = ((s_in + _dot_rw(d["s0"], d["g"], nn_dims)) + d["d"]) * d["p_last"]
    for pi in range(pp):
        s_scr[pi] = s_cur[pi]
    for d in st:
        d["xr"] = _dot_rw(d["ar"], d["s0"], nt_dims)
    for d in st:
        d["u"] = (_dot_rw(d["minv"], d["xr"][0:2 * lc], nn_dims) + d["mwv"]).astype(BF16)
    for d in st:
        c, pi = d["c"], d["pi"]
        y_scr[c * lc:(c + 1) * lc, pi * LANE:(pi + 1) * LANE] = (
            (d["xr"][2 * lc:3 * lc] + _dot_rw(d["w_rb"], d["u"], nn_dims)) + d["y_c"])

    y = y_scr[...]
    mean = per_head(y, avg_blk)
    dev = y - mean
    var = per_head(dev * dev, avg_blk)
    yn = (dev * lax.rsqrt(var + RW_GN_EPS)) * ln_w + ln_b
    bonus = per_head((r * k2) * r_k, ones_blk) * v
    mix_ref[0] = ((yn + bonus) * g).astype(mix_ref.dtype)

    @pl.when(t == nt - 1)
    def _():
        sfin_ref[0] = s_scr[...]


def _rwkv(z, shift_prev, s0_pairs, mu, vecs, w2, a2, g2, d_model):
    b, t = z.shape[0], z.shape[1]
    npairs = d_model // LANE
    tc = min(t, 512)
    lc = min(CHUNK, t)
    nt = t // tc
    pp = 2 if tc // lc >= 4 else 8
    wp = pp * LANE
    ngroups = npairs // pp
    cb_w, cb_a, cb_g = 3 * npairs, 3 * npairs + 1, (3 * npairs + 2) // 2

    def zspec(width, cbf):
        return pl.BlockSpec((1, tc, width), lambda bi, p, ti: (bi, ti, cbf(p)))

    def sspec(width, cbf):
        return pl.BlockSpec((1, 1, width), lambda bi, p, ti: (bi, 0, cbf(p)))

    def mspec(width, cbf):
        return pl.BlockSpec((1, width), lambda bi, p, ti: (0, cbf(p)))

    cbfs = [(wp, lambda p: p), (wp, lambda p: ngroups + p), (wp, lambda p: 2 * ngroups + p),
            (LANE, lambda p: cb_w), (LANE, lambda p: cb_a), (2 * LANE, lambda p: cb_g)]
    in_specs = ([zspec(w, f) for w, f in cbfs] + [sspec(w, f) for w, f in cbfs] + [mspec(w, f) for w, f in cbfs]
                + [pl.BlockSpec((8, wp), lambda bi, p, ti: (0, p)),
                   pl.BlockSpec((LANE, wp), lambda bi, p, ti: (0, p)),
                   pl.BlockSpec((LANE, wp), lambda bi, p, ti: (0, p)),
                   pl.BlockSpec((2 * LANE, wp), lambda bi, p, ti: (0, p)),
                   pl.BlockSpec((1, pp, LANE, LANE), lambda bi, p, ti: (bi, p, 0, 0))])
    return pl.pallas_call(
        functools.partial(_rwkv_body, tc=tc, lc=lc, nt=nt, pp=pp),
        out_shape=(jax.ShapeDtypeStruct((b, t, d_model), BF16),
                   jax.ShapeDtypeStruct((b, npairs, LANE, LANE), F32)),
        grid=(b, ngroups, nt),
        in_specs=in_specs,
        out_specs=(pl.BlockSpec((1, tc, wp), lambda bi, p, ti: (bi, ti, p)),
                   pl.BlockSpec((1, pp, LANE, LANE), lambda bi, p, ti: (bi, p, 0, 0))),
        scratch_shapes=[pltpu.VMEM((pp, LANE, LANE), F32),
                        pltpu.VMEM((8, wp), F32),
                        pltpu.VMEM((8, LANE), F32),
                        pltpu.VMEM((8, 2 * LANE), F32),
                        pltpu.VMEM((tc, wp), F32)],
        compiler_params=_cparams(("parallel", "parallel", "arbitrary")),
        name="rwkv7",
    )(*([z] * 6), *([shift_prev] * 6), *([mu] * 6), vecs, w2, a2, g2, s0_pairs)


def _pad_cols(x, segments):
    parts = []
    for start, width, padded in segments:
        seg = x[..., start:start + width]
        if padded > width:
            seg = jnp.concatenate([seg, jnp.zeros(seg.shape[:-1] + (padded - width,), seg.dtype)], axis=-1)
        parts.append(seg)
    return jnp.concatenate(parts, axis=-1)


def _pad_rows(x, padded):
    return jnp.concatenate([x, jnp.zeros((padded - x.shape[0],) + x.shape[1:], x.dtype)], axis=0)


def _pairs_from_heads(s):
    b, h = s.shape[0], s.shape[1]
    s = s.reshape(b, h // 2, 2, HEAD_DIM, HEAD_DIM)
    z = jnp.zeros_like(s[:, :, 0])
    top = jnp.concatenate([s[:, :, 0], z], axis=-1)
    bot = jnp.concatenate([z, s[:, :, 1]], axis=-1)
    return jnp.concatenate([top, bot], axis=-2)


def _heads_from_pairs(sp):
    b, npairs = sp.shape[0], sp.shape[1]
    s = jnp.stack([sp[:, :, :HEAD_DIM, :HEAD_DIM], sp[:, :, HEAD_DIM:, HEAD_DIM:]], axis=2)
    return s.reshape(b, 2 * npairs, HEAD_DIM, HEAD_DIM)


def _tile_gain(g, width):
    return jnp.tile(g.astype(F32), width // HEAD_DIM).reshape(1, width)


def _mem_attend(z, memq_cb, km, vm, q_gain):
    t_q = z.shape[1]
    wq = MEM_HEADS * HEAD_DIM
    tq = min(t_q, 512)
    no_sink = jnp.full((1, wq), -jnp.inf, F32)
    return _flash(z, memq_cb, wq, km, 0, vm, 0, _tile_gain(q_gain, LANE), no_sink,
                  mode="all", tq=tq, kb=km.shape[1], group=1, n_valid=km.shape[1])


def _conv_ffn(x2d, b, t, gain, w_up, conv_w, w_down, prev):
    act, u_last = _up_conv(x2d, gain, w_up, conv_w, prev, t)
    return _mm_res([act], [w_down], x2d), u_last


def kernel(x_prompt, x_sample, state_rwkv_wkv, state_rwkv_shift, cache_swa_k, cache_swa_v, cache_dsa_k, cache_dsa_v, cache_dsa_idx_k, cache_mem_k, cache_mem_v, state_ffn_conv, mem_prompt, attn_norm, ffn_norm, mem_norm, mem_w_kv, mem_q_norm, mem_k_norm, a_w_in, a_mu, a_w0, a_w2, a_a0, a_a2, a_g2, a_k_k, a_k_a, a_r_k, a_ln_w, a_ln_b, a_w_out, b_w_in, b_q_norm, b_k_norm, b_sink, b_w_out, c_w_in, c_q_norm, c_k_norm, c_idx_k_norm, c_w_out, ffn_w_up, ffn_conv, ffn_w_down):
    bp, t, d = x_prompt.shape
    bd, s_len = x_sample.shape[:2]
    depth = attn_norm.shape[0]
    win_rows = cache_swa_k.shape[2]
    past = cache_dsa_k.shape[2] if cache_dsa_k.shape[0] else 0
    d_ff = ffn_w_down.shape[1]
    mem_tokens = mem_prompt.shape[1]
    q_cols = d
    kv_cols = ATT_KV_HEADS * HEAD_DIM
    memq_cols = MEM_HEADS * HEAD_DIM
    att_group = (d // HEAD_DIM) // ATT_KV_HEADS
    dec_lora = a_w2.shape[1]
    a_lora = a_a2.shape[1]
    g_lora = a_g2.shape[1]
    rw_cols = 3 * d + dec_lora + a_lora + g_lora
    k_sel_p = min(TOPK_MAX, t // 4)
    k_sel_s = min(TOPK_MAX, (past + s_len) // 4)
    assert g_lora == 2 * LANE and dec_lora <= LANE and a_lora <= LANE

    xp = x_prompt.reshape(bp * t, d)
    xs = x_sample.reshape(bd * s_len, d)

    rw_segments = [(0, 3 * d, 3 * d), (3 * d, dec_lora, LANE), (3 * d + dec_lora, a_lora, LANE),
                   (3 * d + dec_lora + a_lora, g_lora, g_lora)]
    rw_padded = 3 * d + 2 * LANE + g_lora
    o_qi = q_cols + 2 * kv_cols
    o_ki = o_qi + IDX_HEADS * IDX_DIM
    c_cols = o_ki + IDX_DIM + IDX_HEADS

    outs = {k: [] for k in ("p_rw_wkv", "p_rw_sh", "p_sw_k", "p_sw_v", "p_ds_k", "p_ds_v", "p_ds_i", "p_mk",
                            "p_mv", "p_cv", "s_rw_wkv", "s_rw_sh", "s_sw_k", "s_sw_v", "s_ds_k", "s_ds_v",
                            "s_ds_i", "s_cv")}

    def unpad_rw(row):
        return jnp.concatenate([row[..., :3 * d], row[..., 3 * d:3 * d + dec_lora],
                                row[..., 3 * d + LANE:3 * d + LANE + a_lora],
                                row[..., 3 * d + 2 * LANE:3 * d + 2 * LANE + g_lora]], axis=-1)

    for i in range(depth):
        kind, j = i % 3, i // 3
        if kind == 0:
            w_in = jnp.concatenate([_pad_cols(a_w_in[j], rw_segments), a_w_in[j][:, rw_cols:]], axis=1).astype(BF16)
            memq_cb = rw_padded // memq_cols
            zp = _mm_norm(xp, attn_norm[i], w_in).reshape(bp, t, -1)
            zs = _mm_norm(xs, attn_norm[i], w_in).reshape(bd, s_len, -1)
            mu = _pad_cols(a_mu[j].reshape(1, -1), rw_segments)
            mu = jnp.concatenate([mu, jnp.zeros((1, memq_cols), F32)], axis=1)
            vecs = jnp.stack([a_w0[j], a_a0[j], a_k_k[j], a_k_a[j], a_r_k[j].reshape(-1), a_ln_w[j], a_ln_b[j],
                              jnp.zeros((d,), F32)], axis=0)
            w2 = _pad_rows(a_w2[j], LANE).astype(BF16)
            a2 = _pad_rows(a_a2[j], LANE).astype(BF16)
            g2 = a_g2[j].astype(BF16)
            sh_p = jnp.zeros((bp, 1, zp.shape[-1]), F32)
            st_p = jnp.zeros((bp, d // LANE, LANE, LANE), F32)
            sh_s = _pad_cols(state_rwkv_shift[j], rw_segments)
            sh_s = jnp.concatenate([sh_s, jnp.zeros((bd, memq_cols), F32)], axis=1).reshape(bd, 1, -1)
            st_s = _pairs_from_heads(state_rwkv_wkv[j])
            mp, stp = _rwkv(zp, sh_p, st_p, mu, vecs, w2, a2, g2, d)
            ms, sts = _rwkv(zs, sh_s, st_s, mu, vecs, w2, a2, g2, d)
            outs["p_rw_sh"].append(unpad_rw(zp[:, -1]))
            outs["p_rw_wkv"].append(_heads_from_pairs(stp))
            outs["s_rw_sh"].append(unpad_rw(zs[:, -1]))
            outs["s_rw_wkv"].append(_heads_from_pairs(sts))
            w_out = a_w_out[j]
        elif kind == 1:
            w_in = b_w_in[j].astype(BF16)
            memq_cb = (q_cols + 2 * kv_cols) // memq_cols
            k_cb, v_cb = q_cols // kv_cols, q_cols // kv_cols + 1
            zp = _mm_norm(xp, attn_norm[i], w_in).reshape(bp, t, -1)
            zs = _mm_norm(xs, attn_norm[i], w_in).reshape(bd, s_len, -1)
            kgain = _tile_gain(b_k_norm[j], kv_cols)
            qgain = _tile_gain(b_q_norm[j], LANE)
            sink = jnp.repeat(b_sink[j].astype(F32), HEAD_DIM).reshape(1, q_cols)
            knp = _headnorm(zp.reshape(bp * t, -1), k_cb, kv_cols, kgain).reshape(bp, t, kv_cols)
            mp = _flash(zp, 0, q_cols, knp, 0, zp, v_cb, qgain, sink, mode="band", tq=WINDOW, kb=WINDOW,
                        group=att_group)
            outs["p_sw_k"].append(knp[:, t - win_rows:].reshape(bp, win_rows, ATT_KV_HEADS, HEAD_DIM))
            outs["p_sw_v"].append(zp[:, t - win_rows:, q_cols + kv_cols:q_cols + 2 * kv_cols]
                                  .reshape(bp, win_rows, ATT_KV_HEADS, HEAD_DIM))
            kns = _headnorm(zs.reshape(bd * s_len, -1), k_cb, kv_cols, kgain).reshape(bd, s_len, kv_cols)
            vs_new = zs[:, :, q_cols + kv_cols:q_cols + 2 * kv_cols]
            k_all = jnp.concatenate([cache_swa_k[j].reshape(bd, win_rows, kv_cols), kns], axis=1)
            v_all = jnp.concatenate([cache_swa_v[j].reshape(bd, win_rows, kv_cols), vs_new], axis=1)
            n_keys = win_rows + s_len
            n_pad = -(-n_keys // LANE) * LANE
            pad = jnp.zeros((bd, n_pad - n_keys, kv_cols), F32)
            ms = _flash(zs, 0, q_cols, jnp.concatenate([k_all, pad], axis=1), 0,
                        jnp.concatenate([v_all, pad], axis=1), 0, qgain, sink, mode="all", tq=s_len, kb=n_pad,
                        group=att_group, n_valid=n_keys)
            outs["s_sw_k"].append(k_all[:, n_keys - win_rows:].reshape(bd, win_rows, ATT_KV_HEADS, HEAD_DIM))
            outs["s_sw_v"].append(v_all[:, n_keys - win_rows:].reshape(bd, win_rows, ATT_KV_HEADS, HEAD_DIM))
            w_out = b_w_out[j]
        else:
            wc = c_w_in[j]
            w_in = jnp.concatenate([wc[:, :o_ki], wc[:, c_cols:],
                                    _pad_cols(wc, [(o_ki, IDX_DIM + IDX_HEADS, LANE)])], axis=1).astype(BF16)
            memq_cb = o_ki // memq_cols
            kw_cb = (o_ki + memq_cols) // LANE
            k_cb, v_cb = q_cols // kv_cols, q_cols // kv_cols + 1
            qi_cb = o_qi // (IDX_HEADS * IDX_DIM)
            zp = _mm_norm(xp, attn_norm[i], w_in).reshape(bp, t, -1)
            zs = _mm_norm(xs, attn_norm[i], w_in).reshape(bd, s_len, -1)
            kgain = _tile_gain(c_k_norm[j], kv_cols)
            qgain = _tile_gain(c_q_norm[j], LANE)
            igain = jnp.concatenate([c_idx_k_norm[j].astype(F32), jnp.zeros((LANE - IDX_DIM,), F32)]).reshape(1, LANE)
            no_sink = jnp.full((1, q_cols), -jnp.inf, F32)
            knp = _headnorm(zp.reshape(bp * t, -1), k_cb, kv_cols, kgain).reshape(bp, t, kv_cols)
            kip = _headnorm(zp.reshape(bp * t, -1), kw_cb, LANE, igain).reshape(bp, t, LANE)
            tq = min(t, DSA_TQ)
            kb = min(t, DSA_KB)
            sc, tau, cut = _dsa_select(zp, qi_cb, kw_cb, kip, tq=tq, kb=kb, k_sel=k_sel_p, n_valid=t, causal=True)
            mp = _flash(zp, 0, q_cols, knp, 0, zp, v_cb, qgain, no_sink, mode="dsa", tq=tq, kb=kb, group=att_group,
                        n_valid=t, causal=True, scores=sc, tau=tau, jb=cut)
            outs["p_ds_k"].append(knp.reshape(bp, t, ATT_KV_HEADS, HEAD_DIM))
            outs["p_ds_v"].append(zp[:, :, q_cols + kv_cols:q_cols + 2 * kv_cols].reshape(bp, t, ATT_KV_HEADS, HEAD_DIM))
            outs["p_ds_i"].append(kip[:, :, :IDX_DIM])
            kns = _headnorm(zs.reshape(bd * s_len, -1), k_cb, kv_cols, kgain).reshape(bd, s_len, kv_cols)
            kis = _headnorm(zs.reshape(bd * s_len, -1), kw_cb, LANE, igain).reshape(bd, s_len, LANE)
            vs_new = zs[:, :, q_cols + kv_cols:q_cols + 2 * kv_cols]
            n_keys = past + s_len
            n_pad = -(-n_keys // LANE) * LANE
            zpad = lambda w: jnp.zeros((bd, n_pad - n_keys, w), F32)
            k_all = jnp.concatenate([cache_dsa_k[j].reshape(bd, past, kv_cols), kns, zpad(kv_cols)], axis=1)
            v_all = jnp.concatenate([cache_dsa_v[j].reshape(bd, past, kv_cols), vs_new, zpad(kv_cols)], axis=1)
            ki_cache = jnp.concatenate([cache_dsa_idx_k[j], jnp.zeros((bd, past, LANE - IDX_DIM), F32)], axis=-1)
            ki_all = jnp.concatenate([ki_cache, kis, zpad(LANE)], axis=1)
            sc, tau, cut = _dsa_select(zs, qi_cb, kw_cb, ki_all, tq=s_len, kb=n_pad, k_sel=k_sel_s, n_valid=n_keys,
                                       causal=False)
            ms = _flash(zs, 0, q_cols, k_all, 0, v_all, 0, qgain, no_sink, mode="dsa", tq=s_len, kb=n_pad,
                        group=att_group, n_valid=n_keys, causal=False, scores=sc, tau=tau, jb=cut)
            outs["s_ds_k"].append(kns.reshape(bd, s_len, ATT_KV_HEADS, HEAD_DIM))
            outs["s_ds_v"].append(vs_new.reshape(bd, s_len, ATT_KV_HEADS, HEAD_DIM))
            outs["s_ds_i"].append(kis[:, :, :IDX_DIM])
            w_out = c_w_out[j]

        kv_mem = _mm_norm(mem_prompt.reshape(bp * mem_tokens, d), mem_norm[i], mem_w_kv[i].astype(BF16))
        km_p = _headnorm(kv_mem, 0, memq_cols, _tile_gain(mem_k_norm[i], memq_cols)).reshape(bp, mem_tokens, memq_cols)
        vm_p = kv_mem[:, memq_cols:].reshape(bp, mem_tokens, memq_cols)
        outs["p_mk"].append(km_p.reshape(bp, mem_tokens, MEM_HEADS, HEAD_DIM))
        outs["p_mv"].append(vm_p.reshape(bp, mem_tokens, MEM_HEADS, HEAD_DIM))
        mo_p = _mem_attend(zp, memq_cb, km_p, vm_p, mem_q_norm[i])
        mo_s = _mem_attend(zs, memq_cb, cache_mem_k[i].reshape(bd, mem_tokens, memq_cols),
                           cache_mem_v[i].reshape(bd, mem_tokens, memq_cols), mem_q_norm[i])
        w_mix, w_mem = w_out[:d].astype(BF16), w_out[d:].astype(BF16)
        xp = _mm_res([mp.reshape(bp * t, d), mo_p.reshape(bp * t, memq_cols)], [w_mix, w_mem], xp)
        xs = _mm_res([ms.reshape(bd * s_len, d), mo_s.reshape(bd * s_len, memq_cols)], [w_mix, w_mem], xs)

        w_up, w_down = ffn_w_up[i].astype(BF16), ffn_w_down[i].astype(BF16)
        xp, cp = _conv_ffn(xp, bp, t, ffn_norm[i], w_up, ffn_conv[i], w_down,
                           jnp.zeros((bp, CONV_W - 1, 2 * d_ff), F32))
        xs, cs = _conv_ffn(xs, bd, s_len, ffn_norm[i], w_up, ffn_conv[i], w_down, state_ffn_conv[i])
        outs["p_cv"].append(cp)
        outs["s_cv"].append(cs)

    st = jnp.stack
    order = ("p_rw_wkv", "p_rw_sh", "p_sw_k", "p_sw_v", "p_ds_k", "p_ds_v", "p_ds_i", "p_mk", "p_mv", "p_cv",
             "s_rw_wkv", "s_rw_sh", "s_sw_k", "s_sw_v", "s_ds_k", "s_ds_v", "s_ds_i", "s_cv")
    return (xp.reshape(bp, t, d), xs.reshape(bd, s_len, d)) + tuple(st(outs[k]) for k in order)
```

```python
import functools

import jax
import jax.numpy as jnp
from jax import lax
from jax.experimental import pallas as pl
from jax.experimental.pallas import tpu as pltpu

F32 = jnp.float32
BF16 = jnp.bfloat16

HEAD_DIM = 64
CHUNK = 64
NORM_EPS = 1e-6
RW_GN_EPS = HEAD_DIM * 1e-5
ATT_KV_HEADS = 4
WINDOW = 128
IDX_HEADS = 8
IDX_DIM = 64
TOPK_MAX = 256
MEM_HEADS = 4
CONV_W = 3

LANE = 128
VMEM_LIMIT = 52 * 1024 * 1024
NEG_BIG = -(2.0 ** 100)
INT_MIN = -2147483648
MM_ROWS = 1024
DSA_TQ = 256
DSA_KB = 512


def _cparams(sem, vmem=VMEM_LIMIT):
    return pltpu.CompilerParams(dimension_semantics=sem, vmem_limit_bytes=vmem)


def _split3(a):
    a1 = a.astype(BF16)
    r1 = a - a1.astype(F32)
    a2 = r1.astype(BF16)
    r2 = r1 - a2.astype(F32)
    return a1, a2, r2.astype(BF16)


def _dot_exact_rhs(a, e):
    a1, a2, a3 = _split3(a)
    d = lambda x: jnp.dot(x, e, preferred_element_type=F32)
    return (d(a3) + d(a2)) + d(a1)


def _dot_exact_lhs(e, a):
    a1, a2, a3 = _split3(a)
    d = lambda x: jnp.dot(e, x, preferred_element_type=F32)
    return (d(a3) + d(a2)) + d(a1)


def _head_block_matrix(width, value):
    r = lax.broadcasted_iota(jnp.int32, (width, width), 0) // HEAD_DIM
    c = lax.broadcasted_iota(jnp.int32, (width, width), 1) // HEAD_DIM
    return jnp.where(r == c, value, 0.0).astype(BF16)


def _dot_rw(a, b, dims):
    return lax.dot_general(a.astype(BF16), b.astype(BF16), (dims, ((), ())), preferred_element_type=F32)


def _pick_tile(n, cap):
    best = None
    for t in range(LANE, min(n, cap) + 1, LANE):
        if n % t == 0:
            best = t
    assert best is not None, n
    return best


def _mm_norm_body(x_ref, g_ref, w_ref, o_ref, xn_ref):
    @pl.when(pl.program_id(1) == 0)
    def _():
        x = x_ref[...]
        ms = jnp.mean(x * x, axis=-1, keepdims=True)
        xn_ref[...] = ((x * lax.rsqrt(ms + NORM_EPS)) * g_ref[...]).astype(BF16)

    o_ref[...] = jnp.dot(xn_ref[...], w_ref[...], preferred_element_type=F32)


def _mm_norm(x, gain, w):
    m, k = x.shape
    n = w.shape[1]
    tm = min(m, MM_ROWS)
    tn = _pick_tile(n, 1536)
    return pl.pallas_call(
        _mm_norm_body,
        out_shape=jax.ShapeDtypeStruct((m, n), F32),
        grid=(m // tm, n // tn),
        in_specs=[pl.BlockSpec((tm, k), lambda i, j: (i, 0)),
                  pl.BlockSpec((1, k), lambda i, j: (0, 0)),
                  pl.BlockSpec((k, tn), lambda i, j: (0, j))],
        out_specs=pl.BlockSpec((tm, tn), lambda i, j: (i, j)),
        scratch_shapes=[pltpu.VMEM((tm, k), BF16)],
        compiler_params=_cparams(("parallel", "arbitrary")),
        name="mm_norm",
    )(x, gain.reshape(1, k), w)


def _mm_res_body(*refs, n_lhs):
    lhs = refs[:n_lhs]
    ws = refs[n_lhs:2 * n_lhs]
    r_ref, o_ref = refs[2 * n_lhs], refs[2 * n_lhs + 1]
    acc = jnp.dot(lhs[0][...], ws[0][...], preferred_element_type=F32)
    for a, w in zip(lhs[1:], ws[1:]):
        acc = acc + jnp.dot(a[...], w[...], preferred_element_type=F32)
    o_ref[...] = r_ref[...] + acc


def _mm_res(lhs_list, w_list, res):
    m, n = res.shape
    ktot = sum(a.shape[1] for a in lhs_list)
    tm = min(m, MM_ROWS)
    tn = _pick_tile(n, 1024 if ktot <= 3072 else 512)
    n_lhs = len(lhs_list)
    in_specs = [pl.BlockSpec((tm, a.shape[1]), lambda i, j: (i, 0)) for a in lhs_list]
    in_specs += [pl.BlockSpec((w.shape[0], tn), lambda i, j: (0, j)) for w in w_list]
    in_specs += [pl.BlockSpec((tm, tn), lambda i, j: (i, j))]
    return pl.pallas_call(
        functools.partial(_mm_res_body, n_lhs=n_lhs),
        out_shape=jax.ShapeDtypeStruct((m, n), F32),
        grid=(m // tm, n // tn),
        in_specs=in_specs,
        out_specs=pl.BlockSpec((tm, tn), lambda i, j: (i, j)),
        compiler_params=_cparams(("parallel", "arbitrary")),
        name="mm_res",
    )(*lhs_list, *w_list, res)


def _headnorm_body(x_ref, g_ref, o_ref, *, width):
    avg = _head_block_matrix(LANE, 1.0 / HEAD_DIM)
    for c in range(width // LANE):
        x = x_ref[:, c * LANE:(c + 1) * LANE]
        ms = _dot_exact_rhs(x * x, avg)
        o_ref[:, c * LANE:(c + 1) * LANE] = (x * lax.rsqrt(ms + NORM_EPS)) * g_ref[:, c * LANE:(c + 1) * LANE]


def _headnorm(x, col_block, width, gain_row):
    m = x.shape[0]
    tm = min(m, 1024)
    return pl.pallas_call(
        functools.partial(_headnorm_body, width=width),
        out_shape=jax.ShapeDtypeStruct((m, width), F32),
        grid=(m // tm,),
        in_specs=[pl.BlockSpec((tm, width), lambda i: (i, col_block)),
                  pl.BlockSpec((1, width), lambda i: (0, 0))],
        out_specs=pl.BlockSpec((tm, width), lambda i: (i, 0)),
        compiler_params=_cparams(("parallel",)),
        name="headnorm",
    )(x, gain_row)


def _up_conv_body(x_ref, g_ref, wa_ref, wb_ref, pa_ref, pb_ref, cwa_ref, cwb_ref, o_ref, la_ref, lb_ref,
                  xn_ref, ca_ref, cb_ref, *, tm, tiles_per_batch, bpt):
    i = pl.program_id(0)
    j = pl.program_id(1)
    rpb = tm // bpt

    @pl.when(j == 0)
    def _():
        x = x_ref[...]
        ms = jnp.mean(x * x, axis=-1, keepdims=True)
        xn_ref[...] = ((x * lax.rsqrt(ms + NORM_EPS)) * g_ref[...]).astype(BF16)

    if bpt == 1:
        @pl.when(i % tiles_per_batch == 0)
        def _():
            ca_ref[j] = pa_ref[0]
            cb_ref[j] = pb_ref[0]

    xn = xn_ref[...]
    row = lax.broadcasted_iota(jnp.int32, o_ref.shape, 0)
    off = row if bpt == 1 else row % rpb
    if bpt > 1:
        pick = (lax.broadcasted_iota(jnp.int32, (tm, bpt), 0) // rpb
                == lax.broadcasted_iota(jnp.int32, (tm, bpt), 1)).astype(BF16)

    def conv(w_ref, p_ref, c_ref, cw_ref, last_ref):
        u = jnp.dot(xn, w_ref[...], preferred_element_type=F32)
        if bpt == 1:
            car = c_ref[j]
            c0, c1 = car[0:1, :], car[1:2, :]
            c_ref[j] = u[tm - 2:tm, :]
            last_ref[0] = u[tm - 2:tm, :]
        else:
            c0 = _dot_exact_lhs(pick, p_ref[:, 0, :])
            c1 = _dot_exact_lhs(pick, p_ref[:, 1, :])
            for bi in range(bpt):
                last_ref[bi] = u[(bi + 1) * rpb - 2:(bi + 1) * rpb, :]
        u1 = jnp.where(off == 0, c1, pltpu.roll(u, 1, axis=0))
        u2 = jnp.where(off == 0, c0, jnp.where(off == 1, c1, pltpu.roll(u, 2, axis=0)))
        return (u2 * cw_ref[0:1, :] + u1 * cw_ref[1:2, :]) + u * cw_ref[2:3, :]

    a = conv(wa_ref, pa_ref, ca_ref, cwa_ref, la_ref)
    b = conv(wb_ref, pb_ref, cb_ref, cwb_ref, lb_ref)
    o_ref[...] = ((a * jax.nn.sigmoid(a)) * b).astype(o_ref.dtype)


def _up_conv(x, gain, w_up, conv_w, prev, t):
    m, k = x.shape
    f = w_up.shape[1] // 2
    b = m // t
    tm = min(m, MM_ROWS)
    bpt = max(1, tm // t)
    assert tm % t == 0 or t % tm == 0
    tn = _pick_tile(f, 512)
    nj = f // tn
    tpb = max(1, t // tm)
    act, la, lb = pl.pallas_call(
        functools.partial(_up_conv_body, tm=tm, tiles_per_batch=tpb, bpt=bpt),
        out_shape=(jax.ShapeDtypeStruct((m, f), BF16),
                   jax.ShapeDtypeStruct((b, CONV_W - 1, f), F32),
                   jax.ShapeDtypeStruct((b, CONV_W - 1, f), F32)),
        grid=(m // tm, nj),
        in_specs=[pl.BlockSpec((tm, k), lambda i, j: (i, 0)),
                  pl.BlockSpec((1, k), lambda i, j: (0, 0)),
                  pl.BlockSpec((k, tn), lambda i, j: (0, j)),
                  pl.BlockSpec((k, tn), lambda i, j: (0, nj + j)),
                  pl.BlockSpec((bpt, CONV_W - 1, tn), lambda i, j: (i // tpb, 0, j)),
                  pl.BlockSpec((bpt, CONV_W - 1, tn), lambda i, j: (i // tpb, 0, nj + j)),
                  pl.BlockSpec((CONV_W, tn), lambda i, j: (0, j)),
                  pl.BlockSpec((CONV_W, tn), lambda i, j: (0, nj + j))],
        out_specs=(pl.BlockSpec((tm, tn), lambda i, j: (i, j)),
                   pl.BlockSpec((bpt, CONV_W - 1, tn), lambda i, j: (i // tpb, 0, j)),
                   pl.BlockSpec((bpt, CONV_W - 1, tn), lambda i, j: (i // tpb, 0, j))),
        scratch_shapes=[pltpu.VMEM((tm, k), BF16),
                        pltpu.VMEM((nj, CONV_W - 1, tn), F32),
                        pltpu.VMEM((nj, CONV_W - 1, tn), F32)],
        compiler_params=_cparams(("arbitrary", "arbitrary")),
        name="up_conv_gate",
    )(x, gain.reshape(1, k), w_up, w_up, prev, prev, conv_w, conv_w)
    return act, jnp.concatenate([la, lb], axis=-1)


def _flash_body(*refs, mode, tq, kb, wq, group, n_valid, nk, causal):
    q_ref, k_ref, v_ref, qg_ref, sink_ref = refs[:5]
    pos = 5
    if mode == "dsa":
        sc_ref, tau_ref, jb_ref = refs[pos:pos + 3]
        pos += 3
    o_ref = refs[pos]
    qn_scr, acc_scr, m_scr, l_scr, bias_scr = refs[pos + 1:]

    i = pl.program_id(1)
    j = pl.program_id(2)
    npairs = wq // LANE
    pairs_per_unit = group // 2 if group > 1 else 1
    lane = lax.broadcasted_iota(jnp.int32, (1, LANE), 1)
    lo_half = lane < HEAD_DIM

    @pl.when(j == 0)
    def _():
        avg = _head_block_matrix(LANE, 1.0 / HEAD_DIM)
        for p in range(npairs):
            x = q_ref[0, :, p * LANE:(p + 1) * LANE]
            ms = _dot_exact_rhs(x * x, avg)
            qn = ((x * lax.rsqrt(ms + NORM_EPS)) * qg_ref[...]) * (HEAD_DIM ** -0.5)
            qn_scr[p] = qn.astype(BF16)
        acc_scr[...] = jnp.zeros(acc_scr.shape, F32)
        l_scr[...] = jnp.zeros(l_scr.shape, F32)
        m_scr[...] = jnp.full(m_scr.shape, NEG_BIG, F32)

    if mode == "band":
        kblk = i - WINDOW // kb + j
        active = kblk >= 0
    elif mode == "dsa" and causal:
        kblk = j
        active = j * kb < (i + 1) * tq
    else:
        kblk = j
        active = None

    def step():
        kidx = kblk * kb + lax.broadcasted_iota(jnp.int32, (tq, kb), 1)
        qpos = i * tq + lax.broadcasted_iota(jnp.int32, (tq, kb), 0)
        if mode == "band":
            qchunk = qpos // CHUNK
            sel = (kidx >= (qchunk - WINDOW // CHUNK) * CHUNK) & (kidx < (qchunk + 1) * CHUNK) & (kidx >= 0)
        elif mode == "dsa":
            x = sc_ref[0]
            tau = tau_ref[0][:, 0:1]
            jb = jb_ref[0][:, 0:1]
            adm = kidx < ((qpos // CHUNK + 1) * CHUNK if causal else n_valid)
            sel = adm & ((x > tau) | ((x == tau) & (kidx <= jb)))
        else:
            sel = kidx < n_valid
        bias_scr[...] = jnp.where(sel, 0.0, NEG_BIG).astype(BF16)

        ppu = pairs_per_unit
        for u in range(npairs // ppu):
            ps = slice(u * ppu, (u + 1) * ppu)
            s = lax.dot_general(qn_scr[ps].reshape(ppu * tq, LANE), k_ref[0, 0, u], (((1,), (1,)), ((), ())),
                                preferred_element_type=F32).reshape(ppu, tq, 2 * kb)
            bias = bias_scr[...][None]
            m_old = m_scr[ps]
            s0 = s[:, :, 0:kb].astype(BF16) + bias
            s1 = s[:, :, kb:2 * kb].astype(BF16) + bias
            mn0 = jnp.maximum(m_old[:, :, 0:1], jnp.max(s0, axis=2, keepdims=True).astype(F32))
            mn1 = jnp.maximum(m_old[:, :, HEAD_DIM:HEAD_DIM + 1], jnp.max(s1, axis=2, keepdims=True).astype(F32))
            p0 = jnp.exp(s0 - mn0.astype(BF16))
            p1 = jnp.exp(s1 - mn1.astype(BF16))
            pcat = jnp.concatenate([p0, p1], axis=2).reshape(ppu * tq, 2 * kb)
            pv = jnp.dot(pcat, v_ref[0, 0, u], preferred_element_type=F32)
            pv = pv.reshape(ppu, tq, 2 * LANE)
            mn = jnp.where(lo_half, mn0, mn1)
            alpha = jnp.exp(m_old - mn)
            acc_scr[ps] = acc_scr[ps] * alpha + pv[:, :, 0:LANE]
            l_scr[ps] = l_scr[ps] * alpha + pv[:, :, LANE:2 * LANE]
            m_scr[ps] = mn

    if active is None:
        step()
    else:
        pl.when(active)(step)

    @pl.when(j == nk - 1)
    def _():
        for p in range(npairs):
            den = l_scr[p] + jnp.exp(sink_ref[:, p * LANE:(p + 1) * LANE] - m_scr[p])
            o_ref[0, :, p * LANE:(p + 1) * LANE] = (acc_scr[p] / den).astype(o_ref.dtype)


def _pair_operands_body(k_ref, v_ref, kab_ref, vab_ref, *, kb, group, n_units):
    lane = lax.broadcasted_iota(jnp.int32, (1, LANE), 1)
    lo_half = lane < HEAD_DIM
    ones_lo = jnp.broadcast_to(jnp.where(lo_half, 1.0, 0.0), (kb, LANE)).astype(BF16)
    ones_hi = jnp.broadcast_to(jnp.where(lo_half, 0.0, 1.0), (kb, LANE)).astype(BF16)
    for u in range(n_units):
        if group > 1:
            tile_idx, half = u // 2, u % 2
        else:
            tile_idx, half = u, None
        for src, dst in ((k_ref, kab_ref), (v_ref, vab_ref)):
            tile = src[0, :, tile_idx * LANE:(tile_idx + 1) * LANE]
            if half is None:
                a_part = jnp.where(lo_half, tile, 0.0)
                b_part = jnp.where(lo_half, 0.0, tile)
            elif half == 0:
                a_part = jnp.where(lo_half, tile, 0.0)
                b_part = pltpu.roll(a_part, HEAD_DIM, axis=1)
            else:
                b_part = jnp.where(lo_half, 0.0, tile)
                a_part = pltpu.roll(b_part, HEAD_DIM, axis=1)
            dst[0, 0, u, 0:kb, 0:LANE] = a_part.astype(BF16)
            dst[0, 0, u, kb:2 * kb, 0:LANE] = b_part.astype(BF16)
        vab_ref[0, 0, u, 0:kb, LANE:2 * LANE] = ones_lo
        vab_ref[0, 0, u, kb:2 * kb, LANE:2 * LANE] = ones_hi


def _pair_operands(k_arr, k_cb, v_arr, v_cb, kb, group, npairs):
    b, t_k = k_arr.shape[0], k_arr.shape[1]
    wk = ATT_KV_HEADS * HEAD_DIM
    n_units = npairs // (group // 2) if group > 1 else npairs
    nkb = t_k // kb
    return pl.pallas_call(
        functools.partial(_pair_operands_body, kb=kb, group=group, n_units=n_units),
        out_shape=(jax.ShapeDtypeStruct((b, nkb, n_units, 2 * kb, LANE), BF16),
                   jax.ShapeDtypeStruct((b, nkb, n_units, 2 * kb, 2 * LANE), BF16)),
        grid=(b, nkb),
        in_specs=[pl.BlockSpec((1, kb, wk), lambda bi, j: (bi, j, k_cb)),
                  pl.BlockSpec((1, kb, wk), lambda bi, j: (bi, j, v_cb))],
        out_specs=(pl.BlockSpec((1, 1, n_units, 2 * kb, LANE), lambda bi, j: (bi, j, 0, 0, 0)),
                   pl.BlockSpec((1, 1, n_units, 2 * kb, 2 * LANE), lambda bi, j: (bi, j, 0, 0, 0))),
        compiler_params=_cparams(("parallel", "parallel")),
        name="pair_operands",
    )(k_arr, v_arr)


def _flash(q_arr, q_cb, wq, k_arr, k_cb, v_arr, v_cb, q_gain, sink_row, *, mode, tq, kb, group,
           n_valid=None, causal=False, scores=None, tau=None, jb=None):
    b, t_q = q_arr.shape[0], q_arr.shape[1]
    t_k = k_arr.shape[1]
    nq = t_q // tq
    if mode == "band":
        assert tq == kb == WINDOW
        nk = 2
        kmap = lambda i, j: jnp.maximum(i - 1 + j, 0)
    elif mode == "dsa" and causal:
        nk = t_k // kb
        kmap = lambda i, j: jnp.minimum(j, ((i + 1) * tq - 1) // kb)
    else:
        nk = t_k // kb
        kmap = lambda i, j: j
    npairs = wq // LANE
    kab, vab = _pair_operands(k_arr, k_cb, v_arr, v_cb, kb, group, npairs)
    n_units = kab.shape[2]
    in_specs = [pl.BlockSpec((1, tq, wq), lambda bi, i, j: (bi, i, q_cb)),
                pl.BlockSpec((1, 1, n_units, 2 * kb, LANE), lambda bi, i, j: (bi, kmap(i, j), 0, 0, 0)),
                pl.BlockSpec((1, 1, n_units, 2 * kb, 2 * LANE), lambda bi, i, j: (bi, kmap(i, j), 0, 0, 0)),
                pl.BlockSpec((1, LANE), lambda bi, i, j: (0, 0)),
                pl.BlockSpec((1, wq), lambda bi, i, j: (0, 0))]
    args = [q_arr, kab, vab, q_gain, sink_row]
    if mode == "dsa":
        in_specs += [pl.BlockSpec((1, tq, kb), lambda bi, i, j: (bi, i, kmap(i, j))),
                     pl.BlockSpec((1, tq, LANE), lambda bi, i, j: (bi, i, 0)),
                     pl.BlockSpec((1, tq, LANE), lambda bi, i, j: (bi, i, 0))]
        args += [scores, tau, jb]
    body = functools.partial(_flash_body, mode=mode, tq=tq, kb=kb, wq=wq, group=group,
                             n_valid=n_valid, nk=nk, causal=causal)
    return pl.pallas_call(
        body,
        out_shape=jax.ShapeDtypeStruct((b, t_q, wq), BF16),
        grid=(b, nq, nk),
        in_specs=in_specs,
        out_specs=pl.BlockSpec((1, tq, wq), lambda bi, i, j: (bi, i, 0)),
        scratch_shapes=[pltpu.VMEM((npairs, tq, LANE), BF16),
                        pltpu.VMEM((npairs, tq, LANE), F32),
                        pltpu.VMEM((npairs, tq, LANE), F32),
                        pltpu.VMEM((npairs, tq, LANE), F32),
                        pltpu.VMEM((tq, kb), BF16)],
        compiler_params=_cparams(("parallel", "parallel", "arbitrary")),
        name="flash_" + mode,
    )(*args)


def _dsa_select_body(qi_ref, kw_ref, ki_ref, sc_ref, tau_ref, jb_ref, sc_scr, *, tq, kb, nk, k_sel,
                     n_valid, causal, idx_bits):
    i = pl.program_id(1)
    j = pl.program_id(2)
    qpos = i * tq + lax.broadcasted_iota(jnp.int32, (tq, 1), 0)
    if causal:
        n_adm = (qpos // CHUNK + 1) * CHUNK
        active = j * kb < (i + 1) * tq
        n_blocks = ((i + 1) * tq + kb - 1) // kb
    else:
        n_adm = jnp.full((tq, 1), n_valid, jnp.int32)
        active = None
        n_blocks = nk

    def compute():
        ka = ki_ref[0]
        kab = jnp.concatenate([ka, pltpu.roll(ka, HEAD_DIM, axis=1)], axis=0).astype(BF16)
        kw = kw_ref[0]
        acc = jnp.zeros((tq, kb), F32)
        q4 = jnp.concatenate([qi_ref[0, :, p * LANE:(p + 1) * LANE].astype(BF16) for p in range(IDX_HEADS // 2)],
                             axis=0)
        s = lax.dot_general(q4, kab, (((1,), (1,)), ((), ())), preferred_element_type=F32)
        for p in range(IDX_HEADS // 2):
            for h in range(2):
                col = HEAD_DIM + 2 * p + h
                w = (kw[:, col:col + 1] * IDX_HEADS ** -0.5) * IDX_DIM ** -0.5
                acc = acc + jnp.maximum(s[p * tq:(p + 1) * tq, h * kb:(h + 1) * kb], 0.0) * w
        acc = jnp.where(acc == 0.0, 0.0, acc)
        kidx = j * kb + lax.broadcasted_iota(jnp.int32, (tq, kb), 1)
        x = jnp.where(kidx < n_adm, acc, -jnp.inf)
        sc_scr[j] = x
        sc_ref[0] = x

    if active is None:
        compute()
    else:
        pl.when(active)(compute)

        @pl.when(jnp.logical_not(active))
        def _():
            sc_ref[0] = jnp.full((tq, kb), -jnp.inf, F32)

    @pl.when(j == nk - 1)
    def _():
        kf = float(k_sel)

        rsz = min(tq, LANE)

        def count(make_pred):
            starts = range(0, tq, rsz)
            preds = [make_pred(lambda col, r0=r0: jnp.broadcast_to(col[r0:r0 + rsz], (rsz, LANE))) for r0 in starts]
            accs = []
            for r0, pred in zip(starts, preds):
                def blk(jj, acc, r0=r0, pred=pred):
                    for a in range(kb // LANE):
                        x = sc_scr[jj, r0:r0 + rsz, a * LANE:(a + 1) * LANE]
                        acc = acc + jnp.where(pred(x, jj * kb + a * LANE), 1.0, 0.0)
                    return acc
                accs.append(lax.fori_loop(0, n_blocks, blk, jnp.zeros((rsz, LANE), F32)))
            return jnp.sum(jnp.concatenate(accs, axis=0), axis=1, keepdims=True)

        def key_to_float(tu):
            cs = tu ^ INT_MIN
            fb = jnp.where(cs >= 0, cs, cs ^ 0x7FFFFFFF)
            return lax.bitcast_convert_type(fb, F32)

        def bit_body(it, tu):
            cand = tu | jnp.left_shift(jnp.int32(1), 31 - it)
            thr = key_to_float(cand)
            c = count(lambda widen: (lambda x, base, t=widen(thr): x >= t))
            return jnp.where(c >= kf, cand, tu)

        tu = lax.fori_loop(0, 32, bit_body, jnp.zeros((tq, 1), jnp.int32))
        full = n_adm >= k_sel
        tau = jnp.where(full, key_to_float(tu), -jnp.inf)
        c_gt = count(lambda widen: (lambda x, base, t=widen(tau): x > t))
        c_ge = count(lambda widen: (lambda x, base, t=widen(tau): x >= t))
        need = kf - c_gt
        tau_ref[0] = jnp.broadcast_to(tau, (tq, LANE))
        jb_ref[0] = jnp.full((tq, LANE), 2 ** 30, jnp.int32)
        surplus = jnp.max(jnp.where(full, (c_ge - c_gt) - need, 0.0))

        @pl.when(surplus > 0.5)
        def _():
            def tie_body(it, cut):
                cand = cut | jnp.left_shift(jnp.int32(1), idx_bits - 1 - it)

                def make_pred(widen):
                    t, cnd = widen(tau), widen(cand)
                    lane_idx = lax.broadcasted_iota(jnp.int32, (rsz, LANE), 1)
                    return lambda x, base: (x == t) & (base + lane_idx < cnd)
                c = count(make_pred)
                return jnp.where(c < need, cand, cut)

            cut = lax.fori_loop(0, idx_bits, tie_body, jnp.zeros((tq, 1), jnp.int32))
            jb_ref[0] = jnp.broadcast_to(cut, (tq, LANE))


def _dsa_select(z, qi_cb, kw_cb, ki_n, *, tq, kb, k_sel, n_valid, causal):
    b, t_q = z.shape[0], z.shape[1]
    t_k = ki_n.shape[1]
    nq, nk = t_q // tq, t_k // kb
    kmap = (lambda i, j: jnp.minimum(j, ((i + 1) * tq - 1) // kb)) if causal else (lambda i, j: j)
    body = functools.partial(_dsa_select_body, tq=tq, kb=kb, nk=nk, k_sel=k_sel, n_valid=n_valid,
                             causal=causal, idx_bits=max(1, (t_k - 1).bit_length()))
    return pl.pallas_call(
        body,
        out_shape=(jax.ShapeDtypeStruct((b, t_q, t_k), F32),
                   jax.ShapeDtypeStruct((b, t_q, LANE), F32),
                   jax.ShapeDtypeStruct((b, t_q, LANE), jnp.int32)),
        grid=(b, nq, nk),
        in_specs=[pl.BlockSpec((1, tq, IDX_HEADS * IDX_DIM), lambda bi, i, j: (bi, i, qi_cb)),
                  pl.BlockSpec((1, tq, LANE), lambda bi, i, j: (bi, i, kw_cb)),
                  pl.BlockSpec((1, kb, LANE), lambda bi, i, j: (bi, kmap(i, j), 0))],
        out_specs=(pl.BlockSpec((1, tq, kb), lambda bi, i, j: (bi, i, j)),
                   pl.BlockSpec((1, tq, LANE), lambda bi, i, j: (bi, i, 0)),
                   pl.BlockSpec((1, tq, LANE), lambda bi, i, j: (bi, i, 0))),
        scratch_shapes=[pltpu.VMEM((nk, tq, kb), F32)],
        compiler_params=_cparams(("parallel", "parallel", "arbitrary")),
        name="dsa_select",
    )(z, z, ki_n)


def _rwkv_body(zr_ref, zk_ref, zv_ref, zw_ref, za_ref, zg_ref,
               sr_ref, sk_ref, sv_ref, sw_ref, sa_ref, sg_ref,
               mr_ref, mk_ref, mv_ref, mw_ref, ma_ref, mg_ref,
               vec_ref, w2_ref, a2_ref, g2_ref, s0_ref,
               mix_ref, sfin_ref,
               s_scr, prev_scr, prevl_scr, prevg_scr, y_scr, *, tc, lc, nt, pp):
    t = pl.program_id(2)

    @pl.when(t == 0)
    def _():
        s_scr[...] = s0_ref[0]
        prev_scr[0:1, :] = sr_ref[0]
        prev_scr[1:2, :] = sk_ref[0]
        prev_scr[2:3, :] = sv_ref[0]
        prevl_scr[0:1, :] = sw_ref[0]
        prevl_scr[1:2, :] = sa_ref[0]
        prevg_scr[0:1, :] = sg_ref[0]

    row = lax.broadcasted_iota(jnp.int32, (tc, 1), 0)

    def shifted(z_ref, prow, mu_ref):
        z = z_ref[0]
        zp = jnp.where(row == 0, prow, pltpu.roll(z, 1, axis=0))
        return z + (zp - z) * mu_ref[...], z[tc - 1:tc, :]

    r, last_r = shifted(zr_ref, prev_scr[0:1, :], mr_ref)
    k, last_k = shifted(zk_ref, prev_scr[1:2, :], mk_ref)
    v, last_v = shifted(zv_ref, prev_scr[2:3, :], mv_ref)
    zw, last_w = shifted(zw_ref, prevl_scr[0:1, :], mw_ref)
    za, last_a = shifted(za_ref, prevl_scr[1:2, :], ma_ref)
    zg, last_g = shifted(zg_ref, prevg_scr[0:1, :], mg_ref)
    prev_scr[0:1, :] = last_r
    prev_scr[1:2, :] = last_k
    prev_scr[2:3, :] = last_v
    prevl_scr[0:1, :] = last_w
    prevl_scr[1:2, :] = last_a
    prevg_scr[0:1, :] = last_g

    w0, a0 = vec_ref[0:1, :], vec_ref[1:2, :]
    k_k, k_a, r_k = vec_ref[2:3, :], vec_ref[3:4, :], vec_ref[4:5, :]
    ln_w, ln_b = vec_ref[5:6, :], vec_ref[6:7, :]

    ones_blk = _head_block_matrix(LANE, 1.0)
    avg_blk = _head_block_matrix(LANE, 1.0 / HEAD_DIM)

    def per_head(x, blk):
        return jnp.concatenate([_dot_exact_rhs(x[:, i * LANE:(i + 1) * LANE], blk) for i in range(pp)], axis=1)

    xw = w0 + jnp.dot(jnp.tanh(zw).astype(BF16), w2_ref[...], preferred_element_type=F32)
    nx = -xw
    softplus = jnp.maximum(nx, 0.0) + jnp.log1p(jnp.exp(-jnp.abs(nx)))
    w_log = -softplus - 0.5
    lw = -jnp.exp(w_log)
    a = jax.nn.sigmoid(a0 + jnp.dot(za.astype(BF16), a2_ref[...], preferred_element_type=F32))
    g = jnp.dot(jax.nn.sigmoid(zg).astype(BF16), g2_ref[...], preferred_element_type=F32)
    kk = k * k_k
    kk = kk / jnp.maximum(jnp.sqrt(per_head(kk * kk, ones_blk)), 1e-12)
    k2 = k * (1.0 + (a - 1.0) * k_a)

    a_step = -kk
    b_step = kk * a

    lane = lax.broadcasted_iota(jnp.int32, (1, LANE), 1)
    m0 = lane < HEAD_DIM
    rr = lax.broadcasted_iota(jnp.int32, (lc, 2 * lc), 0)
    cc = lax.broadcasted_iota(jnp.int32, (lc, 2 * lc), 1)
    incl = jnp.where(cc < lc, cc, cc - lc) <= rr
    strict = (lax.broadcasted_iota(jnp.int32, (2 * lc, 2 * lc), 1)
              < lax.broadcasted_iota(jnp.int32, (2 * lc, 2 * lc), 0))
    tri = (lax.broadcasted_iota(jnp.int32, (lc, lc), 1)
           <= lax.broadcasted_iota(jnp.int32, (lc, lc), 0)).astype(BF16)
    nsteps = lc.bit_length() - 1
    nt_dims = ((1,), (1,))
    nn_dims = ((1,), (0,))
    tn_dims = ((0,), (0,))

    eye = (lax.broadcasted_iota(jnp.int32, (2 * lc, 2 * lc), 0)
           == lax.broadcasted_iota(jnp.int32, (2 * lc, 2 * lc), 1)).astype(F32)

    chunks = range(tc // lc)
    st = []
    for c in chunks:
        rows = slice(c * lc, (c + 1) * lc)
        lwc = lw[rows]
        cs = _dot_exact_lhs(tri, lwc)
        p_in = jnp.exp(-cs)
        at = a_step[rows] * jnp.exp(cs - lwc)
        bt = b_step[rows] * p_in
        kt = k2[rows] * p_in
        rt = r[rows] * jnp.exp(cs)
        vc = v[rows]
        p_last = jnp.exp(cs[lc - 1:lc, :])
        for pi in range(pp):
            cols = slice(pi * LANE, (pi + 1) * LANE)
            stack = lambda x: jnp.concatenate([jnp.where(m0, x[:, cols], 0.0),
                                               jnp.where(m0, 0.0, x[:, cols])], axis=0).astype(BF16)
            st.append(dict(c=c, pi=pi,
                           ar=jnp.concatenate([stack(at), rt[:, cols].astype(BF16)], axis=0),
                           bk=jnp.concatenate([stack(bt), stack(kt)], axis=0),
                           v_s=stack(vc), p_last=p_last[:, cols]))
    for d in st:
        gram = _dot_rw(d["ar"], d["bk"], nt_dims)
        d["t_p"] = jnp.where(strict, gram[0:2 * lc, 0:2 * lc], 0.0)
        d["w_ak"] = jnp.where(strict, gram[0:2 * lc, 2 * lc:4 * lc], 0.0).astype(BF16)
        d["w_rb"] = jnp.where(incl, gram[2 * lc:3 * lc, 0:2 * lc], 0.0).astype(BF16)
        d["w_rk"] = jnp.where(incl, gram[2 * lc:3 * lc, 2 * lc:4 * lc], 0.0).astype(BF16)
        d["minv"] = eye + d["t_p"]
    for d in st:
        d["t_p"] = _dot_rw(d["t_p"], d["t_p"], nn_dims)
    for step in range(nsteps - 1):
        for d in st:
            if step == nsteps - 2:
                d["minv"] = d["minv"] + _dot_rw(d["minv"], d["t_p"], nn_dims)
            else:
                both = _dot_rw(jnp.concatenate([d["minv"], d["t_p"]], axis=0), d["t_p"], nn_dims)
                d["minv"] = d["minv"] + both[0:2 * lc]
                d["t_p"] = both[2 * lc:4 * lc]
    for d in st:
        d["minv"] = d["minv"].astype(BF16)
        both = _dot_rw(jnp.concatenate([d["w_ak"], d["w_rk"]], axis=0), d["v_s"], nn_dims)
        d["wv"] = both[0:2 * lc]
        d["y_c"] = both[2 * lc:3 * lc]
    for d in st:
        both = _dot_rw(d["minv"], jnp.concatenate([d["ar"][0:2 * lc], d["wv"].astype(BF16)], axis=1), nn_dims)
        d["ma"] = both[:, 0:LANE]
        d["mwv"] = both[:, LANE:2 * LANE]
    for d in st:
        d["g"] = _dot_rw(d["ma"], d["bk"][0:2 * lc], tn_dims).astype(BF16)
        d["d"] = _dot_rw(jnp.concatenate([d["mwv"].astype(BF16), d["v_s"]], axis=0), d["bk"], tn_dims)
    s_cur = [s_scr[pi] for pi in range(pp)]
    for d in st:
        s_in = s_cur[d["pi"]]
        d["s0"] = s_in.astype(BF16)
        s_cur[d["pi"]] = ((s_in + _dot_rw(d["s0"], d["g"], nn_dims)) + d["d"]) * d["p_last"]
    for pi in range(pp):
        s_scr[pi] = s_cur[pi]
    for d in st:
        d["xr"] = _dot_rw(d["ar"], d["s0"], nt_dims)
    for d in st:
        d["u"] = (_dot_rw(d["minv"], d["xr"][0:2 * lc], nn_dims) + d["mwv"]).astype(BF16)
    for d in st:
        c, pi = d["c"], d["pi"]
        y_scr[c * lc:(c + 1) * lc, pi * LANE:(pi + 1) * LANE] = (
            (d["xr"][2 * lc:3 * lc] + _dot_rw(d["w_rb"], d["u"], nn_dims)) + d["y_c"])

    y = y_scr[...]
    mean = per_head(y, avg_blk)
    dev = y - mean
    var = per_head(dev * dev, avg_blk)
    yn = (dev * lax.rsqrt(var + RW_GN_EPS)) * ln_w + ln_b
    bonus = per_head((r * k2) * r_k, ones_blk) * v
    mix_ref[0] = ((yn + bonus) * g).astype(mix_ref.dtype)

    @pl.when(t == nt - 1)
    def _():
        sfin_ref[0] = s_scr[...]


def _rwkv(z, shift_prev, s0_pairs, mu, vecs, w2, a2, g2, d_model):
    b, t = z.shape[0], z.shape[1]
    npairs = d_model // LANE
    tc = min(t, 512)
    lc = min(CHUNK, t)
    nt = t // tc
    pp = 2 if tc // lc >= 4 else 8
    wp = pp * LANE
    ngroups = npairs // pp
    cb_w, cb_a, cb_g = 3 * npairs, 3 * npairs + 1, (3 * npairs + 2) // 2

    def zspec(width, cbf):
        return pl.BlockSpec((1, tc, width), lambda bi, p, ti: (bi, ti, cbf(p)))

    def sspec(width, cbf):
        return pl.BlockSpec((1, 1, width), lambda bi, p, ti: (bi, 0, cbf(p)))

    def mspec(width, cbf):
        return pl.BlockSpec((1, width), lambda bi, p, ti: (0, cbf(p)))

    cbfs = [(wp, lambda p: p), (wp, lambda p: ngroups + p), (wp, lambda p: 2 * ngroups + p),
            (LANE, lambda p: cb_w), (LANE, lambda p: cb_a), (2 * LANE, lambda p: cb_g)]
    in_specs = ([zspec(w, f) for w, f in cbfs] + [sspec(w, f) for w, f in cbfs] + [mspec(w, f) for w, f in cbfs]
                + [pl.BlockSpec((8, wp), lambda bi, p, ti: (0, p)),
                   pl.BlockSpec((LANE, wp), lambda bi, p, ti: (0, p)),
                   pl.BlockSpec((LANE, wp), lambda bi, p, ti: (0, p)),
                   pl.BlockSpec((2 * LANE, wp), lambda bi, p, ti: (0, p)),
                   pl.BlockSpec((1, pp, LANE, LANE), lambda bi, p, ti: (bi, p, 0, 0))])
    return pl.pallas_call(
        functools.partial(_rwkv_body, tc=tc, lc=lc, nt=nt, pp=pp),
        out_shape=(jax.ShapeDtypeStruct((b, t, d_model), BF16),
                   jax.ShapeDtypeStruct((b, npairs, LANE, LANE), F32)),
        grid=(b, ngroups, nt),
        in_specs=in_specs,
        out_specs=(pl.BlockSpec((1, tc, wp), lambda bi, p, ti: (bi, ti, p)),
                   pl.BlockSpec((1, pp, LANE, LANE), lambda bi, p, ti: (bi, p, 0, 0))),
        scratch_shapes=[pltpu.VMEM((pp, LANE, LANE), F32),
                        pltpu.VMEM((8, wp), F32),
                        pltpu.VMEM((8, LANE), F32),
                        pltpu.VMEM((8, 2 * LANE), F32),
                        pltpu.VMEM((tc, wp), F32)],
        compiler_params=_cparams(("parallel", "parallel", "arbitrary")),
        name="rwkv7",
    )(*([z] * 6), *([shift_prev] * 6), *([mu] * 6), vecs, w2, a2, g2, s0_pairs)


def _pad_cols(x, segments):
    parts = []
    for start, width, padded in segments:
        seg = x[..., start:start + width]
        if padded > width:
            seg = jnp.concatenate([seg, jnp.zeros(seg.shape[:-1] + (padded - width,), seg.dtype)], axis=-1)
        parts.append(seg)
    return jnp.concatenate(parts, axis=-1)


def _pad_rows(x, padded):
    return jnp.concatenate([x, jnp.zeros((padded - x.shape[0],) + x.shape[1:], x.dtype)], axis=0)


def _pairs_from_heads(s):
    b, h = s.shape[0], s.shape[1]
    s = s.reshape(b, h // 2, 2, HEAD_DIM, HEAD_DIM)
    z = jnp.zeros_like(s[:, :, 0])
    top = jnp.concatenate([s[:, :, 0], z], axis=-1)
    bot = jnp.concatenate([z, s[:, :, 1]], axis=-1)
    return jnp.concatenate([top, bot], axis=-2)


def _heads_from_pairs(sp):
    b, npairs = sp.shape[0], sp.shape[1]
    s = jnp.stack([sp[:, :, :HEAD_DIM, :HEAD_DIM], sp[:, :, HEAD_DIM:, HEAD_DIM:]], axis=2)
    return s.reshape(b, 2 * npairs, HEAD_DIM, HEAD_DIM)


def _tile_gain(g, width):
    return jnp.tile(g.astype(F32), width // HEAD_DIM).reshape(1, width)


def _mem_attend(z, memq_cb, km, vm, q_gain):
    t_q = z.shape[1]
    wq = MEM_HEADS * HEAD_DIM
    tq = min(t_q, 512)
    no_sink = jnp.full((1, wq), -jnp.inf, F32)
    return _flash(z, memq_cb, wq, km, 0, vm, 0, _tile_gain(q_gain, LANE), no_sink,
                  mode="all", tq=tq, kb=km.shape[1], group=1, n_valid=km.shape[1])


def _conv_ffn(x2d, b, t, gain, w_up, conv_w, w_down, prev):
    act, u_last = _up_conv(x2d, gain, w_up, conv_w, prev, t)
    return _mm_res([act], [w_down], x2d), u_last


def kernel(x_prompt, x_sample, state_rwkv_wkv, state_rwkv_shift, cache_swa_k, cache_swa_v, cache_dsa_k, cache_dsa_v, cache_dsa_idx_k, cache_mem_k, cache_mem_v, state_ffn_conv, mem_prompt, attn_norm, ffn_norm, mem_norm, mem_w_kv, mem_q_norm, mem_k_norm, a_w_in, a_mu, a_w0, a_w2, a_a0, a_a2, a_g2, a_k_k, a_k_a, a_r_k, a_ln_w, a_ln_b, a_w_out, b_w_in, b_q_norm, b_k_norm, b_sink, b_w_out, c_w_in, c_q_norm, c_k_norm, c_idx_k_norm, c_w_out, ffn_w_up, ffn_conv, ffn_w_down):
    bp, t, d = x_prompt.shape
    bd, s_len = x_sample.shape[:2]
    depth = attn_norm.shape[0]
    win_rows = cache_swa_k.shape[2]
    past = cache_dsa_k.shape[2] if cache_dsa_k.shape[0] else 0
    d_ff = ffn_w_down.shape[1]
    mem_tokens = mem_prompt.shape[1]
    q_cols = d
    kv_cols = ATT_KV_HEADS * HEAD_DIM
    memq_cols = MEM_HEADS * HEAD_DIM
    att_group = (d // HEAD_DIM) // ATT_KV_HEADS
    dec_lora = a_w2.shape[1]
    a_lora = a_a2.shape[1]
    g_lora = a_g2.shape[1]
    rw_cols = 3 * d + dec_lora + a_lora + g_lora
    k_sel_p = min(TOPK_MAX, t // 4)
    k_sel_s = min(TOPK_MAX, (past + s_len) // 4)
    assert g_lora == 2 * LANE and dec_lora <= LANE and a_lora <= LANE

    xp = x_prompt.reshape(bp * t, d)
    xs = x_sample.reshape(bd * s_len, d)

    rw_segments = [(0, 3 * d, 3 * d), (3 * d, dec_lora, LANE), (3 * d + dec_lora, a_lora, LANE),
                   (3 * d + dec_lora + a_lora, g_lora, g_lora)]
    rw_padded = 3 * d + 2 * LANE + g_lora
    o_qi = q_cols + 2 * kv_cols
    o_ki = o_qi + IDX_HEADS * IDX_DIM
    c_cols = o_ki + IDX_DIM + IDX_HEADS

    outs = {k: [] for k in ("p_rw_wkv", "p_rw_sh", "p_sw_k", "p_sw_v", "p_ds_k", "p_ds_v", "p_ds_i", "p_mk",
                            "p_mv", "p_cv", "s_rw_wkv", "s_rw_sh", "s_sw_k", "s_sw_v", "s_ds_k", "s_ds_v",
                            "s_ds_i", "s_cv")}

    def unpad_rw(row):
        return jnp.concatenate([row[..., :3 * d], row[..., 3 * d:3 * d + dec_lora],
                                row[..., 3 * d + LANE:3 * d + LANE + a_lora],
                                row[..., 3 * d + 2 * LANE:3 * d + 2 * LANE + g_lora]], axis=-1)

    for i in range(depth):
        kind, j = i % 3, i // 3
        if kind == 0:
            w_in = jnp.concatenate([_pad_cols(a_w_in[j], rw_segments), a_w_in[j][:, rw_cols:]], axis=1).astype(BF16)
            memq_cb = rw_padded // memq_cols
            zp = _mm_norm(xp, attn_norm[i], w_in).reshape(bp, t, -1)
            zs = _mm_norm(xs, attn_norm[i], w_in).reshape(bd, s_len, -1)
            mu = _pad_cols(a_mu[j].reshape(1, -1), rw_segments)
            mu = jnp.concatenate([mu, jnp.zeros((1, memq_cols), F32)], axis=1)
            vecs = jnp.stack([a_w0[j], a_a0[j], a_k_k[j], a_k_a[j], a_r_k[j].reshape(-1), a_ln_w[j], a_ln_b[j],
                              jnp.zeros((d,), F32)], axis=0)
            w2 = _pad_rows(a_w2[j], LANE).astype(BF16)
            a2 = _pad_rows(a_a2[j], LANE).astype(BF16)
            g2 = a_g2[j].astype(BF16)
            sh_p = jnp.zeros((bp, 1, zp.shape[-1]), F32)
            st_p = jnp.zeros((bp, d // LANE, LANE, LANE), F32)
            sh_s = _pad_cols(state_rwkv_shift[j], rw_segments)
            sh_s = jnp.concatenate([sh_s, jnp.zeros((bd, memq_cols), F32)], axis=1).reshape(bd, 1, -1)
            st_s = _pairs_from_heads(state_rwkv_wkv[j])
            mp, stp = _rwkv(zp, sh_p, st_p, mu, vecs, w2, a2, g2, d)
            ms, sts = _rwkv(zs, sh_s, st_s, mu, vecs, w2, a2, g2, d)
            outs["p_rw_sh"].append(unpad_rw(zp[:, -1]))
            outs["p_rw_wkv"].append(_heads_from_pairs(stp))
            outs["s_rw_sh"].append(unpad_rw(zs[:, -1]))
            outs["s_rw_wkv"].append(_heads_from_pairs(sts))
            w_out = a_w_out[j]
        elif kind == 1:
            w_in = b_w_in[j].astype(BF16)
            memq_cb = (q_cols + 2 * kv_cols) // memq_cols
            k_cb, v_cb = q_cols // kv_cols, q_cols // kv_cols + 1
            zp = _mm_norm(xp, attn_norm[i], w_in).reshape(bp, t, -1)
            zs = _mm_norm(xs, attn_norm[i], w_in).reshape(bd, s_len, -1)
            kgain = _tile_gain(b_k_norm[j], kv_cols)
            qgain = _tile_gain(b_q_norm[j], LANE)
            sink = jnp.repeat(b_sink[j].astype(F32), HEAD_DIM).reshape(1, q_cols)
            knp = _headnorm(zp.reshape(bp * t, -1), k_cb, kv_cols, kgain).reshape(bp, t, kv_cols)
            mp = _flash(zp, 0, q_cols, knp, 0, zp, v_cb, qgain, sink, mode="band", tq=WINDOW, kb=WINDOW,
                        group=att_group)
            outs["p_sw_k"].append(knp[:, t - win_rows:].reshape(bp, win_rows, ATT_KV_HEADS, HEAD_DIM))
            outs["p_sw_v"].append(zp[:, t - win_rows:, q_cols + kv_cols:q_cols + 2 * kv_cols]
                                  .reshape(bp, win_rows, ATT_KV_HEADS, HEAD_DIM))
            kns = _headnorm(zs.reshape(bd * s_len, -1), k_cb, kv_cols, kgain).reshape(bd, s_len, kv_cols)
            vs_new = zs[:, :, q_cols + kv_cols:q_cols + 2 * kv_cols]
            k_all = jnp.concatenate([cache_swa_k[j].reshape(bd, win_rows, kv_cols), kns], axis=1)
            v_all = jnp.concatenate([cache_swa_v[j].reshape(bd, win_rows, kv_cols), vs_new], axis=1)
            n_keys = win_rows + s_len
            n_pad = -(-n_keys // LANE) * LANE
            pad = jnp.zeros((bd, n_pad - n_keys, kv_cols), F32)
            ms = _flash(zs, 0, q_cols, jnp.concatenate([k_all, pad], axis=1), 0,
                        jnp.concatenate([v_all, pad], axis=1), 0, qgain, sink, mode="all", tq=s_len, kb=n_pad,
                        group=att_group, n_valid=n_keys)
            outs["s_sw_k"].append(k_all[:, n_keys - win_rows:].reshape(bd, win_rows, ATT_KV_HEADS, HEAD_DIM))
            outs["s_sw_v"].append(v_all[:, n_keys - win_rows:].reshape(bd, win_rows, ATT_KV_HEADS, HEAD_DIM))
            w_out = b_w_out[j]
        else:
            wc = c_w_in[j]
            w_in = jnp.concatenate([wc[:, :o_ki], wc[:, c_cols:],
                                    _pad_cols(wc, [(o_ki, IDX_DIM + IDX_HEADS, LANE)])], axis=1).astype(BF16)
            memq_cb = o_ki // memq_cols
            kw_cb = (o_ki + memq_cols) // LANE
            k_cb, v_cb = q_cols // kv_cols, q_cols // kv_cols + 1
            qi_cb = o_qi // (IDX_HEADS * IDX_DIM)
            zp = _mm_norm(xp, attn_norm[i], w_in).reshape(bp, t, -1)
            zs = _mm_norm(xs, attn_norm[i], w_in).reshape(bd, s_len, -1)
            kgain = _tile_gain(c_k_norm[j], kv_cols)
            qgain = _tile_gain(c_q_norm[j], LANE)
            igain = jnp.concatenate([c_idx_k_norm[j].astype(F32), jnp.zeros((LANE - IDX_DIM,), F32)]).reshape(1, LANE)
            no_sink = jnp.full((1, q_cols), -jnp.inf, F32)
            knp = _headnorm(zp.reshape(bp * t, -1), k_cb, kv_cols, kgain).reshape(bp, t, kv_cols)
            kip = _headnorm(zp.reshape(bp * t, -1), kw_cb, LANE, igain).reshape(bp, t, LANE)
            tq = min(t, DSA_TQ)
            kb = min(t, DSA_KB)
            sc, tau, cut = _dsa_select(zp, qi_cb, kw_cb, kip, tq=tq, kb=kb, k_sel=k_sel_p, n_valid=t, causal=True)
            mp = _flash(zp, 0, q_cols, knp, 0, zp, v_cb, qgain, no_sink, mode="dsa", tq=tq, kb=kb, group=att_group,
                        n_valid=t, causal=True, scores=sc, tau=tau, jb=cut)
            outs["p_ds_k"].append(knp.reshape(bp, t, ATT_KV_HEADS, HEAD_DIM))
            outs["p_ds_v"].append(zp[:, :, q_cols + kv_cols:q_cols + 2 * kv_cols].reshape(bp, t, ATT_KV_HEADS, HEAD_DIM))
            outs["p_ds_i"].append(kip[:, :, :IDX_DIM])
            kns = _headnorm(zs.reshape(bd * s_len, -1), k_cb, kv_cols, kgain).reshape(bd, s_len, kv_cols)
            kis = _headnorm(zs.reshape(bd * s_len, -1), kw_cb, LANE, igain).reshape(bd, s_len, LANE)
            vs_new = zs[:, :, q_cols + kv_cols:q_cols + 2 * kv_cols]
            n_keys = past + s_len
            n_pad = -(-n_keys // LANE) * LANE
            zpad = lambda w: jnp.zeros((bd, n_pad - n_keys, w), F32)
            k_all = jnp.concatenate([cache_dsa_k[j].reshape(bd, past, kv_cols), kns, zpad(kv_cols)], axis=1)
            v_all = jnp.concatenate([cache_dsa_v[j].reshape(bd, past, kv_cols), vs_new, zpad(kv_cols)], axis=1)
            ki_cache = jnp.concatenate([cache_dsa_idx_k[j], jnp.zeros((bd, past, LANE - IDX_DIM), F32)], axis=-1)
            ki_all = jnp.concatenate([ki_cache, kis, zpad(LANE)], axis=1)
            sc, tau, cut = _dsa_select(zs, qi_cb, kw_cb, ki_all, tq=s_len, kb=n_pad, k_sel=k_sel_s, n_valid=n_keys,
                                       causal=False)
            ms = _flash(zs, 0, q_cols, k_all, 0, v_all, 0, qgain, no_sink, mode="dsa", tq=s_len, kb=n_pad,
                        group=att_group, n_valid=n_keys, causal=False, scores=sc, tau=tau, jb=cut)
            outs["s_ds_k"].append(kns.reshape(bd, s_len, ATT_KV_HEADS, HEAD_DIM))
            outs["s_ds_v"].append(vs_new.reshape(bd, s_len, ATT_KV_HEADS, HEAD_DIM))
            outs["s_ds_i"].append(kis[:, :, :IDX_DIM])
            w_out = c_w_out[j]

        kv_mem = _mm_norm(mem_prompt.reshape(bp * mem_tokens, d), mem_norm[i], mem_w_kv[i].astype(BF16))
        km_p = _headnorm(kv_mem, 0, memq_cols, _tile_gain(mem_k_norm[i], memq_cols)).reshape(bp, mem_tokens, memq_cols)
        vm_p = kv_mem[:, memq_cols:].reshape(bp, mem_tokens, memq_cols)
        outs["p_mk"].append(km_p.reshape(bp, mem_tokens, MEM_HEADS, HEAD_DIM))
        outs["p_mv"].append(vm_p.reshape(bp, mem_tokens, MEM_HEADS, HEAD_DIM))
        mo_p = _mem_attend(zp, memq_cb, km_p, vm_p, mem_q_norm[i])
        mo_s = _mem_attend(zs, memq_cb, cache_mem_k[i].reshape(bd, mem_tokens, memq_cols),
                           cache_mem_v[i].reshape(bd, mem_tokens, memq_cols), mem_q_norm[i])
        w_mix, w_mem = w_out[:d].astype(BF16), w_out[d:].astype(BF16)
        xp = _mm_res([mp.reshape(bp * t, d), mo_p.reshape(bp * t, memq_cols)], [w_mix, w_mem], xp)
        xs = _mm_res([ms.reshape(bd * s_len, d), mo_s.reshape(bd * s_len, memq_cols)], [w_mix, w_mem], xs)

        w_up, w_down = ffn_w_up[i].astype(BF16), ffn_w_down[i].astype(BF16)
        xp, cp = _conv_ffn(xp, bp, t, ffn_norm[i], w_up, ffn_conv[i], w_down,
                           jnp.zeros((bp, CONV_W - 1, 2 * d_ff), F32))
        xs, cs = _conv_ffn(xs, bd, s_len, ffn_norm[i], w_up, ffn_conv[i], w_down, state_ffn_conv[i])
        outs["p_cv"].append(cp)
        outs["s_cv"].append(cs)

    st = jnp.stack
    order = ("p_rw_wkv", "p_rw_sh", "p_sw_k", "p_sw_v", "p_ds_k", "p_ds_v", "p_ds_i", "p_mk", "p_mv", "p_cv",
             "s_rw_wkv", "s_rw_sh", "s_sw_k", "s_sw_v", "s_ds_k", "s_ds_v", "s_ds_i", "s_cv")
    return (xp.reshape(bp, t, d), xs.reshape(bd, s_len, d)) + tuple(st(outs[k]) for k in order)
```

```python
import functools

import jax
import jax.numpy as jnp
from jax import lax
from jax.experimental import pallas as pl
from jax.experimental.pallas import tpu as pltpu

F32 = jnp.float32
BF16 = jnp.bfloat16

HEAD_DIM = 64
CHUNK = 64
NORM_EPS = 1e-6
RW_GN_EPS = HEAD_DIM * 1e-5
ATT_KV_HEADS = 4
WINDOW = 128
IDX_HEADS = 8
IDX_DIM = 64
TOPK_MAX = 256
MEM_HEADS = 4
CONV_W = 3

LANE = 128
VMEM_LIMIT = 52 * 1024 * 1024
NEG_BIG = -(2.0 ** 100)
INT_MIN = -2147483648
MM_ROWS = 1024
DSA_TQ = 256
DSA_KB = 512


def _cparams(sem, vmem=VMEM_LIMIT):
    return pltpu.CompilerParams(dimension_semantics=sem, vmem_limit_bytes=vmem)


def _split3(a):
    a1 = a.astype(BF16)
    r1 = a - a1.astype(F32)
    a2 = r1.astype(BF16)
    r2 = r1 - a2.astype(F32)
    return a1, a2, r2.astype(BF16)


def _dot_exact_rhs(a, e):
    a1, a2, a3 = _split3(a)
    d = lambda x: jnp.dot(x, e, preferred_element_type=F32)
    return (d(a3) + d(a2)) + d(a1)


def _dot_exact_lhs(e, a):
    a1, a2, a3 = _split3(a)
    d = lambda x: jnp.dot(e, x, preferred_element_type=F32)
    return (d(a3) + d(a2)) + d(a1)


def _head_block_matrix(width, value):
    r = lax.broadcasted_iota(jnp.int32, (width, width), 0) // HEAD_DIM
    c = lax.broadcasted_iota(jnp.int32, (width, width), 1) // HEAD_DIM
    return jnp.where(r == c, value, 0.0).astype(BF16)


def _dot_rw(a, b, dims):
    return lax.dot_general(a.astype(BF16), b.astype(BF16), (dims, ((), ())), preferred_element_type=F32)


def _pick_tile(n, cap):
    best = None
    for t in range(LANE, min(n, cap) + 1, LANE):
        if n % t == 0:
            best = t
    assert best is not None, n
    return best


def _mm_norm_body(x_ref, g_ref, w_ref, o_ref, xn_ref):
    @pl.when(pl.program_id(1) == 0)
    def _():
        x = x_ref[...]
        ms = jnp.mean(x * x, axis=-1, keepdims=True)
        xn_ref[...] = ((x * lax.rsqrt(ms + NORM_EPS)) * g_ref[...]).astype(BF16)

    o_ref[...] = jnp.dot(xn_ref[...], w_ref[...], preferred_element_type=F32)


def _mm_norm(x, gain, w):
    m, k = x.shape
    n = w.shape[1]
    tm = min(m, MM_ROWS)
    tn = _pick_tile(n, 1536)
    return pl.pallas_call(
        _mm_norm_body,
        out_shape=jax.ShapeDtypeStruct((m, n), F32),
        grid=(m // tm, n // tn),
        in_specs=[pl.BlockSpec((tm, k), lambda i, j: (i, 0)),
                  pl.BlockSpec((1, k), lambda i, j: (0, 0)),
                  pl.BlockSpec((k, tn), lambda i, j: (0, j))],
        out_specs=pl.BlockSpec((tm, tn), lambda i, j: (i, j)),
        scratch_shapes=[pltpu.VMEM((tm, k), BF16)],
        compiler_params=_cparams(("parallel", "arbitrary")),
        name="mm_norm",
    )(x, gain.reshape(1, k), w)


def _mm_res_body(*refs, n_lhs):
    lhs = refs[:n_lhs]
    ws = refs[n_lhs:2 * n_lhs]
    r_ref, o_ref = refs[2 * n_lhs], refs[2 * n_lhs + 1]
    acc = jnp.dot(lhs[0][...], ws[0][...], preferred_element_type=F32)
    for a, w in zip(lhs[1:], ws[1:]):
        acc = acc + jnp.dot(a[...], w[...], preferred_element_type=F32)
    o_ref[...] = r_ref[...] + acc


def _mm_res(lhs_list, w_list, res):
    m, n = res.shape
    ktot = sum(a.shape[1] for a in lhs_list)
    tm = min(m, MM_ROWS)
    tn = _pick_tile(n, 1024 if ktot <= 3072 else 512)
    n_lhs = len(lhs_list)
    in_specs = [pl.BlockSpec((tm, a.shape[1]), lambda i, j: (i, 0)) for a in lhs_list]
    in_specs += [pl.BlockSpec((w.shape[0], tn), lambda i, j: (0, j)) for w in w_list]
    in_specs += [pl.BlockSpec((tm, tn), lambda i, j: (i, j))]
    return pl.pallas_call(
        functools.partial(_mm_res_body, n_lhs=n_lhs),
        out_shape=jax.ShapeDtypeStruct((m, n), F32),
        grid=(m // tm, n // tn),
        in_specs=in_specs,
        out_specs=pl.BlockSpec((tm, tn), lambda i, j: (i, j)),
        compiler_params=_cparams(("parallel", "arbitrary")),
        name="mm_res",
    )(*lhs_list, *w_list, res)


def _headnorm_body(x_ref, g_ref, o_ref, *, width):
    avg = _head_block_matrix(LANE, 1.0 / HEAD_DIM)
    for c in range(width // LANE):
        x = x_ref[:, c * LANE:(c + 1) * LANE]
        ms = _dot_exact_rhs(x * x, avg)
        o_ref[:, c * LANE:(c + 1) * LANE] = (x * lax.rsqrt(ms + NORM_EPS)) * g_ref[:, c * LANE:(c + 1) * LANE]


def _headnorm(x, col_block, width, gain_row):
    m = x.shape[0]
    tm = min(m, 1024)
    return pl.pallas_call(
        functools.partial(_headnorm_body, width=width),
        out_shape=jax.ShapeDtypeStruct((m, width), F32),
        grid=(m // tm,),
        in_specs=[pl.BlockSpec((tm, width), lambda i: (i, col_block)),
                  pl.BlockSpec((1, width), lambda i: (0, 0))],
        out_specs=pl.BlockSpec((tm, width), lambda i: (i, 0)),
        compiler_params=_cparams(("parallel",)),
        name="headnorm",
    )(x, gain_row)


def _up_conv_body(x_ref, g_ref, wa_ref, wb_ref, pa_ref, pb_ref, cwa_ref, cwb_ref, o_ref, la_ref, lb_ref,
                  xn_ref, ca_ref, cb_ref, *, tm, tiles_per_batch, bpt):
    i = pl.program_id(0)
    j = pl.program_id(1)
    rpb = tm // bpt

    @pl.when(j == 0)
    def _():
        x = x_ref[...]
        ms = jnp.mean(x * x, axis=-1, keepdims=True)
        xn_ref[...] = ((x * lax.rsqrt(ms + NORM_EPS)) * g_ref[...]).astype(BF16)

    if bpt == 1:
        @pl.when(i % tiles_per_batch == 0)
        def _():
            ca_ref[j] = pa_ref[0]
            cb_ref[j] = pb_ref[0]

    xn = xn_ref[...]
    row = lax.broadcasted_iota(jnp.int32, o_ref.shape, 0)
    off = row if bpt == 1 else row % rpb
    if bpt > 1:
        pick = (lax.broadcasted_iota(jnp.int32, (tm, bpt), 0) // rpb
                == lax.broadcasted_iota(jnp.int32, (tm, bpt), 1)).astype(BF16)

    def conv(w_ref, p_ref, c_ref, cw_ref, last_ref):
        u = jnp.dot(xn, w_ref[...], preferred_element_type=F32)
        if bpt == 1:
            car = c_ref[j]
            c0, c1 = car[0:1, :], car[1:2, :]
            c_ref[j] = u[tm - 2:tm, :]
            last_ref[0] = u[tm - 2:tm, :]
        else:
            c0 = _dot_exact_lhs(pick, p_ref[:, 0, :])
            c1 = _dot_exact_lhs(pick, p_ref[:, 1, :])
            for bi in range(bpt):
                last_ref[bi] = u[(bi + 1) * rpb - 2:(bi + 1) * rpb, :]
        u1 = jnp.where(off == 0, c1, pltpu.roll(u, 1, axis=0))
        u2 = jnp.where(off == 0, c0, jnp.where(off == 1, c1, pltpu.roll(u, 2, axis=0)))
        return (u2 * cw_ref[0:1, :] + u1 * cw_ref[1:2, :]) + u * cw_ref[2:3, :]

    a = conv(wa_ref, pa_ref, ca_ref, cwa_ref, la_ref)
    b = conv(wb_ref, pb_ref, cb_ref, cwb_ref, lb_ref)
    o_ref[...] = ((a * jax.nn.sigmoid(a)) * b).astype(o_ref.dtype)


def _up_conv(x, gain, w_up, conv_w, prev, t):
    m, k = x.shape
    f = w_up.shape[1] // 2
    b = m // t
    tm = min(m, MM_ROWS)
    bpt = max(1, tm // t)
    assert tm % t == 0 or t % tm == 0
    tn = _pick_tile(f, 512)
    nj = f // tn
    tpb = max(1, t // tm)
    act, la, lb = pl.pallas_call(
        functools.partial(_up_conv_body, tm=tm, tiles_per_batch=tpb, bpt=bpt),
        out_shape=(jax.ShapeDtypeStruct((m, f), BF16),
                   jax.ShapeDtypeStruct((b, CONV_W - 1, f), F32),
                   jax.ShapeDtypeStruct((b, CONV_W - 1, f), F32)),
        grid=(m // tm, nj),
        in_specs=[pl.BlockSpec((tm, k), lambda i, j: (i, 0)),
                  pl.BlockSpec((1, k), lambda i, j: (0, 0)),
                  pl.BlockSpec((k, tn), lambda i, j: (0, j)),
                  pl.BlockSpec((k, tn), lambda i, j: (0, nj + j)),
                  pl.BlockSpec((bpt, CONV_W - 1, tn), lambda i, j: (i // tpb, 0, j)),
                  pl.BlockSpec((bpt, CONV_W - 1, tn), lambda i, j: (i // tpb, 0, nj + j)),
                  pl.BlockSpec((CONV_W, tn), lambda i, j: (0, j)),
                  pl.BlockSpec((CONV_W, tn), lambda i, j: (0, nj + j))],
        out_specs=(pl.BlockSpec((tm, tn), lambda i, j: (i, j)),
                   pl.BlockSpec((bpt, CONV_W - 1, tn), lambda i, j: (i // tpb, 0, j)),
                   pl.BlockSpec((bpt, CONV_W - 1, tn), lambda i, j: (i // tpb, 0, j))),
        scratch_shapes=[pltpu.VMEM((tm, k), BF16),
                        pltpu.VMEM((nj, CONV_W - 1, tn), F32),
                        pltpu.VMEM((nj, CONV_W - 1, tn), F32)],
        compiler_params=_cparams(("arbitrary", "arbitrary")),
        name="up_conv_gate",
    )(x, gain.reshape(1, k), w_up, w_up, prev, prev, conv_w, conv_w)
    return act, jnp.concatenate([la, lb], axis=-1)


def _causal_blocks(i, tq, kb):
    return ((i + 1) * tq + kb - 1) // kb


def _fold_steps(nq, tq, kb):
    if nq % 2:
        return None
    return max(_causal_blocks(r, tq, kb) + _causal_blocks(nq - 1 - r, tq, kb) for r in range(nq // 2))


def _fold_step(r, step, nq, tq, kb):
    n_lo = _causal_blocks(r, tq, kb)
    hi = nq - 1 - r
    in_lo = step < n_lo
    i = jnp.where(in_lo, r, hi)
    last_blk = _causal_blocks(i, tq, kb) - 1
    j = jnp.minimum(jnp.where(in_lo, step, step - n_lo), last_blk)
    active = step < n_lo + _causal_blocks(hi, tq, kb)
    return i, j, active & (jnp.where(in_lo, step, step - n_lo) == 0), active & (j == last_blk), active


def _flash_body(*refs, mode, tq, kb, wq, group, n_valid, nk, causal, fold_nq):
    q_ref, k_ref, v_ref, qg_ref, sink_ref = refs[:5]
    pos = 5
    if mode == "dsa":
        sc_ref, tau_ref, jb_ref = refs[pos:pos + 3]
        pos += 3
    o_ref = refs[pos]
    qn_scr, acc_scr, m_scr, l_scr, bias_scr = refs[pos + 1:]

    if fold_nq:
        i, j, is_first, is_last, fold_active = _fold_step(pl.program_id(1), pl.program_id(2), fold_nq, tq, kb)
    else:
        i, j = pl.program_id(1), pl.program_id(2)
        is_first, is_last = j == 0, j == nk - 1
    npairs = wq // LANE
    pairs_per_unit = group // 2 if group > 1 else 1
    lane = lax.broadcasted_iota(jnp.int32, (1, LANE), 1)
    lo_half = lane < HEAD_DIM

    @pl.when(is_first)
    def _():
        avg = _head_block_matrix(LANE, 1.0 / HEAD_DIM)
        for p in range(npairs):
            x = q_ref[0, :, p * LANE:(p + 1) * LANE]
            ms = _dot_exact_rhs(x * x, avg)
            qn = ((x * lax.rsqrt(ms + NORM_EPS)) * qg_ref[...]) * (HEAD_DIM ** -0.5)
            qn_scr[p] = qn.astype(BF16)
        acc_scr[...] = jnp.zeros(acc_scr.shape, F32)
        l_scr[...] = jnp.zeros(l_scr.shape, F32)
        m_scr[...] = jnp.full(m_scr.shape, NEG_BIG, F32)

    if mode == "band":
        kblk = i - WINDOW // kb + j
        active = kblk >= 0
    elif mode == "dsa" and causal:
        kblk = j
        active = fold_active if fold_nq else j * kb < (i + 1) * tq
    else:
        kblk = j
        active = None

    def step():
        kidx = kblk * kb + lax.broadcasted_iota(jnp.int32, (tq, kb), 1)
        qpos = i * tq + lax.broadcasted_iota(jnp.int32, (tq, kb), 0)
        if mode == "band":
            qchunk = qpos // CHUNK
            sel = (kidx >= (qchunk - WINDOW // CHUNK) * CHUNK) & (kidx < (qchunk + 1) * CHUNK) & (kidx >= 0)
        elif mode == "dsa":
            x = sc_ref[0]
            tau = tau_ref[0][:, 0:1]
            jb = jb_ref[0][:, 0:1]
            adm = kidx < ((qpos // CHUNK + 1) * CHUNK if causal else n_valid)
            sel = adm & ((x > tau) | ((x == tau) & (kidx <= jb)))
        else:
            sel = kidx < n_valid
        bias_scr[...] = jnp.where(sel, 0.0, NEG_BIG).astype(BF16)

        ppu = pairs_per_unit
        for u in range(npairs // ppu):
            ps = slice(u * ppu, (u + 1) * ppu)
            s = lax.dot_general(qn_scr[ps].reshape(ppu * tq, LANE), k_ref[0, 0, u], (((1,), (1,)), ((), ())),
                                preferred_element_type=F32).reshape(ppu, tq, 2 * kb)
            bias = bias_scr[...][None]
            m_old = m_scr[ps]
            s0 = s[:, :, 0:kb].astype(BF16) + bias
            s1 = s[:, :, kb:2 * kb].astype(BF16) + bias
            mn0 = jnp.maximum(m_old[:, :, 0:1], jnp.max(s0, axis=2, keepdims=True).astype(F32))
            mn1 = jnp.maximum(m_old[:, :, HEAD_DIM:HEAD_DIM + 1], jnp.max(s1, axis=2, keepdims=True).astype(F32))
            p0 = jnp.exp(s0 - mn0.astype(BF16))
            p1 = jnp.exp(s1 - mn1.astype(BF16))
            pcat = jnp.concatenate([p0, p1], axis=2).reshape(ppu * tq, 2 * kb)
            pv = jnp.dot(pcat, v_ref[0, 0, u], preferred_element_type=F32)
            pv = pv.reshape(ppu, tq, 2 * LANE)
            mn = jnp.where(lo_half, mn0, mn1)
            alpha = jnp.exp(m_old - mn)
            acc_scr[ps] = acc_scr[ps] * alpha + pv[:, :, 0:LANE]
            l_scr[ps] = l_scr[ps] * alpha + pv[:, :, LANE:2 * LANE]
            m_scr[ps] = mn

    if active is None:
        step()
    else:
        pl.when(active)(step)

    @pl.when(is_last)
    def _():
        for p in range(npairs):
            den = l_scr[p] + jnp.exp(sink_ref[:, p * LANE:(p + 1) * LANE] - m_scr[p])
            o_ref[0, :, p * LANE:(p + 1) * LANE] = (acc_scr[p] / den).astype(o_ref.dtype)


def _pair_operands_body(k_ref, v_ref, kab_ref, vab_ref, *, kb, group, n_units):
    lane = lax.broadcasted_iota(jnp.int32, (1, LANE), 1)
    lo_half = lane < HEAD_DIM
    ones_lo = jnp.broadcast_to(jnp.where(lo_half, 1.0, 0.0), (kb, LANE)).astype(BF16)
    ones_hi = jnp.broadcast_to(jnp.where(lo_half, 0.0, 1.0), (kb, LANE)).astype(BF16)
    for u in range(n_units):
        if group > 1:
            tile_idx, half = u // 2, u % 2
        else:
            tile_idx, half = u, None
        for src, dst in ((k_ref, kab_ref), (v_ref, vab_ref)):
            tile = src[0, :, tile_idx * LANE:(tile_idx + 1) * LANE]
            if half is None:
                a_part = jnp.where(lo_half, tile, 0.0)
                b_part = jnp.where(lo_half, 0.0, tile)
            elif half == 0:
                a_part = jnp.where(lo_half, tile, 0.0)
                b_part = pltpu.roll(a_part, HEAD_DIM, axis=1)
            else:
                b_part = jnp.where(lo_half, 0.0, tile)
                a_part = pltpu.roll(b_part, HEAD_DIM, axis=1)
            dst[0, 0, u, 0:kb, 0:LANE] = a_part.astype(BF16)
            dst[0, 0, u, kb:2 * kb, 0:LANE] = b_part.astype(BF16)
        vab_ref[0, 0, u, 0:kb, LANE:2 * LANE] = ones_lo
        vab_ref[0, 0, u, kb:2 * kb, LANE:2 * LANE] = ones_hi


def _pair_operands(k_arr, k_cb, v_arr, v_cb, kb, group, npairs):
    b, t_k = k_arr.shape[0], k_arr.shape[1]
    wk = ATT_KV_HEADS * HEAD_DIM
    n_units = npairs // (group // 2) if group > 1 else npairs
    nkb = t_k // kb
    return pl.pallas_call(
        functools.partial(_pair_operands_body, kb=kb, group=group, n_units=n_units),
        out_shape=(jax.ShapeDtypeStruct((b, nkb, n_units, 2 * kb, LANE), BF16),
                   jax.ShapeDtypeStruct((b, nkb, n_units, 2 * kb, 2 * LANE), BF16)),
        grid=(b, nkb),
        in_specs=[pl.BlockSpec((1, kb, wk), lambda bi, j: (bi, j, k_cb)),
                  pl.BlockSpec((1, kb, wk), lambda bi, j: (bi, j, v_cb))],
        out_specs=(pl.BlockSpec((1, 1, n_units, 2 * kb, LANE), lambda bi, j: (bi, j, 0, 0, 0)),
                   pl.BlockSpec((1, 1, n_units, 2 * kb, 2 * LANE), lambda bi, j: (bi, j, 0, 0, 0))),
        compiler_params=_cparams(("parallel", "parallel")),
        name="pair_operands",
    )(k_arr, v_arr)


def _flash(q_arr, q_cb, wq, k_arr, k_cb, v_arr, v_cb, q_gain, sink_row, *, mode, tq, kb, group,
           n_valid=None, causal=False, scores=None, tau=None, jb=None):
    b, t_q = q_arr.shape[0], q_arr.shape[1]
    t_k = k_arr.shape[1]
    nq = t_q // tq
    if mode == "band":
        assert tq == kb == WINDOW
        nk = 2
        kmap = lambda i, j: jnp.maximum(i - 1 + j, 0)
    elif mode == "dsa" and causal:
        nk = t_k // kb
        kmap = lambda i, j: jnp.minimum(j, ((i + 1) * tq - 1) // kb)
    else:
        nk = t_k // kb
        kmap = lambda i, j: j
    grid = (b, nq, nk)
    qmap = lambda i, j: i
    fold = _fold_steps(nq, tq, kb) if (mode == "dsa" and causal) else None
    if fold:
        grid = (b, nq // 2, fold)
        qmap = lambda r, s: _fold_step(r, s, nq, tq, kb)[0]
        kmap = lambda r, s: _fold_step(r, s, nq, tq, kb)[1]
    npairs = wq // LANE
    kab, vab = _pair_operands(k_arr, k_cb, v_arr, v_cb, kb, group, npairs)
    n_units = kab.shape[2]
    in_specs = [pl.BlockSpec((1, tq, wq), lambda bi, i, j: (bi, qmap(i, j), q_cb)),
                pl.BlockSpec((1, 1, n_units, 2 * kb, LANE), lambda bi, i, j: (bi, kmap(i, j), 0, 0, 0)),
                pl.BlockSpec((1, 1, n_units, 2 * kb, 2 * LANE), lambda bi, i, j: (bi, kmap(i, j), 0, 0, 0)),
                pl.BlockSpec((1, LANE), lambda bi, i, j: (0, 0)),
                pl.BlockSpec((1, wq), lambda bi, i, j: (0, 0))]
    args = [q_arr, kab, vab, q_gain, sink_row]
    if mode == "dsa":
        in_specs += [pl.BlockSpec((1, tq, kb), lambda bi, i, j: (bi, qmap(i, j), kmap(i, j))),
                     pl.BlockSpec((1, tq, LANE), lambda bi, i, j: (bi, qmap(i, j), 0)),
                     pl.BlockSpec((1, tq, LANE), lambda bi, i, j: (bi, qmap(i, j), 0))]
        args += [scores, tau, jb]
    body = functools.partial(_flash_body, mode=mode, tq=tq, kb=kb, wq=wq, group=group,
                             n_valid=n_valid, nk=nk, causal=causal, fold_nq=nq if fold else 0)
    return pl.pallas_call(
        body,
        out_shape=jax.ShapeDtypeStruct((b, t_q, wq), BF16),
        grid=grid,
        in_specs=in_specs,
        out_specs=pl.BlockSpec((1, tq, wq), lambda bi, i, j: (bi, qmap(i, j), 0)),
        scratch_shapes=[pltpu.VMEM((npairs, tq, LANE), BF16),
                        pltpu.VMEM((npairs, tq, LANE), F32),
                        pltpu.VMEM((npairs, tq, LANE), F32),
                        pltpu.VMEM((npairs, tq, LANE), F32),
                        pltpu.VMEM((tq, kb), BF16)],
        compiler_params=_cparams(("parallel", "parallel", "arbitrary")),
        name="flash_" + mode,
    )(*args)


def _dsa_select_body(qi_ref, kw_ref, ki_ref, sc_ref, tau_ref, jb_ref, sc_scr, *, tq, kb, nk, k_sel,
                     n_valid, causal, idx_bits, fold_nq):
    if fold_nq:
        i, j, _, is_last, fold_active = _fold_step(pl.program_id(1), pl.program_id(2), fold_nq, tq, kb)
    else:
        i, j = pl.program_id(1), pl.program_id(2)
        is_last = j == nk - 1
    qpos = i * tq + lax.broadcasted_iota(jnp.int32, (tq, 1), 0)
    if causal:
        n_adm = (qpos // CHUNK + 1) * CHUNK
        active = fold_active if fold_nq else j * kb < (i + 1) * tq
        n_blocks = _causal_blocks(i, tq, kb)
    else:
        n_adm = jnp.full((tq, 1), n_valid, jnp.int32)
        active = None
        n_blocks = nk

    def compute():
        ka = ki_ref[0]
        kab = jnp.concatenate([ka, pltpu.roll(ka, HEAD_DIM, axis=1)], axis=0).astype(BF16)
        kw = kw_ref[0]
        acc = jnp.zeros((tq, kb), F32)
        q4 = jnp.concatenate([qi_ref[0, :, p * LANE:(p + 1) * LANE].astype(BF16) for p in range(IDX_HEADS // 2)],
                             axis=0)
        s = lax.dot_general(q4, kab, (((1,), (1,)), ((), ())), preferred_element_type=F32)
        for p in range(IDX_HEADS // 2):
            for h in range(2):
                col = HEAD_DIM + 2 * p + h
                w = (kw[:, col:col + 1] * IDX_HEADS ** -0.5) * IDX_DIM ** -0.5
                acc = acc + jnp.maximum(s[p * tq:(p + 1) * tq, h * kb:(h + 1) * kb], 0.0) * w
        acc = jnp.where(acc == 0.0, 0.0, acc)
        kidx = j * kb + lax.broadcasted_iota(jnp.int32, (tq, kb), 1)
        x = jnp.where(kidx < n_adm, acc, -jnp.inf)
        sc_scr[j] = x
        sc_ref[0] = x

    if active is None:
        compute()
    else:
        pl.when(active)(compute)

        if not fold_nq:
            @pl.when(jnp.logical_not(active))
            def _():
                sc_ref[0] = jnp.full((tq, kb), -jnp.inf, F32)

    @pl.when(is_last)
    def _():
        kf = float(k_sel)

        rsz = min(tq, LANE)

        def count(make_pred):
            starts = range(0, tq, rsz)
            preds = [make_pred(lambda col, r0=r0: jnp.broadcast_to(col[r0:r0 + rsz], (rsz, LANE))) for r0 in starts]
            accs = []
            for r0, pred in zip(starts, preds):
                def blk(jj, acc, r0=r0, pred=pred):
                    for a in range(kb // LANE):
                        x = sc_scr[jj, r0:r0 + rsz, a * LANE:(a + 1) * LANE]
                        acc = acc + jnp.where(pred(x, jj * kb + a * LANE), 1.0, 0.0)
                    return acc
                accs.append(lax.fori_loop(0, n_blocks, blk, jnp.zeros((rsz, LANE), F32)))
            return jnp.sum(jnp.concatenate(accs, axis=0), axis=1, keepdims=True)

        def key_to_float(tu):
            cs = tu ^ INT_MIN
            fb = jnp.where(cs >= 0, cs, cs ^ 0x7FFFFFFF)
            return lax.bitcast_convert_type(fb, F32)

        def bit_body(it, tu):
            cand = tu | jnp.left_shift(jnp.int32(1), 31 - it)
            thr = key_to_float(cand)
            c = count(lambda widen: (lambda x, base, t=widen(thr): x >= t))
            return jnp.where(c >= kf, cand, tu)

        tu = lax.fori_loop(0, 32, bit_body, jnp.zeros((tq, 1), jnp.int32))
        full = n_adm >= k_sel
        tau = jnp.where(full, key_to_float(tu), -jnp.inf)
        c_gt = count(lambda widen: (lambda x, base, t=widen(tau): x > t))
        c_ge = count(lambda widen: (lambda x, base, t=widen(tau): x >= t))
        need = kf - c_gt
        tau_ref[0] = jnp.broadcast_to(tau, (tq, LANE))
        jb_ref[0] = jnp.full((tq, LANE), 2 ** 30, jnp.int32)
        surplus = jnp.max(jnp.where(full, (c_ge - c_gt) - need, 0.0))

        @pl.when(surplus > 0.5)
        def _():
            def tie_body(it, cut):
                cand = cut | jnp.left_shift(jnp.int32(1), idx_bits - 1 - it)

                def make_pred(widen):
                    t, cnd = widen(tau), widen(cand)
                    lane_idx = lax.broadcasted_iota(jnp.int32, (rsz, LANE), 1)
                    return lambda x, base: (x == t) & (base + lane_idx < cnd)
                c = count(make_pred)
                return jnp.where(c < need, cand, cut)

            cut = lax.fori_loop(0, idx_bits, tie_body, jnp.zeros((tq, 1), jnp.int32))
            jb_ref[0] = jnp.broadcast_to(cut, (tq, LANE))


def _dsa_select(z, qi_cb, kw_cb, ki_n, *, tq, kb, k_sel, n_valid, causal):
    b, t_q = z.shape[0], z.shape[1]
    t_k = ki_n.shape[1]
    nq, nk = t_q // tq, t_k // kb
    kmap = (lambda i, j: jnp.minimum(j, ((i + 1) * tq - 1) // kb)) if causal else (lambda i, j: j)
    omap = lambda i, j: j
    qmap = lambda i, j: i
    grid = (b, nq, nk)
    fold = _fold_steps(nq, tq, kb) if causal else None
    if fold:
        grid = (b, nq // 2, fold)
        qmap = lambda r, s: _fold_step(r, s, nq, tq, kb)[0]
        kmap = omap = lambda r, s: _fold_step(r, s, nq, tq, kb)[1]
    body = functools.partial(_dsa_select_body, tq=tq, kb=kb, nk=nk, k_sel=k_sel, n_valid=n_valid,
                             causal=causal, idx_bits=max(1, (t_k - 1).bit_length()), fold_nq=nq if fold else 0)
    return pl.pallas_call(
        body,
        out_shape=(jax.ShapeDtypeStruct((b, t_q, t_k), F32),
                   jax.ShapeDtypeStruct((b, t_q, LANE), F32),
                   jax.ShapeDtypeStruct((b, t_q, LANE), jnp.int32)),
        grid=grid,
        in_specs=[pl.BlockSpec((1, tq, IDX_HEADS * IDX_DIM), lambda bi, i, j: (bi, qmap(i, j), qi_cb)),
                  pl.BlockSpec((1, tq, LANE), lambda bi, i, j: (bi, qmap(i, j), kw_cb)),
                  pl.BlockSpec((1, kb, LANE), lambda bi, i, j: (bi, kmap(i, j), 0))],
        out_specs=(pl.BlockSpec((1, tq, kb), lambda bi, i, j: (bi, qmap(i, j), omap(i, j))),
                   pl.BlockSpec((1, tq, LANE), lambda bi, i, j: (bi, qmap(i, j), 0)),
                   pl.BlockSpec((1, tq, LANE), lambda bi, i, j: (bi, qmap(i, j), 0))),
        scratch_shapes=[pltpu.VMEM((nk, tq, kb), F32)],
        compiler_params=_cparams(("parallel", "parallel", "arbitrary")),
        name="dsa_select",
    )(z, z, ki_n)


def _rwkv_body(zr_ref, zk_ref, zv_ref, zw_ref, za_ref, zg_ref,
               sr_ref, sk_ref, sv_ref, sw_ref, sa_ref, sg_ref,
               mr_ref, mk_ref, mv_ref, mw_ref, ma_ref, mg_ref,
               vec_ref, w2_ref, a2_ref, g2_ref, s0_ref,
               mix_ref, sfin_ref,
               s_scr, prev_scr, prevl_scr, prevg_scr, y_scr, *, tc, lc, nt, pp):
    t = pl.program_id(2)

    @pl.when(t == 0)
    def _():
        s_scr[...] = s0_ref[0]
        prev_scr[0:1, :] = sr_ref[0]
        prev_scr[1:2, :] = sk_ref[0]
        prev_scr[2:3, :] = sv_ref[0]
        prevl_scr[0:1, :] = sw_ref[0]
        prevl_scr[1:2, :] = sa_ref[0]
        prevg_scr[0:1, :] = sg_ref[0]

    row = lax.broadcasted_iota(jnp.int32, (tc, 1), 0)

    def shifted(z_ref, prow, mu_ref):
        z = z_ref[0]
        zp = jnp.where(row == 0, prow, pltpu.roll(z, 1, axis=0))
        return z + (zp - z) * mu_ref[...], z[tc - 1:tc, :]

    r, last_r = shifted(zr_ref, prev_scr[0:1, :], mr_ref)
    k, last_k = shifted(zk_ref, prev_scr[1:2, :], mk_ref)
    v, last_v = shifted(zv_ref, prev_scr[2:3, :], mv_ref)
    zw, last_w = shifted(zw_ref, prevl_scr[0:1, :], mw_ref)
    za, last_a = shifted(za_ref, prevl_scr[1:2, :], ma_ref)
    zg, last_g = shifted(zg_ref, prevg_scr[0:1, :], mg_ref)
    prev_scr[0:1, :] = last_r
    prev_scr[1:2, :] = last_k
    prev_scr[2:3, :] = last_v
    prevl_scr[0:1, :] = last_w
    prevl_scr[1:2, :] = last_a
    prevg_scr[0:1, :] = last_g

    w0, a0 = vec_ref[0:1, :], vec_ref[1:2, :]
    k_k, k_a, r_k = vec_ref[2:3, :], vec_ref[3:4, :], vec_ref[4:5, :]
    ln_w, ln_b = vec_ref[5:6, :], vec_ref[6:7, :]

    ones_blk = _head_block_matrix(LANE, 1.0)
    avg_blk = _head_block_matrix(LANE, 1.0 / HEAD_DIM)

    def per_head(x, blk):
        return jnp.concatenate([_dot_exact_rhs(x[:, i * LANE:(i + 1) * LANE], blk) for i in range(pp)], axis=1)

    xw = w0 + jnp.dot(jnp.tanh(zw).astype(BF16), w2_ref[...], preferred_element_type=F32)
    nx = -xw
    softplus = jnp.maximum(nx, 0.0) + jnp.log1p(jnp.exp(-jnp.abs(nx)))
    w_log = -softplus - 0.5
    lw = -jnp.exp(w_log)
    a = jax.nn.sigmoid(a0 + jnp.dot(za.astype(BF16), a2_ref[...], preferred_element_type=F32))
    g = jnp.dot(jax.nn.sigmoid(zg).astype(BF16), g2_ref[...], preferred_element_type=F32)
    kk = k * k_k
    kk = kk / jnp.maximum(jnp.sqrt(per_head(kk * kk, ones_blk)), 1e-12)
    k2 = k * (1.0 + (a - 1.0) * k_a)

    a_step = -kk
    b_step = kk * a

    lane = lax.broadcasted_iota(jnp.int32, (1, LANE), 1)
    m0 = lane < HEAD_DIM
    rr = lax.broadcasted_iota(jnp.int32, (lc, 2 * lc), 0)
    cc = lax.broadcasted_iota(jnp.int32, (lc, 2 * lc), 1)
    incl = jnp.where(cc < lc, cc, cc - lc) <= rr
    strict = (lax.broadcasted_iota(jnp.int32, (2 * lc, 2 * lc), 1)
              < lax.broadcasted_iota(jnp.int32, (2 * lc, 2 * lc), 0))
    tri = (lax.broadcasted_iota(jnp.int32, (lc, lc), 1)
           <= lax.broadcasted_iota(jnp.int32, (lc, lc), 0)).astype(BF16)
    nsteps = lc.bit_length() - 1
    nt_dims = ((1,), (1,))
    nn_dims = ((1,), (0,))
    tn_dims = ((0,), (0,))

    eye = (lax.broadcasted_iota(jnp.int32, (2 * lc, 2 * lc), 0)
           == lax.broadcasted_iota(jnp.int32, (2 * lc, 2 * lc), 1)).astype(F32)

    chunks = range(tc // lc)
    st = []
    for c in chunks:
        rows = slice(c * lc, (c + 1) * lc)
        lwc = lw[rows]
        cs = _dot_exact_lhs(tri, lwc)
        p_in = jnp.exp(-cs)
        at = a_step[rows] * jnp.exp(cs - lwc)
        bt = b_step[rows] * p_in
        kt = k2[rows] * p_in
        rt = r[rows] * jnp.exp(cs)
        vc = v[rows]
        p_last = jnp.exp(cs[lc - 1:lc, :])
        for pi in range(pp):
            cols = slice(pi * LANE, (pi + 1) * LANE)
            stack = lambda x: jnp.concatenate([jnp.where(m0, x[:, cols], 0.0),
                                               jnp.where(m0, 0.0, x[:, cols])], axis=0).astype(BF16)
            st.append(dict(c=c, pi=pi,
                           ar=jnp.concatenate([stack(at), rt[:, cols].astype(BF16)], axis=0),
                           bk=jnp.concatenate([stack(bt), stack(kt)], axis=0),
                           v_s=stack(vc), p_last=p_last[:, cols]))
    for d in st:
        gram = _dot_rw(d["ar"], d["bk"], nt_dims)
        d["t_p"] = jnp.where(strict, gram[0:2 * lc, 0:2 * lc], 0.0)
        d["w_ak"] = jnp.where(strict, gram[0:2 * lc, 2 * lc:4 * lc], 0.0).astype(BF16)
        d["w_rb"] = jnp.where(incl, gram[2 * lc:3 * lc, 0:2 * lc], 0.0).astype(BF16)
        d["w_rk"] = jnp.where(incl, gram[2 * lc:3 * lc, 2 * lc:4 * lc], 0.0).astype(BF16)
        d["minv"] = eye + d["t_p"]
    for d in st:
        d["t_p"] = _dot_rw(d["t_p"], d["t_p"], nn_dims)
    for step in range(nsteps - 1):
        for d in st:
            if step == nsteps - 2:
                d["minv"] = d["minv"] + _dot_rw(d["minv"], d["t_p"], nn_dims)
            else:
                both = _dot_rw(jnp.concatenate([d["minv"], d["t_p"]], axis=0), d["t_p"], nn_dims)
                d["minv"] = d["minv"] + both[0:2 * lc]
                d["t_p"] = both[2 * lc:4 * lc]
    for d in st:
        d["minv"] = d["minv"].astype(BF16)
        both = _dot_rw(jnp.concatenate([d["w_ak"], d["w_rk"]], axis=0), d["v_s"], nn_dims)
        d["wv"] = both[0:2 * lc]
        d["y_c"] = both[2 * lc:3 * lc]
    for d in st:
        both = _dot_rw(d["minv"], jnp.concatenate([d["ar"][0:2 * lc], d["wv"].astype(BF16)], axis=1), nn_dims)
        d["ma"] = both[:, 0:LANE]
        d["mwv"] = both[:, LANE:2 * LANE]
    for d in st:
        d["g"] = _dot_rw(d["ma"], d["bk"][0:2 * lc], tn_dims).astype(BF16)
        d["d"] = _dot_rw(jnp.concatenate([d["mwv"].astype(BF16), d["v_s"]], axis=0), d["bk"], tn_dims)
    s_cur = [s_scr[pi] for pi in range(pp)]
    for d in st:
        s_in = s_cur[d["pi"]]
        d["s0"] = s_in.astype(BF16)
        s_cur[d["pi"]] = ((s_in + _dot_rw(d["s0"], d["g"], nn_dims)) + d["d"]) * d["p_last"]
    for pi in range(pp):
        s_scr[pi] = s_cur[pi]
    for d in st:
        d["xr"] = _dot_rw(d["ar"], d["s0"], nt_dims)
    for d in st:
        d["u"] = (_dot_rw(d["minv"], d["xr"][0:2 * lc], nn_dims) + d["mwv"]).astype(BF16)
    for d in st:
        c, pi = d["c"], d["pi"]
        y_scr[c * lc:(c + 1) * lc, pi * LANE:(pi + 1) * LANE] = (
            (d["xr"][2 * lc:3 * lc] + _dot_rw(d["w_rb"], d["u"], nn_dims)) + d["y_c"])

    y = y_scr[...]
    mean = per_head(y, avg_blk)
    dev = y - mean
    var = per_head(dev * dev, avg_blk)
    yn = (dev * lax.rsqrt(var + RW_GN_EPS)) * ln_w + ln_b
    bonus = per_head((r * k2) * r_k, ones_blk) * v
    mix_ref[0] = ((yn + bonus) * g).astype(mix_ref.dtype)

    @pl.when(t == nt - 1)
    def _():
        sfin_ref[0] = s_scr[...]


def _rwkv(z, shift_prev, s0_pairs, mu, vecs, w2, a2, g2, d_model):
    b, t = z.shape[0], z.shape[1]
    npairs = d_model // LANE
    tc = min(t, 512)
    lc = min(CHUNK, t)
    nt = t // tc
    pp = 2 if tc // lc >= 4 else 8
    wp = pp * LANE
    ngroups = npairs // pp
    cb_w, cb_a, cb_g = 3 * npairs, 3 * npairs + 1, (3 * npairs + 2) // 2

    def zspec(width, cbf):
        return pl.BlockSpec((1, tc, width), lambda bi, p, ti: (bi, ti, cbf(p)))

    def sspec(width, cbf):
        return pl.BlockSpec((1, 1, width), lambda bi, p, ti: (bi, 0, cbf(p)))

    def mspec(width, cbf):
        return pl.BlockSpec((1, width), lambda bi, p, ti: (0, cbf(p)))

    cbfs = [(wp, lambda p: p), (wp, lambda p: ngroups + p), (wp, lambda p: 2 * ngroups + p),
            (LANE, lambda p: cb_w), (LANE, lambda p: cb_a), (2 * LANE, lambda p: cb_g)]
    in_specs = ([zspec(w, f) for w, f in cbfs] + [sspec(w, f) for w, f in cbfs] + [mspec(w, f) for w, f in cbfs]
                + [pl.BlockSpec((8, wp), lambda bi, p, ti: (0, p)),
                   pl.BlockSpec((LANE, wp), lambda bi, p, ti: (0, p)),
                   pl.BlockSpec((LANE, wp), lambda bi, p, ti: (0, p)),
                   pl.BlockSpec((2 * LANE, wp), lambda bi, p, ti: (0, p)),
                   pl.BlockSpec((1, pp, LANE, LANE), lambda bi, p, ti: (bi, p, 0, 0))])
    return pl.pallas_call(
        functools.partial(_rwkv_body, tc=tc, lc=lc, nt=nt, pp=pp),
        out_shape=(jax.ShapeDtypeStruct((b, t, d_model), BF16),
                   jax.ShapeDtypeStruct((b, npairs, LANE, LANE), F32)),
        grid=(b, ngroups, nt),
        in_specs=in_specs,
        out_specs=(pl.BlockSpec((1, tc, wp), lambda bi, p, ti: (bi, ti, p)),
                   pl.BlockSpec((1, pp, LANE, LANE), lambda bi, p, ti: (bi, p, 0, 0))),
        scratch_shapes=[pltpu.VMEM((pp, LANE, LANE), F32),
                        pltpu.VMEM((8, wp), F32),
                        pltpu.VMEM((8, LANE), F32),
                        pltpu.VMEM((8, 2 * LANE), F32),
                        pltpu.VMEM((tc, wp), F32)],
        compiler_params=_cparams(("parallel", "parallel", "arbitrary")),
        name="rwkv7",
    )(*([z] * 6), *([shift_prev] * 6), *([mu] * 6), vecs, w2, a2, g2, s0_pairs)


def _pad_cols(x, segments):
    parts = []
    for start, width, padded in segments:
        seg = x[..., start:start + width]
        if padded > width:
            seg = jnp.concatenate([seg, jnp.zeros(seg.shape[:-1] + (padded - width,), seg.dtype)], axis=-1)
        parts.append(seg)
    return jnp.concatenate(parts, axis=-1)


def _pad_rows(x, padded):
    return jnp.concatenate([x, jnp.zeros((padded - x.shape[0],) + x.shape[1:], x.dtype)], axis=0)


def _pairs_from_heads(s):
    b, h = s.shape[0], s.shape[1]
    s = s.reshape(b, h // 2, 2, HEAD_DIM, HEAD_DIM)
    z = jnp.zeros_like(s[:, :, 0])
    top = jnp.concatenate([s[:, :, 0], z], axis=-1)
    bot = jnp.concatenate([z, s[:, :, 1]], axis=-1)
    return jnp.concatenate([top, bot], axis=-2)


def _heads_from_pairs(sp):
    b, npairs = sp.shape[0], sp.shape[1]
    s = jnp.stack([sp[:, :, :HEAD_DIM, :HEAD_DIM], sp[:, :, HEAD_DIM:, HEAD_DIM:]], axis=2)
    return s.reshape(b, 2 * npairs, HEAD_DIM, HEAD_DIM)


def _tile_gain(g, width):
    return jnp.tile(g.astype(F32), width // HEAD_DIM).reshape(1, width)


def _mem_attend(z, memq_cb, km, vm, q_gain):
    t_q = z.shape[1]
    wq = MEM_HEADS * HEAD_DIM
    tq = min(t_q, 512)
    no_sink = jnp.full((1, wq), -jnp.inf, F32)
    return _flash(z, memq_cb, wq, km, 0, vm, 0, _tile_gain(q_gain, LANE), no_sink,
                  mode="all", tq=tq, kb=km.shape[1], group=1, n_valid=km.shape[1])


def _conv_ffn(x2d, b, t, gain, w_up, conv_w, w_down, prev):
    act, u_last = _up_conv(x2d, gain, w_up, conv_w, prev, t)
    return _mm_res([act], [w_down], x2d), u_last


def kernel(x_prompt, x_sample, state_rwkv_wkv, state_rwkv_shift, cache_swa_k, cache_swa_v, cache_dsa_k, cache_dsa_v, cache_dsa_idx_k, cache_mem_k, cache_mem_v, state_ffn_conv, mem_prompt, attn_norm, ffn_norm, mem_norm, mem_w_kv, mem_q_norm, mem_k_norm, a_w_in, a_mu, a_w0, a_w2, a_a0, a_a2, a_g2, a_k_k, a_k_a, a_r_k, a_ln_w, a_ln_b, a_w_out, b_w_in, b_q_norm, b_k_norm, b_sink, b_w_out, c_w_in, c_q_norm, c_k_norm, c_idx_k_norm, c_w_out, ffn_w_up, ffn_conv, ffn_w_down):
    bp, t, d = x_prompt.shape
    bd, s_len = x_sample.shape[:2]
    depth = attn_norm.shape[0]
    win_rows = cache_swa_k.shape[2]
    past = cache_dsa_k.shape[2] if cache_dsa_k.shape[0] else 0
    d_ff = ffn_w_down.shape[1]
    mem_tokens = mem_prompt.shape[1]
    q_cols = d
    kv_cols = ATT_KV_HEADS * HEAD_DIM
    memq_cols = MEM_HEADS * HEAD_DIM
    att_group = (d // HEAD_DIM) // ATT_KV_HEADS
    dec_lora = a_w2.shape[1]
    a_lora = a_a2.shape[1]
    g_lora = a_g2.shape[1]
    rw_cols = 3 * d + dec_lora + a_lora + g_lora
    k_sel_p = min(TOPK_MAX, t // 4)
    k_sel_s = min(TOPK_MAX, (past + s_len) // 4)
    assert g_lora == 2 * LANE and dec_lora <= LANE and a_lora <= LANE

    xp = x_prompt.reshape(bp * t, d)
    xs = x_sample.reshape(bd * s_len, d)

    rw_segments = [(0, 3 * d, 3 * d), (3 * d, dec_lora, LANE), (3 * d + dec_lora, a_lora, LANE),
                   (3 * d + dec_lora + a_lora, g_lora, g_lora)]
    rw_padded = 3 * d + 2 * LANE + g_lora
    o_qi = q_cols + 2 * kv_cols
    o_ki = o_qi + IDX_HEADS * IDX_DIM
    c_cols = o_ki + IDX_DIM + IDX_HEADS

    outs = {k: [] for k in ("p_rw_wkv", "p_rw_sh", "p_sw_k", "p_sw_v", "p_ds_k", "p_ds_v", "p_ds_i", "p_mk",
                            "p_mv", "p_cv", "s_rw_wkv", "s_rw_sh", "s_sw_k", "s_sw_v", "s_ds_k", "s_ds_v",
                            "s_ds_i", "s_cv")}

    def unpad_rw(row):
        return jnp.concatenate([row[..., :3 * d], row[..., 3 * d:3 * d + dec_lora],
                                row[..., 3 * d + LANE:3 * d + LANE + a_lora],
                                row[..., 3 * d + 2 * LANE:3 * d + 2 * LANE + g_lora]], axis=-1)

    for i in range(depth):
        kind, j = i % 3, i // 3
        if kind == 0:
            w_in = jnp.concatenate([_pad_cols(a_w_in[j], rw_segments), a_w_in[j][:, rw_cols:]], axis=1).astype(BF16)
            memq_cb = rw_padded // memq_cols
            zp = _mm_norm(xp, attn_norm[i], w_in).reshape(bp, t, -1)
            zs = _mm_norm(xs, attn_norm[i], w_in).reshape(bd, s_len, -1)
            mu = _pad_cols(a_mu[j].reshape(1, -1), rw_segments)
            mu = jnp.concatenate([mu, jnp.zeros((1, memq_cols), F32)], axis=1)
            vecs = jnp.stack([a_w0[j], a_a0[j], a_k_k[j], a_k_a[j], a_r_k[j].reshape(-1), a_ln_w[j], a_ln_b[j],
                              jnp.zeros((d,), F32)], axis=0)
            w2 = _pad_rows(a_w2[j], LANE).astype(BF16)
            a2 = _pad_rows(a_a2[j], LANE).astype(BF16)
            g2 = a_g2[j].astype(BF16)
            sh_p = jnp.zeros((bp, 1, zp.shape[-1]), F32)
            st_p = jnp.zeros((bp, d // LANE, LANE, LANE), F32)
            sh_s = _pad_cols(state_rwkv_shift[j], rw_segments)
            sh_s = jnp.concatenate([sh_s, jnp.zeros((bd, memq_cols), F32)], axis=1).reshape(bd, 1, -1)
            st_s = _pairs_from_heads(state_rwkv_wkv[j])
            mp, stp = _rwkv(zp, sh_p, st_p, mu, vecs, w2, a2, g2, d)
            ms, sts = _rwkv(zs, sh_s, st_s, mu, vecs, w2, a2, g2, d)
            outs["p_rw_sh"].append(unpad_rw(zp[:, -1]))
            outs["p_rw_wkv"].append(_heads_from_pairs(stp))
            outs["s_rw_sh"].append(unpad_rw(zs[:, -1]))
            outs["s_rw_wkv"].append(_heads_from_pairs(sts))
            w_out = a_w_out[j]
        elif kind == 1:
            w_in = b_w_in[j].astype(BF16)
            memq_cb = (q_cols + 2 * kv_cols) // memq_cols
            k_cb, v_cb = q_cols // kv_cols, q_cols // kv_cols + 1
            zp = _mm_norm(xp, attn_norm[i], w_in).reshape(bp, t, -1)
            zs = _mm_norm(xs, attn_norm[i], w_in).reshape(bd, s_len, -1)
            kgain = _tile_gain(b_k_norm[j], kv_cols)
            qgain = _tile_gain(b_q_norm[j], LANE)
            sink = jnp.repeat(b_sink[j].astype(F32), HEAD_DIM).reshape(1, q_cols)
            knp = _headnorm(zp.reshape(bp * t, -1), k_cb, kv_cols, kgain).reshape(bp, t, kv_cols)
            mp = _flash(zp, 0, q_cols, knp, 0, zp, v_cb, qgain, sink, mode="band", tq=WINDOW, kb=WINDOW,
                        group=att_group)
            outs["p_sw_k"].append(knp[:, t - win_rows:].reshape(bp, win_rows, ATT_KV_HEADS, HEAD_DIM))
            outs["p_sw_v"].append(zp[:, t - win_rows:, q_cols + kv_cols:q_cols + 2 * kv_cols]
                                  .reshape(bp, win_rows, ATT_KV_HEADS, HEAD_DIM))
            kns = _headnorm(zs.reshape(bd * s_len, -1), k_cb, kv_cols, kgain).reshape(bd, s_len, kv_cols)
            vs_new = zs[:, :, q_cols + kv_cols:q_cols + 2 * kv_cols]
            k_all = jnp.concatenate([cache_swa_k[j].reshape(bd, win_rows, kv_cols), kns], axis=1)
            v_all = jnp.concatenate([cache_swa_v[j].reshape(bd, win_rows, kv_cols), vs_new], axis=1)
            n_keys = win_rows + s_len
            n_pad = -(-n_keys // LANE) * LANE
            pad = jnp.zeros((bd, n_pad - n_keys, kv_cols), F32)
            ms = _flash(zs, 0, q_cols, jnp.concatenate([k_all, pad], axis=1), 0,
                        jnp.concatenate([v_all, pad], axis=1), 0, qgain, sink, mode="all", tq=s_len, kb=n_pad,
                        group=att_group, n_valid=n_keys)
            outs["s_sw_k"].append(k_all[:, n_keys - win_rows:].reshape(bd, win_rows, ATT_KV_HEADS, HEAD_DIM))
            outs["s_sw_v"].append(v_all[:, n_keys - win_rows:].reshape(bd, win_rows, ATT_KV_HEADS, HEAD_DIM))
            w_out = b_w_out[j]
        else:
            wc = c_w_in[j]
            w_in = jnp.concatenate([wc[:, :o_ki], wc[:, c_cols:],
                                    _pad_cols(wc, [(o_ki, IDX_DIM + IDX_HEADS, LANE)])], axis=1).astype(BF16)
            memq_cb = o_ki // memq_cols
            kw_cb = (o_ki + memq_cols) // LANE
            k_cb, v_cb = q_cols // kv_cols, q_cols // kv_cols + 1
            qi_cb = o_qi // (IDX_HEADS * IDX_DIM)
            zp = _mm_norm(xp, attn_norm[i], w_in).reshape(bp, t, -1)
            zs = _mm_norm(xs, attn_norm[i], w_in).reshape(bd, s_len, -1)
            kgain = _tile_gain(c_k_norm[j], kv_cols)
            qgain = _tile_gain(c_q_norm[j], LANE)
            igain = jnp.concatenate([c_idx_k_norm[j].astype(F32), jnp.zeros((LANE - IDX_DIM,), F32)]).reshape(1, LANE)
            no_sink = jnp.full((1, q_cols), -jnp.inf, F32)
            knp = _headnorm(zp.reshape(bp * t, -1), k_cb, kv_cols, kgain).reshape(bp, t, kv_cols)
            kip = _headnorm(zp.reshape(bp * t, -1), kw_cb, LANE, igain).reshape(bp, t, LANE)
            tq = min(t, DSA_TQ)
            kb = min(t, DSA_KB)
            sc, tau, cut = _dsa_select(zp, qi_cb, kw_cb, kip, tq=tq, kb=kb, k_sel=k_sel_p, n_valid=t, causal=True)
            mp = _flash(zp, 0, q_cols, knp, 0, zp, v_cb, qgain, no_sink, mode="dsa", tq=tq, kb=kb, group=att_group,
                        n_valid=t, causal=True, scores=sc, tau=tau, jb=cut)
            outs["p_ds_k"].append(knp.reshape(bp, t, ATT_KV_HEADS, HEAD_DIM))
            outs["p_ds_v"].append(zp[:, :, q_cols + kv_cols:q_cols + 2 * kv_cols].reshape(bp, t, ATT_KV_HEADS, HEAD_DIM))
            outs["p_ds_i"].append(kip[:, :, :IDX_DIM])
            kns = _headnorm(zs.reshape(bd * s_len, -1), k_cb, kv_cols, kgain).reshape(bd, s_len, kv_cols)
            kis = _headnorm(zs.reshape(bd * s_len, -1), kw_cb, LANE, igain).reshape(bd, s_len, LANE)
            vs_new = zs[:, :, q_cols + kv_cols:q_cols + 2 * kv_cols]
            n_keys = past + s_len
            n_pad = -(-n_keys // LANE) * LANE
            zpad = lambda w: jnp.zeros((bd, n_pad - n_keys, w), F32)
            k_all = jnp.concatenate([cache_dsa_k[j].reshape(bd, past, kv_cols), kns, zpad(kv_cols)], axis=1)
            v_all = jnp.concatenate([cache_dsa_v[j].reshape(bd, past, kv_cols), vs_new, zpad(kv_cols)], axis=1)
            ki_cache = jnp.concatenate([cache_dsa_idx_k[j], jnp.zeros((bd, past, LANE - IDX_DIM), F32)], axis=-1)
            ki_all = jnp.concatenate([ki_cache, kis, zpad(LANE)], axis=1)
            sc, tau, cut = _dsa_select(zs, qi_cb, kw_cb, ki_all, tq=s_len, kb=n_pad, k_sel=k_sel_s, n_valid=n_keys,
                                       causal=False)
            ms = _flash(zs, 0, q_cols, k_all, 0, v_all, 0, qgain, no_sink, mode="dsa", tq=s_len, kb=n_pad,
                        group=att_group, n_valid=n_keys, causal=False, scores=sc, tau=tau, jb=cut)
            outs["s_ds_k"].append(kns.reshape(bd, s_len, ATT_KV_HEADS, HEAD_DIM))
            outs["s_ds_v"].append(vs_new.reshape(bd, s_len, ATT_KV_HEADS, HEAD_DIM))
            outs["s_ds_i"].append(kis[:, :, :IDX_DIM])
            w_out = c_w_out[j]

        kv_mem = _mm_norm(mem_prompt.reshape(bp * mem_tokens, d), mem_norm[i], mem_w_kv[i].astype(BF16))
        km_p = _headnorm(kv_mem, 0, memq_cols, _tile_gain(mem_k_norm[i], memq_cols)).reshape(bp, mem_tokens, memq_cols)
        vm_p = kv_mem[:, memq_cols:].reshape(bp, mem_tokens, memq_cols)
        outs["p_mk"].append(km_p.reshape(bp, mem_tokens, MEM_HEADS, HEAD_DIM))
        outs["p_mv"].append(vm_p.reshape(bp, mem_tokens, MEM_HEADS, HEAD_DIM))
        mo_p = _mem_attend(zp, memq_cb, km_p, vm_p, mem_q_norm[i])
        mo_s = _mem_attend(zs, memq_cb, cache_mem_k[i].reshape(bd, mem_tokens, memq_cols),
                           cache_mem_v[i].reshape(bd, mem_tokens, memq_cols), mem_q_norm[i])
        w_mix, w_mem = w_out[:d].astype(BF16), w_out[d:].astype(BF16)
        xp = _mm_res([mp.reshape(bp * t, d), mo_p.reshape(bp * t, memq_cols)], [w_mix, w_mem], xp)
        xs = _mm_res([ms.reshape(bd * s_len, d), mo_s.reshape(bd * s_len, memq_cols)], [w_mix, w_mem], xs)

        w_up, w_down = ffn_w_up[i].astype(BF16), ffn_w_down[i].astype(BF16)
        xp, cp = _conv_ffn(xp, bp, t, ffn_norm[i], w_up, ffn_conv[i], w_down,
                           jnp.zeros((bp, CONV_W - 1, 2 * d_ff), F32))
        xs, cs = _conv_ffn(xs, bd, s_len, ffn_norm[i], w_up, ffn_conv[i], w_down, state_ffn_conv[i])
        outs["p_cv"].append(cp)
        outs["s_cv"].append(cs)

    st = jnp.stack
    order = ("p_rw_wkv", "p_rw_sh", "p_sw_k", "p_sw_v", "p_ds_k", "p_ds_v", "p_ds_i", "p_mk", "p_mv", "p_cv",
             "s_rw_wkv", "s_rw_sh", "s_sw_k", "s_sw_v", "s_ds_k", "s_ds_v", "s_ds_i", "s_cv")
    return (xp.reshape(bp, t, d), xs.reshape(bd, s_len, d)) + tuple(st(outs[k]) for k in order)
```

```python
import functools

import jax
import jax.numpy as jnp
from jax import lax
from jax.experimental import pallas as pl
from jax.experimental.pallas import tpu as pltpu

F32 = jnp.float32
BF16 = jnp.bfloat16

HEAD_DIM = 64
CHUNK = 64
NORM_EPS = 1e-6
RW_GN_EPS = HEAD_DIM * 1e-5
ATT_KV_HEADS = 4
WINDOW = 128
IDX_HEADS = 8
IDX_DIM = 64
TOPK_MAX = 256
MEM_HEADS = 4
CONV_W = 3

LANE = 128
VMEM_LIMIT = 52 * 1024 * 1024
NEG_BIG = -(2.0 ** 100)
INT_MIN = -2147483648
MM_ROWS = 1024
DSA_TQ = 256
DSA_KB = 512
DSA_FLASH_TQ = 256
DSA_FLASH_KB = 512


def _cparams(sem, vmem=VMEM_LIMIT):
    return pltpu.CompilerParams(dimension_semantics=sem, vmem_limit_bytes=vmem)


def _split3(a):
    a1 = a.astype(BF16)
    r1 = a - a1.astype(F32)
    a2 = r1.astype(BF16)
    r2 = r1 - a2.astype(F32)
    return a1, a2, r2.astype(BF16)


def _dot_exact_rhs(a, e):
    a1, a2, a3 = _split3(a)
    d = lambda x: jnp.dot(x, e, preferred_element_type=F32)
    return (d(a3) + d(a2)) + d(a1)


def _dot_exact_lhs(e, a):
    a1, a2, a3 = _split3(a)
    d = lambda x: jnp.dot(e, x, preferred_element_type=F32)
    return (d(a3) + d(a2)) + d(a1)


def _head_block_matrix(width, value):
    r = lax.broadcasted_iota(jnp.int32, (width, width), 0) // HEAD_DIM
    c = lax.broadcasted_iota(jnp.int32, (width, width), 1) // HEAD_DIM
    return jnp.where(r == c, value, 0.0).astype(BF16)


def _dot_rw(a, b, dims):
    return lax.dot_general(a.astype(BF16), b.astype(BF16), (dims, ((), ())), preferred_element_type=F32)


def _pick_tile(n, cap):
    best = None
    for t in range(LANE, min(n, cap) + 1, LANE):
        if n % t == 0:
            best = t
    assert best is not None, n
    return best


def _mm_norm_body(x_ref, g_ref, w_ref, o_ref, xn_ref):
    @pl.when(pl.program_id(1) == 0)
    def _():
        x = x_ref[...]
        ms = jnp.mean(x * x, axis=-1, keepdims=True)
        xn_ref[...] = ((x * lax.rsqrt(ms + NORM_EPS)) * g_ref[...]).astype(BF16)

    o_ref[...] = jnp.dot(xn_ref[...], w_ref[...], preferred_element_type=F32)


def _mm_norm(x, gain, w):
    m, k = x.shape
    n = w.shape[1]
    tm = min(m, MM_ROWS)
    tn = _pick_tile(n, 1536)
    return pl.pallas_call(
        _mm_norm_body,
        out_shape=jax.ShapeDtypeStruct((m, n), F32),
        grid=(m // tm, n // tn),
        in_specs=[pl.BlockSpec((tm, k), lambda i, j: (i, 0)),
                  pl.BlockSpec((1, k), lambda i, j: (0, 0)),
                  pl.BlockSpec((k, tn), lambda i, j: (0, j))],
        out_specs=pl.BlockSpec((tm, tn), lambda i, j: (i, j)),
        scratch_shapes=[pltpu.VMEM((tm, k), BF16)],
        compiler_params=_cparams(("parallel", "arbitrary")),
        name="mm_norm",
    )(x, gain.reshape(1, k), w)


def _mm_res_body(*refs, n_lhs):
    lhs = refs[:n_lhs]
    ws = refs[n_lhs:2 * n_lhs]
    r_ref, o_ref = refs[2 * n_lhs], refs[2 * n_lhs + 1]
    acc = jnp.dot(lhs[0][...], ws[0][...], preferred_element_type=F32)
    for a, w in zip(lhs[1:], ws[1:]):
        acc = acc + jnp.dot(a[...], w[...], preferred_element_type=F32)
    o_ref[...] = r_ref[...] + acc


def _mm_res(lhs_list, w_list, res):
    m, n = res.shape
    ktot = sum(a.shape[1] for a in lhs_list)
    tm = min(m, MM_ROWS)
    tn = _pick_tile(n, 1024 if ktot <= 3072 else 512)
    n_lhs = len(lhs_list)
    in_specs = [pl.BlockSpec((tm, a.shape[1]), lambda i, j: (i, 0)) for a in lhs_list]
    in_specs += [pl.BlockSpec((w.shape[0], tn), lambda i, j: (0, j)) for w in w_list]
    in_specs += [pl.BlockSpec((tm, tn), lambda i, j: (i, j))]
    return pl.pallas_call(
        functools.partial(_mm_res_body, n_lhs=n_lhs),
        out_shape=jax.ShapeDtypeStruct((m, n), F32),
        grid=(m // tm, n // tn),
        in_specs=in_specs,
        out_specs=pl.BlockSpec((tm, tn), lambda i, j: (i, j)),
        compiler_params=_cparams(("parallel", "arbitrary")),
        name="mm_res",
    )(*lhs_list, *w_list, res)


def _headnorm_body(x_ref, g_ref, o_ref, *, width):
    avg = _head_block_matrix(LANE, 1.0 / HEAD_DIM)
    for c in range(width // LANE):
        x = x_ref[:, c * LANE:(c + 1) * LANE]
        ms = _dot_exact_rhs(x * x, avg)
        o_ref[:, c * LANE:(c + 1) * LANE] = (x * lax.rsqrt(ms + NORM_EPS)) * g_ref[:, c * LANE:(c + 1) * LANE]


def _headnorm(x, col_block, width, gain_row):
    m = x.shape[0]
    tm = min(m, 1024)
    return pl.pallas_call(
        functools.partial(_headnorm_body, width=width),
        out_shape=jax.ShapeDtypeStruct((m, width), F32),
        grid=(m // tm,),
        in_specs=[pl.BlockSpec((tm, width), lambda i: (i, col_block)),
                  pl.BlockSpec((1, width), lambda i: (0, 0))],
        out_specs=pl.BlockSpec((tm, width), lambda i: (i, 0)),
        compiler_params=_cparams(("parallel",)),
        name="headnorm",
    )(x, gain_row)


def _up_conv_body(x_ref, g_ref, wa_ref, wb_ref, pa_ref, pb_ref, cwa_ref, cwb_ref, o_ref, la_ref, lb_ref,
                  xn_ref, ca_ref, cb_ref, *, tm, tiles_per_batch, bpt):
    i = pl.program_id(0)
    j = pl.program_id(1)
    rpb = tm // bpt

    @pl.when(j == 0)
    def _():
        x = x_ref[...]
        ms = jnp.mean(x * x, axis=-1, keepdims=True)
        xn_ref[...] = ((x * lax.rsqrt(ms + NORM_EPS)) * g_ref[...]).astype(BF16)

    if bpt == 1:
        @pl.when(i % tiles_per_batch == 0)
        def _():
            ca_ref[j] = pa_ref[0]
            cb_ref[j] = pb_ref[0]

    xn = xn_ref[...]
    row = lax.broadcasted_iota(jnp.int32, o_ref.shape, 0)
    off = row if bpt == 1 else row % rpb
    if bpt > 1:
        pick = (lax.broadcasted_iota(jnp.int32, (tm, bpt), 0) // rpb
                == lax.broadcasted_iota(jnp.int32, (tm, bpt), 1)).astype(BF16)

    def conv(w_ref, p_ref, c_ref, cw_ref, last_ref):
        u = jnp.dot(xn, w_ref[...], preferred_element_type=F32)
        if bpt == 1:
            car = c_ref[j]
            c0, c1 = car[0:1, :], car[1:2, :]
            c_ref[j] = u[tm - 2:tm, :]
            last_ref[0] = u[tm - 2:tm, :]
        else:
            c0 = _dot_exact_lhs(pick, p_ref[:, 0, :])
            c1 = _dot_exact_lhs(pick, p_ref[:, 1, :])
            for bi in range(bpt):
                last_ref[bi] = u[(bi + 1) * rpb - 2:(bi + 1) * rpb, :]
        u1 = jnp.where(off == 0, c1, pltpu.roll(u, 1, axis=0))
        u2 = jnp.where(off == 0, c0, jnp.where(off == 1, c1, pltpu.roll(u, 2, axis=0)))
        return (u2 * cw_ref[0:1, :] + u1 * cw_ref[1:2, :]) + u * cw_ref[2:3, :]

    a = conv(wa_ref, pa_ref, ca_ref, cwa_ref, la_ref)
    b = conv(wb_ref, pb_ref, cb_ref, cwb_ref, lb_ref)
    o_ref[...] = ((a * jax.nn.sigmoid(a)) * b).astype(o_ref.dtype)


def _up_conv(x, gain, w_up, conv_w, prev, t):
    m, k = x.shape
    f = w_up.shape[1] // 2
    b = m // t
    tm = min(m, MM_ROWS)
    bpt = max(1, tm // t)
    assert tm % t == 0 or t % tm == 0
    tn = _pick_tile(f, 512)
    nj = f // tn
    tpb = max(1, t // tm)
    act, la, lb = pl.pallas_call(
        functools.partial(_up_conv_body, tm=tm, tiles_per_batch=tpb, bpt=bpt),
        out_shape=(jax.ShapeDtypeStruct((m, f), BF16),
                   jax.ShapeDtypeStruct((b, CONV_W - 1, f), F32),
                   jax.ShapeDtypeStruct((b, CONV_W - 1, f), F32)),
        grid=(m // tm, nj),
        in_specs=[pl.BlockSpec((tm, k), lambda i, j: (i, 0)),
                  pl.BlockSpec((1, k), lambda i, j: (0, 0)),
                  pl.BlockSpec((k, tn), lambda i, j: (0, j)),
                  pl.BlockSpec((k, tn), lambda i, j: (0, nj + j)),
                  pl.BlockSpec((bpt, CONV_W - 1, tn), lambda i, j: (i // tpb, 0, j)),
                  pl.BlockSpec((bpt, CONV_W - 1, tn), lambda i, j: (i // tpb, 0, nj + j)),
                  pl.BlockSpec((CONV_W, tn), lambda i, j: (0, j)),
                  pl.BlockSpec((CONV_W, tn), lambda i, j: (0, nj + j))],
        out_specs=(pl.BlockSpec((tm, tn), lambda i, j: (i, j)),
                   pl.BlockSpec((bpt, CONV_W - 1, tn), lambda i, j: (i // tpb, 0, j)),
                   pl.BlockSpec((bpt, CONV_W - 1, tn), lambda i, j: (i // tpb, 0, j))),
        scratch_shapes=[pltpu.VMEM((tm, k), BF16),
                        pltpu.VMEM((nj, CONV_W - 1, tn), F32),
                        pltpu.VMEM((nj, CONV_W - 1, tn), F32)],
        compiler_params=_cparams(("arbitrary", "arbitrary")),
        name="up_conv_gate",
    )(x, gain.reshape(1, k), w_up, w_up, prev, prev, conv_w, conv_w)
    return act, jnp.concatenate([la, lb], axis=-1)


def _causal_blocks(i, tq, kb):
    return ((i + 1) * tq + kb - 1) // kb


def _fold_steps(nq, tq, kb):
    if nq % 2:
        return None
    return max(_causal_blocks(r, tq, kb) + _causal_blocks(nq - 1 - r, tq, kb) for r in range(nq // 2))


def _fold_step(r, step, nq, tq, kb):
    n_lo = _causal_blocks(r, tq, kb)
    hi = nq - 1 - r
    in_lo = step < n_lo
    i = jnp.where(in_lo, r, hi)
    last_blk = _causal_blocks(i, tq, kb) - 1
    j = jnp.minimum(jnp.where(in_lo, step, step - n_lo), last_blk)
    active = step < n_lo + _causal_blocks(hi, tq, kb)
    return i, j, active & (jnp.where(in_lo, step, step - n_lo) == 0), active & (j == last_blk), active


def _flash_body(*refs, mode, tq, kb, wq, group, n_valid, nk, causal, fold_nq):
    q_ref, k_ref, v_ref, qg_ref, sink_ref = refs[:5]
    pos = 5
    if mode == "dsa":
        sc_ref, tau_ref, jb_ref = refs[pos:pos + 3]
        pos += 3
    o_ref = refs[pos]
    qn_scr, acc_scr, m_scr, l_scr, bias_scr = refs[pos + 1:]

    if fold_nq:
        i, j, is_first, is_last, fold_active = _fold_step(pl.program_id(1), pl.program_id(2), fold_nq, tq, kb)
    else:
        i, j = pl.program_id(1), pl.program_id(2)
        is_first, is_last = j == 0, j == nk - 1
    npairs = wq // LANE
    pairs_per_unit = group // 2 if group > 1 else 1
    lane = lax.broadcasted_iota(jnp.int32, (1, LANE), 1)
    lo_half = lane < HEAD_DIM

    @pl.when(is_first)
    def _():
        avg = _head_block_matrix(LANE, 1.0 / HEAD_DIM)
        for p in range(npairs):
            x = q_ref[0, :, p * LANE:(p + 1) * LANE]
            ms = _dot_exact_rhs(x * x, avg)
            qn = ((x * lax.rsqrt(ms + NORM_EPS)) * qg_ref[...]) * (HEAD_DIM ** -0.5)
            qn_scr[p] = qn.astype(BF16)
        acc_scr[...] = jnp.zeros(acc_scr.shape, F32)
        l_scr[...] = jnp.zeros(l_scr.shape, F32)
        m_scr[...] = jnp.full(m_scr.shape, NEG_BIG, F32)

    if mode == "band":
        kblk = i - WINDOW // kb + j
        active = kblk >= 0
    elif mode == "dsa" and causal:
        kblk = j
        active = fold_active if fold_nq else j * kb < (i + 1) * tq
    else:
        kblk = j
        active = None

    def step():
        kidx = kblk * kb + lax.broadcasted_iota(jnp.int32, (tq, kb), 1)
        qpos = i * tq + lax.broadcasted_iota(jnp.int32, (tq, kb), 0)
        if mode == "band":
            qchunk = qpos // CHUNK
            sel = (kidx >= (qchunk - WINDOW // CHUNK) * CHUNK) & (kidx < (qchunk + 1) * CHUNK) & (kidx >= 0)
        elif mode == "dsa":
            x = sc_ref[0]
            tau = tau_ref[0][:, 0:1]
            jb = jb_ref[0][:, 0:1]
            adm = kidx < ((qpos // CHUNK + 1) * CHUNK if causal else n_valid)
            sel = adm & ((x > tau) | ((x == tau) & (kidx <= jb)))
        else:
            sel = kidx < n_valid
        bias_scr[...] = jnp.where(sel, 0.0, NEG_BIG).astype(BF16)

        ppu = pairs_per_unit
        for u in range(npairs // ppu):
            ps = slice(u * ppu, (u + 1) * ppu)
            s = lax.dot_general(qn_scr[ps].reshape(ppu * tq, LANE), k_ref[0, 0, u], (((1,), (1,)), ((), ())),
                                preferred_element_type=F32).reshape(ppu, tq, 2 * kb)
            bias = bias_scr[...][None]
            m_old = m_scr[ps]
            s0 = s[:, :, 0:kb].astype(BF16) + bias
            s1 = s[:, :, kb:2 * kb].astype(BF16) + bias
            mn0 = jnp.maximum(m_old[:, :, 0:1], jnp.max(s0, axis=2, keepdims=True).astype(F32))
            mn1 = jnp.maximum(m_old[:, :, HEAD_DIM:HEAD_DIM + 1], jnp.max(s1, axis=2, keepdims=True).astype(F32))
            p0 = jnp.exp(s0 - mn0.astype(BF16))
            p1 = jnp.exp(s1 - mn1.astype(BF16))
            pcat = jnp.concatenate([p0, p1], axis=2).reshape(ppu * tq, 2 * kb)
            pv = jnp.dot(pcat, v_ref[0, 0, u], preferred_element_type=F32)
            pv = pv.reshape(ppu, tq, 2 * LANE)
            mn = jnp.where(lo_half, mn0, mn1)
            alpha = jnp.exp(m_old - mn)
            acc_scr[ps] = acc_scr[ps] * alpha + pv[:, :, 0:LANE]
            l_scr[ps] = l_scr[ps] * alpha + pv[:, :, LANE:2 * LANE]
            m_scr[ps] = mn

    if active is None:
        step()
    else:
        pl.when(active)(step)

    @pl.when(is_last)
    def _():
        for p in range(npairs):
            den = l_scr[p] + jnp.exp(sink_ref[:, p * LANE:(p + 1) * LANE] - m_scr[p])
            o_ref[0, :, p * LANE:(p + 1) * LANE] = (acc_scr[p] / den).astype(o_ref.dtype)


def _pair_operands_body(k_ref, v_ref, kab_ref, vab_ref, *, kb, group, n_units):
    lane = lax.broadcasted_iota(jnp.int32, (1, LANE), 1)
    lo_half = lane < HEAD_DIM
    ones_lo = jnp.broadcast_to(jnp.where(lo_half, 1.0, 0.0), (kb, LANE)).astype(BF16)
    ones_hi = jnp.broadcast_to(jnp.where(lo_half, 0.0, 1.0), (kb, LANE)).astype(BF16)
    for u in range(n_units):
        if group > 1:
            tile_idx, half = u // 2, u % 2
        else:
            tile_idx, half = u, None
        for src, dst in ((k_ref, kab_ref), (v_ref, vab_ref)):
            tile = src[0, :, tile_idx * LANE:(tile_idx + 1) * LANE]
            if half is None:
                a_part = jnp.where(lo_half, tile, 0.0)
                b_part = jnp.where(lo_half, 0.0, tile)
            elif half == 0:
                a_part = jnp.where(lo_half, tile, 0.0)
                b_part = pltpu.roll(a_part, HEAD_DIM, axis=1)
            else:
                b_part = jnp.where(lo_half, 0.0, tile)
                a_part = pltpu.roll(b_part, HEAD_DIM, axis=1)
            dst[0, 0, u, 0:kb, 0:LANE] = a_part.astype(BF16)
            dst[0, 0, u, kb:2 * kb, 0:LANE] = b_part.astype(BF16)
        vab_ref[0, 0, u, 0:kb, LANE:2 * LANE] = ones_lo
        vab_ref[0, 0, u, kb:2 * kb, LANE:2 * LANE] = ones_hi


def _pair_operands(k_arr, k_cb, v_arr, v_cb, kb, group, npairs):
    b, t_k = k_arr.shape[0], k_arr.shape[1]
    wk = ATT_KV_HEADS * HEAD_DIM
    n_units = npairs // (group // 2) if group > 1 else npairs
    nkb = t_k // kb
    return pl.pallas_call(
        functools.partial(_pair_operands_body, kb=kb, group=group, n_units=n_units),
        out_shape=(jax.ShapeDtypeStruct((b, nkb, n_units, 2 * kb, LANE), BF16),
                   jax.ShapeDtypeStruct((b, nkb, n_units, 2 * kb, 2 * LANE), BF16)),
        grid=(b, nkb),
        in_specs=[pl.BlockSpec((1, kb, wk), lambda bi, j: (bi, j, k_cb)),
                  pl.BlockSpec((1, kb, wk), lambda bi, j: (bi, j, v_cb))],
        out_specs=(pl.BlockSpec((1, 1, n_units, 2 * kb, LANE), lambda bi, j: (bi, j, 0, 0, 0)),
                   pl.BlockSpec((1, 1, n_units, 2 * kb, 2 * LANE), lambda bi, j: (bi, j, 0, 0, 0))),
        compiler_params=_cparams(("parallel", "parallel")),
        name="pair_operands",
    )(k_arr, v_arr)


def _flash(q_arr, q_cb, wq, k_arr, k_cb, v_arr, v_cb, q_gain, sink_row, *, mode, tq, kb, group,
           n_valid=None, causal=False, scores=None, tau=None, jb=None):
    b, t_q = q_arr.shape[0], q_arr.shape[1]
    t_k = k_arr.shape[1]
    nq = t_q // tq
    if mode == "band":
        assert tq == kb == WINDOW
        nk = 2
        kmap = lambda i, j: jnp.maximum(i - 1 + j, 0)
    elif mode == "dsa" and causal:
        nk = t_k // kb
        kmap = lambda i, j: jnp.minimum(j, ((i + 1) * tq - 1) // kb)
    else:
        nk = t_k // kb
        kmap = lambda i, j: j
    grid = (b, nq, nk)
    qmap = lambda i, j: i
    fold = _fold_steps(nq, tq, kb) if (mode == "dsa" and causal) else None
    if fold:
        grid = (b, nq // 2, fold)
        qmap = lambda r, s: _fold_step(r, s, nq, tq, kb)[0]
        kmap = lambda r, s: _fold_step(r, s, nq, tq, kb)[1]
    npairs = wq // LANE
    kab, vab = _pair_operands(k_arr, k_cb, v_arr, v_cb, kb, group, npairs)
    n_units = kab.shape[2]
    in_specs = [pl.BlockSpec((1, tq, wq), lambda bi, i, j: (bi, qmap(i, j), q_cb)),
                pl.BlockSpec((1, 1, n_units, 2 * kb, LANE), lambda bi, i, j: (bi, kmap(i, j), 0, 0, 0)),
                pl.BlockSpec((1, 1, n_units, 2 * kb, 2 * LANE), lambda bi, i, j: (bi, kmap(i, j), 0, 0, 0)),
                pl.BlockSpec((1, LANE), lambda bi, i, j: (0, 0)),
                pl.BlockSpec((1, wq), lambda bi, i, j: (0, 0))]
    args = [q_arr, kab, vab, q_gain, sink_row]
    if mode == "dsa":
        in_specs += [pl.BlockSpec((1, tq, kb), lambda bi, i, j: (bi, qmap(i, j), kmap(i, j))),
                     pl.BlockSpec((1, tq, LANE), lambda bi, i, j: (bi, qmap(i, j), 0)),
                     pl.BlockSpec((1, tq, LANE), lambda bi, i, j: (bi, qmap(i, j), 0))]
        args += [scores, tau, jb]
    body = functools.partial(_flash_body, mode=mode, tq=tq, kb=kb, wq=wq, group=group,
                             n_valid=n_valid, nk=nk, causal=causal, fold_nq=nq if fold else 0)
    return pl.pallas_call(
        body,
        out_shape=jax.ShapeDtypeStruct((b, t_q, wq), BF16),
        grid=grid,
        in_specs=in_specs,
        out_specs=pl.BlockSpec((1, tq, wq), lambda bi, i, j: (bi, qmap(i, j), 0)),
        scratch_shapes=[pltpu.VMEM((npairs, tq, LANE), BF16),
                        pltpu.VMEM((npairs, tq, LANE), F32),
                        pltpu.VMEM((npairs, tq, LANE), F32),
                        pltpu.VMEM((npairs, tq, LANE), F32),
                        pltpu.VMEM((tq, kb), BF16)],
        compiler_params=_cparams(("parallel", "parallel", "arbitrary")),
        name="flash_" + mode,
    )(*args)


def _dsa_select_body(qi_ref, kw_ref, ki_ref, sc_ref, tau_ref, jb_ref, sc_scr, top1_scr, top2_scr, *, tq, kb, nk, k_sel,
                     n_valid, causal, idx_bits, fold_nq):
    if fold_nq:
        i, j, _, is_last, fold_active = _fold_step(pl.program_id(1), pl.program_id(2), fold_nq, tq, kb)
    else:
        i, j = pl.program_id(1), pl.program_id(2)
        is_last = j == nk - 1
    qpos = i * tq + lax.broadcasted_iota(jnp.int32, (tq, 1), 0)
    if causal:
        n_adm = (qpos // CHUNK + 1) * CHUNK
        active = fold_active if fold_nq else j * kb < (i + 1) * tq
        n_blocks = _causal_blocks(i, tq, kb)
    else:
        n_adm = jnp.full((tq, 1), n_valid, jnp.int32)
        active = None
        n_blocks = nk

    def compute():
        ka = ki_ref[0]
        kab = jnp.concatenate([ka, pltpu.roll(ka, HEAD_DIM, axis=1)], axis=0).astype(BF16)
        kw = kw_ref[0]
        acc = jnp.zeros((tq, kb), F32)
        q4 = jnp.concatenate([qi_ref[0, :, p * LANE:(p + 1) * LANE].astype(BF16) for p in range(IDX_HEADS // 2)],
                             axis=0)
        s = lax.dot_general(q4, kab, (((1,), (1,)), ((), ())), preferred_element_type=F32)
        for p in range(IDX_HEADS // 2):
            for h in range(2):
                col = HEAD_DIM + 2 * p + h
                w = (kw[:, col:col + 1] * IDX_HEADS ** -0.5) * IDX_DIM ** -0.5
                acc = acc + jnp.maximum(s[p * tq:(p + 1) * tq, h * kb:(h + 1) * kb], 0.0) * w
        acc = jnp.where(acc == 0.0, 0.0, acc)
        kidx = j * kb + lax.broadcasted_iota(jnp.int32, (tq, kb), 1)
        x = jnp.where(kidx < n_adm, acc, -jnp.inf)
        sc_scr[j] = x
        sc_ref[0] = x
        m1, m2 = top1_scr[...], top2_scr[...]
        first = j == 0
        m1 = jnp.where(first, -jnp.inf, m1)
        m2 = jnp.where(first, -jnp.inf, m2)
        for a in range(kb // LANE):
            xa = x[:, a * LANE:(a + 1) * LANE]
            m2 = jnp.maximum(m2, jnp.minimum(m1, xa))
            m1 = jnp.maximum(m1, xa)
        top1_scr[...] = m1
        top2_scr[...] = m2

    if active is None:
        compute()
    else:
        pl.when(active)(compute)

        if not fold_nq:
            @pl.when(jnp.logical_not(active))
            def _():
                sc_ref[0] = jnp.full((tq, kb), -jnp.inf, F32)

    @pl.when(is_last)
    def _():
        kf = float(k_sel)

        rsz = min(tq, LANE)

        def count(make_pred):
            starts = range(0, tq, rsz)
            preds = [make_pred(lambda col, r0=r0: jnp.broadcast_to(col[r0:r0 + rsz], (rsz, LANE))) for r0 in starts]
            accs = []
            for r0, pred in zip(starts, preds):
                def blk(jj, acc, r0=r0, pred=pred):
                    for a in range(kb // LANE):
                        x = sc_scr[jj, r0:r0 + rsz, a * LANE:(a + 1) * LANE]
                        acc = acc + jnp.where(pred(x, jj * kb + a * LANE), 1.0, 0.0)
                    return acc
                accs.append(lax.fori_loop(0, n_blocks, blk, jnp.zeros((rsz, LANE), F32)))
            return jnp.sum(jnp.concatenate(accs, axis=0), axis=1, keepdims=True)

        def key_to_float(tu):
            cs = tu ^ INT_MIN
            fb = jnp.where(cs >= 0, cs, cs ^ 0x7FFFFFFF)
            return lax.bitcast_convert_type(fb, F32)

        def bit_body(it, tu):
            cand = tu | jnp.left_shift(jnp.int32(1), 31 - it)
            thr = key_to_float(cand)
            c = count(lambda widen: (lambda x, base, t=widen(thr): x >= t))
            return jnp.where(c >= kf, cand, tu)

        full = n_adm >= k_sel

        def float_to_key(x):
            bits = lax.bitcast_convert_type(x, jnp.int32)
            return (bits ^ (jnp.right_shift(bits, 31) & 0x7FFFFFFF)) ^ INT_MIN

        hi_key = float_to_key(jnp.max(top1_scr[...], axis=1, keepdims=True))
        lo_key = float_to_key(jnp.min(top2_scr[...], axis=1, keepdims=True))
        shared = jnp.where(full, lax.clz(hi_key ^ lo_key), 32)
        n_skip = jnp.min(shared.astype(F32)).astype(jnp.int32)
        keep = jnp.where(n_skip == 0, 0, jnp.left_shift(jnp.int32(-1), jnp.minimum(32 - n_skip, 31)))
        tu = lax.fori_loop(n_skip, 32, bit_body, hi_key & keep)
        tau = jnp.where(full, key_to_float(tu), -jnp.inf)
        c_gt = count(lambda widen: (lambda x, base, t=widen(tau): x > t))
        c_ge = count(lambda widen: (lambda x, base, t=widen(tau): x >= t))
        need = kf - c_gt
        tau_ref[0] = jnp.broadcast_to(tau, (tq, LANE))
        jb_ref[0] = jnp.full((tq, LANE), 2 ** 30, jnp.int32)
        surplus = jnp.max(jnp.where(full, (c_ge - c_gt) - need, 0.0))

        @pl.when(surplus > 0.5)
        def _():
            def tie_body(it, cut):
                cand = cut | jnp.left_shift(jnp.int32(1), idx_bits - 1 - it)

                def make_pred(widen):
                    t, cnd = widen(tau), widen(cand)
                    lane_idx = lax.broadcasted_iota(jnp.int32, (rsz, LANE), 1)
                    return lambda x, base: (x == t) & (base + lane_idx < cnd)
                c = count(make_pred)
                return jnp.where(c < need, cand, cut)

            cut = lax.fori_loop(0, idx_bits, tie_body, jnp.zeros((tq, 1), jnp.int32))
            jb_ref[0] = jnp.broadcast_to(cut, (tq, LANE))


def _dsa_select(z, qi_cb, kw_cb, ki_n, *, tq, kb, k_sel, n_valid, causal):
    b, t_q = z.shape[0], z.shape[1]
    t_k = ki_n.shape[1]
    nq, nk = t_q // tq, t_k // kb
    assert k_sel <= 2 * LANE, "the search bounds use the top two scores of each of the 128 lane columns"
    kmap = (lambda i, j: jnp.minimum(j, ((i + 1) * tq - 1) // kb)) if causal else (lambda i, j: j)
    omap = lambda i, j: j
    qmap = lambda i, j: i
    grid = (b, nq, nk)
    fold = _fold_steps(nq, tq, kb) if causal else None
    if fold:
        grid = (b, nq // 2, fold)
        qmap = lambda r, s: _fold_step(r, s, nq, tq, kb)[0]
        kmap = omap = lambda r, s: _fold_step(r, s, nq, tq, kb)[1]
    body = functools.partial(_dsa_select_body, tq=tq, kb=kb, nk=nk, k_sel=k_sel, n_valid=n_valid,
                             causal=causal, idx_bits=max(1, (t_k - 1).bit_length()), fold_nq=nq if fold else 0)
    return pl.pallas_call(
        body,
        out_shape=(jax.ShapeDtypeStruct((b, t_q, t_k), F32),
                   jax.ShapeDtypeStruct((b, t_q, LANE), F32),
                   jax.ShapeDtypeStruct((b, t_q, LANE), jnp.int32)),
        grid=grid,
        in_specs=[pl.BlockSpec((1, tq, IDX_HEADS * IDX_DIM), lambda bi, i, j: (bi, qmap(i, j), qi_cb)),
                  pl.BlockSpec((1, tq, LANE), lambda bi, i, j: (bi, qmap(i, j), kw_cb)),
                  pl.BlockSpec((1, kb, LANE), lambda bi, i, j: (bi, kmap(i, j), 0))],
        out_specs=(pl.BlockSpec((1, tq, kb), lambda bi, i, j: (bi, qmap(i, j), omap(i, j))),
                   pl.BlockSpec((1, tq, LANE), lambda bi, i, j: (bi, qmap(i, j), 0)),
                   pl.BlockSpec((1, tq, LANE), lambda bi, i, j: (bi, qmap(i, j), 0))),
        scratch_shapes=[pltpu.VMEM((nk, tq, kb), F32), pltpu.VMEM((tq, LANE), F32), pltpu.VMEM((tq, LANE), F32)],
        compiler_params=_cparams(("parallel", "parallel", "arbitrary")),
        name="dsa_select",
    )(z, z, ki_n)


def _rwkv_body(zr_ref, zk_ref, zv_ref, zw_ref, za_ref, zg_ref,
               sr_ref, sk_ref, sv_ref, sw_ref, sa_ref, sg_ref,
               mr_ref, mk_ref, mv_ref, mw_ref, ma_ref, mg_ref,
               vec_ref, w2_ref, a2_ref, g2_ref, s0_ref,
               mix_ref, sfin_ref,
               s_scr, prev_scr, prevl_scr, prevg_scr, y_scr, *, tc, lc, nt, pp):
    t = pl.program_id(2)

    @pl.when(t == 0)
    def _():
        s_scr[...] = s0_ref[0]
        prev_scr[0:1, :] = sr_ref[0]
        prev_scr[1:2, :] = sk_ref[0]
        prev_scr[2:3, :] = sv_ref[0]
        prevl_scr[0:1, :] = sw_ref[0]
        prevl_scr[1:2, :] = sa_ref[0]
        prevg_scr[0:1, :] = sg_ref[0]

    row = lax.broadcasted_iota(jnp.int32, (tc, 1), 0)

    def shifted(z_ref, prow, mu_ref):
        z = z_ref[0]
        zp = jnp.where(row == 0, prow, pltpu.roll(z, 1, axis=0))
        return z + (zp - z) * mu_ref[...], z[tc - 1:tc, :]

    r, last_r = shifted(zr_ref, prev_scr[0:1, :], mr_ref)
    k, last_k = shifted(zk_ref, prev_scr[1:2, :], mk_ref)
    v, last_v = shifted(zv_ref, prev_scr[2:3, :], mv_ref)
    zw, last_w = shifted(zw_ref, prevl_scr[0:1, :], mw_ref)
    za, last_a = shifted(za_ref, prevl_scr[1:2, :], ma_ref)
    zg, last_g = shifted(zg_ref, prevg_scr[0:1, :], mg_ref)
    prev_scr[0:1, :] = last_r
    prev_scr[1:2, :] = last_k
    prev_scr[2:3, :] = last_v
    prevl_scr[0:1, :] = last_w
    prevl_scr[1:2, :] = last_a
    prevg_scr[0:1, :] = last_g

    w0, a0 = vec_ref[0:1, :], vec_ref[1:2, :]
    k_k, k_a, r_k = vec_ref[2:3, :], vec_ref[3:4, :], vec_ref[4:5, :]
    ln_w, ln_b = vec_ref[5:6, :], vec_ref[6:7, :]

    ones_blk = _head_block_matrix(LANE, 1.0)
    avg_blk = _head_block_matrix(LANE, 1.0 / HEAD_DIM)

    def per_head(x, blk):
        return jnp.concatenate([_dot_exact_rhs(x[:, i * LANE:(i + 1) * LANE], blk) for i in range(pp)], axis=1)

    xw = w0 + jnp.dot(jnp.tanh(zw).astype(BF16), w2_ref[...], preferred_element_type=F32)
    nx = -xw
    softplus = jnp.maximum(nx, 0.0) + jnp.log1p(jnp.exp(-jnp.abs(nx)))
    w_log = -softplus - 0.5
    lw = -jnp.exp(w_log)
    a = jax.nn.sigmoid(a0 + jnp.dot(za.astype(BF16), a2_ref[...], preferred_element_type=F32))
    g = jnp.dot(jax.nn.sigmoid(zg).astype(BF16), g2_ref[...], preferred_element_type=F32)
    kk = k * k_k
    kk = kk / jnp.maximum(jnp.sqrt(per_head(kk * kk, ones_blk)), 1e-12)
    k2 = k * (1.0 + (a - 1.0) * k_a)

    a_step = -kk
    b_step = kk * a

    lane = lax.broadcasted_iota(jnp.int32, (1, LANE), 1)
    m0 = lane < HEAD_DIM
    rr = lax.broadcasted_iota(jnp.int32, (lc, 2 * lc), 0)
    cc = lax.broadcasted_iota(jnp.int32, (lc, 2 * lc), 1)
    incl = jnp.where(cc < lc, cc, cc - lc) <= rr
    strict = (lax.broadcasted_iota(jnp.int32, (2 * lc, 2 * lc), 1)
              < lax.broadcasted_iota(jnp.int32, (2 * lc, 2 * lc), 0))
    tri = (lax.broadcasted_iota(jnp.int32, (lc, lc), 1)
           <= lax.broadcasted_iota(jnp.int32, (lc, lc), 0)).astype(BF16)
    nsteps = lc.bit_length() - 1
    nt_dims = ((1,), (1,))
    nn_dims = ((1,), (0,))
    tn_dims = ((0,), (0,))

    eye = (lax.broadcasted_iota(jnp.int32, (2 * lc, 2 * lc), 0)
           == lax.broadcasted_iota(jnp.int32, (2 * lc, 2 * lc), 1)).astype(F32)

    chunks = range(tc // lc)
    st = []
    for c in chunks:
        rows = slice(c * lc, (c + 1) * lc)
        lwc = lw[rows]
        cs = _dot_exact_lhs(tri, lwc)
        p_in = jnp.exp(-cs)
        at = a_step[rows] * jnp.exp(cs - lwc)
        bt = b_step[rows] * p_in
        kt = k2[rows] * p_in
        rt = r[rows] * jnp.exp(cs)
        vc = v[rows]
        p_last = jnp.exp(cs[lc - 1:lc, :])
        for pi in range(pp):
            cols = slice(pi * LANE, (pi + 1) * LANE)
            stack = lambda x: jnp.concatenate([jnp.where(m0, x[:, cols], 0.0),
                                               jnp.where(m0, 0.0, x[:, cols])], axis=0).astype(BF16)
            st.append(dict(c=c, pi=pi,
                           ar=jnp.concatenate([stack(at), rt[:, cols].astype(BF16)], axis=0),
                           bk=jnp.concatenate([stack(bt), stack(kt)], axis=0),
                           v_s=stack(vc), p_last=p_last[:, cols]))
    for d in st:
        gram = _dot_rw(d["ar"], d["bk"], nt_dims)
        d["t_p"] = jnp.where(strict, gram[0:2 * lc, 0:2 * lc], 0.0)
        d["w_ak"] = jnp.where(strict, gram[0:2 * lc, 2 * lc:4 * lc], 0.0).astype(BF16)
        d["w_rb"] = jnp.where(incl, gram[2 * lc:3 * lc, 0:2 * lc], 0.0).astype(BF16)
        d["w_rk"] = jnp.where(incl, gram[2 * lc:3 * lc, 2 * lc:4 * lc], 0.0).astype(BF16)
        d["minv"] = eye + d["t_p"]
    for d in st:
        d["t_p"] = _dot_rw(d["t_p"], d["t_p"], nn_dims)
    for step in range(nsteps - 1):
        for d in st:
            if step == nsteps - 2:
                d["minv"] = d["minv"] + _dot_rw(d["minv"], d["t_p"], nn_dims)
            else:
                both = _dot_rw(jnp.concatenate([d["minv"], d["t_p"]], axis=0), d["t_p"], nn_dims)
                d["minv"] = d["minv"] + both[0:2 * lc]
                d["t_p"] = both[2 * lc:4 * lc]
    for d in st:
        d["minv"] = d["minv"].astype(BF16)
        both = _dot_rw(jnp.concatenate([d["w_ak"], d["w_rk"]], axis=0), d["v_s"], nn_dims)
        d["wv"] = both[0:2 * lc]
        d["y_c"] = both[2 * lc:3 * lc]
    for d in st:
        both = _dot_rw(d["minv"], jnp.concatenate([d["ar"][0:2 * lc], d["wv"].astype(BF16)], axis=1), nn_dims)
        d["ma"] = both[:, 0:LANE]
        d["mwv"] = both[:, LANE:2 * LANE]
    for d in st:
        d["g"] = _dot_rw(d["ma"], d["bk"][0:2 * lc], tn_dims).astype(BF16)
        d["d"] = _dot_rw(jnp.concatenate([d["mwv"].astype(BF16), d["v_s"]], axis=0), d["bk"], tn_dims)
    s_cur = [s_scr[pi] for pi in range(pp)]
    for d in st:
        s_in = s_cur[d["pi"]]
        d["s0"] = s_in.astype(BF16)
        s_cur[d["pi"]] = ((s_in + _dot_rw(d["s0"], d["g"], nn_dims)) + d["d"]) * d["p_last"]
    for pi in range(pp):
        s_scr[pi] = s_cur[pi]
    for d in st:
        d["xr"] = _dot_rw(d["ar"], d["s0"], nt_dims)
    for d in st:
        d["u"] = (_dot_rw(d["minv"], d["xr"][0:2 * lc], nn_dims) + d["mwv"]).astype(BF16)
    for d in st:
        c, pi = d["c"], d["pi"]
        y_scr[c * lc:(c + 1) * lc, pi * LANE:(pi + 1) * LANE] = (
            (d["xr"][2 * lc:3 * lc] + _dot_rw(d["w_rb"], d["u"], nn_dims)) + d["y_c"])

    y = y_scr[...]
    mean = per_head(y, avg_blk)
    dev = y - mean
    var = per_head(dev * dev, avg_blk)
    yn = (dev * lax.rsqrt(var + RW_GN_EPS)) * ln_w + ln_b
    bonus = per_head((r * k2) * r_k, ones_blk) * v
    mix_ref[0] = ((yn + bonus) * g).astype(mix_ref.dtype)

    @pl.when(t == nt - 1)
    def _():
        sfin_ref[0] = s_scr[...]


def _rwkv(z, shift_prev, s0_pairs, mu, vecs, w2, a2, g2, d_model):
    b, t = z.shape[0], z.shape[1]
    npairs = d_model // LANE
    tc = min(t, 512)
    lc = min(CHUNK, t)
    nt = t // tc
    pp = 2 if tc // lc >= 4 else 8
    wp = pp * LANE
    ngroups = npairs // pp
    cb_w, cb_a, cb_g = 3 * npairs, 3 * npairs + 1, (3 * npairs + 2) // 2

    def zspec(width, cbf):
        return pl.BlockSpec((1, tc, width), lambda bi, p, ti: (bi, ti, cbf(p)))

    def sspec(width, cbf):
        return pl.BlockSpec((1, 1, width), lambda bi, p, ti: (bi, 0, cbf(p)))

    def mspec(width, cbf):
        return pl.BlockSpec((1, width), lambda bi, p, ti: (0, cbf(p)))

    cbfs = [(wp, lambda p: p), (wp, lambda p: ngroups + p), (wp, lambda p: 2 * ngroups + p),
            (LANE, lambda p: cb_w), (LANE, lambda p: cb_a), (2 * LANE, lambda p: cb_g)]
    in_specs = ([zspec(w, f) for w, f in cbfs] + [sspec(w, f) for w, f in cbfs] + [mspec(w, f) for w, f in cbfs]
                + [pl.BlockSpec((8, wp), lambda bi, p, ti: (0, p)),
                   pl.BlockSpec((LANE, wp), lambda bi, p, ti: (0, p)),
                   pl.BlockSpec((LANE, wp), lambda bi, p, ti: (0, p)),
                   pl.BlockSpec((2 * LANE, wp), lambda bi, p, ti: (0, p)),
                   pl.BlockSpec((1, pp, LANE, LANE), lambda bi, p, ti: (bi, p, 0, 0))])
    return pl.pallas_call(
        functools.partial(_rwkv_body, tc=tc, lc=lc, nt=nt, pp=pp),
        out_shape=(jax.ShapeDtypeStruct((b, t, d_model), BF16),
                   jax.ShapeDtypeStruct((b, npairs, LANE, LANE), F32)),
        grid=(b, ngroups, nt),
        in_specs=in_specs,
        out_specs=(pl.BlockSpec((1, tc, wp), lambda bi, p, ti: (bi, ti, p)),
                   pl.BlockSpec((1, pp, LANE, LANE), lambda bi, p, ti: (bi, p, 0, 0))),
        scratch_shapes=[pltpu.VMEM((pp, LANE, LANE), F32),
                        pltpu.VMEM((8, wp), F32),
                        pltpu.VMEM((8, LANE), F32),
                        pltpu.VMEM((8, 2 * LANE), F32),
                        pltpu.VMEM((tc, wp), F32)],
        compiler_params=_cparams(("parallel", "parallel", "arbitrary")),
        name="rwkv7",
    )(*([z] * 6), *([shift_prev] * 6), *([mu] * 6), vecs, w2, a2, g2, s0_pairs)


def _pad_cols(x, segments):
    parts = []
    for start, width, padded in segments:
        seg = x[..., start:start + width]
        if padded > width:
            seg = jnp.concatenate([seg, jnp.zeros(seg.shape[:-1] + (padded - width,), seg.dtype)], axis=-1)
        parts.append(seg)
    return jnp.concatenate(parts, axis=-1)


def _pad_rows(x, padded):
    return jnp.concatenate([x, jnp.zeros((padded - x.shape[0],) + x.shape[1:], x.dtype)], axis=0)


def _pairs_from_heads(s):
    b, h = s.shape[0], s.shape[1]
    s = s.reshape(b, h // 2, 2, HEAD_DIM, HEAD_DIM)
    z = jnp.zeros_like(s[:, :, 0])
    top = jnp.concatenate([s[:, :, 0], z], axis=-1)
    bot = jnp.concatenate([z, s[:, :, 1]], axis=-1)
    return jnp.concatenate([top, bot], axis=-2)


def _heads_from_pairs(sp):
    b, npairs = sp.shape[0], sp.shape[1]
    s = jnp.stack([sp[:, :, :HEAD_DIM, :HEAD_DIM], sp[:, :, HEAD_DIM:, HEAD_DIM:]], axis=2)
    return s.reshape(b, 2 * npairs, HEAD_DIM, HEAD_DIM)


def _tile_gain(g, width):
    return jnp.tile(g.astype(F32), width // HEAD_DIM).reshape(1, width)


def _mem_attend(z, memq_cb, km, vm, q_gain):
    t_q = z.shape[1]
    wq = MEM_HEADS * HEAD_DIM
    tq = min(t_q, 512)
    no_sink = jnp.full((1, wq), -jnp.inf, F32)
    return _flash(z, memq_cb, wq, km, 0, vm, 0, _tile_gain(q_gain, LANE), no_sink,
                  mode="all", tq=tq, kb=km.shape[1], group=1, n_valid=km.shape[1])


def _conv_ffn(x2d, b, t, gain, w_up, conv_w, w_down, prev):
    act, u_last = _up_conv(x2d, gain, w_up, conv_w, prev, t)
    return _mm_res([act], [w_down], x2d), u_last


def kernel(x_prompt, x_sample, state_rwkv_wkv, state_rwkv_shift, cache_swa_k, cache_swa_v, cache_dsa_k, cache_dsa_v, cache_dsa_idx_k, cache_mem_k, cache_mem_v, state_ffn_conv, mem_prompt, attn_norm, ffn_norm, mem_norm, mem_w_kv, mem_q_norm, mem_k_norm, a_w_in, a_mu, a_w0, a_w2, a_a0, a_a2, a_g2, a_k_k, a_k_a, a_r_k, a_ln_w, a_ln_b, a_w_out, b_w_in, b_q_norm, b_k_norm, b_sink, b_w_out, c_w_in, c_q_norm, c_k_norm, c_idx_k_norm, c_w_out, ffn_w_up, ffn_conv, ffn_w_down):
    bp, t, d = x_prompt.shape
    bd, s_len = x_sample.shape[:2]
    depth = attn_norm.shape[0]
    win_rows = cache_swa_k.shape[2]
    past = cache_dsa_k.shape[2] if cache_dsa_k.shape[0] else 0
    d_ff = ffn_w_down.shape[1]
    mem_tokens = mem_prompt.shape[1]
    q_cols = d
    kv_cols = ATT_KV_HEADS * HEAD_DIM
    memq_cols = MEM_HEADS * HEAD_DIM
    att_group = (d // HEAD_DIM) // ATT_KV_HEADS
    dec_lora = a_w2.shape[1]
    a_lora = a_a2.shape[1]
    g_lora = a_g2.shape[1]
    rw_cols = 3 * d + dec_lora + a_lora + g_lora
    k_sel_p = min(TOPK_MAX, t // 4)
    k_sel_s = min(TOPK_MAX, (past + s_len) // 4)
    assert g_lora == 2 * LANE and dec_lora <= LANE and a_lora <= LANE

    xp = x_prompt.reshape(bp * t, d)
    xs = x_sample.reshape(bd * s_len, d)

    rw_segments = [(0, 3 * d, 3 * d), (3 * d, dec_lora, LANE), (3 * d + dec_lora, a_lora, LANE),
                   (3 * d + dec_lora + a_lora, g_lora, g_lora)]
    rw_padded = 3 * d + 2 * LANE + g_lora
    o_qi = q_cols + 2 * kv_cols
    o_ki = o_qi + IDX_HEADS * IDX_DIM
    c_cols = o_ki + IDX_DIM + IDX_HEADS

    outs = {k: [] for k in ("p_rw_wkv", "p_rw_sh", "p_sw_k", "p_sw_v", "p_ds_k", "p_ds_v", "p_ds_i", "p_mk",
                            "p_mv", "p_cv", "s_rw_wkv", "s_rw_sh", "s_sw_k", "s_sw_v", "s_ds_k", "s_ds_v",
                            "s_ds_i", "s_cv")}

    def unpad_rw(row):
        return jnp.concatenate([row[..., :3 * d], row[..., 3 * d:3 * d + dec_lora],
                                row[..., 3 * d + LANE:3 * d + LANE + a_lora],
                                row[..., 3 * d + 2 * LANE:3 * d + 2 * LANE + g_lora]], axis=-1)

    for i in range(depth):
        kind, j = i % 3, i // 3
        if kind == 0:
            w_in = jnp.concatenate([_pad_cols(a_w_in[j], rw_segments), a_w_in[j][:, rw_cols:]], axis=1).astype(BF16)
            memq_cb = rw_padded // memq_cols
            zp = _mm_norm(xp, attn_norm[i], w_in).reshape(bp, t, -1)
            zs = _mm_norm(xs, attn_norm[i], w_in).reshape(bd, s_len, -1)
            mu = _pad_cols(a_mu[j].reshape(1, -1), rw_segments)
            mu = jnp.concatenate([mu, jnp.zeros((1, memq_cols), F32)], axis=1)
            vecs = jnp.stack([a_w0[j], a_a0[j], a_k_k[j], a_k_a[j], a_r_k[j].reshape(-1), a_ln_w[j], a_ln_b[j],
                              jnp.zeros((d,), F32)], axis=0)
            w2 = _pad_rows(a_w2[j], LANE).astype(BF16)
            a2 = _pad_rows(a_a2[j], LANE).astype(BF16)
            g2 = a_g2[j].astype(BF16)
            sh_p = jnp.zeros((bp, 1, zp.shape[-1]), F32)
            st_p = jnp.zeros((bp, d // LANE, LANE, LANE), F32)
            sh_s = _pad_cols(state_rwkv_shift[j], rw_segments)
            sh_s = jnp.concatenate([sh_s, jnp.zeros((bd, memq_cols), F32)], axis=1).reshape(bd, 1, -1)
            st_s = _pairs_from_heads(state_rwkv_wkv[j])
            mp, stp = _rwkv(zp, sh_p, st_p, mu, vecs, w2, a2, g2, d)
            ms, sts = _rwkv(zs, sh_s, st_s, mu, vecs, w2, a2, g2, d)
            outs["p_rw_sh"].append(unpad_rw(zp[:, -1]))
            outs["p_rw_wkv"].append(_heads_from_pairs(stp))
            outs["s_rw_sh"].append(unpad_rw(zs[:, -1]))
            outs["s_rw_wkv"].append(_heads_from_pairs(sts))
            w_out = a_w_out[j]
        elif kind == 1:
            w_in = b_w_in[j].astype(BF16)
            memq_cb = (q_cols + 2 * kv_cols) // memq_cols
            k_cb, v_cb = q_cols // kv_cols, q_cols // kv_cols + 1
            zp = _mm_norm(xp, attn_norm[i], w_in).reshape(bp, t, -1)
            zs = _mm_norm(xs, attn_norm[i], w_in).reshape(bd, s_len, -1)
            kgain = _tile_gain(b_k_norm[j], kv_cols)
            qgain = _tile_gain(b_q_norm[j], LANE)
            sink = jnp.repeat(b_sink[j].astype(F32), HEAD_DIM).reshape(1, q_cols)
            knp = _headnorm(zp.reshape(bp * t, -1), k_cb, kv_cols, kgain).reshape(bp, t, kv_cols)
            mp = _flash(zp, 0, q_cols, knp, 0, zp, v_cb, qgain, sink, mode="band", tq=WINDOW, kb=WINDOW,
                        group=att_group)
            outs["p_sw_k"].append(knp[:, t - win_rows:].reshape(bp, win_rows, ATT_KV_HEADS, HEAD_DIM))
            outs["p_sw_v"].append(zp[:, t - win_rows:, q_cols + kv_cols:q_cols + 2 * kv_cols]
                                  .reshape(bp, win_rows, ATT_KV_HEADS, HEAD_DIM))
            kns = _headnorm(zs.reshape(bd * s_len, -1), k_cb, kv_cols, kgain).reshape(bd, s_len, kv_cols)
            vs_new = zs[:, :, q_cols + kv_cols:q_cols + 2 * kv_cols]
            k_all = jnp.concatenate([cache_swa_k[j].reshape(bd, win_rows, kv_cols), kns], axis=1)
            v_all = jnp.concatenate([cache_swa_v[j].reshape(bd, win_rows, kv_cols), vs_new], axis=1)
            n_keys = win_rows + s_len
            n_pad = -(-n_keys // LANE) * LANE
            pad = jnp.zeros((bd, n_pad - n_keys, kv_cols), F32)
            ms = _flash(zs, 0, q_cols, jnp.concatenate([k_all, pad], axis=1), 0,
                        jnp.concatenate([v_all, pad], axis=1), 0, qgain, sink, mode="all", tq=s_len, kb=n_pad,
                        group=att_group, n_valid=n_keys)
            outs["s_sw_k"].append(k_all[:, n_keys - win_rows:].reshape(bd, win_rows, ATT_KV_HEADS, HEAD_DIM))
            outs["s_sw_v"].append(v_all[:, n_keys - win_rows:].reshape(bd, win_rows, ATT_KV_HEADS, HEAD_DIM))
            w_out = b_w_out[j]
        else:
            wc = c_w_in[j]
            w_in = jnp.concatenate([wc[:, :o_ki], wc[:, c_cols:],
                                    _pad_cols(wc, [(o_ki, IDX_DIM + IDX_HEADS, LANE)])], axis=1).astype(BF16)
            memq_cb = o_ki // memq_cols
            kw_cb = (o_ki + memq_cols) // LANE
            k_cb, v_cb = q_cols // kv_cols, q_cols // kv_cols + 1
            qi_cb = o_qi // (IDX_HEADS * IDX_DIM)
            zp = _mm_norm(xp, attn_norm[i], w_in).reshape(bp, t, -1)
            zs = _mm_norm(xs, attn_norm[i], w_in).reshape(bd, s_len, -1)
            kgain = _tile_gain(c_k_norm[j], kv_cols)
            qgain = _tile_gain(c_q_norm[j], LANE)
            igain = jnp.concatenate([c_idx_k_norm[j].astype(F32), jnp.zeros((LANE - IDX_DIM,), F32)]).reshape(1, LANE)
            no_sink = jnp.full((1, q_cols), -jnp.inf, F32)
            knp = _headnorm(zp.reshape(bp * t, -1), k_cb, kv_cols, kgain).reshape(bp, t, kv_cols)
            kip = _headnorm(zp.reshape(bp * t, -1), kw_cb, LANE, igain).reshape(bp, t, LANE)
            tq = min(t, DSA_TQ)
            kb = min(t, DSA_KB)
            sc, tau, cut = _dsa_select(zp, qi_cb, kw_cb, kip, tq=tq, kb=kb, k_sel=k_sel_p, n_valid=t, causal=True)
            ftq, fkb = min(t, DSA_FLASH_TQ), min(t, DSA_FLASH_KB)
            assert all(_causal_blocks(r // tq, tq, kb) * kb >= _causal_blocks(r // ftq, ftq, fkb) * fkb
                       for r in range(0, t, CHUNK))
            mp = _flash(zp, 0, q_cols, knp, 0, zp, v_cb, qgain, no_sink, mode="dsa", tq=ftq, kb=fkb,
                        group=att_group, n_valid=t, causal=True, scores=sc, tau=tau, jb=cut)
            outs["p_ds_k"].append(knp.reshape(bp, t, ATT_KV_HEADS, HEAD_DIM))
            outs["p_ds_v"].append(zp[:, :, q_cols + kv_cols:q_cols + 2 * kv_cols].reshape(bp, t, ATT_KV_HEADS, HEAD_DIM))
            outs["p_ds_i"].append(kip[:, :, :IDX_DIM])
            kns = _headnorm(zs.reshape(bd * s_len, -1), k_cb, kv_cols, kgain).reshape(bd, s_len, kv_cols)
            kis = _headnorm(zs.reshape(bd * s_len, -1), kw_cb, LANE, igain).reshape(bd, s_len, LANE)
            vs_new = zs[:, :, q_cols + kv_cols:q_cols + 2 * kv_cols]
            n_keys = past + s_len
            n_pad = -(-n_keys // LANE) * LANE
            zpad = lambda w: jnp.zeros((bd, n_pad - n_keys, w), F32)
            k_all = jnp.concatenate([cache_dsa_k[j].reshape(bd, past, kv_cols), kns, zpad(kv_cols)], axis=1)
            v_all = jnp.concatenate([cache_dsa_v[j].reshape(bd, past, kv_cols), vs_new, zpad(kv_cols)], axis=1)
            ki_cache = jnp.concatenate([cache_dsa_idx_k[j], jnp.zeros((bd, past, LANE - IDX_DIM), F32)], axis=-1)
            ki_all = jnp.concatenate([ki_cache, kis, zpad(LANE)], axis=1)
            sc, tau, cut = _dsa_select(zs, qi_cb, kw_cb, ki_all, tq=s_len, kb=n_pad, k_sel=k_sel_s, n_valid=n_keys,
                                       causal=False)
            ms = _flash(zs, 0, q_cols, k_all, 0, v_all, 0, qgain, no_sink, mode="dsa", tq=s_len, kb=n_pad,
                        group=att_group, n_valid=n_keys, causal=False, scores=sc, tau=tau, jb=cut)
            outs["s_ds_k"].append(kns.reshape(bd, s_len, ATT_KV_HEADS, HEAD_DIM))
            outs["s_ds_v"].append(vs_new.reshape(bd, s_len, ATT_KV_HEADS, HEAD_DIM))
            outs["s_ds_i"].append(kis[:, :, :IDX_DIM])
            w_out = c_w_out[j]

        kv_mem = _mm_norm(mem_prompt.reshape(bp * mem_tokens, d), mem_norm[i], mem_w_kv[i].astype(BF16))
        km_p = _headnorm(kv_mem, 0, memq_cols, _tile_gain(mem_k_norm[i], memq_cols)).reshape(bp, mem_tokens, memq_cols)
        vm_p = kv_mem[:, memq_cols:].reshape(bp, mem_tokens, memq_cols)
        outs["p_mk"].append(km_p.reshape(bp, mem_tokens, MEM_HEADS, HEAD_DIM))
        outs["p_mv"].append(vm_p.reshape(bp, mem_tokens, MEM_HEADS, HEAD_DIM))
        mo_p = _mem_attend(zp, memq_cb, km_p, vm_p, mem_q_norm[i])
        mo_s = _mem_attend(zs, memq_cb, cache_mem_k[i].reshape(bd, mem_tokens, memq_cols),
                           cache_mem_v[i].reshape(bd, mem_tokens, memq_cols), mem_q_norm[i])
        w_mix, w_mem = w_out[:d].astype(BF16), w_out[d:].astype(BF16)
        xp = _mm_res([mp.reshape(bp * t, d), mo_p.reshape(bp * t, memq_cols)], [w_mix, w_mem], xp)
        xs = _mm_res([ms.reshape(bd * s_len, d), mo_s.reshape(bd * s_len, memq_cols)], [w_mix, w_mem], xs)

        w_up, w_down = ffn_w_up[i].astype(BF16), ffn_w_down[i].astype(BF16)
        xp, cp = _conv_ffn(xp, bp, t, ffn_norm[i], w_up, ffn_conv[i], w_down,
                           jnp.zeros((bp, CONV_W - 1, 2 * d_ff), F32))
        xs, cs = _conv_ffn(xs, bd, s_len, ffn_norm[i], w_up, ffn_conv[i], w_down, state_ffn_conv[i])
        outs["p_cv"].append(cp)
        outs["s_cv"].append(cs)

    st = jnp.stack
    order = ("p_rw_wkv", "p_rw_sh", "p_sw_k", "p_sw_v", "p_ds_k", "p_ds_v", "p_ds_i", "p_mk", "p_mv", "p_cv",
             "s_rw_wkv", "s_rw_sh", "s_sw_k", "s_sw_v", "s_ds_k", "s_ds_v", "s_ds_i", "s_cv")
    return (xp.reshape(bp, t, d), xs.reshape(bd, s_len, d)) + tuple(st(outs[k]) for k in order)
```

```python
import functools
import math

import jax
import jax.numpy as jnp
from jax import lax
from jax.experimental import pallas as pl
from jax.experimental.pallas import tpu as pltpu

F32 = jnp.float32
BF16 = jnp.bfloat16

HEAD_DIM = 64
CHUNK = 64
NORM_EPS = 1e-6
RW_GN_EPS = HEAD_DIM * 1e-5
ATT_KV_HEADS = 4
WINDOW = 128
IDX_HEADS = 8
IDX_DIM = 64
TOPK_MAX = 256
MEM_HEADS = 4
CONV_W = 3

LANE = 128
VMEM_LIMIT = 52 * 1024 * 1024
NEG_BIG = -(2.0 ** 100)
INT_MIN = -2147483648
MM_ROWS = 1024
DSA_TQ = 256
DSA_KB = 512
DSA_FLASH_TQ = 256
DSA_FLASH_KB = 512


def _cparams(sem, vmem=VMEM_LIMIT):
    return pltpu.CompilerParams(dimension_semantics=sem, vmem_limit_bytes=vmem)


def _split3(a):
    a1 = a.astype(BF16)
    r1 = a - a1.astype(F32)
    a2 = r1.astype(BF16)
    r2 = r1 - a2.astype(F32)
    return a1, a2, r2.astype(BF16)


def _dot_exact_rhs(a, e):
    a1, a2, a3 = _split3(a)
    d = lambda x: jnp.dot(x, e, preferred_element_type=F32)
    return (d(a3) + d(a2)) + d(a1)


def _dot_exact_lhs(e, a):
    a1, a2, a3 = _split3(a)
    d = lambda x: jnp.dot(e, x, preferred_element_type=F32)
    return (d(a3) + d(a2)) + d(a1)


def _head_block_matrix(width, value):
    r = lax.broadcasted_iota(jnp.int32, (width, width), 0) // HEAD_DIM
    c = lax.broadcasted_iota(jnp.int32, (width, width), 1) // HEAD_DIM
    return jnp.where(r == c, value, 0.0).astype(BF16)


def _dot_rw(a, b, dims):
    return lax.dot_general(a.astype(BF16), b.astype(BF16), (dims, ((), ())), preferred_element_type=F32)


def _pick_tile(n, cap):
    best = None
    for t in range(LANE, min(n, cap) + 1, LANE):
        if n % t == 0:
            best = t
    assert best is not None, n
    return best


def _mm_norm_body(x_ref, g_ref, w_ref, o_ref, xn_ref):
    @pl.when(pl.program_id(1) == 0)
    def _():
        x = x_ref[...]
        ms = jnp.mean(x * x, axis=-1, keepdims=True)
        xn_ref[...] = ((x * lax.rsqrt(ms + NORM_EPS)) * g_ref[...]).astype(BF16)

    o_ref[...] = jnp.dot(xn_ref[...], w_ref[...], preferred_element_type=F32)


def _mm_norm(x, gain, w):
    m, k = x.shape
    n = w.shape[1]
    tm = min(m, MM_ROWS)
    tn = _pick_tile(n, 1536)
    return pl.pallas_call(
        _mm_norm_body,
        out_shape=jax.ShapeDtypeStruct((m, n), F32),
        grid=(m // tm, n // tn),
        in_specs=[pl.BlockSpec((tm, k), lambda i, j: (i, 0)),
                  pl.BlockSpec((1, k), lambda i, j: (0, 0)),
                  pl.BlockSpec((k, tn), lambda i, j: (0, j))],
        out_specs=pl.BlockSpec((tm, tn), lambda i, j: (i, j)),
        scratch_shapes=[pltpu.VMEM((tm, k), BF16)],
        compiler_params=_cparams(("parallel", "arbitrary")),
        name="mm_norm",
    )(x, gain.reshape(1, k), w)


def _mm_res_body(*refs, n_lhs):
    lhs = refs[:n_lhs]
    ws = refs[n_lhs:2 * n_lhs]
    r_ref, o_ref = refs[2 * n_lhs], refs[2 * n_lhs + 1]
    acc = jnp.dot(lhs[0][...], ws[0][...], preferred_element_type=F32)
    for a, w in zip(lhs[1:], ws[1:]):
        acc = acc + jnp.dot(a[...], w[...], preferred_element_type=F32)
    o_ref[...] = r_ref[...] + acc


def _mm_res(lhs_list, w_list, res):
    m, n = res.shape
    ktot = sum(a.shape[1] for a in lhs_list)
    tm = min(m, MM_ROWS)
    tn = _pick_tile(n, 1024 if ktot <= 3072 else 512)
    n_lhs = len(lhs_list)
    in_specs = [pl.BlockSpec((tm, a.shape[1]), lambda i, j: (i, 0)) for a in lhs_list]
    in_specs += [pl.BlockSpec((w.shape[0], tn), lambda i, j: (0, j)) for w in w_list]
    in_specs += [pl.BlockSpec((tm, tn), lambda i, j: (i, j))]
    return pl.pallas_call(
        functools.partial(_mm_res_body, n_lhs=n_lhs),
        out_shape=jax.ShapeDtypeStruct((m, n), F32),
        grid=(m // tm, n // tn),
        in_specs=in_specs,
        out_specs=pl.BlockSpec((tm, tn), lambda i, j: (i, j)),
        compiler_params=_cparams(("parallel", "arbitrary")),
        name="mm_res",
    )(*lhs_list, *w_list, res)


def _headnorm_body(x_ref, g_ref, o_ref, *, width):
    avg = _head_block_matrix(LANE, 1.0 / HEAD_DIM)
    for c in range(width // LANE):
        x = x_ref[:, c * LANE:(c + 1) * LANE]
        ms = _dot_exact_rhs(x * x, avg)
        o_ref[:, c * LANE:(c + 1) * LANE] = (x * lax.rsqrt(ms + NORM_EPS)) * g_ref[:, c * LANE:(c + 1) * LANE]


def _headnorm(x, col_block, width, gain_row):
    m = x.shape[0]
    tm = min(m, 1024)
    return pl.pallas_call(
        functools.partial(_headnorm_body, width=width),
        out_shape=jax.ShapeDtypeStruct((m, width), F32),
        grid=(m // tm,),
        in_specs=[pl.BlockSpec((tm, width), lambda i: (i, col_block)),
                  pl.BlockSpec((1, width), lambda i: (0, 0))],
        out_specs=pl.BlockSpec((tm, width), lambda i: (i, 0)),
        compiler_params=_cparams(("parallel",)),
        name="headnorm",
    )(x, gain_row)


def _up_conv_body(x_ref, g_ref, wa_ref, wb_ref, pa_ref, pb_ref, cwa_ref, cwb_ref, o_ref, la_ref, lb_ref,
                  xn_ref, ca_ref, cb_ref, *, tm, tiles_per_batch, bpt):
    i = pl.program_id(0)
    j = pl.program_id(1)
    rpb = tm // bpt

    @pl.when(j == 0)
    def _():
        x = x_ref[...]
        ms = jnp.mean(x * x, axis=-1, keepdims=True)
        xn_ref[...] = ((x * lax.rsqrt(ms + NORM_EPS)) * g_ref[...]).astype(BF16)

    if bpt == 1:
        @pl.when(i % tiles_per_batch == 0)
        def _():
            ca_ref[j] = pa_ref[0]
            cb_ref[j] = pb_ref[0]

    xn = xn_ref[...]
    row = lax.broadcasted_iota(jnp.int32, o_ref.shape, 0)
    off = row if bpt == 1 else row % rpb
    if bpt > 1:
        pick = (lax.broadcasted_iota(jnp.int32, (tm, bpt), 0) // rpb
                == lax.broadcasted_iota(jnp.int32, (tm, bpt), 1)).astype(BF16)

    def conv(w_ref, p_ref, c_ref, cw_ref, last_ref):
        u = jnp.dot(xn, w_ref[...], preferred_element_type=F32)
        if bpt == 1:
            car = c_ref[j]
            c0, c1 = car[0:1, :], car[1:2, :]
            c_ref[j] = u[tm - 2:tm, :]
            last_ref[0] = u[tm - 2:tm, :]
        else:
            c0 = _dot_exact_lhs(pick, p_ref[:, 0, :])
            c1 = _dot_exact_lhs(pick, p_ref[:, 1, :])
            for bi in range(bpt):
                last_ref[bi] = u[(bi + 1) * rpb - 2:(bi + 1) * rpb, :]
        u1 = jnp.where(off == 0, c1, pltpu.roll(u, 1, axis=0))
        u2 = jnp.where(off == 0, c0, jnp.where(off == 1, c1, pltpu.roll(u, 2, axis=0)))
        return (u2 * cw_ref[0:1, :] + u1 * cw_ref[1:2, :]) + u * cw_ref[2:3, :]

    a = conv(wa_ref, pa_ref, ca_ref, cwa_ref, la_ref)
    b = conv(wb_ref, pb_ref, cb_ref, cwb_ref, lb_ref)
    o_ref[...] = ((a * jax.nn.sigmoid(a)) * b).astype(o_ref.dtype)


def _up_conv(x, gain, w_up, conv_w, prev, t):
    m, k = x.shape
    f = w_up.shape[1] // 2
    b = m // t
    tm = min(m, MM_ROWS)
    bpt = max(1, tm // t)
    assert tm % t == 0 or t % tm == 0
    tn = _pick_tile(f, 512)
    nj = f // tn
    tpb = max(1, t // tm)
    act, la, lb = pl.pallas_call(
        functools.partial(_up_conv_body, tm=tm, tiles_per_batch=tpb, bpt=bpt),
        out_shape=(jax.ShapeDtypeStruct((m, f), BF16),
                   jax.ShapeDtypeStruct((b, CONV_W - 1, f), F32),
                   jax.ShapeDtypeStruct((b, CONV_W - 1, f), F32)),
        grid=(m // tm, nj),
        in_specs=[pl.BlockSpec((tm, k), lambda i, j: (i, 0)),
                  pl.BlockSpec((1, k), lambda i, j: (0, 0)),
                  pl.BlockSpec((k, tn), lambda i, j: (0, j)),
                  pl.BlockSpec((k, tn), lambda i, j: (0, nj + j)),
                  pl.BlockSpec((bpt, CONV_W - 1, tn), lambda i, j: (i // tpb, 0, j)),
                  pl.BlockSpec((bpt, CONV_W - 1, tn), lambda i, j: (i // tpb, 0, nj + j)),
                  pl.BlockSpec((CONV_W, tn), lambda i, j: (0, j)),
                  pl.BlockSpec((CONV_W, tn), lambda i, j: (0, nj + j))],
        out_specs=(pl.BlockSpec((tm, tn), lambda i, j: (i, j)),
                   pl.BlockSpec((bpt, CONV_W - 1, tn), lambda i, j: (i // tpb, 0, j)),
                   pl.BlockSpec((bpt, CONV_W - 1, tn), lambda i, j: (i // tpb, 0, j))),
        scratch_shapes=[pltpu.VMEM((tm, k), BF16),
                        pltpu.VMEM((nj, CONV_W - 1, tn), F32),
                        pltpu.VMEM((nj, CONV_W - 1, tn), F32)],
        compiler_params=_cparams(("arbitrary", "arbitrary")),
        name="up_conv_gate",
    )(x, gain.reshape(1, k), w_up, w_up, prev, prev, conv_w, conv_w)
    return act, jnp.concatenate([la, lb], axis=-1)


def _causal_blocks(i, tq, kb):
    return ((i + 1) * tq + kb - 1) // kb


def _fold_steps(nq, tq, kb):
    if nq % 2:
        return None
    return max(_causal_blocks(r, tq, kb) + _causal_blocks(nq - 1 - r, tq, kb) for r in range(nq // 2))


def _fold_step(r, step, nq, tq, kb):
    n_lo = _causal_blocks(r, tq, kb)
    hi = nq - 1 - r
    in_lo = step < n_lo
    i = jnp.where(in_lo, r, hi)
    last_blk = _causal_blocks(i, tq, kb) - 1
    j = jnp.minimum(jnp.where(in_lo, step, step - n_lo), last_blk)
    active = step < n_lo + _causal_blocks(hi, tq, kb)
    return i, j, active & (jnp.where(in_lo, step, step - n_lo) == 0), active & (j == last_blk), active


def _flash_body(*refs, mode, tq, kb, wq, group, n_valid, nk, causal, fold_nq):
    q_ref, k_ref, v_ref, qg_ref, sink_ref = refs[:5]
    pos = 5
    if mode == "dsa":
        sc_ref, tau_ref, jb_ref = refs[pos:pos + 3]
        pos += 3
    o_ref = refs[pos]
    qn_scr, acc_scr, m_scr, l_scr, bias_scr = refs[pos + 1:]

    if fold_nq:
        i, j, is_first, is_last, fold_active = _fold_step(pl.program_id(1), pl.program_id(2), fold_nq, tq, kb)
    else:
        i, j = pl.program_id(1), pl.program_id(2)
        is_first, is_last = j == 0, j == nk - 1
    npairs = wq // LANE
    pairs_per_unit = group // 2 if group > 1 else 1
    lane = lax.broadcasted_iota(jnp.int32, (1, LANE), 1)
    lo_half = lane < HEAD_DIM

    @pl.when(is_first)
    def _():
        avg = _head_block_matrix(LANE, 1.0 / HEAD_DIM)
        for p in range(npairs):
            x = q_ref[0, :, p * LANE:(p + 1) * LANE]
            ms = _dot_exact_rhs(x * x, avg)
            qn = ((x * lax.rsqrt(ms + NORM_EPS)) * qg_ref[...]) * (HEAD_DIM ** -0.5)
            qn_scr[p] = qn.astype(BF16)
        acc_scr[...] = jnp.zeros(acc_scr.shape, F32)
        l_scr[...] = jnp.zeros(l_scr.shape, F32)
        m_scr[...] = jnp.full(m_scr.shape, NEG_BIG, F32)

    if mode == "band":
        kblk = i - WINDOW // kb + j
        active = kblk >= 0
    elif mode == "dsa" and causal:
        kblk = j
        active = fold_active if fold_nq else j * kb < (i + 1) * tq
    else:
        kblk = j
        active = None

    def step():
        kidx = kblk * kb + lax.broadcasted_iota(jnp.int32, (tq, kb), 1)
        qpos = i * tq + lax.broadcasted_iota(jnp.int32, (tq, kb), 0)
        if mode == "band":
            qchunk = qpos // CHUNK
            sel = (kidx >= (qchunk - WINDOW // CHUNK) * CHUNK) & (kidx < (qchunk + 1) * CHUNK) & (kidx >= 0)
        elif mode == "dsa":
            x = sc_ref[0]
            tau = tau_ref[0][:, 0:1]
            jb = jb_ref[0][:, 0:1]
            adm = kidx < ((qpos // CHUNK + 1) * CHUNK if causal else n_valid)
            sel = adm & ((x > tau) | ((x == tau) & (kidx <= jb)))
        else:
            sel = kidx < n_valid
        bias_scr[...] = jnp.where(sel, 0.0, NEG_BIG).astype(BF16)

        ppu = pairs_per_unit
        for u in range(npairs // ppu):
            ps = slice(u * ppu, (u + 1) * ppu)
            s = lax.dot_general(qn_scr[ps].reshape(ppu * tq, LANE), k_ref[0, 0, u], (((1,), (1,)), ((), ())),
                                preferred_element_type=F32).reshape(ppu, tq, 2 * kb)
            bias = bias_scr[...][None]
            m_old = m_scr[ps]
            s0 = s[:, :, 0:kb].astype(BF16) + bias
            s1 = s[:, :, kb:2 * kb].astype(BF16) + bias
            mn0 = jnp.maximum(m_old[:, :, 0:1], jnp.max(s0, axis=2, keepdims=True).astype(F32))
            mn1 = jnp.maximum(m_old[:, :, HEAD_DIM:HEAD_DIM + 1], jnp.max(s1, axis=2, keepdims=True).astype(F32))
            p0 = jnp.exp(s0 - mn0.astype(BF16))
            p1 = jnp.exp(s1 - mn1.astype(BF16))
            pcat = jnp.concatenate([p0, p1], axis=2).reshape(ppu * tq, 2 * kb)
            pv = jnp.dot(pcat, v_ref[0, 0, u], preferred_element_type=F32)
            pv = pv.reshape(ppu, tq, 2 * LANE)
            mn = jnp.where(lo_half, mn0, mn1)
            alpha = jnp.exp(m_old - mn)
            acc_scr[ps] = acc_scr[ps] * alpha + pv[:, :, 0:LANE]
            l_scr[ps] = l_scr[ps] * alpha + pv[:, :, LANE:2 * LANE]
            m_scr[ps] = mn

    if active is None:
        step()
    else:
        pl.when(active)(step)

    @pl.when(is_last)
    def _():
        for p in range(npairs):
            den = l_scr[p] + jnp.exp(sink_ref[:, p * LANE:(p + 1) * LANE] - m_scr[p])
            o_ref[0, :, p * LANE:(p + 1) * LANE] = (acc_scr[p] / den).astype(o_ref.dtype)


def _pair_operands_body(k_ref, v_ref, kab_ref, vab_ref, *, kb, group, n_units):
    lane = lax.broadcasted_iota(jnp.int32, (1, LANE), 1)
    lo_half = lane < HEAD_DIM
    ones_lo = jnp.broadcast_to(jnp.where(lo_half, 1.0, 0.0), (kb, LANE)).astype(BF16)
    ones_hi = jnp.broadcast_to(jnp.where(lo_half, 0.0, 1.0), (kb, LANE)).astype(BF16)
    for u in range(n_units):
        if group > 1:
            tile_idx, half = u // 2, u % 2
        else:
            tile_idx, half = u, None
        for src, dst in ((k_ref, kab_ref), (v_ref, vab_ref)):
            tile = src[0, :, tile_idx * LANE:(tile_idx + 1) * LANE]
            if half is None:
                a_part = jnp.where(lo_half, tile, 0.0)
                b_part = jnp.where(lo_half, 0.0, tile)
            elif half == 0:
                a_part = jnp.where(lo_half, tile, 0.0)
                b_part = pltpu.roll(a_part, HEAD_DIM, axis=1)
            else:
                b_part = jnp.where(lo_half, 0.0, tile)
                a_part = pltpu.roll(b_part, HEAD_DIM, axis=1)
            dst[0, 0, u, 0:kb, 0:LANE] = a_part.astype(BF16)
            dst[0, 0, u, kb:2 * kb, 0:LANE] = b_part.astype(BF16)
        vab_ref[0, 0, u, 0:kb, LANE:2 * LANE] = ones_lo
        vab_ref[0, 0, u, kb:2 * kb, LANE:2 * LANE] = ones_hi


def _pair_operands(k_arr, k_cb, v_arr, v_cb, kb, group, npairs):
    b, t_k = k_arr.shape[0], k_arr.shape[1]
    wk = ATT_KV_HEADS * HEAD_DIM
    n_units = npairs // (group // 2) if group > 1 else npairs
    nkb = t_k // kb
    return pl.pallas_call(
        functools.partial(_pair_operands_body, kb=kb, group=group, n_units=n_units),
        out_shape=(jax.ShapeDtypeStruct((b, nkb, n_units, 2 * kb, LANE), BF16),
                   jax.ShapeDtypeStruct((b, nkb, n_units, 2 * kb, 2 * LANE), BF16)),
        grid=(b, nkb),
        in_specs=[pl.BlockSpec((1, kb, wk), lambda bi, j: (bi, j, k_cb)),
                  pl.BlockSpec((1, kb, wk), lambda bi, j: (bi, j, v_cb))],
        out_specs=(pl.BlockSpec((1, 1, n_units, 2 * kb, LANE), lambda bi, j: (bi, j, 0, 0, 0)),
                   pl.BlockSpec((1, 1, n_units, 2 * kb, 2 * LANE), lambda bi, j: (bi, j, 0, 0, 0))),
        compiler_params=_cparams(("parallel", "parallel")),
        name="pair_operands",
    )(k_arr, v_arr)


def _flash(q_arr, q_cb, wq, k_arr, k_cb, v_arr, v_cb, q_gain, sink_row, *, mode, tq, kb, group,
           n_valid=None, causal=False, scores=None, tau=None, jb=None):
    b, t_q = q_arr.shape[0], q_arr.shape[1]
    t_k = k_arr.shape[1]
    nq = t_q // tq
    if mode == "band":
        assert tq == kb == WINDOW
        nk = 2
        kmap = lambda i, j: jnp.maximum(i - 1 + j, 0)
    elif mode == "dsa" and causal:
        nk = t_k // kb
        kmap = lambda i, j: jnp.minimum(j, ((i + 1) * tq - 1) // kb)
    else:
        nk = t_k // kb
        kmap = lambda i, j: j
    grid = (b, nq, nk)
    qmap = lambda i, j: i
    fold = _fold_steps(nq, tq, kb) if (mode == "dsa" and causal) else None
    if fold:
        grid = (b, nq // 2, fold)
        qmap = lambda r, s: _fold_step(r, s, nq, tq, kb)[0]
        kmap = lambda r, s: _fold_step(r, s, nq, tq, kb)[1]
    npairs = wq // LANE
    kab, vab = _pair_operands(k_arr, k_cb, v_arr, v_cb, kb, group, npairs)
    n_units = kab.shape[2]
    in_specs = [pl.BlockSpec((1, tq, wq), lambda bi, i, j: (bi, qmap(i, j), q_cb)),
                pl.BlockSpec((1, 1, n_units, 2 * kb, LANE), lambda bi, i, j: (bi, kmap(i, j), 0, 0, 0)),
                pl.BlockSpec((1, 1, n_units, 2 * kb, 2 * LANE), lambda bi, i, j: (bi, kmap(i, j), 0, 0, 0)),
                pl.BlockSpec((1, LANE), lambda bi, i, j: (0, 0)),
                pl.BlockSpec((1, wq), lambda bi, i, j: (0, 0))]
    args = [q_arr, kab, vab, q_gain, sink_row]
    if mode == "dsa":
        in_specs += [pl.BlockSpec((1, tq, kb), lambda bi, i, j: (bi, qmap(i, j), kmap(i, j))),
                     pl.BlockSpec((1, tq, LANE), lambda bi, i, j: (bi, qmap(i, j), 0)),
                     pl.BlockSpec((1, tq, LANE), lambda bi, i, j: (bi, qmap(i, j), 0))]
        args += [scores, tau, jb]
    body = functools.partial(_flash_body, mode=mode, tq=tq, kb=kb, wq=wq, group=group,
                             n_valid=n_valid, nk=nk, causal=causal, fold_nq=nq if fold else 0)
    return pl.pallas_call(
        body,
        out_shape=jax.ShapeDtypeStruct((b, t_q, wq), BF16),
        grid=grid,
        in_specs=in_specs,
        out_specs=pl.BlockSpec((1, tq, wq), lambda bi, i, j: (bi, qmap(i, j), 0)),
        scratch_shapes=[pltpu.VMEM((npairs, tq, LANE), BF16),
                        pltpu.VMEM((npairs, tq, LANE), F32),
                        pltpu.VMEM((npairs, tq, LANE), F32),
                        pltpu.VMEM((npairs, tq, LANE), F32),
                        pltpu.VMEM((tq, kb), BF16)],
        compiler_params=_cparams(("parallel", "parallel", "arbitrary")),
        name="flash_" + mode,
    )(*args)


def _dsa_select_body(qi_ref, kw_ref, ki_ref, sc_ref, tau_ref, jb_ref, sc_scr, *, tq, kb, nk, k_sel,
                     n_valid, causal, idx_bits, fold_nq):
    if fold_nq:
        i, j, _, is_last, fold_active = _fold_step(pl.program_id(1), pl.program_id(2), fold_nq, tq, kb)
    else:
        i, j = pl.program_id(1), pl.program_id(2)
        is_last = j == nk - 1
    qpos = i * tq + lax.broadcasted_iota(jnp.int32, (tq, 1), 0)
    if causal:
        n_adm = (qpos // CHUNK + 1) * CHUNK
        active = fold_active if fold_nq else j * kb < (i + 1) * tq
        n_blocks = _causal_blocks(i, tq, kb)
    else:
        n_adm = jnp.full((tq, 1), n_valid, jnp.int32)
        active = None
        n_blocks = nk

    def compute():
        ka = ki_ref[0]
        kab = jnp.concatenate([ka, pltpu.roll(ka, HEAD_DIM, axis=1)], axis=0).astype(BF16)
        kw = kw_ref[0]
        acc = jnp.zeros((tq, kb), F32)
        q4 = jnp.concatenate([qi_ref[0, :, p * LANE:(p + 1) * LANE].astype(BF16) for p in range(IDX_HEADS // 2)],
                             axis=0)
        s = lax.dot_general(q4, kab, (((1,), (1,)), ((), ())), preferred_element_type=F32)
        for p in range(IDX_HEADS // 2):
            for h in range(2):
                col = HEAD_DIM + 2 * p + h
                w = (kw[:, col:col + 1] * IDX_HEADS ** -0.5) * IDX_DIM ** -0.5
                acc = acc + jnp.maximum(s[p * tq:(p + 1) * tq, h * kb:(h + 1) * kb], 0.0) * w
        acc = jnp.where(acc == 0.0, 0.0, acc)
        kidx = j * kb + lax.broadcasted_iota(jnp.int32, (tq, kb), 1)
        x = jnp.where(kidx < n_adm, acc, -jnp.inf)
        sc_scr[j] = x
        sc_ref[0] = x

    if active is None:
        compute()
    else:
        pl.when(active)(compute)

        if not fold_nq:
            @pl.when(jnp.logical_not(active))
            def _():
                sc_ref[0] = jnp.full((tq, kb), -jnp.inf, F32)

    @pl.when(is_last)
    def _():
        kf = float(k_sel)

        rsz = min(tq, LANE)

        def count(make_pred):
            starts = range(0, tq, rsz)
            preds = [make_pred(lambda col, r0=r0: jnp.broadcast_to(col[r0:r0 + rsz], (rsz, LANE))) for r0 in starts]
            accs = []
            for r0, pred in zip(starts, preds):
                def blk(jj, acc, r0=r0, pred=pred):
                    for a in range(kb // LANE):
                        x = sc_scr[jj, r0:r0 + rsz, a * LANE:(a + 1) * LANE]
                        acc = acc + jnp.where(pred(x, jj * kb + a * LANE), 1.0, 0.0)
                    return acc
                accs.append(lax.fori_loop(0, n_blocks, blk, jnp.zeros((rsz, LANE), F32)))
            return jnp.sum(jnp.concatenate(accs, axis=0), axis=1, keepdims=True)

        def key_to_float(tu):
            cs = tu ^ INT_MIN
            fb = jnp.where(cs >= 0, cs, cs ^ 0x7FFFFFFF)
            return lax.bitcast_convert_type(fb, F32)

        def bit_body(it, tu):
            cand = tu | jnp.left_shift(jnp.int32(1), 31 - it)
            thr = key_to_float(cand)
            c = count(lambda widen: (lambda x, base, t=widen(thr): x >= t))
            return jnp.where(c >= kf, cand, tu)

        tu = lax.fori_loop(0, 32, bit_body, jnp.zeros((tq, 1), jnp.int32))
        full = n_adm >= k_sel
        tau = jnp.where(full, key_to_float(tu), -jnp.inf)
        c_gt = count(lambda widen: (lambda x, base, t=widen(tau): x > t))
        c_ge = count(lambda widen: (lambda x, base, t=widen(tau): x >= t))
        need = kf - c_gt
        tau_ref[0] = jnp.broadcast_to(tau, (tq, LANE))
        jb_ref[0] = jnp.full((tq, LANE), 2 ** 30, jnp.int32)
        surplus = jnp.max(jnp.where(full, (c_ge - c_gt) - need, 0.0))

        @pl.when(surplus > 0.5)
        def _():
            def tie_body(it, cut):
                cand = cut | jnp.left_shift(jnp.int32(1), idx_bits - 1 - it)

                def make_pred(widen):
                    t, cnd = widen(tau), widen(cand)
                    lane_idx = lax.broadcasted_iota(jnp.int32, (rsz, LANE), 1)
                    return lambda x, base: (x == t) & (base + lane_idx < cnd)
                c = count(make_pred)
                return jnp.where(c < need, cand, cut)

            cut = lax.fori_loop(0, idx_bits, tie_body, jnp.zeros((tq, 1), jnp.int32))
            jb_ref[0] = jnp.broadcast_to(cut, (tq, LANE))


def _dsa_select(z, qi_cb, kw_cb, ki_n, *, tq, kb, k_sel, n_valid, causal):
    b, t_q = z.shape[0], z.shape[1]
    t_k = ki_n.shape[1]
    nq, nk = t_q // tq, t_k // kb
    kmap = (lambda i, j: jnp.minimum(j, ((i + 1) * tq - 1) // kb)) if causal else (lambda i, j: j)
    omap = lambda i, j: j
    qmap = lambda i, j: i
    grid = (b, nq, nk)
    fold = _fold_steps(nq, tq, kb) if causal else None
    if fold:
        grid = (b, nq // 2, fold)
        qmap = lambda r, s: _fold_step(r, s, nq, tq, kb)[0]
        kmap = omap = lambda r, s: _fold_step(r, s, nq, tq, kb)[1]
    body = functools.partial(_dsa_select_body, tq=tq, kb=kb, nk=nk, k_sel=k_sel, n_valid=n_valid,
                             causal=causal, idx_bits=max(1, (t_k - 1).bit_length()), fold_nq=nq if fold else 0)
    return pl.pallas_call(
        body,
        out_shape=(jax.ShapeDtypeStruct((b, t_q, t_k), F32),
                   jax.ShapeDtypeStruct((b, t_q, LANE), F32),
                   jax.ShapeDtypeStruct((b, t_q, LANE), jnp.int32)),
        grid=grid,
        in_specs=[pl.BlockSpec((1, tq, IDX_HEADS * IDX_DIM), lambda bi, i, j: (bi, qmap(i, j), qi_cb)),
                  pl.BlockSpec((1, tq, LANE), lambda bi, i, j: (bi, qmap(i, j), kw_cb)),
                  pl.BlockSpec((1, kb, LANE), lambda bi, i, j: (bi, kmap(i, j), 0))],
        out_specs=(pl.BlockSpec((1, tq, kb), lambda bi, i, j: (bi, qmap(i, j), omap(i, j))),
                   pl.BlockSpec((1, tq, LANE), lambda bi, i, j: (bi, qmap(i, j), 0)),
                   pl.BlockSpec((1, tq, LANE), lambda bi, i, j: (bi, qmap(i, j), 0))),
        scratch_shapes=[pltpu.VMEM((nk, tq, kb), F32)],
        compiler_params=_cparams(("parallel", "parallel", "arbitrary")),
        name="dsa_select",
    )(z, z, ki_n)


def _rwkv_body(zr_ref, zk_ref, zv_ref, zw_ref, za_ref, zg_ref,
               sr_ref, sk_ref, sv_ref, sw_ref, sa_ref, sg_ref,
               mr_ref, mk_ref, mv_ref, mw_ref, ma_ref, mg_ref,
               vec_ref, w2_ref, a2_ref, g2_ref, s0_ref,
               mix_ref, sfin_ref,
               s_scr, prev_scr, prevl_scr, prevg_scr, y_scr, *, tc, lc, nt, pp):
    t = pl.program_id(2)

    @pl.when(t == 0)
    def _():
        s_scr[...] = s0_ref[0]
        prev_scr[0:1, :] = sr_ref[0]
        prev_scr[1:2, :] = sk_ref[0]
        prev_scr[2:3, :] = sv_ref[0]
        prevl_scr[0:1, :] = sw_ref[0]
        prevl_scr[1:2, :] = sa_ref[0]
        prevg_scr[0:1, :] = sg_ref[0]

    row = lax.broadcasted_iota(jnp.int32, (tc, 1), 0)

    def shifted(z_ref, prow, mu_ref):
        z = z_ref[0]
        zp = jnp.where(row == 0, prow, pltpu.roll(z, 1, axis=0))
        return z + (zp - z) * mu_ref[...], z[tc - 1:tc, :]

    r, last_r = shifted(zr_ref, prev_scr[0:1, :], mr_ref)
    k, last_k = shifted(zk_ref, prev_scr[1:2, :], mk_ref)
    v, last_v = shifted(zv_ref, prev_scr[2:3, :], mv_ref)
    zw, last_w = shifted(zw_ref, prevl_scr[0:1, :], mw_ref)
    za, last_a = shifted(za_ref, prevl_scr[1:2, :], ma_ref)
    zg, last_g = shifted(zg_ref, prevg_scr[0:1, :], mg_ref)
    prev_scr[0:1, :] = last_r
    prev_scr[1:2, :] = last_k
    prev_scr[2:3, :] = last_v
    prevl_scr[0:1, :] = last_w
    prevl_scr[1:2, :] = last_a
    prevg_scr[0:1, :] = last_g

    w0, a0 = vec_ref[0:1, :], vec_ref[1:2, :]
    k_k, k_a, r_k = vec_ref[2:3, :], vec_ref[3:4, :], vec_ref[4:5, :]
    ln_w, ln_b = vec_ref[5:6, :], vec_ref[6:7, :]

    ones_blk = _head_block_matrix(LANE, 1.0)
    avg_blk = _head_block_matrix(LANE, 1.0 / HEAD_DIM)

    def per_head(x, blk):
        return jnp.concatenate([_dot_exact_rhs(x[:, i * LANE:(i + 1) * LANE], blk) for i in range(pp)], axis=1)

    xw = w0 + jnp.dot(jnp.tanh(zw).astype(BF16), w2_ref[...], preferred_element_type=F32)
    lw = -math.exp(-0.5) * jax.nn.sigmoid(xw)
    a = jax.nn.sigmoid(a0 + jnp.dot(za.astype(BF16), a2_ref[...], preferred_element_type=F32))
    g = jnp.dot(jax.nn.sigmoid(zg).astype(BF16), g2_ref[...], preferred_element_type=F32)
    kk = k * k_k
    kk = kk * jnp.minimum(lax.rsqrt(per_head(kk * kk, ones_blk)), 1e12)
    k2 = k * (1.0 + (a - 1.0) * k_a)

    a_step = -kk
    b_step = kk * a

    lane = lax.broadcasted_iota(jnp.int32, (1, LANE), 1)
    m0 = lane < HEAD_DIM
    rr = lax.broadcasted_iota(jnp.int32, (lc, 2 * lc), 0)
    cc = lax.broadcasted_iota(jnp.int32, (lc, 2 * lc), 1)
    incl = jnp.where(cc < lc, cc, cc - lc) <= rr
    strict = (lax.broadcasted_iota(jnp.int32, (2 * lc, 2 * lc), 1)
              < lax.broadcasted_iota(jnp.int32, (2 * lc, 2 * lc), 0))
    tri = (lax.broadcasted_iota(jnp.int32, (lc, lc), 1)
           <= lax.broadcasted_iota(jnp.int32, (lc, lc), 0)).astype(BF16)
    nsteps = lc.bit_length() - 1
    nt_dims = ((1,), (1,))
    nn_dims = ((1,), (0,))
    tn_dims = ((0,), (0,))

    chunks = range(tc // lc)
    st = []
    for c in chunks:
        rows = slice(c * lc, (c + 1) * lc)
        lwc = lw[rows]
        cs = _dot_exact_lhs(tri, lwc)
        p_in = jnp.exp(-cs)
        at = a_step[rows] * jnp.exp(cs - lwc)
        bt = b_step[rows] * p_in
        kt = k2[rows] * p_in
        rt = r[rows] * jnp.exp(cs)
        vc = v[rows]
        p_last = jnp.exp(cs[lc - 1:lc, :])
        for pi in range(pp):
            cols = slice(pi * LANE, (pi + 1) * LANE)
            stack = lambda x: jnp.concatenate([jnp.where(m0, x[:, cols], 0.0),
                                               jnp.where(m0, 0.0, x[:, cols])], axis=0).astype(BF16)
            st.append(dict(c=c, pi=pi,
                           ar=jnp.concatenate([stack(at), rt[:, cols].astype(BF16)], axis=0),
                           bk=jnp.concatenate([stack(bt), stack(kt)], axis=0),
                           v_s=stack(vc), p_last=p_last[:, cols]))
    for d in st:
        gram = _dot_rw(d["ar"], d["bk"], nt_dims)
        d["t_p"] = jnp.where(strict, gram[0:2 * lc, 0:2 * lc], 0.0)
        d["w_ak"] = jnp.where(strict, gram[0:2 * lc, 2 * lc:4 * lc], 0.0).astype(BF16)
        d["w_rb"] = jnp.where(incl, gram[2 * lc:3 * lc, 0:2 * lc], 0.0).astype(BF16)
        d["w_rk"] = jnp.where(incl, gram[2 * lc:3 * lc, 2 * lc:4 * lc], 0.0).astype(BF16)
    gw = 4 * lc
    lane_head = lax.broadcasted_iota(jnp.int32, (1, gw), 1) // lc
    eye_sbs = (lax.broadcasted_iota(jnp.int32, (lc, gw), 0)
               == lax.broadcasted_iota(jnp.int32, (lc, gw), 1) % lc).astype(F32)
    lane_pair = lax.broadcasted_iota(jnp.int32, (1, 2 * lc), 1)

    def blockdiag(y):
        yb = y.astype(BF16)
        return jnp.concatenate([jnp.where(lane_head == h, yb, jnp.zeros_like(yb)) for h in range(4)], axis=0)

    groups = []
    for gi in range(len(st) // 2):
        members = (st[2 * gi], st[2 * gi + 1])
        t_sbs = jnp.concatenate([m["t_p"][0:lc] + m["t_p"][lc:2 * lc] for m in members], axis=1)
        groups.append(dict(members=members, t=t_sbs, minv=eye_sbs + t_sbs))
    for grp in groups:
        grp["t"] = _dot_rw(grp["t"], blockdiag(grp["t"]), nn_dims)
    for step in range(nsteps - 1):
        for grp in groups:
            if step == nsteps - 2:
                grp["minv"] = grp["minv"] + _dot_rw(grp["minv"], blockdiag(grp["t"]), nn_dims)
            else:
                both = _dot_rw(jnp.concatenate([grp["minv"], grp["t"]], axis=0), blockdiag(grp["t"]), nn_dims)
                grp["minv"] = grp["minv"] + both[0:lc]
                grp["t"] = both[lc:2 * lc]
    for grp in groups:
        for idx, m in enumerate(grp["members"]):
            tile = grp["minv"][:, idx * 2 * lc:(idx + 1) * 2 * lc]
            m["minv"] = jnp.concatenate([jnp.where(lane_pair < lc, tile, 0.0),
                                         jnp.where(lane_pair < lc, 0.0, tile)], axis=0).astype(BF16)
    for d in st:
        both = _dot_rw(jnp.concatenate([d["w_ak"], d["w_rk"]], axis=0), d["v_s"], nn_dims)
        d["wv"] = both[0:2 * lc]
        d["y_c"] = both[2 * lc:3 * lc]
    for d in st:
        both = _dot_rw(d["minv"], jnp.concatenate([d["ar"][0:2 * lc], d["wv"].astype(BF16)], axis=1), nn_dims)
        d["ma"] = both[:, 0:LANE]
        d["mwv"] = both[:, LANE:2 * LANE]
    for d in st:
        d["g"] = _dot_rw(d["ma"], d["bk"][0:2 * lc], tn_dims).astype(BF16)
        d["d"] = _dot_rw(jnp.concatenate([d["mwv"].astype(BF16), d["v_s"]], axis=0), d["bk"], tn_dims)
    s_cur = [s_scr[pi] for pi in range(pp)]
    for d in st:
        s_in = s_cur[d["pi"]]
        d["s0"] = s_in.astype(BF16)
        s_cur[d["pi"]] = ((s_in + _dot_rw(d["s0"], d["g"], nn_dims)) + d["d"]) * d["p_last"]
    for pi in range(pp):
        s_scr[pi] = s_cur[pi]
    for d in st:
        d["xr"] = _dot_rw(d["ar"], d["s0"], nt_dims)
    for d in st:
        d["u"] = (_dot_rw(d["minv"], d["xr"][0:2 * lc], nn_dims) + d["mwv"]).astype(BF16)
    for d in st:
        c, pi = d["c"], d["pi"]
        y_scr[c * lc:(c + 1) * lc, pi * LANE:(pi + 1) * LANE] = (
            (d["xr"][2 * lc:3 * lc] + _dot_rw(d["w_rb"], d["u"], nn_dims)) + d["y_c"])

    y = y_scr[...]
    mean = per_head(y, avg_blk)
    dev = y - mean
    var = per_head(dev * dev, avg_blk)
    yn = (dev * lax.rsqrt(var + RW_GN_EPS)) * ln_w + ln_b
    bonus = per_head((r * k2) * r_k, ones_blk) * v
    mix_ref[0] = ((yn + bonus) * g).astype(mix_ref.dtype)

    @pl.when(t == nt - 1)
    def _():
        sfin_ref[0] = s_scr[...]


def _rwkv(z, shift_prev, s0_pairs, mu, vecs, w2, a2, g2, d_model):
    b, t = z.shape[0], z.shape[1]
    npairs = d_model // LANE
    tc = min(t, 512)
    lc = min(CHUNK, t)
    nt = t // tc
    pp = 2 if tc // lc >= 4 else 8
    wp = pp * LANE
    ngroups = npairs // pp
    cb_w, cb_a, cb_g = 3 * npairs, 3 * npairs + 1, (3 * npairs + 2) // 2

    def zspec(width, cbf):
        return pl.BlockSpec((1, tc, width), lambda bi, p, ti: (bi, ti, cbf(p)))

    def sspec(width, cbf):
        return pl.BlockSpec((1, 1, width), lambda bi, p, ti: (bi, 0, cbf(p)))

    def mspec(width, cbf):
        return pl.BlockSpec((1, width), lambda bi, p, ti: (0, cbf(p)))

    cbfs = [(wp, lambda p: p), (wp, lambda p: ngroups + p), (wp, lambda p: 2 * ngroups + p),
            (LANE, lambda p: cb_w), (LANE, lambda p: cb_a), (2 * LANE, lambda p: cb_g)]
    in_specs = ([zspec(w, f) for w, f in cbfs] + [sspec(w, f) for w, f in cbfs] + [mspec(w, f) for w, f in cbfs]
                + [pl.BlockSpec((8, wp), lambda bi, p, ti: (0, p)),
                   pl.BlockSpec((LANE, wp), lambda bi, p, ti: (0, p)),
                   pl.BlockSpec((LANE, wp), lambda bi, p, ti: (0, p)),
                   pl.BlockSpec((2 * LANE, wp), lambda bi, p, ti: (0, p)),
                   pl.BlockSpec((1, pp, LANE, LANE), lambda bi, p, ti: (bi, p, 0, 0))])
    return pl.pallas_call(
        functools.partial(_rwkv_body, tc=tc, lc=lc, nt=nt, pp=pp),
        out_shape=(jax.ShapeDtypeStruct((b, t, d_model), BF16),
                   jax.ShapeDtypeStruct((b, npairs, LANE, LANE), F32)),
        grid=(b, ngroups, nt),
        in_specs=in_specs,
        out_specs=(pl.BlockSpec((1, tc, wp), lambda bi, p, ti: (bi, ti, p)),
                   pl.BlockSpec((1, pp, LANE, LANE), lambda bi, p, ti: (bi, p, 0, 0))),
        scratch_shapes=[pltpu.VMEM((pp, LANE, LANE), F32),
                        pltpu.VMEM((8, wp), F32),
                        pltpu.VMEM((8, LANE), F32),
                        pltpu.VMEM((8, 2 * LANE), F32),
                        pltpu.VMEM((tc, wp), F32)],
        compiler_params=_cparams(("parallel", "parallel", "arbitrary")),
        name="rwkv7",
    )(*([z] * 6), *([shift_prev] * 6), *([mu] * 6), vecs, w2, a2, g2, s0_pairs)


def _pad_cols(x, segments):
    parts = []
    for start, width, padded in segments:
        seg = x[..., start:start + width]
        if padded > width:
            seg = jnp.concatenate([seg, jnp.zeros(seg.shape[:-1] + (padded - width,), seg.dtype)], axis=-1)
        parts.append(seg)
    return jnp.concatenate(parts, axis=-1)


def _pad_rows(x, padded):
    return jnp.concatenate([x, jnp.zeros((padded - x.shape[0],) + x.shape[1:], x.dtype)], axis=0)


def _pairs_from_heads(s):
    b, h = s.shape[0], s.shape[1]
    s = s.reshape(b, h // 2, 2, HEAD_DIM, HEAD_DIM)
    z = jnp.zeros_like(s[:, :, 0])
    top = jnp.concatenate([s[:, :, 0], z], axis=-1)
    bot = jnp.concatenate([z, s[:, :, 1]], axis=-1)
    return jnp.concatenate([top, bot], axis=-2)


def _heads_from_pairs(sp):
    b, npairs = sp.shape[0], sp.shape[1]
    s = jnp.stack([sp[:, :, :HEAD_DIM, :HEAD_DIM], sp[:, :, HEAD_DIM:, HEAD_DIM:]], axis=2)
    return s.reshape(b, 2 * npairs, HEAD_DIM, HEAD_DIM)


def _tile_gain(g, width):
    return jnp.tile(g.astype(F32), width // HEAD_DIM).reshape(1, width)


def _mem_attend(z, memq_cb, km, vm, q_gain):
    t_q = z.shape[1]
    wq = MEM_HEADS * HEAD_DIM
    tq = min(t_q, 512)
    no_sink = jnp.full((1, wq), -jnp.inf, F32)
    return _flash(z, memq_cb, wq, km, 0, vm, 0, _tile_gain(q_gain, LANE), no_sink,
                  mode="all", tq=tq, kb=km.shape[1], group=1, n_valid=km.shape[1])


def _conv_ffn(x2d, b, t, gain, w_up, conv_w, w_down, prev):
    act, u_last = _up_conv(x2d, gain, w_up, conv_w, prev, t)
    return _mm_res([act], [w_down], x2d), u_last


def kernel(x_prompt, x_sample, state_rwkv_wkv, state_rwkv_shift, cache_swa_k, cache_swa_v, cache_dsa_k, cache_dsa_v, cache_dsa_idx_k, cache_mem_k, cache_mem_v, state_ffn_conv, mem_prompt, attn_norm, ffn_norm, mem_norm, mem_w_kv, mem_q_norm, mem_k_norm, a_w_in, a_mu, a_w0, a_w2, a_a0, a_a2, a_g2, a_k_k, a_k_a, a_r_k, a_ln_w, a_ln_b, a_w_out, b_w_in, b_q_norm, b_k_norm, b_sink, b_w_out, c_w_in, c_q_norm, c_k_norm, c_idx_k_norm, c_w_out, ffn_w_up, ffn_conv, ffn_w_down):
    bp, t, d = x_prompt.shape
    bd, s_len = x_sample.shape[:2]
    depth = attn_norm.shape[0]
    win_rows = cache_swa_k.shape[2]
    past = cache_dsa_k.shape[2] if cache_dsa_k.shape[0] else 0
    d_ff = ffn_w_down.shape[1]
    mem_tokens = mem_prompt.shape[1]
    q_cols = d
    kv_cols = ATT_KV_HEADS * HEAD_DIM
    memq_cols = MEM_HEADS * HEAD_DIM
    att_group = (d // HEAD_DIM) // ATT_KV_HEADS
    dec_lora = a_w2.shape[1]
    a_lora = a_a2.shape[1]
    g_lora = a_g2.shape[1]
    rw_cols = 3 * d + dec_lora + a_lora + g_lora
    k_sel_p = min(TOPK_MAX, t // 4)
    k_sel_s = min(TOPK_MAX, (past + s_len) // 4)
    assert g_lora == 2 * LANE and dec_lora <= LANE and a_lora <= LANE

    xp = x_prompt.reshape(bp * t, d)
    xs = x_sample.reshape(bd * s_len, d)

    rw_segments = [(0, 3 * d, 3 * d), (3 * d, dec_lora, LANE), (3 * d + dec_lora, a_lora, LANE),
                   (3 * d + dec_lora + a_lora, g_lora, g_lora)]
    rw_padded = 3 * d + 2 * LANE + g_lora
    o_qi = q_cols + 2 * kv_cols
    o_ki = o_qi + IDX_HEADS * IDX_DIM
    c_cols = o_ki + IDX_DIM + IDX_HEADS

    outs = {k: [] for k in ("p_rw_wkv", "p_rw_sh", "p_sw_k", "p_sw_v", "p_ds_k", "p_ds_v", "p_ds_i", "p_mk",
                            "p_mv", "p_cv", "s_rw_wkv", "s_rw_sh", "s_sw_k", "s_sw_v", "s_ds_k", "s_ds_v",
                            "s_ds_i", "s_cv")}

    def unpad_rw(row):
        return jnp.concatenate([row[..., :3 * d], row[..., 3 * d:3 * d + dec_lora],
                                row[..., 3 * d + LANE:3 * d + LANE + a_lora],
                                row[..., 3 * d + 2 * LANE:3 * d + 2 * LANE + g_lora]], axis=-1)

    for i in range(depth):
        kind, j = i % 3, i // 3
        if kind == 0:
            w_in = jnp.concatenate([_pad_cols(a_w_in[j], rw_segments), a_w_in[j][:, rw_cols:]], axis=1).astype(BF16)
            memq_cb = rw_padded // memq_cols
            zp = _mm_norm(xp, attn_norm[i], w_in).reshape(bp, t, -1)
            zs = _mm_norm(xs, attn_norm[i], w_in).reshape(bd, s_len, -1)
            mu = _pad_cols(a_mu[j].reshape(1, -1), rw_segments)
            mu = jnp.concatenate([mu, jnp.zeros((1, memq_cols), F32)], axis=1)
            vecs = jnp.stack([a_w0[j], a_a0[j], a_k_k[j], a_k_a[j], a_r_k[j].reshape(-1), a_ln_w[j], a_ln_b[j],
                              jnp.zeros((d,), F32)], axis=0)
            w2 = _pad_rows(a_w2[j], LANE).astype(BF16)
            a2 = _pad_rows(a_a2[j], LANE).astype(BF16)
            g2 = a_g2[j].astype(BF16)
            sh_p = jnp.zeros((bp, 1, zp.shape[-1]), F32)
            st_p = jnp.zeros((bp, d // LANE, LANE, LANE), F32)
            sh_s = _pad_cols(state_rwkv_shift[j], rw_segments)
            sh_s = jnp.concatenate([sh_s, jnp.zeros((bd, memq_cols), F32)], axis=1).reshape(bd, 1, -1)
            st_s = _pairs_from_heads(state_rwkv_wkv[j])
            mp, stp = _rwkv(zp, sh_p, st_p, mu, vecs, w2, a2, g2, d)
            ms, sts = _rwkv(zs, sh_s, st_s, mu, vecs, w2, a2, g2, d)
            outs["p_rw_sh"].append(unpad_rw(zp[:, -1]))
            outs["p_rw_wkv"].append(_heads_from_pairs(stp))
            outs["s_rw_sh"].append(unpad_rw(zs[:, -1]))
            outs["s_rw_wkv"].append(_heads_from_pairs(sts))
            w_out = a_w_out[j]
        elif kind == 1:
            w_in = b_w_in[j].astype(BF16)
            memq_cb = (q_cols + 2 * kv_cols) // memq_cols
            k_cb, v_cb = q_cols // kv_cols, q_cols // kv_cols + 1
            zp = _mm_norm(xp, attn_norm[i], w_in).reshape(bp, t, -1)
            zs = _mm_norm(xs, attn_norm[i], w_in).reshape(bd, s_len, -1)
            kgain = _tile_gain(b_k_norm[j], kv_cols)
            qgain = _tile_gain(b_q_norm[j], LANE)
            sink = jnp.repeat(b_sink[j].astype(F32), HEAD_DIM).reshape(1, q_cols)
            knp = _headnorm(zp.reshape(bp * t, -1), k_cb, kv_cols, kgain).reshape(bp, t, kv_cols)
            mp = _flash(zp, 0, q_cols, knp, 0, zp, v_cb, qgain, sink, mode="band", tq=WINDOW, kb=WINDOW,
                        group=att_group)
            outs["p_sw_k"].append(knp[:, t - win_rows:].reshape(bp, win_rows, ATT_KV_HEADS, HEAD_DIM))
            outs["p_sw_v"].append(zp[:, t - win_rows:, q_cols + kv_cols:q_cols + 2 * kv_cols]
                                  .reshape(bp, win_rows, ATT_KV_HEADS, HEAD_DIM))
            kns = _headnorm(zs.reshape(bd * s_len, -1), k_cb, kv_cols, kgain).reshape(bd, s_len, kv_cols)
            vs_new = zs[:, :, q_cols + kv_cols:q_cols + 2 * kv_cols]
            k_all = jnp.concatenate([cache_swa_k[j].reshape(bd, win_rows, kv_cols), kns], axis=1)
            v_all = jnp.concatenate([cache_swa_v[j].reshape(bd, win_rows, kv_cols), vs_new], axis=1)
            n_keys = win_rows + s_len
            n_pad = -(-n_keys // LANE) * LANE
            pad = jnp.zeros((bd, n_pad - n_keys, kv_cols), F32)
            ms = _flash(zs, 0, q_cols, jnp.concatenate([k_all, pad], axis=1), 0,
                        jnp.concatenate([v_all, pad], axis=1), 0, qgain, sink, mode="all", tq=s_len, kb=n_pad,
                        group=att_group, n_valid=n_keys)
            outs["s_sw_k"].append(k_all[:, n_keys - win_rows:].reshape(bd, win_rows, ATT_KV_HEADS, HEAD_DIM))
            outs["s_sw_v"].append(v_all[:, n_keys - win_rows:].reshape(bd, win_rows, ATT_KV_HEADS, HEAD_DIM))
            w_out = b_w_out[j]
        else:
            wc = c_w_in[j]
            w_in = jnp.concatenate([wc[:, :o_ki], wc[:, c_cols:],
                                    _pad_cols(wc, [(o_ki, IDX_DIM + IDX_HEADS, LANE)])], axis=1).astype(BF16)
            memq_cb = o_ki // memq_cols
            kw_cb = (o_ki + memq_cols) // LANE
            k_cb, v_cb = q_cols // kv_cols, q_cols // kv_cols + 1
            qi_cb = o_qi // (IDX_HEADS * IDX_DIM)
            zp = _mm_norm(xp, attn_norm[i], w_in).reshape(bp, t, -1)
            zs = _mm_norm(xs, attn_norm[i], w_in).reshape(bd, s_len, -1)
            kgain = _tile_gain(c_k_norm[j], kv_cols)
            qgain = _tile_gain(c_q_norm[j], LANE)
            igain = jnp.concatenate([c_idx_k_norm[j].astype(F32), jnp.zeros((LANE - IDX_DIM,), F32)]).reshape(1, LANE)
            no_sink = jnp.full((1, q_cols), -jnp.inf, F32)
            knp = _headnorm(zp.reshape(bp * t, -1), k_cb, kv_cols, kgain).reshape(bp, t, kv_cols)
            kip = _headnorm(zp.reshape(bp * t, -1), kw_cb, LANE, igain).reshape(bp, t, LANE)
            tq = min(t, DSA_TQ)
            kb = min(t, DSA_KB)
            sc, tau, cut = _dsa_select(zp, qi_cb, kw_cb, kip, tq=tq, kb=kb, k_sel=k_sel_p, n_valid=t, causal=True)
            ftq, fkb = min(t, DSA_FLASH_TQ), min(t, DSA_FLASH_KB)
            assert all(_causal_blocks(r // tq, tq, kb) * kb >= _causal_blocks(r // ftq, ftq, fkb) * fkb
                       for r in range(0, t, CHUNK))
            mp = _flash(zp, 0, q_cols, knp, 0, zp, v_cb, qgain, no_sink, mode="dsa", tq=ftq, kb=fkb,
                        group=att_group, n_valid=t, causal=True, scores=sc, tau=tau, jb=cut)
            outs["p_ds_k"].append(knp.reshape(bp, t, ATT_KV_HEADS, HEAD_DIM))
            outs["p_ds_v"].append(zp[:, :, q_cols + kv_cols:q_cols + 2 * kv_cols].reshape(bp, t, ATT_KV_HEADS, HEAD_DIM))
            outs["p_ds_i"].append(kip[:, :, :IDX_DIM])
            kns = _headnorm(zs.reshape(bd * s_len, -1), k_cb, kv_cols, kgain).reshape(bd, s_len, kv_cols)
            kis = _headnorm(zs.reshape(bd * s_len, -1), kw_cb, LANE, igain).reshape(bd, s_len, LANE)
            vs_new = zs[:, :, q_cols + kv_cols:q_cols + 2 * kv_cols]
            n_keys = past + s_len
            n_pad = -(-n_keys // LANE) * LANE
            zpad = lambda w: jnp.zeros((bd, n_pad - n_keys, w), F32)
            k_all = jnp.concatenate([cache_dsa_k[j].reshape(bd, past, kv_cols), kns, zpad(kv_cols)], axis=1)
            v_all = jnp.concatenate([cache_dsa_v[j].reshape(bd, past, kv_cols), vs_new, zpad(kv_cols)], axis=1)
            ki_cache = jnp.concatenate([cache_dsa_idx_k[j], jnp.zeros((bd, past, LANE - IDX_DIM), F32)], axis=-1)
            ki_all = jnp.concatenate([ki_cache, kis, zpad(LANE)], axis=1)
            sc, tau, cut = _dsa_select(zs, qi_cb, kw_cb, ki_all, tq=s_len, kb=n_pad, k_sel=k_sel_s, n_valid=n_keys,
                                       causal=False)
            ms = _flash(zs, 0, q_cols, k_all, 0, v_all, 0, qgain, no_sink, mode="dsa", tq=s_len, kb=n_pad,
                        group=att_group, n_valid=n_keys, causal=False, scores=sc, tau=tau, jb=cut)
            outs["s_ds_k"].append(kns.reshape(bd, s_len, ATT_KV_HEADS, HEAD_DIM))
            outs["s_ds_v"].append(vs_new.reshape(bd, s_len, ATT_KV_HEADS, HEAD_DIM))
            outs["s_ds_i"].append(kis[:, :, :IDX_DIM])
            w_out = c_w_out[j]

        kv_mem = _mm_norm(mem_prompt.reshape(bp * mem_tokens, d), mem_norm[i], mem_w_kv[i].astype(BF16))
        km_p = _headnorm(kv_mem, 0, memq_cols, _tile_gain(mem_k_norm[i], memq_cols)).reshape(bp, mem_tokens, memq_cols)
        vm_p = kv_mem[:, memq_cols:].reshape(bp, mem_tokens, memq_cols)
        outs["p_mk"].append(km_p.reshape(bp, mem_tokens, MEM_HEADS, HEAD_DIM))
        outs["p_mv"].append(vm_p.reshape(bp, mem_tokens, MEM_HEADS, HEAD_DIM))
        mo_p = _mem_attend(zp, memq_cb, km_p, vm_p, mem_q_norm[i])
        mo_s = _mem_attend(zs, memq_cb, cache_mem_k[i].reshape(bd, mem_tokens, memq_cols),
                           cache_mem_v[i].reshape(bd, mem_tokens, memq_cols), mem_q_norm[i])
        w_mix, w_mem = w_out[:d].astype(BF16), w_out[d:].astype(BF16)
        xp = _mm_res([mp.reshape(bp * t, d), mo_p.reshape(bp * t, memq_cols)], [w_mix, w_mem], xp)
        xs = _mm_res([ms.reshape(bd * s_len, d), mo_s.reshape(bd * s_len, memq_cols)], [w_mix, w_mem], xs)

        w_up, w_down = ffn_w_up[i].astype(BF16), ffn_w_down[i].astype(BF16)
        xp, cp = _conv_ffn(xp, bp, t, ffn_norm[i], w_up, ffn_conv[i], w_down,
                           jnp.zeros((bp, CONV_W - 1, 2 * d_ff), F32))
        xs, cs = _conv_ffn(xs, bd, s_len, ffn_norm[i], w_up, ffn_conv[i], w_down, state_ffn_conv[i])
        outs["p_cv"].append(cp)
        outs["s_cv"].append(cs)

    st = jnp.stack
    order = ("p_rw_wkv", "p_rw_sh", "p_sw_k", "p_sw_v", "p_ds_k", "p_ds_v", "p_ds_i", "p_mk", "p_mv", "p_cv",
             "s_rw_wkv", "s_rw_sh", "s_sw_k", "s_sw_v", "s_ds_k", "s_ds_v", "s_ds_i", "s_cv")
    return (xp.reshape(bp, t, d), xs.reshape(bd, s_len, d)) + tuple(st(outs[k]) for k in order)
```

```python
import functools
import math

import jax
import jax.numpy as jnp
from jax import lax
from jax.experimental import pallas as pl
from jax.experimental.pallas import tpu as pltpu

F32 = jnp.float32
BF16 = jnp.bfloat16

HEAD_DIM = 64
CHUNK = 64
NORM_EPS = 1e-6
RW_GN_EPS = HEAD_DIM * 1e-5
ATT_KV_HEADS = 4
WINDOW = 128
IDX_HEADS = 8
IDX_DIM = 64
TOPK_MAX = 256
MEM_HEADS = 4
CONV_W = 3

LANE = 128
VMEM_LIMIT = 52 * 1024 * 1024
NEG_BIG = -(2.0 ** 100)
INT_MIN = -2147483648
MM_ROWS = 1024
DSA_TQ = 256
DSA_KB = 512
DSA_FLASH_TQ = 256
DSA_FLASH_KB = 512


def _cparams(sem, vmem=VMEM_LIMIT):
    return pltpu.CompilerParams(dimension_semantics=sem, vmem_limit_bytes=vmem)


def _split3(a):
    a1 = a.astype(BF16)
    r1 = a - a1.astype(F32)
    a2 = r1.astype(BF16)
    r2 = r1 - a2.astype(F32)
    return a1, a2, r2.astype(BF16)


def _dot_exact_rhs(a, e):
    a1, a2, a3 = _split3(a)
    d = lambda x: jnp.dot(x, e, preferred_element_type=F32)
    return (d(a3) + d(a2)) + d(a1)


def _dot_exact_lhs(e, a):
    a1, a2, a3 = _split3(a)
    d = lambda x: jnp.dot(e, x, preferred_element_type=F32)
    return (d(a3) + d(a2)) + d(a1)


def _head_block_matrix(width, value):
    r = lax.broadcasted_iota(jnp.int32, (width, width), 0) // HEAD_DIM
    c = lax.broadcasted_iota(jnp.int32, (width, width), 1) // HEAD_DIM
    return jnp.where(r == c, value, 0.0).astype(BF16)


def _dot_rw(a, b, dims):
    return lax.dot_general(a.astype(BF16), b.astype(BF16), (dims, ((), ())), preferred_element_type=F32)


def _pick_tile(n, cap):
    best = None
    for t in range(LANE, min(n, cap) + 1, LANE):
        if n % t == 0:
            best = t
    assert best is not None, n
    return best


def _mm_norm_body(x_ref, g_ref, w_ref, o_ref, xn_ref):
    @pl.when(pl.program_id(1) == 0)
    def _():
        x = x_ref[...]
        ms = jnp.mean(x * x, axis=-1, keepdims=True)
        xn_ref[...] = ((x * lax.rsqrt(ms + NORM_EPS)) * g_ref[...]).astype(BF16)

    o_ref[...] = jnp.dot(xn_ref[...], w_ref[...], preferred_element_type=F32)


def _mm_norm(x, gain, w):
    m, k = x.shape
    n = w.shape[1]
    tm = min(m, MM_ROWS)
    tn = _pick_tile(n, 1536)
    return pl.pallas_call(
        _mm_norm_body,
        out_shape=jax.ShapeDtypeStruct((m, n), F32),
        grid=(m // tm, n // tn),
        in_specs=[pl.BlockSpec((tm, k), lambda i, j: (i, 0)),
                  pl.BlockSpec((1, k), lambda i, j: (0, 0)),
                  pl.BlockSpec((k, tn), lambda i, j: (0, j))],
        out_specs=pl.BlockSpec((tm, tn), lambda i, j: (i, j)),
        scratch_shapes=[pltpu.VMEM((tm, k), BF16)],
        compiler_params=_cparams(("parallel", "arbitrary")),
        name="mm_norm",
    )(x, gain.reshape(1, k), w)


def _mm_res_body(*refs, n_lhs):
    lhs = refs[:n_lhs]
    ws = refs[n_lhs:2 * n_lhs]
    r_ref, o_ref = refs[2 * n_lhs], refs[2 * n_lhs + 1]
    acc = jnp.dot(lhs[0][...], ws[0][...], preferred_element_type=F32)
    for a, w in zip(lhs[1:], ws[1:]):
        acc = acc + jnp.dot(a[...], w[...], preferred_element_type=F32)
    o_ref[...] = r_ref[...] + acc


def _mm_res(lhs_list, w_list, res):
    m, n = res.shape
    ktot = sum(a.shape[1] for a in lhs_list)
    tm = min(m, MM_ROWS)
    tn = _pick_tile(n, 1024 if ktot <= 3072 else 512)
    n_lhs = len(lhs_list)
    in_specs = [pl.BlockSpec((tm, a.shape[1]), lambda i, j: (i, 0)) for a in lhs_list]
    in_specs += [pl.BlockSpec((w.shape[0], tn), lambda i, j: (0, j)) for w in w_list]
    in_specs += [pl.BlockSpec((tm, tn), lambda i, j: (i, j))]
    return pl.pallas_call(
        functools.partial(_mm_res_body, n_lhs=n_lhs),
        out_shape=jax.ShapeDtypeStruct((m, n), F32),
        grid=(m // tm, n // tn),
        in_specs=in_specs,
        out_specs=pl.BlockSpec((tm, tn), lambda i, j: (i, j)),
        compiler_params=_cparams(("parallel", "arbitrary")),
        name="mm_res",
    )(*lhs_list, *w_list, res)


def _headnorm_body(x_ref, g_ref, o_ref, *, width):
    avg = _head_block_matrix(LANE, 1.0 / HEAD_DIM)
    for c in range(width // LANE):
        x = x_ref[:, c * LANE:(c + 1) * LANE]
        ms = _dot_exact_rhs(x * x, avg)
        o_ref[:, c * LANE:(c + 1) * LANE] = (x * lax.rsqrt(ms + NORM_EPS)) * g_ref[:, c * LANE:(c + 1) * LANE]


def _headnorm(x, col_block, width, gain_row):
    m = x.shape[0]
    tm = min(m, 1024)
    return pl.pallas_call(
        functools.partial(_headnorm_body, width=width),
        out_shape=jax.ShapeDtypeStruct((m, width), F32),
        grid=(m // tm,),
        in_specs=[pl.BlockSpec((tm, width), lambda i: (i, col_block)),
                  pl.BlockSpec((1, width), lambda i: (0, 0))],
        out_specs=pl.BlockSpec((tm, width), lambda i: (i, 0)),
        compiler_params=_cparams(("parallel",)),
        name="headnorm",
    )(x, gain_row)


def _up_conv_body(x_ref, g_ref, wa_ref, wb_ref, pa_ref, pb_ref, cwa_ref, cwb_ref, o_ref, la_ref, lb_ref,
                  xn_ref, ca_ref, cb_ref, *, tm, tiles_per_batch, bpt):
    i = pl.program_id(0)
    j = pl.program_id(1)
    rpb = tm // bpt

    @pl.when(j == 0)
    def _():
        x = x_ref[...]
        ms = jnp.mean(x * x, axis=-1, keepdims=True)
        xn_ref[...] = ((x * lax.rsqrt(ms + NORM_EPS)) * g_ref[...]).astype(BF16)

    if bpt == 1:
        @pl.when(i % tiles_per_batch == 0)
        def _():
            ca_ref[j] = pa_ref[0]
            cb_ref[j] = pb_ref[0]

    xn = xn_ref[...]
    row = lax.broadcasted_iota(jnp.int32, o_ref.shape, 0)
    off = row if bpt == 1 else row % rpb
    if bpt > 1:
        pick = (lax.broadcasted_iota(jnp.int32, (tm, bpt), 0) // rpb
                == lax.broadcasted_iota(jnp.int32, (tm, bpt), 1)).astype(BF16)

    def conv(w_ref, p_ref, c_ref, cw_ref, last_ref):
        u = jnp.dot(xn, w_ref[...], preferred_element_type=F32)
        if bpt == 1:
            car = c_ref[j]
            c0, c1 = car[0:1, :], car[1:2, :]
            c_ref[j] = u[tm - 2:tm, :]
            last_ref[0] = u[tm - 2:tm, :]
        else:
            c0 = _dot_exact_lhs(pick, p_ref[:, 0, :])
            c1 = _dot_exact_lhs(pick, p_ref[:, 1, :])
            for bi in range(bpt):
                last_ref[bi] = u[(bi + 1) * rpb - 2:(bi + 1) * rpb, :]
        u1 = jnp.where(off == 0, c1, pltpu.roll(u, 1, axis=0))
        u2 = jnp.where(off == 0, c0, jnp.where(off == 1, c1, pltpu.roll(u, 2, axis=0)))
        return (u2 * cw_ref[0:1, :] + u1 * cw_ref[1:2, :]) + u * cw_ref[2:3, :]

    a = conv(wa_ref, pa_ref, ca_ref, cwa_ref, la_ref)
    b = conv(wb_ref, pb_ref, cb_ref, cwb_ref, lb_ref)
    o_ref[...] = ((a * jax.nn.sigmoid(a)) * b).astype(o_ref.dtype)


def _up_conv(x, gain, w_up, conv_w, prev, t):
    m, k = x.shape
    f = w_up.shape[1] // 2
    b = m // t
    tm = min(m, MM_ROWS)
    bpt = max(1, tm // t)
    assert tm % t == 0 or t % tm == 0
    tn = _pick_tile(f, 512)
    nj = f // tn
    tpb = max(1, t // tm)
    act, la, lb = pl.pallas_call(
        functools.partial(_up_conv_body, tm=tm, tiles_per_batch=tpb, bpt=bpt),
        out_shape=(jax.ShapeDtypeStruct((m, f), BF16),
                   jax.ShapeDtypeStruct((b, CONV_W - 1, f), F32),
                   jax.ShapeDtypeStruct((b, CONV_W - 1, f), F32)),
        grid=(m // tm, nj),
        in_specs=[pl.BlockSpec((tm, k), lambda i, j: (i, 0)),
                  pl.BlockSpec((1, k), lambda i, j: (0, 0)),
                  pl.BlockSpec((k, tn), lambda i, j: (0, j)),
                  pl.BlockSpec((k, tn), lambda i, j: (0, nj + j)),
                  pl.BlockSpec((bpt, CONV_W - 1, tn), lambda i, j: (i // tpb, 0, j)),
                  pl.BlockSpec((bpt, CONV_W - 1, tn), lambda i, j: (i // tpb, 0, nj + j)),
                  pl.BlockSpec((CONV_W, tn), lambda i, j: (0, j)),
                  pl.BlockSpec((CONV_W, tn), lambda i, j: (0, nj + j))],
        out_specs=(pl.BlockSpec((tm, tn), lambda i, j: (i, j)),
                   pl.BlockSpec((bpt, CONV_W - 1, tn), lambda i, j: (i // tpb, 0, j)),
                   pl.BlockSpec((bpt, CONV_W - 1, tn), lambda i, j: (i // tpb, 0, j))),
        scratch_shapes=[pltpu.VMEM((tm, k), BF16),
                        pltpu.VMEM((nj, CONV_W - 1, tn), F32),
                        pltpu.VMEM((nj, CONV_W - 1, tn), F32)],
        compiler_params=_cparams(("arbitrary", "arbitrary")),
        name="up_conv_gate",
    )(x, gain.reshape(1, k), w_up, w_up, prev, prev, conv_w, conv_w)
    return act, jnp.concatenate([la, lb], axis=-1)


def _causal_blocks(i, tq, kb):
    return ((i + 1) * tq + kb - 1) // kb


def _fold_steps(nq, tq, kb):
    if nq % 2:
        return None
    return max(_causal_blocks(r, tq, kb) + _causal_blocks(nq - 1 - r, tq, kb) for r in range(nq // 2))


def _fold_step(r, step, nq, tq, kb):
    n_lo = _causal_blocks(r, tq, kb)
    hi = nq - 1 - r
    in_lo = step < n_lo
    i = jnp.where(in_lo, r, hi)
    last_blk = _causal_blocks(i, tq, kb) - 1
    j = jnp.minimum(jnp.where(in_lo, step, step - n_lo), last_blk)
    active = step < n_lo + _causal_blocks(hi, tq, kb)
    return i, j, active & (jnp.where(in_lo, step, step - n_lo) == 0), active & (j == last_blk), active


def _flash_body(*refs, mode, tq, kb, wq, group, n_valid, nk, causal, fold_nq):
    q_ref, k_ref, v_ref, qg_ref, sink_ref = refs[:5]
    pos = 5
    if mode == "dsa":
        sc_ref, tau_ref, jb_ref = refs[pos:pos + 3]
        pos += 3
    o_ref = refs[pos]
    qn_scr, acc_scr, m_scr, l_scr, bias_scr = refs[pos + 1:]

    if fold_nq:
        i, j, is_first, is_last, fold_active = _fold_step(pl.program_id(1), pl.program_id(2), fold_nq, tq, kb)
    else:
        i, j = pl.program_id(1), pl.program_id(2)
        is_first, is_last = j == 0, j == nk - 1
    npairs = wq // LANE
    pairs_per_unit = group // 2 if group > 1 else 1
    lane = lax.broadcasted_iota(jnp.int32, (1, LANE), 1)
    lo_half = lane < HEAD_DIM

    @pl.when(is_first)
    def _():
        avg = _head_block_matrix(LANE, 1.0 / HEAD_DIM)
        for p in range(npairs):
            x = q_ref[0, :, p * LANE:(p + 1) * LANE]
            ms = _dot_exact_rhs(x * x, avg)
            qn = ((x * lax.rsqrt(ms + NORM_EPS)) * qg_ref[...]) * (HEAD_DIM ** -0.5)
            qn_scr[p] = qn.astype(BF16)
        acc_scr[...] = jnp.zeros(acc_scr.shape, F32)
        l_scr[...] = jnp.zeros(l_scr.shape, F32)
        m_scr[...] = jnp.full(m_scr.shape, NEG_BIG, F32)

    if mode == "band":
        kblk = i - WINDOW // kb + j
        active = kblk >= 0
    elif mode == "dsa" and causal:
        kblk = j
        active = fold_active if fold_nq else j * kb < (i + 1) * tq
    else:
        kblk = j
        active = None

    def step():
        kidx = kblk * kb + lax.broadcasted_iota(jnp.int32, (tq, kb), 1)
        qpos = i * tq + lax.broadcasted_iota(jnp.int32, (tq, kb), 0)
        if mode == "band":
            qchunk = qpos // CHUNK
            sel = (kidx >= (qchunk - WINDOW // CHUNK) * CHUNK) & (kidx < (qchunk + 1) * CHUNK) & (kidx >= 0)
        elif mode == "dsa":
            x = sc_ref[0]
            tau = tau_ref[0][:, 0:1]
            jb = jb_ref[0][:, 0:1]
            adm = kidx < ((qpos // CHUNK + 1) * CHUNK if causal else n_valid)
            sel = adm & ((x > tau) | ((x == tau) & (kidx <= jb)))
        else:
            sel = kidx < n_valid
        bias_scr[...] = jnp.where(sel, 0.0, NEG_BIG).astype(BF16)

        ppu = pairs_per_unit
        for u in range(npairs // ppu):
            ps = slice(u * ppu, (u + 1) * ppu)
            s = lax.dot_general(qn_scr[ps].reshape(ppu * tq, LANE), k_ref[0, 0, u], (((1,), (1,)), ((), ())),
                                preferred_element_type=F32).reshape(ppu, tq, 2 * kb)
            bias = bias_scr[...][None]
            m_old = m_scr[ps]
            s0 = s[:, :, 0:kb].astype(BF16) + bias
            s1 = s[:, :, kb:2 * kb].astype(BF16) + bias
            mn0 = jnp.maximum(m_old[:, :, 0:1], jnp.max(s0, axis=2, keepdims=True).astype(F32))
            mn1 = jnp.maximum(m_old[:, :, HEAD_DIM:HEAD_DIM + 1], jnp.max(s1, axis=2, keepdims=True).astype(F32))
            p0 = jnp.exp(s0 - mn0.astype(BF16))
            p1 = jnp.exp(s1 - mn1.astype(BF16))
            pcat = jnp.concatenate([p0, p1], axis=2).reshape(ppu * tq, 2 * kb)
            pv = jnp.dot(pcat, v_ref[0, 0, u], preferred_element_type=F32)
            pv = pv.reshape(ppu, tq, 2 * LANE)
            mn = jnp.where(lo_half, mn0, mn1)
            alpha = jnp.exp(m_old - mn)
            acc_scr[ps] = acc_scr[ps] * alpha + pv[:, :, 0:LANE]
            l_scr[ps] = l_scr[ps] * alpha + pv[:, :, LANE:2 * LANE]
            m_scr[ps] = mn

    if active is None:
        step()
    else:
        pl.when(active)(step)

    @pl.when(is_last)
    def _():
        for p in range(npairs):
            den = l_scr[p] + jnp.exp(sink_ref[:, p * LANE:(p + 1) * LANE] - m_scr[p])
            o_ref[0, :, p * LANE:(p + 1) * LANE] = (acc_scr[p] / den).astype(o_ref.dtype)


def _pair_operands_body(k_ref, v_ref, kab_ref, vab_ref, *, kb, group, n_units):
    lane = lax.broadcasted_iota(jnp.int32, (1, LANE), 1)
    lo_half = lane < HEAD_DIM
    ones_lo = jnp.broadcast_to(jnp.where(lo_half, 1.0, 0.0), (kb, LANE)).astype(BF16)
    ones_hi = jnp.broadcast_to(jnp.where(lo_half, 0.0, 1.0), (kb, LANE)).astype(BF16)
    for u in range(n_units):
        if group > 1:
            tile_idx, half = u // 2, u % 2
        else:
            tile_idx, half = u, None
        for src, dst in ((k_ref, kab_ref), (v_ref, vab_ref)):
            tile = src[0, :, tile_idx * LANE:(tile_idx + 1) * LANE]
            if half is None:
                a_part = jnp.where(lo_half, tile, 0.0)
                b_part = jnp.where(lo_half, 0.0, tile)
            elif half == 0:
                a_part = jnp.where(lo_half, tile, 0.0)
                b_part = pltpu.roll(a_part, HEAD_DIM, axis=1)
            else:
                b_part = jnp.where(lo_half, 0.0, tile)
                a_part = pltpu.roll(b_part, HEAD_DIM, axis=1)
            dst[0, 0, u, 0:kb, 0:LANE] = a_part.astype(BF16)
            dst[0, 0, u, kb:2 * kb, 0:LANE] = b_part.astype(BF16)
        vab_ref[0, 0, u, 0:kb, LANE:2 * LANE] = ones_lo
        vab_ref[0, 0, u, kb:2 * kb, LANE:2 * LANE] = ones_hi


def _pair_operands(k_arr, k_cb, v_arr, v_cb, kb, group, npairs):
    b, t_k = k_arr.shape[0], k_arr.shape[1]
    wk = ATT_KV_HEADS * HEAD_DIM
    n_units = npairs // (group // 2) if group > 1 else npairs
    nkb = t_k // kb
    return pl.pallas_call(
        functools.partial(_pair_operands_body, kb=kb, group=group, n_units=n_units),
        out_shape=(jax.ShapeDtypeStruct((b, nkb, n_units, 2 * kb, LANE), BF16),
                   jax.ShapeDtypeStruct((b, nkb, n_units, 2 * kb, 2 * LANE), BF16)),
        grid=(b, nkb),
        in_specs=[pl.BlockSpec((1, kb, wk), lambda bi, j: (bi, j, k_cb)),
                  pl.BlockSpec((1, kb, wk), lambda bi, j: (bi, j, v_cb))],
        out_specs=(pl.BlockSpec((1, 1, n_units, 2 * kb, LANE), lambda bi, j: (bi, j, 0, 0, 0)),
                   pl.BlockSpec((1, 1, n_units, 2 * kb, 2 * LANE), lambda bi, j: (bi, j, 0, 0, 0))),
        compiler_params=_cparams(("parallel", "parallel")),
        name="pair_operands",
    )(k_arr, v_arr)


def _flash(q_arr, q_cb, wq, k_arr, k_cb, v_arr, v_cb, q_gain, sink_row, *, mode, tq, kb, group,
           n_valid=None, causal=False, scores=None, tau=None, jb=None):
    b, t_q = q_arr.shape[0], q_arr.shape[1]
    t_k = k_arr.shape[1]
    nq = t_q // tq
    if mode == "band":
        assert tq == kb == WINDOW
        nk = 2
        kmap = lambda i, j: jnp.maximum(i - 1 + j, 0)
    elif mode == "dsa" and causal:
        nk = t_k // kb
        kmap = lambda i, j: jnp.minimum(j, ((i + 1) * tq - 1) // kb)
    else:
        nk = t_k // kb
        kmap = lambda i, j: j
    grid = (b, nq, nk)
    qmap = lambda i, j: i
    fold = _fold_steps(nq, tq, kb) if (mode == "dsa" and causal) else None
    if fold:
        grid = (b, nq // 2, fold)
        qmap = lambda r, s: _fold_step(r, s, nq, tq, kb)[0]
        kmap = lambda r, s: _fold_step(r, s, nq, tq, kb)[1]
    npairs = wq // LANE
    kab, vab = _pair_operands(k_arr, k_cb, v_arr, v_cb, kb, group, npairs)
    n_units = kab.shape[2]
    in_specs = [pl.BlockSpec((1, tq, wq), lambda bi, i, j: (bi, qmap(i, j), q_cb)),
                pl.BlockSpec((1, 1, n_units, 2 * kb, LANE), lambda bi, i, j: (bi, kmap(i, j), 0, 0, 0)),
                pl.BlockSpec((1, 1, n_units, 2 * kb, 2 * LANE), lambda bi, i, j: (bi, kmap(i, j), 0, 0, 0)),
                pl.BlockSpec((1, LANE), lambda bi, i, j: (0, 0)),
                pl.BlockSpec((1, wq), lambda bi, i, j: (0, 0))]
    args = [q_arr, kab, vab, q_gain, sink_row]
    if mode == "dsa":
        in_specs += [pl.BlockSpec((1, tq, kb), lambda bi, i, j: (bi, qmap(i, j), kmap(i, j))),
                     pl.BlockSpec((1, tq, LANE), lambda bi, i, j: (bi, qmap(i, j), 0)),
                     pl.BlockSpec((1, tq, LANE), lambda bi, i, j: (bi, qmap(i, j), 0))]
        args += [scores, tau, jb]
    body = functools.partial(_flash_body, mode=mode, tq=tq, kb=kb, wq=wq, group=group,
                             n_valid=n_valid, nk=nk, causal=causal, fold_nq=nq if fold else 0)
    return pl.pallas_call(
        body,
        out_shape=jax.ShapeDtypeStruct((b, t_q, wq), BF16),
        grid=grid,
        in_specs=in_specs,
        out_specs=pl.BlockSpec((1, tq, wq), lambda bi, i, j: (bi, qmap(i, j), 0)),
        scratch_shapes=[pltpu.VMEM((npairs, tq, LANE), BF16),
                        pltpu.VMEM((npairs, tq, LANE), F32),
                        pltpu.VMEM((npairs, tq, LANE), F32),
                        pltpu.VMEM((npairs, tq, LANE), F32),
                        pltpu.VMEM((tq, kb), BF16)],
        compiler_params=_cparams(("parallel", "parallel", "arbitrary")),
        name="flash_" + mode,
    )(*args)


def _dsa_select_body(qi_ref, kw_ref, ki_ref, sc_ref, tau_ref, jb_ref, hi_scr, mid_scr, low_scr, work_scr, *, tq, kb, nk, k_sel,
                     n_valid, causal, idx_bits, fold_nq):
    if fold_nq:
        i, j, _, is_last, fold_active = _fold_step(pl.program_id(1), pl.program_id(2), fold_nq, tq, kb)
    else:
        i, j = pl.program_id(1), pl.program_id(2)
        is_last = j == nk - 1
    qpos = i * tq + lax.broadcasted_iota(jnp.int32, (tq, 1), 0)
    if causal:
        n_adm = (qpos // CHUNK + 1) * CHUNK
        active = fold_active if fold_nq else j * kb < (i + 1) * tq
        n_blocks = _causal_blocks(i, tq, kb)
    else:
        n_adm = jnp.full((tq, 1), n_valid, jnp.int32)
        active = None
        n_blocks = nk

    def compute():
        ka = ki_ref[0]
        kab = jnp.concatenate([ka, pltpu.roll(ka, HEAD_DIM, axis=1)], axis=0).astype(BF16)
        kw = kw_ref[0]
        acc = jnp.zeros((tq, kb), F32)
        q4 = jnp.concatenate([qi_ref[0, :, p * LANE:(p + 1) * LANE].astype(BF16) for p in range(IDX_HEADS // 2)],
                             axis=0)
        s = lax.dot_general(q4, kab, (((1,), (1,)), ((), ())), preferred_element_type=F32)
        for p in range(IDX_HEADS // 2):
            for h in range(2):
                col = HEAD_DIM + 2 * p + h
                w = (kw[:, col:col + 1] * IDX_HEADS ** -0.5) * IDX_DIM ** -0.5
                acc = acc + jnp.maximum(s[p * tq:(p + 1) * tq, h * kb:(h + 1) * kb], 0.0) * w
        acc = jnp.where(acc == 0.0, 0.0, acc)
        kidx = j * kb + lax.broadcasted_iota(jnp.int32, (tq, kb), 1)
        x = jnp.where(kidx < n_adm, acc, -jnp.inf)
        sc_ref[0] = x
        bits = lax.bitcast_convert_type(x, jnp.int32)
        neg = bits < 0
        mid = jnp.right_shift(bits, 8) & 0xFF
        low = bits & 0xFF
        hi_scr[j] = lax.bitcast_convert_type(bits & jnp.int32(-65536), F32).astype(BF16)
        mid_scr[j] = jnp.where(neg, 255 - mid, mid).astype(F32).astype(BF16)
        low_scr[j] = jnp.where(neg, 255 - low, low).astype(F32).astype(BF16)

    if active is None:
        compute()
    else:
        pl.when(active)(compute)

        if not fold_nq:
            @pl.when(jnp.logical_not(active))
            def _():
                sc_ref[0] = jnp.full((tq, kb), -jnp.inf, F32)

    @pl.when(is_last)
    def _():
        kf = float(k_sel)
        one, zero, below = jnp.bfloat16(1), jnp.bfloat16(0), jnp.bfloat16(-1)
        lane_tiles = range(kb // LANE)

        def widen(col):
            return jnp.broadcast_to(col, (tq, LANE)).astype(BF16)

        def count(src_scr, pred):
            def blk(jj, acc):
                for a in lane_tiles:
                    acc = acc + jnp.where(pred(src_scr[jj, :, a * LANE:(a + 1) * LANE]), one, zero)
                return acc
            acc = lax.fori_loop(0, n_blocks, blk, jnp.zeros((tq, LANE), BF16))
            return jnp.sum(acc.astype(F32), axis=1, keepdims=True)

        def pattern16(tu):
            cs = tu ^ 0x8000
            return jnp.where(cs >= 0x8000, cs ^ 0x7FFF, cs)

        def plane_float(tu):
            return lax.bitcast_convert_type(jnp.left_shift(pattern16(tu), 16), F32)

        def hi_body(it, tu):
            cand = tu | jnp.left_shift(jnp.int32(1), 15 - it)
            t = widen(plane_float(cand))
            return jnp.where(count(hi_scr, lambda x: x >= t) >= kf, cand, tu)

        tu = lax.fori_loop(0, 16, hi_body, jnp.zeros((tq, 1), jnp.int32))
        t_hi = widen(plane_float(tu))
        need_mid = kf - count(hi_scr, lambda x: x > t_hi)

        def byte_level(byte_scr, t_prev, need):
            def fill(jj, carry):
                for a in lane_tiles:
                    cols = slice(a * LANE, (a + 1) * LANE)
                    prev = hi_scr[jj, :, cols] if byte_scr is mid_scr else work_scr[jj, :, cols]
                    work_scr[jj, :, cols] = jnp.where(prev == t_prev, byte_scr[jj, :, cols], below)
                return carry
            lax.fori_loop(0, n_blocks, fill, 0)

            def body(it, tb):
                cand = tb + jnp.left_shift(jnp.int32(1), 7 - it).astype(F32)
                t = widen(cand)
                return jnp.where(count(work_scr, lambda x: x >= t) >= need, cand, tb)
            tb = lax.fori_loop(0, 8, body, jnp.zeros((tq, 1), F32))
            t = widen(tb)
            return tb, t, need - count(work_scr, lambda x: x > t)

        t_mid_val, t_mid, need_low = byte_level(mid_scr, t_hi, need_mid)
        t_low_val, t_low, need = byte_level(low_scr, t_mid, need_low)
        n_equal = count(work_scr, lambda x: x == t_low)

        pat = pattern16(tu)
        neg = pat >= 0x8000
        mid_b, low_b = t_mid_val.astype(jnp.int32), t_low_val.astype(jnp.int32)
        bits = (jnp.left_shift(pat, 16) | jnp.left_shift(jnp.where(neg, 255 - mid_b, mid_b), 8)
                | jnp.where(neg, 255 - low_b, low_b))
        full = n_adm >= k_sel
        tau = jnp.where(full, lax.bitcast_convert_type(bits, F32), -jnp.inf)
        tau_ref[0] = jnp.broadcast_to(tau, (tq, LANE))
        jb_ref[0] = jnp.full((tq, LANE), 2 ** 30, jnp.int32)
        surplus = jnp.max(jnp.where(full, n_equal - need, 0.0))

        @pl.when(surplus > 0.5)
        def _():
            lane_idx = lax.broadcasted_iota(jnp.int32, (tq, LANE), 1)
            t_eq = jnp.broadcast_to(t_low_val, (tq, LANE))

            def tie_body(it, cut):
                cand = cut | jnp.left_shift(jnp.int32(1), idx_bits - 1 - it)
                cnd = jnp.broadcast_to(cand, (tq, LANE))

                def blk(jj, acc):
                    for a in lane_tiles:
                        x = work_scr[jj, :, a * LANE:(a + 1) * LANE].astype(F32)
                        hit = jnp.where(x == t_eq, jnp.where(jj * kb + a * LANE + lane_idx < cnd, 1.0, 0.0), 0.0)
                        acc = acc + hit
                    return acc
                acc = lax.fori_loop(0, n_blocks, blk, jnp.zeros((tq, LANE), F32))
                return jnp.where(jnp.sum(acc, axis=1, keepdims=True) < need, cand, cut)

            cut = lax.fori_loop(0, idx_bits, tie_body, jnp.zeros((tq, 1), jnp.int32))
            jb_ref[0] = jnp.broadcast_to(cut, (tq, LANE))


def _dsa_select(z, qi_cb, kw_cb, ki_n, *, tq, kb, k_sel, n_valid, causal):
    b, t_q = z.shape[0], z.shape[1]
    t_k = ki_n.shape[1]
    nq, nk = t_q // tq, t_k // kb
    assert t_k // LANE <= 256, "per-lane counts are accumulated in bf16, exact up to 256"
    kmap = (lambda i, j: jnp.minimum(j, ((i + 1) * tq - 1) // kb)) if causal else (lambda i, j: j)
    omap = lambda i, j: j
    qmap = lambda i, j: i
    grid = (b, nq, nk)
    fold = _fold_steps(nq, tq, kb) if causal else None
    if fold:
        grid = (b, nq // 2, fold)
        qmap = lambda r, s: _fold_step(r, s, nq, tq, kb)[0]
        kmap = omap = lambda r, s: _fold_step(r, s, nq, tq, kb)[1]
    body = functools.partial(_dsa_select_body, tq=tq, kb=kb, nk=nk, k_sel=k_sel, n_valid=n_valid,
                             causal=causal, idx_bits=max(1, (t_k - 1).bit_length()), fold_nq=nq if fold else 0)
    return pl.pallas_call(
        body,
        out_shape=(jax.ShapeDtypeStruct((b, t_q, t_k), F32),
                   jax.ShapeDtypeStruct((b, t_q, LANE), F32),
                   jax.ShapeDtypeStruct((b, t_q, LANE), jnp.int32)),
        grid=grid,
        in_specs=[pl.BlockSpec((1, tq, IDX_HEADS * IDX_DIM), lambda bi, i, j: (bi, qmap(i, j), qi_cb)),
                  pl.BlockSpec((1, tq, LANE), lambda bi, i, j: (bi, qmap(i, j), kw_cb)),
                  pl.BlockSpec((1, kb, LANE), lambda bi, i, j: (bi, kmap(i, j), 0))],
        out_specs=(pl.BlockSpec((1, tq, kb), lambda bi, i, j: (bi, qmap(i, j), omap(i, j))),
                   pl.BlockSpec((1, tq, LANE), lambda bi, i, j: (bi, qmap(i, j), 0)),
                   pl.BlockSpec((1, tq, LANE), lambda bi, i, j: (bi, qmap(i, j), 0))),
        scratch_shapes=[pltpu.VMEM((nk, tq, kb), BF16)] * 4,
        compiler_params=_cparams(("parallel", "parallel", "arbitrary")),
        name="dsa_select",
    )(z, z, ki_n)


def _rwkv_body(zr_ref, zk_ref, zv_ref, zw_ref, za_ref, zg_ref,
               sr_ref, sk_ref, sv_ref, sw_ref, sa_ref, sg_ref,
               mr_ref, mk_ref, mv_ref, mw_ref, ma_ref, mg_ref,
               vec_ref, w2_ref, a2_ref, g2_ref, s0_ref,
               mix_ref, sfin_ref,
               s_scr, prev_scr, prevl_scr, prevg_scr, y_scr, *, tc, lc, nt, pp):
    t = pl.program_id(2)

    @pl.when(t == 0)
    def _():
        s_scr[...] = s0_ref[0]
        prev_scr[0:1, :] = sr_ref[0]
        prev_scr[1:2, :] = sk_ref[0]
        prev_scr[2:3, :] = sv_ref[0]
        prevl_scr[0:1, :] = sw_ref[0]
        prevl_scr[1:2, :] = sa_ref[0]
        prevg_scr[0:1, :] = sg_ref[0]

    row = lax.broadcasted_iota(jnp.int32, (tc, 1), 0)

    def shifted(z_ref, prow, mu_ref):
        z = z_ref[0]
        zp = jnp.where(row == 0, prow, pltpu.roll(z, 1, axis=0))
        return z + (zp - z) * mu_ref[...], z[tc - 1:tc, :]

    r, last_r = shifted(zr_ref, prev_scr[0:1, :], mr_ref)
    k, last_k = shifted(zk_ref, prev_scr[1:2, :], mk_ref)
    v, last_v = shifted(zv_ref, prev_scr[2:3, :], mv_ref)
    zw, last_w = shifted(zw_ref, prevl_scr[0:1, :], mw_ref)
    za, last_a = shifted(za_ref, prevl_scr[1:2, :], ma_ref)
    zg, last_g = shifted(zg_ref, prevg_scr[0:1, :], mg_ref)
    prev_scr[0:1, :] = last_r
    prev_scr[1:2, :] = last_k
    prev_scr[2:3, :] = last_v
    prevl_scr[0:1, :] = last_w
    prevl_scr[1:2, :] = last_a
    prevg_scr[0:1, :] = last_g

    w0, a0 = vec_ref[0:1, :], vec_ref[1:2, :]
    k_k, k_a, r_k = vec_ref[2:3, :], vec_ref[3:4, :], vec_ref[4:5, :]
    ln_w, ln_b = vec_ref[5:6, :], vec_ref[6:7, :]

    ones_blk = _head_block_matrix(LANE, 1.0)
    avg_blk = _head_block_matrix(LANE, 1.0 / HEAD_DIM)

    def per_head(x, blk):
        return jnp.concatenate([_dot_exact_rhs(x[:, i * LANE:(i + 1) * LANE], blk) for i in range(pp)], axis=1)

    xw = w0 + jnp.dot(jnp.tanh(zw).astype(BF16), w2_ref[...], preferred_element_type=F32)
    lw = -math.exp(-0.5) * jax.nn.sigmoid(xw)
    a = jax.nn.sigmoid(a0 + jnp.dot(za.astype(BF16), a2_ref[...], preferred_element_type=F32))
    g = jnp.dot(jax.nn.sigmoid(zg).astype(BF16), g2_ref[...], preferred_element_type=F32)
    kk = k * k_k
    kk = kk * jnp.minimum(lax.rsqrt(per_head(kk * kk, ones_blk)), 1e12)
    k2 = k * (1.0 + (a - 1.0) * k_a)

    a_step = -kk
    b_step = kk * a

    lane = lax.broadcasted_iota(jnp.int32, (1, LANE), 1)
    m0 = lane < HEAD_DIM
    rr = lax.broadcasted_iota(jnp.int32, (lc, 2 * lc), 0)
    cc = lax.broadcasted_iota(jnp.int32, (lc, 2 * lc), 1)
    incl = jnp.where(cc < lc, cc, cc - lc) <= rr
    strict = (lax.broadcasted_iota(jnp.int32, (2 * lc, 2 * lc), 1)
              < lax.broadcasted_iota(jnp.int32, (2 * lc, 2 * lc), 0))
    tri = (lax.broadcasted_iota(jnp.int32, (lc, lc), 1)
           <= lax.broadcasted_iota(jnp.int32, (lc, lc), 0)).astype(BF16)
    nsteps = lc.bit_length() - 1
    nt_dims = ((1,), (1,))
    nn_dims = ((1,), (0,))
    tn_dims = ((0,), (0,))

    chunks = range(tc // lc)
    st = []
    for c in chunks:
        rows = slice(c * lc, (c + 1) * lc)
        lwc = lw[rows]
        cs = _dot_exact_lhs(tri, lwc)
        p_in = jnp.exp(-cs)
        at = a_step[rows] * jnp.exp(cs - lwc)
        bt = b_step[rows] * p_in
        kt = k2[rows] * p_in
        rt = r[rows] * jnp.exp(cs)
        vc = v[rows]
        p_last = jnp.exp(cs[lc - 1:lc, :])
        for pi in range(pp):
            cols = slice(pi * LANE, (pi + 1) * LANE)
            stack = lambda x: jnp.concatenate([jnp.where(m0, x[:, cols], 0.0),
                                               jnp.where(m0, 0.0, x[:, cols])], axis=0).astype(BF16)
            st.append(dict(c=c, pi=pi,
                           ar=jnp.concatenate([stack(at), rt[:, cols].astype(BF16)], axis=0),
                           bk=jnp.concatenate([stack(bt), stack(kt)], axis=0),
                           v_s=stack(vc), p_last=p_last[:, cols]))
    for d in st:
        gram = _dot_rw(d["ar"], d["bk"], nt_dims)
        d["t_p"] = jnp.where(strict, gram[0:2 * lc, 0:2 * lc], 0.0)
        d["w_ak"] = jnp.where(strict, gram[0:2 * lc, 2 * lc:4 * lc], 0.0).astype(BF16)
        d["w_rb"] = jnp.where(incl, gram[2 * lc:3 * lc, 0:2 * lc], 0.0).astype(BF16)
        d["w_rk"] = jnp.where(incl, gram[2 * lc:3 * lc, 2 * lc:4 * lc], 0.0).astype(BF16)
    gw = 4 * lc
    lane_head = lax.broadcasted_iota(jnp.int32, (1, gw), 1) // lc
    eye_sbs = (lax.broadcasted_iota(jnp.int32, (lc, gw), 0)
               == lax.broadcasted_iota(jnp.int32, (lc, gw), 1) % lc).astype(F32)
    lane_pair = lax.broadcasted_iota(jnp.int32, (1, 2 * lc), 1)

    def blockdiag(y):
        yb = y.astype(BF16)
        return jnp.concatenate([jnp.where(lane_head == h, yb, jnp.zeros_like(yb)) for h in range(4)], axis=0)

    groups = []
    for gi in range(len(st) // 2):
        members = (st[2 * gi], st[2 * gi + 1])
        t_sbs = jnp.concatenate([m["t_p"][0:lc] + m["t_p"][lc:2 * lc] for m in members], axis=1)
        groups.append(dict(members=members, t=t_sbs, minv=eye_sbs + t_sbs))
    for grp in groups:
        grp["t"] = _dot_rw(grp["t"], blockdiag(grp["t"]), nn_dims)
    for step in range(nsteps - 1):
        for grp in groups:
            if step == nsteps - 2:
                grp["minv"] = grp["minv"] + _dot_rw(grp["minv"], blockdiag(grp["t"]), nn_dims)
            else:
                both = _dot_rw(jnp.concatenate([grp["minv"], grp["t"]], axis=0), blockdiag(grp["t"]), nn_dims)
                grp["minv"] = grp["minv"] + both[0:lc]
                grp["t"] = both[lc:2 * lc]
    for grp in groups:
        for idx, m in enumerate(grp["members"]):
            tile = grp["minv"][:, idx * 2 * lc:(idx + 1) * 2 * lc]
            m["minv"] = jnp.concatenate([jnp.where(lane_pair < lc, tile, 0.0),
                                         jnp.where(lane_pair < lc, 0.0, tile)], axis=0).astype(BF16)
    for d in st:
        both = _dot_rw(jnp.concatenate([d["w_ak"], d["w_rk"]], axis=0), d["v_s"], nn_dims)
        d["wv"] = both[0:2 * lc]
        d["y_c"] = both[2 * lc:3 * lc]
    for d in st:
        both = _dot_rw(d["minv"], jnp.concatenate([d["ar"][0:2 * lc], d["wv"].astype(BF16)], axis=1), nn_dims)
        d["ma"] = both[:, 0:LANE]
        d["mwv"] = both[:, LANE:2 * LANE]
    for d in st:
        d["g"] = _dot_rw(d["ma"], d["bk"][0:2 * lc], tn_dims).astype(BF16)
        d["d"] = _dot_rw(jnp.concatenate([d["mwv"].astype(BF16), d["v_s"]], axis=0), d["bk"], tn_dims)
    s_cur = [s_scr[pi] for pi in range(pp)]
    for d in st:
        s_in = s_cur[d["pi"]]
        d["s0"] = s_in.astype(BF16)
        s_cur[d["pi"]] = ((s_in + _dot_rw(d["s0"], d["g"], nn_dims)) + d["d"]) * d["p_last"]
    for pi in range(pp):
        s_scr[pi] = s_cur[pi]
    for d in st:
        d["xr"] = _dot_rw(d["ar"], d["s0"], nt_dims)
    for d in st:
        d["u"] = (_dot_rw(d["minv"], d["xr"][0:2 * lc], nn_dims) + d["mwv"]).astype(BF16)
    for d in st:
        c, pi = d["c"], d["pi"]
        y_scr[c * lc:(c + 1) * lc, pi * LANE:(pi + 1) * LANE] = (
            (d["xr"][2 * lc:3 * lc] + _dot_rw(d["w_rb"], d["u"], nn_dims)) + d["y_c"])

    y = y_scr[...]
    mean = per_head(y, avg_blk)
    dev = y - mean
    var = per_head(dev * dev, avg_blk)
    yn = (dev * lax.rsqrt(var + RW_GN_EPS)) * ln_w + ln_b
    bonus = per_head((r * k2) * r_k, ones_blk) * v
    mix_ref[0] = ((yn + bonus) * g).astype(mix_ref.dtype)

    @pl.when(t == nt - 1)
    def _():
        sfin_ref[0] = s_scr[...]


def _rwkv(z, shift_prev, s0_pairs, mu, vecs, w2, a2, g2, d_model):
    b, t = z.shape[0], z.shape[1]
    npairs = d_model // LANE
    tc = min(t, 512)
    lc = min(CHUNK, t)
    nt = t // tc
    pp = 2 if tc // lc >= 4 else 8
    wp = pp * LANE
    ngroups = npairs // pp
    cb_w, cb_a, cb_g = 3 * npairs, 3 * npairs + 1, (3 * npairs + 2) // 2

    def zspec(width, cbf):
        return pl.BlockSpec((1, tc, width), lambda bi, p, ti: (bi, ti, cbf(p)))

    def sspec(width, cbf):
        return pl.BlockSpec((1, 1, width), lambda bi, p, ti: (bi, 0, cbf(p)))

    def mspec(width, cbf):
        return pl.BlockSpec((1, width), lambda bi, p, ti: (0, cbf(p)))

    cbfs = [(wp, lambda p: p), (wp, lambda p: ngroups + p), (wp, lambda p: 2 * ngroups + p),
            (LANE, lambda p: cb_w), (LANE, lambda p: cb_a), (2 * LANE, lambda p: cb_g)]
    in_specs = ([zspec(w, f) for w, f in cbfs] + [sspec(w, f) for w, f in cbfs] + [mspec(w, f) for w, f in cbfs]
                + [pl.BlockSpec((8, wp), lambda bi, p, ti: (0, p)),
                   pl.BlockSpec((LANE, wp), lambda bi, p, ti: (0, p)),
                   pl.BlockSpec((LANE, wp), lambda bi, p, ti: (0, p)),
                   pl.BlockSpec((2 * LANE, wp), lambda bi, p, ti: (0, p)),
                   pl.BlockSpec((1, pp, LANE, LANE), lambda bi, p, ti: (bi, p, 0, 0))])
    return pl.pallas_call(
        functools.partial(_rwkv_body, tc=tc, lc=lc, nt=nt, pp=pp),
        out_shape=(jax.ShapeDtypeStruct((b, t, d_model), BF16),
                   jax.ShapeDtypeStruct((b, npairs, LANE, LANE), F32)),
        grid=(b, ngroups, nt),
        in_specs=in_specs,
        out_specs=(pl.BlockSpec((1, tc, wp), lambda bi, p, ti: (bi, ti, p)),
                   pl.BlockSpec((1, pp, LANE, LANE), lambda bi, p, ti: (bi, p, 0, 0))),
        scratch_shapes=[pltpu.VMEM((pp, LANE, LANE), F32),
                        pltpu.VMEM((8, wp), F32),
                        pltpu.VMEM((8, LANE), F32),
                        pltpu.VMEM((8, 2 * LANE), F32),
                        pltpu.VMEM((tc, wp), F32)],
        compiler_params=_cparams(("parallel", "parallel", "arbitrary")),
        name="rwkv7",
    )(*([z] * 6), *([shift_prev] * 6), *([mu] * 6), vecs, w2, a2, g2, s0_pairs)


def _pad_cols(x, segments):
    parts = []
    for start, width, padded in segments:
        seg = x[..., start:start + width]
        if padded > width:
            seg = jnp.concatenate([seg, jnp.zeros(seg.shape[:-1] + (padded - width,), seg.dtype)], axis=-1)
        parts.append(seg)
    return jnp.concatenate(parts, axis=-1)


def _pad_rows(x, padded):
    return jnp.concatenate([x, jnp.zeros((padded - x.shape[0],) + x.shape[1:], x.dtype)], axis=0)


def _pairs_from_heads(s):
    b, h = s.shape[0], s.shape[1]
    s = s.reshape(b, h // 2, 2, HEAD_DIM, HEAD_DIM)
    z = jnp.zeros_like(s[:, :, 0])
    top = jnp.concatenate([s[:, :, 0], z], axis=-1)
    bot = jnp.concatenate([z, s[:, :, 1]], axis=-1)
    return jnp.concatenate([top, bot], axis=-2)


def _heads_from_pairs(sp):
    b, npairs = sp.shape[0], sp.shape[1]
    s = jnp.stack([sp[:, :, :HEAD_DIM, :HEAD_DIM], sp[:, :, HEAD_DIM:, HEAD_DIM:]], axis=2)
    return s.reshape(b, 2 * npairs, HEAD_DIM, HEAD_DIM)


def _tile_gain(g, width):
    return jnp.tile(g.astype(F32), width // HEAD_DIM).reshape(1, width)


def _mem_attend(z, memq_cb, km, vm, q_gain):
    t_q = z.shape[1]
    wq = MEM_HEADS * HEAD_DIM
    tq = min(t_q, 512)
    no_sink = jnp.full((1, wq), -jnp.inf, F32)
    return _flash(z, memq_cb, wq, km, 0, vm, 0, _tile_gain(q_gain, LANE), no_sink,
                  mode="all", tq=tq, kb=km.shape[1], group=1, n_valid=km.shape[1])


def _conv_ffn(x2d, b, t, gain, w_up, conv_w, w_down, prev):
    act, u_last = _up_conv(x2d, gain, w_up, conv_w, prev, t)
    return _mm_res([act], [w_down], x2d), u_last


def kernel(x_prompt, x_sample, state_rwkv_wkv, state_rwkv_shift, cache_swa_k, cache_swa_v, cache_dsa_k, cache_dsa_v, cache_dsa_idx_k, cache_mem_k, cache_mem_v, state_ffn_conv, mem_prompt, attn_norm, ffn_norm, mem_norm, mem_w_kv, mem_q_norm, mem_k_norm, a_w_in, a_mu, a_w0, a_w2, a_a0, a_a2, a_g2, a_k_k, a_k_a, a_r_k, a_ln_w, a_ln_b, a_w_out, b_w_in, b_q_norm, b_k_norm, b_sink, b_w_out, c_w_in, c_q_norm, c_k_norm, c_idx_k_norm, c_w_out, ffn_w_up, ffn_conv, ffn_w_down):
    bp, t, d = x_prompt.shape
    bd, s_len = x_sample.shape[:2]
    depth = attn_norm.shape[0]
    win_rows = cache_swa_k.shape[2]
    past = cache_dsa_k.shape[2] if cache_dsa_k.shape[0] else 0
    d_ff = ffn_w_down.shape[1]
    mem_tokens = mem_prompt.shape[1]
    q_cols = d
    kv_cols = ATT_KV_HEADS * HEAD_DIM
    memq_cols = MEM_HEADS * HEAD_DIM
    att_group = (d // HEAD_DIM) // ATT_KV_HEADS
    dec_lora = a_w2.shape[1]
    a_lora = a_a2.shape[1]
    g_lora = a_g2.shape[1]
    rw_cols = 3 * d + dec_lora + a_lora + g_lora
    k_sel_p = min(TOPK_MAX, t // 4)
    k_sel_s = min(TOPK_MAX, (past + s_len) // 4)
    assert g_lora == 2 * LANE and dec_lora <= LANE and a_lora <= LANE

    xp = x_prompt.reshape(bp * t, d)
    xs = x_sample.reshape(bd * s_len, d)

    rw_segments = [(0, 3 * d, 3 * d), (3 * d, dec_lora, LANE), (3 * d + dec_lora, a_lora, LANE),
                   (3 * d + dec_lora + a_lora, g_lora, g_lora)]
    rw_padded = 3 * d + 2 * LANE + g_lora
    o_qi = q_cols + 2 * kv_cols
    o_ki = o_qi + IDX_HEADS * IDX_DIM
    c_cols = o_ki + IDX_DIM + IDX_HEADS

    outs = {k: [] for k in ("p_rw_wkv", "p_rw_sh", "p_sw_k", "p_sw_v", "p_ds_k", "p_ds_v", "p_ds_i", "p_mk",
                            "p_mv", "p_cv", "s_rw_wkv", "s_rw_sh", "s_sw_k", "s_sw_v", "s_ds_k", "s_ds_v",
                            "s_ds_i", "s_cv")}

    def unpad_rw(row):
        return jnp.concatenate([row[..., :3 * d], row[..., 3 * d:3 * d + dec_lora],
                                row[..., 3 * d + LANE:3 * d + LANE + a_lora],
                                row[..., 3 * d + 2 * LANE:3 * d + 2 * LANE + g_lora]], axis=-1)

    for i in range(depth):
        kind, j = i % 3, i // 3
        if kind == 0:
            w_in = jnp.concatenate([_pad_cols(a_w_in[j], rw_segments), a_w_in[j][:, rw_cols:]], axis=1).astype(BF16)
            memq_cb = rw_padded // memq_cols
            zp = _mm_norm(xp, attn_norm[i], w_in).reshape(bp, t, -1)
            zs = _mm_norm(xs, attn_norm[i], w_in).reshape(bd, s_len, -1)
            mu = _pad_cols(a_mu[j].reshape(1, -1), rw_segments)
            mu = jnp.concatenate([mu, jnp.zeros((1, memq_cols), F32)], axis=1)
            vecs = jnp.stack([a_w0[j], a_a0[j], a_k_k[j], a_k_a[j], a_r_k[j].reshape(-1), a_ln_w[j], a_ln_b[j],
                              jnp.zeros((d,), F32)], axis=0)
            w2 = _pad_rows(a_w2[j], LANE).astype(BF16)
            a2 = _pad_rows(a_a2[j], LANE).astype(BF16)
            g2 = a_g2[j].astype(BF16)
            sh_p = jnp.zeros((bp, 1, zp.shape[-1]), F32)
            st_p = jnp.zeros((bp, d // LANE, LANE, LANE), F32)
            sh_s = _pad_cols(state_rwkv_shift[j], rw_segments)
            sh_s = jnp.concatenate([sh_s, jnp.zeros((bd, memq_cols), F32)], axis=1).reshape(bd, 1, -1)
            st_s = _pairs_from_heads(state_rwkv_wkv[j])
            mp, stp = _rwkv(zp, sh_p, st_p, mu, vecs, w2, a2, g2, d)
            ms, sts = _rwkv(zs, sh_s, st_s, mu, vecs, w2, a2, g2, d)
            outs["p_rw_sh"].append(unpad_rw(zp[:, -1]))
            outs["p_rw_wkv"].append(_heads_from_pairs(stp))
            outs["s_rw_sh"].append(unpad_rw(zs[:, -1]))
            outs["s_rw_wkv"].append(_heads_from_pairs(sts))
            w_out = a_w_out[j]
        elif kind == 1:
            w_in = b_w_in[j].astype(BF16)
            memq_cb = (q_cols + 2 * kv_cols) // memq_cols
            k_cb, v_cb = q_cols // kv_cols, q_cols // kv_cols + 1
            zp = _mm_norm(xp, attn_norm[i], w_in).reshape(bp, t, -1)
            zs = _mm_norm(xs, attn_norm[i], w_in).reshape(bd, s_len, -1)
            kgain = _tile_gain(b_k_norm[j], kv_cols)
            qgain = _tile_gain(b_q_norm[j], LANE)
            sink = jnp.repeat(b_sink[j].astype(F32), HEAD_DIM).reshape(1, q_cols)
            knp = _headnorm(zp.reshape(bp * t, -1), k_cb, kv_cols, kgain).reshape(bp, t, kv_cols)
            mp = _flash(zp, 0, q_cols, knp, 0, zp, v_cb, qgain, sink, mode="band", tq=WINDOW, kb=WINDOW,
                        group=att_group)
            outs["p_sw_k"].append(knp[:, t - win_rows:].reshape(bp, win_rows, ATT_KV_HEADS, HEAD_DIM))
            outs["p_sw_v"].append(zp[:, t - win_rows:, q_cols + kv_cols:q_cols + 2 * kv_cols]
                                  .reshape(bp, win_rows, ATT_KV_HEADS, HEAD_DIM))
            kns = _headnorm(zs.reshape(bd * s_len, -1), k_cb, kv_cols, kgain).reshape(bd, s_len, kv_cols)
            vs_new = zs[:, :, q_cols + kv_cols:q_cols + 2 * kv_cols]
            k_all = jnp.concatenate([cache_swa_k[j].reshape(bd, win_rows, kv_cols), kns], axis=1)
            v_all = jnp.concatenate([cache_swa_v[j].reshape(bd, win_rows, kv_cols), vs_new], axis=1)
            n_keys = win_rows + s_len
            n_pad = -(-n_keys // LANE) * LANE
            pad = jnp.zeros((bd, n_pad - n_keys, kv_cols), F32)
            ms = _flash(zs, 0, q_cols, jnp.concatenate([k_all, pad], axis=1), 0,
                        jnp.concatenate([v_all, pad], axis=1), 0, qgain, sink, mode="all", tq=s_len, kb=n_pad,
                        group=att_group, n_valid=n_keys)
            outs["s_sw_k"].append(k_all[:, n_keys - win_rows:].reshape(bd, win_rows, ATT_KV_HEADS, HEAD_DIM))
            outs["s_sw_v"].append(v_all[:, n_keys - win_rows:].reshape(bd, win_rows, ATT_KV_HEADS, HEAD_DIM))
            w_out = b_w_out[j]
        else:
            wc = c_w_in[j]
            w_in = jnp.concatenate([wc[:, :o_ki], wc[:, c_cols:],
                                    _pad_cols(wc, [(o_ki, IDX_DIM + IDX_HEADS, LANE)])], axis=1).astype(BF16)
            memq_cb = o_ki // memq_cols
            kw_cb = (o_ki + memq_cols) // LANE
            k_cb, v_cb = q_cols // kv_cols, q_cols // kv_cols + 1
            qi_cb = o_qi // (IDX_HEADS * IDX_DIM)
            zp = _mm_norm(xp, attn_norm[i], w_in).reshape(bp, t, -1)
            zs = _mm_norm(xs, attn_norm[i], w_in).reshape(bd, s_len, -1)
            kgain = _tile_gain(c_k_norm[j], kv_cols)
            qgain = _tile_gain(c_q_norm[j], LANE)
            igain = jnp.concatenate([c_idx_k_norm[j].astype(F32), jnp.zeros((LANE - IDX_DIM,), F32)]).reshape(1, LANE)
            no_sink = jnp.full((1, q_cols), -jnp.inf, F32)
            knp = _headnorm(zp.reshape(bp * t, -1), k_cb, kv_cols, kgain).reshape(bp, t, kv_cols)
            kip = _headnorm(zp.reshape(bp * t, -1), kw_cb, LANE, igain).reshape(bp, t, LANE)
            tq = min(t, DSA_TQ)
            kb = min(t, DSA_KB)
            sc, tau, cut = _dsa_select(zp, qi_cb, kw_cb, kip, tq=tq, kb=kb, k_sel=k_sel_p, n_valid=t, causal=True)
            ftq, fkb = min(t, DSA_FLASH_TQ), min(t, DSA_FLASH_KB)
            assert all(_causal_blocks(r // tq, tq, kb) * kb >= _causal_blocks(r // ftq, ftq, fkb) * fkb
                       for r in range(0, t, CHUNK))
            mp = _flash(zp, 0, q_cols, knp, 0, zp, v_cb, qgain, no_sink, mode="dsa", tq=ftq, kb=fkb,
                        group=att_group, n_valid=t, causal=True, scores=sc, tau=tau, jb=cut)
            outs["p_ds_k"].append(knp.reshape(bp, t, ATT_KV_HEADS, HEAD_DIM))
            outs["p_ds_v"].append(zp[:, :, q_cols + kv_cols:q_cols + 2 * kv_cols].reshape(bp, t, ATT_KV_HEADS, HEAD_DIM))
            outs["p_ds_i"].append(kip[:, :, :IDX_DIM])
            kns = _headnorm(zs.reshape(bd * s_len, -1), k_cb, kv_cols, kgain).reshape(bd, s_len, kv_cols)
            kis = _headnorm(zs.reshape(bd * s_len, -1), kw_cb, LANE, igain).reshape(bd, s_len, LANE)
            vs_new = zs[:, :, q_cols + kv_cols:q_cols + 2 * kv_cols]
            n_keys = past + s_len
            n_pad = -(-n_keys // LANE) * LANE
            zpad = lambda w: jnp.zeros((bd, n_pad - n_keys, w), F32)
            k_all = jnp.concatenate([cache_dsa_k[j].reshape(bd, past, kv_cols), kns, zpad(kv_cols)], axis=1)
            v_all = jnp.concatenate([cache_dsa_v[j].reshape(bd, past, kv_cols), vs_new, zpad(kv_cols)], axis=1)
            ki_cache = jnp.concatenate([cache_dsa_idx_k[j], jnp.zeros((bd, past, LANE - IDX_DIM), F32)], axis=-1)
            ki_all = jnp.concatenate([ki_cache, kis, zpad(LANE)], axis=1)
            sc, tau, cut = _dsa_select(zs, qi_cb, kw_cb, ki_all, tq=s_len, kb=n_pad, k_sel=k_sel_s, n_valid=n_keys,
                                       causal=False)
            ms = _flash(zs, 0, q_cols, k_all, 0, v_all, 0, qgain, no_sink, mode="dsa", tq=s_len, kb=n_pad,
                        group=att_group, n_valid=n_keys, causal=False, scores=sc, tau=tau, jb=cut)
            outs["s_ds_k"].append(kns.reshape(bd, s_len, ATT_KV_HEADS, HEAD_DIM))
            outs["s_ds_v"].append(vs_new.reshape(bd, s_len, ATT_KV_HEADS, HEAD_DIM))
            outs["s_ds_i"].append(kis[:, :, :IDX_DIM])
            w_out = c_w_out[j]

        kv_mem = _mm_norm(mem_prompt.reshape(bp * mem_tokens, d), mem_norm[i], mem_w_kv[i].astype(BF16))
        km_p = _headnorm(kv_mem, 0, memq_cols, _tile_gain(mem_k_norm[i], memq_cols)).reshape(bp, mem_tokens, memq_cols)
        vm_p = kv_mem[:, memq_cols:].reshape(bp, mem_tokens, memq_cols)
        outs["p_mk"].append(km_p.reshape(bp, mem_tokens, MEM_HEADS, HEAD_DIM))
        outs["p_mv"].append(vm_p.reshape(bp, mem_tokens, MEM_HEADS, HEAD_DIM))
        mo_p = _mem_attend(zp, memq_cb, km_p, vm_p, mem_q_norm[i])
        mo_s = _mem_attend(zs, memq_cb, cache_mem_k[i].reshape(bd, mem_tokens, memq_cols),
                           cache_mem_v[i].reshape(bd, mem_tokens, memq_cols), mem_q_norm[i])
        w_mix, w_mem = w_out[:d].astype(BF16), w_out[d:].astype(BF16)
        xp = _mm_res([mp.reshape(bp * t, d), mo_p.reshape(bp * t, memq_cols)], [w_mix, w_mem], xp)
        xs = _mm_res([ms.reshape(bd * s_len, d), mo_s.reshape(bd * s_len, memq_cols)], [w_mix, w_mem], xs)

        w_up, w_down = ffn_w_up[i].astype(BF16), ffn_w_down[i].astype(BF16)
        xp, cp = _conv_ffn(xp, bp, t, ffn_norm[i], w_up, ffn_conv[i], w_down,
                           jnp.zeros((bp, CONV_W - 1, 2 * d_ff), F32))
        xs, cs = _conv_ffn(xs, bd, s_len, ffn_norm[i], w_up, ffn_conv[i], w_down, state_ffn_conv[i])
        outs["p_cv"].append(cp)
        outs["s_cv"].append(cs)

    st = jnp.stack
    order = ("p_rw_wkv", "p_rw_sh", "p_sw_k", "p_sw_v", "p_ds_k", "p_ds_v", "p_ds_i", "p_mk", "p_mv", "p_cv",
             "s_rw_wkv", "s_rw_sh", "s_sw_k", "s_sw_v", "s_ds_k", "s_ds_v", "s_ds_i", "s_cv")
    return (xp.reshape(bp, t, d), xs.reshape(bd, s_len, d)) + tuple(st(outs[k]) for k in order)
```

```python
import functools
import math

import jax
import jax.numpy as jnp
from jax import lax
from jax.experimental import pallas as pl
from jax.experimental.pallas import tpu as pltpu

F32 = jnp.float32
BF16 = jnp.bfloat16

HEAD_DIM = 64
CHUNK = 64
NORM_EPS = 1e-6
RW_GN_EPS = HEAD_DIM * 1e-5
ATT_KV_HEADS = 4
WINDOW = 128
IDX_HEADS = 8
IDX_DIM = 64
TOPK_MAX = 256
MEM_HEADS = 4
CONV_W = 3

LANE = 128
VMEM_LIMIT = 52 * 1024 * 1024
NEG_BIG = -(2.0 ** 100)
INT_MIN = -2147483648
MM_ROWS = 1024
DSA_TQ = 256
DSA_KB = 512
DSA_FLASH_TQ = 256
DSA_FLASH_KB = 512


def _cparams(sem, vmem=VMEM_LIMIT):
    return pltpu.CompilerParams(dimension_semantics=sem, vmem_limit_bytes=vmem)


def _split3(a):
    a1 = a.astype(BF16)
    r1 = a - a1.astype(F32)
    a2 = r1.astype(BF16)
    r2 = r1 - a2.astype(F32)
    return a1, a2, r2.astype(BF16)


def _dot_exact_rhs(a, e):
    a1, a2, a3 = _split3(a)
    d = lambda x: jnp.dot(x, e, preferred_element_type=F32)
    return (d(a3) + d(a2)) + d(a1)


def _dot_exact_lhs(e, a):
    a1, a2, a3 = _split3(a)
    d = lambda x: jnp.dot(e, x, preferred_element_type=F32)
    return (d(a3) + d(a2)) + d(a1)


def _head_block_matrix(width, value):
    r = lax.broadcasted_iota(jnp.int32, (width, width), 0) // HEAD_DIM
    c = lax.broadcasted_iota(jnp.int32, (width, width), 1) // HEAD_DIM
    return jnp.where(r == c, value, 0.0).astype(BF16)


def _dot_rw(a, b, dims):
    return lax.dot_general(a.astype(BF16), b.astype(BF16), (dims, ((), ())), preferred_element_type=F32)


def _pick_tile(n, cap):
    best = None
    for t in range(LANE, min(n, cap) + 1, LANE):
        if n % t == 0:
            best = t
    assert best is not None, n
    return best


def _mm_norm_body(x_ref, g_ref, w_ref, o_ref, xn_ref):
    @pl.when(pl.program_id(1) == 0)
    def _():
        x = x_ref[...]
        ms = jnp.mean(x * x, axis=-1, keepdims=True)
        xn_ref[...] = ((x * lax.rsqrt(ms + NORM_EPS)) * g_ref[...]).astype(BF16)

    o_ref[...] = jnp.dot(xn_ref[...], w_ref[...], preferred_element_type=F32)


def _mm_norm(x, gain, w):
    m, k = x.shape
    n = w.shape[1]
    tm = min(m, MM_ROWS)
    tn = _pick_tile(n, 1536)
    return pl.pallas_call(
        _mm_norm_body,
        out_shape=jax.ShapeDtypeStruct((m, n), F32),
        grid=(m // tm, n // tn),
        in_specs=[pl.BlockSpec((tm, k), lambda i, j: (i, 0)),
                  pl.BlockSpec((1, k), lambda i, j: (0, 0)),
                  pl.BlockSpec((k, tn), lambda i, j: (0, j))],
        out_specs=pl.BlockSpec((tm, tn), lambda i, j: (i, j)),
        scratch_shapes=[pltpu.VMEM((tm, k), BF16)],
        compiler_params=_cparams(("parallel", "arbitrary")),
        name="mm_norm",
    )(x, gain.reshape(1, k), w)


def _mm_res_body(*refs, n_lhs):
    lhs = refs[:n_lhs]
    ws = refs[n_lhs:2 * n_lhs]
    r_ref, o_ref = refs[2 * n_lhs], refs[2 * n_lhs + 1]
    acc = jnp.dot(lhs[0][...], ws[0][...], preferred_element_type=F32)
    for a, w in zip(lhs[1:], ws[1:]):
        acc = acc + jnp.dot(a[...], w[...], preferred_element_type=F32)
    o_ref[...] = r_ref[...] + acc


def _mm_res(lhs_list, w_list, res):
    m, n = res.shape
    ktot = sum(a.shape[1] for a in lhs_list)
    tm = min(m, MM_ROWS)
    tn = _pick_tile(n, 1024 if ktot <= 3072 else 512)
    n_lhs = len(lhs_list)
    in_specs = [pl.BlockSpec((tm, a.shape[1]), lambda i, j: (i, 0)) for a in lhs_list]
    in_specs += [pl.BlockSpec((w.shape[0], tn), lambda i, j: (0, j)) for w in w_list]
    in_specs += [pl.BlockSpec((tm, tn), lambda i, j: (i, j))]
    return pl.pallas_call(
        functools.partial(_mm_res_body, n_lhs=n_lhs),
        out_shape=jax.ShapeDtypeStruct((m, n), F32),
        grid=(m // tm, n // tn),
        in_specs=in_specs,
        out_specs=pl.BlockSpec((tm, tn), lambda i, j: (i, j)),
        compiler_params=_cparams(("parallel", "arbitrary")),
        name="mm_res",
    )(*lhs_list, *w_list, res)


def _headnorm_body(x_ref, g_ref, o_ref, *, width):
    avg = _head_block_matrix(LANE, 1.0 / HEAD_DIM)
    for c in range(width // LANE):
        x = x_ref[:, c * LANE:(c + 1) * LANE]
        ms = _dot_exact_rhs(x * x, avg)
        o_ref[:, c * LANE:(c + 1) * LANE] = (x * lax.rsqrt(ms + NORM_EPS)) * g_ref[:, c * LANE:(c + 1) * LANE]


def _headnorm(x, col_block, width, gain_row):
    m = x.shape[0]
    tm = min(m, 1024)
    return pl.pallas_call(
        functools.partial(_headnorm_body, width=width),
        out_shape=jax.ShapeDtypeStruct((m, width), F32),
        grid=(m // tm,),
        in_specs=[pl.BlockSpec((tm, width), lambda i: (i, col_block)),
                  pl.BlockSpec((1, width), lambda i: (0, 0))],
        out_specs=pl.BlockSpec((tm, width), lambda i: (i, 0)),
        compiler_params=_cparams(("parallel",)),
        name="headnorm",
    )(x, gain_row)


def _up_conv_body(x_ref, g_ref, wa_ref, wb_ref, pa_ref, pb_ref, cwa_ref, cwb_ref, o_ref, la_ref, lb_ref,
                  xn_ref, ca_ref, cb_ref, *, tm, tiles_per_batch, bpt):
    i = pl.program_id(0)
    j = pl.program_id(1)
    rpb = tm // bpt

    @pl.when(j == 0)
    def _():
        x = x_ref[...]
        ms = jnp.mean(x * x, axis=-1, keepdims=True)
        xn_ref[...] = ((x * lax.rsqrt(ms + NORM_EPS)) * g_ref[...]).astype(BF16)

    if bpt == 1:
        @pl.when(i % tiles_per_batch == 0)
        def _():
            ca_ref[j] = pa_ref[0]
            cb_ref[j] = pb_ref[0]

    xn = xn_ref[...]
    row = lax.broadcasted_iota(jnp.int32, o_ref.shape, 0)
    off = row if bpt == 1 else row % rpb
    if bpt > 1:
        pick = (lax.broadcasted_iota(jnp.int32, (tm, bpt), 0) // rpb
                == lax.broadcasted_iota(jnp.int32, (tm, bpt), 1)).astype(BF16)

    def conv(w_ref, p_ref, c_ref, cw_ref, last_ref):
        u = jnp.dot(xn, w_ref[...], preferred_element_type=F32)
        if bpt == 1:
            car = c_ref[j]
            c0, c1 = car[0:1, :], car[1:2, :]
            c_ref[j] = u[tm - 2:tm, :]
            last_ref[0] = u[tm - 2:tm, :]
        else:
            c0 = _dot_exact_lhs(pick, p_ref[:, 0, :])
            c1 = _dot_exact_lhs(pick, p_ref[:, 1, :])
            for bi in range(bpt):
                last_ref[bi] = u[(bi + 1) * rpb - 2:(bi + 1) * rpb, :]
        u1 = jnp.where(off == 0, c1, pltpu.roll(u, 1, axis=0))
        u2 = jnp.where(off == 0, c0, jnp.where(off == 1, c1, pltpu.roll(u, 2, axis=0)))
        return (u2 * cw_ref[0:1, :] + u1 * cw_ref[1:2, :]) + u * cw_ref[2:3, :]

    a = conv(wa_ref, pa_ref, ca_ref, cwa_ref, la_ref)
    b = conv(wb_ref, pb_ref, cb_ref, cwb_ref, lb_ref)
    o_ref[...] = ((a * jax.nn.sigmoid(a)) * b).astype(o_ref.dtype)


def _up_conv(x, gain, w_up, conv_w, prev, t):
    m, k = x.shape
    f = w_up.shape[1] // 2
    b = m // t
    tm = min(m, MM_ROWS)
    bpt = max(1, tm // t)
    assert tm % t == 0 or t % tm == 0
    tn = _pick_tile(f, 512)
    nj = f // tn
    tpb = max(1, t // tm)
    act, la, lb = pl.pallas_call(
        functools.partial(_up_conv_body, tm=tm, tiles_per_batch=tpb, bpt=bpt),
        out_shape=(jax.ShapeDtypeStruct((m, f), BF16),
                   jax.ShapeDtypeStruct((b, CONV_W - 1, f), F32),
                   jax.ShapeDtypeStruct((b, CONV_W - 1, f), F32)),
        grid=(m // tm, nj),
        in_specs=[pl.BlockSpec((tm, k), lambda i, j: (i, 0)),
                  pl.BlockSpec((1, k), lambda i, j: (0, 0)),
                  pl.BlockSpec((k, tn), lambda i, j: (0, j)),
                  pl.BlockSpec((k, tn), lambda i, j: (0, nj + j)),
                  pl.BlockSpec((bpt, CONV_W - 1, tn), lambda i, j: (i // tpb, 0, j)),
                  pl.BlockSpec((bpt, CONV_W - 1, tn), lambda i, j: (i // tpb, 0, nj + j)),
                  pl.BlockSpec((CONV_W, tn), lambda i, j: (0, j)),
                  pl.BlockSpec((CONV_W, tn), lambda i, j: (0, nj + j))],
        out_specs=(pl.BlockSpec((tm, tn), lambda i, j: (i, j)),
                   pl.BlockSpec((bpt, CONV_W - 1, tn), lambda i, j: (i // tpb, 0, j)),
                   pl.BlockSpec((bpt, CONV_W - 1, tn), lambda i, j: (i // tpb, 0, j))),
        scratch_shapes=[pltpu.VMEM((tm, k), BF16),
                        pltpu.VMEM((nj, CONV_W - 1, tn), F32),
                        pltpu.VMEM((nj, CONV_W - 1, tn), F32)],
        compiler_params=_cparams(("arbitrary", "arbitrary")),
        name="up_conv_gate",
    )(x, gain.reshape(1, k), w_up, w_up, prev, prev, conv_w, conv_w)
    return act, jnp.concatenate([la, lb], axis=-1)


def _causal_blocks(i, tq, kb):
    return ((i + 1) * tq + kb - 1) // kb


def _fold_steps(nq, tq, kb):
    if nq % 2:
        return None
    return max(_causal_blocks(r, tq, kb) + _causal_blocks(nq - 1 - r, tq, kb) for r in range(nq // 2))


def _fold_step(r, step, nq, tq, kb):
    n_lo = _causal_blocks(r, tq, kb)
    hi = nq - 1 - r
    in_lo = step < n_lo
    i = jnp.where(in_lo, r, hi)
    last_blk = _causal_blocks(i, tq, kb) - 1
    j = jnp.minimum(jnp.where(in_lo, step, step - n_lo), last_blk)
    active = step < n_lo + _causal_blocks(hi, tq, kb)
    return i, j, active & (jnp.where(in_lo, step, step - n_lo) == 0), active & (j == last_blk), active


def _flash_body(*refs, mode, tq, kb, wq, group, n_valid, nk, causal, fold_nq):
    q_ref, k_ref, v_ref, qg_ref, sink_ref = refs[:5]
    pos = 5
    if mode == "dsa":
        sc_ref, tau_ref, jb_ref = refs[pos:pos + 3]
        pos += 3
    o_ref = refs[pos]
    qn_scr, acc_scr, m_scr, l_scr, bias_scr = refs[pos + 1:]

    if fold_nq:
        i, j, is_first, is_last, fold_active = _fold_step(pl.program_id(1), pl.program_id(2), fold_nq, tq, kb)
    else:
        i, j = pl.program_id(1), pl.program_id(2)
        is_first, is_last = j == 0, j == nk - 1
    npairs = wq // LANE
    pairs_per_unit = group // 2 if group > 1 else 1
    lane = lax.broadcasted_iota(jnp.int32, (1, LANE), 1)
    lo_half = lane < HEAD_DIM

    @pl.when(is_first)
    def _():
        avg = _head_block_matrix(LANE, 1.0 / HEAD_DIM)
        for p in range(npairs):
            x = q_ref[0, :, p * LANE:(p + 1) * LANE]
            ms = _dot_exact_rhs(x * x, avg)
            qn = ((x * lax.rsqrt(ms + NORM_EPS)) * qg_ref[...]) * (HEAD_DIM ** -0.5)
            qn_scr[p] = qn.astype(BF16)
        acc_scr[...] = jnp.zeros(acc_scr.shape, F32)
        l_scr[...] = jnp.zeros(l_scr.shape, F32)
        m_scr[...] = jnp.full(m_scr.shape, NEG_BIG, F32)

    if mode == "band":
        kblk = i - WINDOW // kb + j
        active = kblk >= 0
    elif mode == "dsa" and causal:
        kblk = j
        active = fold_active if fold_nq else j * kb < (i + 1) * tq
    else:
        kblk = j
        active = None

    def step():
        kidx = kblk * kb + lax.broadcasted_iota(jnp.int32, (tq, kb), 1)
        qpos = i * tq + lax.broadcasted_iota(jnp.int32, (tq, kb), 0)
        if mode == "band":
            qchunk = qpos // CHUNK
            sel = (kidx >= (qchunk - WINDOW // CHUNK) * CHUNK) & (kidx < (qchunk + 1) * CHUNK) & (kidx >= 0)
        elif mode == "dsa":
            x = sc_ref[0]
            tau = tau_ref[0][:, 0:1]
            jb = jb_ref[0][:, 0:1]
            adm = kidx < ((qpos // CHUNK + 1) * CHUNK if causal else n_valid)
            sel = adm & ((x > tau) | ((x == tau) & (kidx <= jb)))
        else:
            sel = kidx < n_valid
        bias_scr[...] = jnp.where(sel, 0.0, NEG_BIG).astype(BF16)

        ppu = pairs_per_unit
        for u in range(npairs // ppu):
            ps = slice(u * ppu, (u + 1) * ppu)
            s = lax.dot_general(qn_scr[ps].reshape(ppu * tq, LANE), k_ref[0, 0, u], (((1,), (1,)), ((), ())),
                                preferred_element_type=F32).reshape(ppu, tq, 2 * kb)
            bias = bias_scr[...][None]
            m_old = m_scr[ps]
            s0 = s[:, :, 0:kb].astype(BF16) + bias
            s1 = s[:, :, kb:2 * kb].astype(BF16) + bias
            mn0 = jnp.maximum(m_old[:, :, 0:1], jnp.max(s0, axis=2, keepdims=True).astype(F32))
            mn1 = jnp.maximum(m_old[:, :, HEAD_DIM:HEAD_DIM + 1], jnp.max(s1, axis=2, keepdims=True).astype(F32))
            p0 = jnp.exp(s0 - mn0.astype(BF16))
            p1 = jnp.exp(s1 - mn1.astype(BF16))
            pcat = jnp.concatenate([p0, p1], axis=2).reshape(ppu * tq, 2 * kb)
            pv = jnp.dot(pcat, v_ref[0, 0, u], preferred_element_type=F32)
            pv = pv.reshape(ppu, tq, 2 * LANE)
            mn = jnp.where(lo_half, mn0, mn1)
            alpha = jnp.exp(m_old - mn)
            acc_scr[ps] = acc_scr[ps] * alpha + pv[:, :, 0:LANE]
            l_scr[ps] = l_scr[ps] * alpha + pv[:, :, LANE:2 * LANE]
            m_scr[ps] = mn

    if active is None:
        step()
    else:
        pl.when(active)(step)

    @pl.when(is_last)
    def _():
        for p in range(npairs):
            den = l_scr[p] + jnp.exp(sink_ref[:, p * LANE:(p + 1) * LANE] - m_scr[p])
            o_ref[0, :, p * LANE:(p + 1) * LANE] = (acc_scr[p] / den).astype(o_ref.dtype)


def _pair_operands_body(k_ref, v_ref, kab_ref, vab_ref, *, kb, group, n_units):
    lane = lax.broadcasted_iota(jnp.int32, (1, LANE), 1)
    lo_half = lane < HEAD_DIM
    ones_lo = jnp.broadcast_to(jnp.where(lo_half, 1.0, 0.0), (kb, LANE)).astype(BF16)
    ones_hi = jnp.broadcast_to(jnp.where(lo_half, 0.0, 1.0), (kb, LANE)).astype(BF16)
    for u in range(n_units):
        if group > 1:
            tile_idx, half = u // 2, u % 2
        else:
            tile_idx, half = u, None
        for src, dst in ((k_ref, kab_ref), (v_ref, vab_ref)):
            tile = src[0, :, tile_idx * LANE:(tile_idx + 1) * LANE]
            if half is None:
                a_part = jnp.where(lo_half, tile, 0.0)
                b_part = jnp.where(lo_half, 0.0, tile)
            elif half == 0:
                a_part = jnp.where(lo_half, tile, 0.0)
                b_part = pltpu.roll(a_part, HEAD_DIM, axis=1)
            else:
                b_part = jnp.where(lo_half, 0.0, tile)
                a_part = pltpu.roll(b_part, HEAD_DIM, axis=1)
            dst[0, 0, u, 0:kb, 0:LANE] = a_part.astype(BF16)
            dst[0, 0, u, kb:2 * kb, 0:LANE] = b_part.astype(BF16)
        vab_ref[0, 0, u, 0:kb, LANE:2 * LANE] = ones_lo
        vab_ref[0, 0, u, kb:2 * kb, LANE:2 * LANE] = ones_hi


def _pair_operands(k_arr, k_cb, v_arr, v_cb, kb, group, npairs):
    b, t_k = k_arr.shape[0], k_arr.shape[1]
    wk = ATT_KV_HEADS * HEAD_DIM
    n_units = npairs // (group // 2) if group > 1 else npairs
    nkb = t_k // kb
    return pl.pallas_call(
        functools.partial(_pair_operands_body, kb=kb, group=group, n_units=n_units),
        out_shape=(jax.ShapeDtypeStruct((b, nkb, n_units, 2 * kb, LANE), BF16),
                   jax.ShapeDtypeStruct((b, nkb, n_units, 2 * kb, 2 * LANE), BF16)),
        grid=(b, nkb),
        in_specs=[pl.BlockSpec((1, kb, wk), lambda bi, j: (bi, j, k_cb)),
                  pl.BlockSpec((1, kb, wk), lambda bi, j: (bi, j, v_cb))],
        out_specs=(pl.BlockSpec((1, 1, n_units, 2 * kb, LANE), lambda bi, j: (bi, j, 0, 0, 0)),
                   pl.BlockSpec((1, 1, n_units, 2 * kb, 2 * LANE), lambda bi, j: (bi, j, 0, 0, 0))),
        compiler_params=_cparams(("parallel", "parallel")),
        name="pair_operands",
    )(k_arr, v_arr)


def _flash(q_arr, q_cb, wq, k_arr, k_cb, v_arr, v_cb, q_gain, sink_row, *, mode, tq, kb, group,
           n_valid=None, causal=False, scores=None, tau=None, jb=None):
    b, t_q = q_arr.shape[0], q_arr.shape[1]
    t_k = k_arr.shape[1]
    nq = t_q // tq
    if mode == "band":
        assert tq == kb == WINDOW
        nk = 2
        kmap = lambda i, j: jnp.maximum(i - 1 + j, 0)
    elif mode == "dsa" and causal:
        nk = t_k // kb
        kmap = lambda i, j: jnp.minimum(j, ((i + 1) * tq - 1) // kb)
    else:
        nk = t_k // kb
        kmap = lambda i, j: j
    grid = (b, nq, nk)
    qmap = lambda i, j: i
    fold = _fold_steps(nq, tq, kb) if (mode == "dsa" and causal) else None
    if fold:
        grid = (b, nq // 2, fold)
        qmap = lambda r, s: _fold_step(r, s, nq, tq, kb)[0]
        kmap = lambda r, s: _fold_step(r, s, nq, tq, kb)[1]
    npairs = wq // LANE
    kab, vab = _pair_operands(k_arr, k_cb, v_arr, v_cb, kb, group, npairs)
    n_units = kab.shape[2]
    in_specs = [pl.BlockSpec((1, tq, wq), lambda bi, i, j: (bi, qmap(i, j), q_cb)),
                pl.BlockSpec((1, 1, n_units, 2 * kb, LANE), lambda bi, i, j: (bi, kmap(i, j), 0, 0, 0)),
                pl.BlockSpec((1, 1, n_units, 2 * kb, 2 * LANE), lambda bi, i, j: (bi, kmap(i, j), 0, 0, 0)),
                pl.BlockSpec((1, LANE), lambda bi, i, j: (0, 0)),
                pl.BlockSpec((1, wq), lambda bi, i, j: (0, 0))]
    args = [q_arr, kab, vab, q_gain, sink_row]
    if mode == "dsa":
        in_specs += [pl.BlockSpec((1, tq, kb), lambda bi, i, j: (bi, qmap(i, j), kmap(i, j))),
                     pl.BlockSpec((1, tq, LANE), lambda bi, i, j: (bi, qmap(i, j), 0)),
                     pl.BlockSpec((1, tq, LANE), lambda bi, i, j: (bi, qmap(i, j), 0))]
        args += [scores, tau, jb]
    body = functools.partial(_flash_body, mode=mode, tq=tq, kb=kb, wq=wq, group=group,
                             n_valid=n_valid, nk=nk, causal=causal, fold_nq=nq if fold else 0)
    return pl.pallas_call(
        body,
        out_shape=jax.ShapeDtypeStruct((b, t_q, wq), BF16),
        grid=grid,
        in_specs=in_specs,
        out_specs=pl.BlockSpec((1, tq, wq), lambda bi, i, j: (bi, qmap(i, j), 0)),
        scratch_shapes=[pltpu.VMEM((npairs, tq, LANE), BF16),
                        pltpu.VMEM((npairs, tq, LANE), F32),
                        pltpu.VMEM((npairs, tq, LANE), F32),
                        pltpu.VMEM((npairs, tq, LANE), F32),
                        pltpu.VMEM((tq, kb), BF16)],
        compiler_params=_cparams(("parallel", "parallel", "arbitrary")),
        name="flash_" + mode,
    )(*args)


def _dsa_select_body(qi_ref, kw_ref, ki_ref, sc_ref, tau_ref, jb_ref, sc_scr, cnt_scr, ebuf_scr, *, tq, kb, nk, k_sel,
                     n_valid, causal, idx_bits, fold_nq):
    if fold_nq:
        i, j, _, is_last, fold_active = _fold_step(pl.program_id(1), pl.program_id(2), fold_nq, tq, kb)
    else:
        i, j = pl.program_id(1), pl.program_id(2)
        is_last = j == nk - 1
    qpos = i * tq + lax.broadcasted_iota(jnp.int32, (tq, 1), 0)
    if causal:
        n_adm = (qpos // CHUNK + 1) * CHUNK
        active = fold_active if fold_nq else j * kb < (i + 1) * tq
        n_blocks = _causal_blocks(i, tq, kb)
    else:
        n_adm = jnp.full((tq, 1), n_valid, jnp.int32)
        active = None
        n_blocks = nk

    def compute():
        ka = ki_ref[0]
        kab = jnp.concatenate([ka, pltpu.roll(ka, HEAD_DIM, axis=1)], axis=0).astype(BF16)
        kw = kw_ref[0]
        acc = jnp.zeros((tq, kb), F32)
        q4 = jnp.concatenate([qi_ref[0, :, p * LANE:(p + 1) * LANE].astype(BF16) for p in range(IDX_HEADS // 2)],
                             axis=0)
        s = lax.dot_general(q4, kab, (((1,), (1,)), ((), ())), preferred_element_type=F32)
        for p in range(IDX_HEADS // 2):
            for h in range(2):
                col = HEAD_DIM + 2 * p + h
                w = (kw[:, col:col + 1] * IDX_HEADS ** -0.5) * IDX_DIM ** -0.5
                acc = acc + jnp.maximum(s[p * tq:(p + 1) * tq, h * kb:(h + 1) * kb], 0.0) * w
        acc = jnp.where(acc == 0.0, 0.0, acc)
        kidx = j * kb + lax.broadcasted_iota(jnp.int32, (tq, kb), 1)
        x = jnp.where(kidx < n_adm, acc, -jnp.inf)
        sc_scr[j] = x
        sc_ref[0] = x

    if active is None:
        compute()
    else:
        pl.when(active)(compute)

        if not fold_nq:
            @pl.when(jnp.logical_not(active))
            def _():
                sc_ref[0] = jnp.full((tq, kb), -jnp.inf, F32)

    @pl.when(is_last)
    def _():
        kf = float(k_sel)

        rsz = min(tq, LANE)
        lane_tiles = range(kb // LANE)

        def count(make_pred):
            starts = range(0, tq, rsz)
            preds = [make_pred(lambda col, r0=r0: jnp.broadcast_to(col[r0:r0 + rsz], (rsz, LANE))) for r0 in starts]
            accs = []
            for r0, pred in zip(starts, preds):
                def blk(jj, acc, r0=r0, pred=pred):
                    for a in lane_tiles:
                        x = sc_scr[jj, r0:r0 + rsz, a * LANE:(a + 1) * LANE]
                        acc = acc + jnp.where(pred(x), 1.0, 0.0)
                    return acc
                accs.append(lax.fori_loop(0, n_blocks, blk, jnp.zeros((rsz, LANE), F32)))
            return jnp.sum(jnp.concatenate(accs, axis=0), axis=1, keepdims=True)

        def key_to_float(tu):
            cs = tu ^ INT_MIN
            fb = jnp.where(cs >= 0, cs, cs ^ 0x7FFFFFFF)
            return lax.bitcast_convert_type(fb, F32)

        def bit_body(it, tu):
            cand = tu | jnp.left_shift(jnp.int32(1), 31 - it)
            thr = key_to_float(cand)
            c = count(lambda widen: (lambda x, t=widen(thr): x >= t))
            return jnp.where(c >= kf, cand, tu)

        tu = lax.fori_loop(0, 32, bit_body, jnp.zeros((tq, 1), jnp.int32))
        full = n_adm >= k_sel
        tau = jnp.where(full, key_to_float(tu), -jnp.inf)
        tau = jnp.where(jnp.abs(tau) < 1.1754944e-38, 0.0, tau)
        c_gt = count(lambda widen: (lambda x, t=widen(tau): x > t))
        c_ge = count(lambda widen: (lambda x, t=widen(tau): x >= t))
        need = kf - c_gt
        tau_ref[0] = jnp.broadcast_to(tau, (tq, LANE))
        jb_ref[0] = jnp.full((tq, LANE), 2 ** 30, jnp.int32)
        surplus = jnp.max(jnp.where(full, (c_ge - c_gt) - need, 0.0))

        @pl.when(surplus > 0.5)
        def _():
            tau_b = jnp.broadcast_to(tau, (tq, LANE))
            need_b = jnp.broadcast_to(need, (tq, LANE))
            ones_mat = jnp.ones((LANE, LANE), BF16)
            row_sum = lambda part: jnp.dot(part.astype(BF16), ones_mat, preferred_element_type=F32)

            def ties(jj, a):
                return jnp.where(sc_scr[jj, :, a * LANE:(a + 1) * LANE] == tau_b, 1.0, 0.0)

            def blk_count(jj, carry):
                part = ties(jj, 0)
                for a in lane_tiles[1:]:
                    part = part + ties(jj, a)
                cnt_scr[jj] = row_sum(part)
                return carry
            lax.fori_loop(0, n_blocks, blk_count, 0)

            def locate(jj, carry):
                run, blk_idx, before = carry
                run = run + cnt_scr[jj]
                below = run < need_b
                return run, blk_idx + jnp.where(below, 1, 0), jnp.where(below, run, before)
            zero = jnp.zeros((tq, LANE), F32)
            _, blk_idx, before = lax.fori_loop(0, n_blocks, locate, (zero, jnp.zeros((tq, LANE), jnp.int32), zero))
            need_in = need_b - before

            ebuf_scr[...] = jnp.zeros((tq, kb), F32)

            def extract(jj, carry):
                here = blk_idx == jj
                for a in lane_tiles:
                    cols = slice(a * LANE, (a + 1) * LANE)
                    ebuf_scr[:, cols] = jnp.where(here, ties(jj, a), ebuf_scr[:, cols])
                return carry
            lax.fori_loop(0, n_blocks, extract, 0)

            lane_idx = lax.broadcasted_iota(jnp.int32, (tq, LANE), 1)
            cut_bits = max(1, (kb - 1).bit_length())

            def cut_body(it, cut):
                cand = cut | jnp.left_shift(jnp.int32(1), cut_bits - 1 - it)
                part = zero
                for a in lane_tiles:
                    part = part + jnp.where(a * LANE + lane_idx < cand, ebuf_scr[:, a * LANE:(a + 1) * LANE], 0.0)
                return jnp.where(row_sum(part) < need_in, cand, cut)
            cut = lax.fori_loop(0, cut_bits, cut_body, jnp.zeros((tq, LANE), jnp.int32))
            jb_ref[0] = blk_idx * kb + cut


def _dsa_select(z, qi_cb, kw_cb, ki_n, *, tq, kb, k_sel, n_valid, causal):
    b, t_q = z.shape[0], z.shape[1]
    t_k = ki_n.shape[1]
    nq, nk = t_q // tq, t_k // kb
    assert kb // LANE <= 256, "per-lane tie counts of one key block go through a bf16 matmul, exact up to 256"
    kmap = (lambda i, j: jnp.minimum(j, ((i + 1) * tq - 1) // kb)) if causal else (lambda i, j: j)
    omap = lambda i, j: j
    qmap = lambda i, j: i
    grid = (b, nq, nk)
    fold = _fold_steps(nq, tq, kb) if causal else None
    if fold:
        grid = (b, nq // 2, fold)
        qmap = lambda r, s: _fold_step(r, s, nq, tq, kb)[0]
        kmap = omap = lambda r, s: _fold_step(r, s, nq, tq, kb)[1]
    body = functools.partial(_dsa_select_body, tq=tq, kb=kb, nk=nk, k_sel=k_sel, n_valid=n_valid,
                             causal=causal, idx_bits=max(1, (t_k - 1).bit_length()), fold_nq=nq if fold else 0)
    return pl.pallas_call(
        body,
        out_shape=(jax.ShapeDtypeStruct((b, t_q, t_k), F32),
                   jax.ShapeDtypeStruct((b, t_q, LANE), F32),
                   jax.ShapeDtypeStruct((b, t_q, LANE), jnp.int32)),
        grid=grid,
        in_specs=[pl.BlockSpec((1, tq, IDX_HEADS * IDX_DIM), lambda bi, i, j: (bi, qmap(i, j), qi_cb)),
                  pl.BlockSpec((1, tq, LANE), lambda bi, i, j: (bi, qmap(i, j), kw_cb)),
                  pl.BlockSpec((1, kb, LANE), lambda bi, i, j: (bi, kmap(i, j), 0))],
        out_specs=(pl.BlockSpec((1, tq, kb), lambda bi, i, j: (bi, qmap(i, j), omap(i, j))),
                   pl.BlockSpec((1, tq, LANE), lambda bi, i, j: (bi, qmap(i, j), 0)),
                   pl.BlockSpec((1, tq, LANE), lambda bi, i, j: (bi, qmap(i, j), 0))),
        scratch_shapes=[pltpu.VMEM((nk, tq, kb), F32), pltpu.VMEM((nk, tq, LANE), F32), pltpu.VMEM((tq, kb), F32)],
        compiler_params=_cparams(("parallel", "parallel", "arbitrary")),
        name="dsa_select",
    )(z, z, ki_n)


def _rwkv_body(zr_ref, zk_ref, zv_ref, zw_ref, za_ref, zg_ref,
               sr_ref, sk_ref, sv_ref, sw_ref, sa_ref, sg_ref,
               mr_ref, mk_ref, mv_ref, mw_ref, ma_ref, mg_ref,
               vec_ref, w2_ref, a2_ref, g2_ref, s0_ref,
               mix_ref, sfin_ref,
               s_scr, prev_scr, prevl_scr, prevg_scr, y_scr, *, tc, lc, nt, pp):
    t = pl.program_id(2)

    @pl.when(t == 0)
    def _():
        s_scr[...] = s0_ref[0]
        prev_scr[0:1, :] = sr_ref[0]
        prev_scr[1:2, :] = sk_ref[0]
        prev_scr[2:3, :] = sv_ref[0]
        prevl_scr[0:1, :] = sw_ref[0]
        prevl_scr[1:2, :] = sa_ref[0]
        prevg_scr[0:1, :] = sg_ref[0]

    row = lax.broadcasted_iota(jnp.int32, (tc, 1), 0)

    def shifted(z_ref, prow, mu_ref):
        z = z_ref[0]
        zp = jnp.where(row == 0, prow, pltpu.roll(z, 1, axis=0))
        return z + (zp - z) * mu_ref[...], z[tc - 1:tc, :]

    r, last_r = shifted(zr_ref, prev_scr[0:1, :], mr_ref)
    k, last_k = shifted(zk_ref, prev_scr[1:2, :], mk_ref)
    v, last_v = shifted(zv_ref, prev_scr[2:3, :], mv_ref)
    zw, last_w = shifted(zw_ref, prevl_scr[0:1, :], mw_ref)
    za, last_a = shifted(za_ref, prevl_scr[1:2, :], ma_ref)
    zg, last_g = shifted(zg_ref, prevg_scr[0:1, :], mg_ref)
    prev_scr[0:1, :] = last_r
    prev_scr[1:2, :] = last_k
    prev_scr[2:3, :] = last_v
    prevl_scr[0:1, :] = last_w
    prevl_scr[1:2, :] = last_a
    prevg_scr[0:1, :] = last_g

    w0, a0 = vec_ref[0:1, :], vec_ref[1:2, :]
    k_k, k_a, r_k = vec_ref[2:3, :], vec_ref[3:4, :], vec_ref[4:5, :]
    ln_w, ln_b = vec_ref[5:6, :], vec_ref[6:7, :]

    ones_blk = _head_block_matrix(LANE, 1.0)
    avg_blk = _head_block_matrix(LANE, 1.0 / HEAD_DIM)

    def per_head(x, blk):
        return jnp.concatenate([_dot_exact_rhs(x[:, i * LANE:(i + 1) * LANE], blk) for i in range(pp)], axis=1)

    xw = w0 + jnp.dot(jnp.tanh(zw).astype(BF16), w2_ref[...], preferred_element_type=F32)
    lw = -math.exp(-0.5) * jax.nn.sigmoid(xw)
    a = jax.nn.sigmoid(a0 + jnp.dot(za.astype(BF16), a2_ref[...], preferred_element_type=F32))
    g = jnp.dot(jax.nn.sigmoid(zg).astype(BF16), g2_ref[...], preferred_element_type=F32)
    kk = k * k_k
    kk = kk * jnp.minimum(lax.rsqrt(per_head(kk * kk, ones_blk)), 1e12)
    k2 = k * (1.0 + (a - 1.0) * k_a)

    a_step = -kk
    b_step = kk * a

    lane = lax.broadcasted_iota(jnp.int32, (1, LANE), 1)
    m0 = lane < HEAD_DIM
    rr = lax.broadcasted_iota(jnp.int32, (lc, 2 * lc), 0)
    cc = lax.broadcasted_iota(jnp.int32, (lc, 2 * lc), 1)
    incl = jnp.where(cc < lc, cc, cc - lc) <= rr
    strict = (lax.broadcasted_iota(jnp.int32, (2 * lc, 2 * lc), 1)
              < lax.broadcasted_iota(jnp.int32, (2 * lc, 2 * lc), 0))
    tri = (lax.broadcasted_iota(jnp.int32, (lc, lc), 1)
           <= lax.broadcasted_iota(jnp.int32, (lc, lc), 0)).astype(BF16)
    nsteps = lc.bit_length() - 1
    nt_dims = ((1,), (1,))
    nn_dims = ((1,), (0,))
    tn_dims = ((0,), (0,))

    chunks = range(tc // lc)
    st = []
    for c in chunks:
        rows = slice(c * lc, (c + 1) * lc)
        lwc = lw[rows]
        cs = _dot_exact_lhs(tri, lwc)
        p_in = jnp.exp(-cs)
        at = a_step[rows] * jnp.exp(cs - lwc)
        bt = b_step[rows] * p_in
        kt = k2[rows] * p_in
        rt = r[rows] * jnp.exp(cs)
        vc = v[rows]
        p_last = jnp.exp(cs[lc - 1:lc, :])
        for pi in range(pp):
            cols = slice(pi * LANE, (pi + 1) * LANE)
            stack = lambda x: jnp.concatenate([jnp.where(m0, x[:, cols], 0.0),
                                               jnp.where(m0, 0.0, x[:, cols])], axis=0).astype(BF16)
            st.append(dict(c=c, pi=pi,
                           ar=jnp.concatenate([stack(at), rt[:, cols].astype(BF16)], axis=0),
                           bk=jnp.concatenate([stack(bt), stack(kt)], axis=0),
                           v_s=stack(vc), p_last=p_last[:, cols]))
    for d in st:
        gram = _dot_rw(d["ar"], d["bk"], nt_dims)
        d["t_p"] = jnp.where(strict, gram[0:2 * lc, 0:2 * lc], 0.0)
        d["w_ak"] = jnp.where(strict, gram[0:2 * lc, 2 * lc:4 * lc], 0.0).astype(BF16)
        d["w_rb"] = jnp.where(incl, gram[2 * lc:3 * lc, 0:2 * lc], 0.0).astype(BF16)
        d["w_rk"] = jnp.where(incl, gram[2 * lc:3 * lc, 2 * lc:4 * lc], 0.0).astype(BF16)
    gw = 4 * lc
    lane_head = lax.broadcasted_iota(jnp.int32, (1, gw), 1) // lc
    eye_sbs = (lax.broadcasted_iota(jnp.int32, (lc, gw), 0)
               == lax.broadcasted_iota(jnp.int32, (lc, gw), 1) % lc).astype(F32)
    lane_pair = lax.broadcasted_iota(jnp.int32, (1, 2 * lc), 1)

    def blockdiag(y):
        yb = y.astype(BF16)
        return jnp.concatenate([jnp.where(lane_head == h, yb, jnp.zeros_like(yb)) for h in range(4)], axis=0)

    groups = []
    for gi in range(len(st) // 2):
        members = (st[2 * gi], st[2 * gi + 1])
        t_sbs = jnp.concatenate([m["t_p"][0:lc] + m["t_p"][lc:2 * lc] for m in members], axis=1)
        groups.append(dict(members=members, t=t_sbs, minv=eye_sbs + t_sbs))
    for grp in groups:
        grp["t"] = _dot_rw(grp["t"], blockdiag(grp["t"]), nn_dims)
    for step in range(nsteps - 1):
        for grp in groups:
            if step == nsteps - 2:
                grp["minv"] = grp["minv"] + _dot_rw(grp["minv"], blockdiag(grp["t"]), nn_dims)
            else:
                both = _dot_rw(jnp.concatenate([grp["minv"], grp["t"]], axis=0), blockdiag(grp["t"]), nn_dims)
                grp["minv"] = grp["minv"] + both[0:lc]
                grp["t"] = both[lc:2 * lc]
    for grp in groups:
        for idx, m in enumerate(grp["members"]):
            tile = grp["minv"][:, idx * 2 * lc:(idx + 1) * 2 * lc]
            m["minv"] = jnp.concatenate([jnp.where(lane_pair < lc, tile, 0.0),
                                         jnp.where(lane_pair < lc, 0.0, tile)], axis=0).astype(BF16)
    for d in st:
        both = _dot_rw(jnp.concatenate([d["w_ak"], d["w_rk"]], axis=0), d["v_s"], nn_dims)
        d["wv"] = both[0:2 * lc]
        d["y_c"] = both[2 * lc:3 * lc]
    for d in st:
        both = _dot_rw(d["minv"], jnp.concatenate([d["ar"][0:2 * lc], d["wv"].astype(BF16)], axis=1), nn_dims)
        d["ma"] = both[:, 0:LANE]
        d["mwv"] = both[:, LANE:2 * LANE]
    for d in st:
        d["g"] = _dot_rw(d["ma"], d["bk"][0:2 * lc], tn_dims).astype(BF16)
        d["d"] = _dot_rw(jnp.concatenate([d["mwv"].astype(BF16), d["v_s"]], axis=0), d["bk"], tn_dims)
    s_cur = [s_scr[pi] for pi in range(pp)]
    for d in st:
        s_in = s_cur[d["pi"]]
        d["s0"] = s_in.astype(BF16)
        s_cur[d["pi"]] = ((s_in + _dot_rw(d["s0"], d["g"], nn_dims)) + d["d"]) * d["p_last"]
    for pi in range(pp):
        s_scr[pi] = s_cur[pi]
    for d in st:
        d["xr"] = _dot_rw(d["ar"], d["s0"], nt_dims)
    for d in st:
        d["u"] = (_dot_rw(d["minv"], d["xr"][0:2 * lc], nn_dims) + d["mwv"]).astype(BF16)
    for d in st:
        c, pi = d["c"], d["pi"]
        y_scr[c * lc:(c + 1) * lc, pi * LANE:(pi + 1) * LANE] = (
            (d["xr"][2 * lc:3 * lc] + _dot_rw(d["w_rb"], d["u"], nn_dims)) + d["y_c"])

    y = y_scr[...]
    mean = per_head(y, avg_blk)
    dev = y - mean
    var = per_head(dev * dev, avg_blk)
    yn = (dev * lax.rsqrt(var + RW_GN_EPS)) * ln_w + ln_b
    bonus = per_head((r * k2) * r_k, ones_blk) * v
    mix_ref[0] = ((yn + bonus) * g).astype(mix_ref.dtype)

    @pl.when(t == nt - 1)
    def _():
        sfin_ref[0] = s_scr[...]


def _rwkv(z, shift_prev, s0_pairs, mu, vecs, w2, a2, g2, d_model):
    b, t = z.shape[0], z.shape[1]
    npairs = d_model // LANE
    tc = min(t, 512)
    lc = min(CHUNK, t)
    nt = t // tc
    pp = 2 if tc // lc >= 4 else 8
    wp = pp * LANE
    ngroups = npairs // pp
    cb_w, cb_a, cb_g = 3 * npairs, 3 * npairs + 1, (3 * npairs + 2) // 2

    def zspec(width, cbf):
        return pl.BlockSpec((1, tc, width), lambda bi, p, ti: (bi, ti, cbf(p)))

    def sspec(width, cbf):
        return pl.BlockSpec((1, 1, width), lambda bi, p, ti: (bi, 0, cbf(p)))

    def mspec(width, cbf):
        return pl.BlockSpec((1, width), lambda bi, p, ti: (0, cbf(p)))

    cbfs = [(wp, lambda p: p), (wp, lambda p: ngroups + p), (wp, lambda p: 2 * ngroups + p),
            (LANE, lambda p: cb_w), (LANE, lambda p: cb_a), (2 * LANE, lambda p: cb_g)]
    in_specs = ([zspec(w, f) for w, f in cbfs] + [sspec(w, f) for w, f in cbfs] + [mspec(w, f) for w, f in cbfs]
                + [pl.BlockSpec((8, wp), lambda bi, p, ti: (0, p)),
                   pl.BlockSpec((LANE, wp), lambda bi, p, ti: (0, p)),
                   pl.BlockSpec((LANE, wp), lambda bi, p, ti: (0, p)),
                   pl.BlockSpec((2 * LANE, wp), lambda bi, p, ti: (0, p)),
                   pl.BlockSpec((1, pp, LANE, LANE), lambda bi, p, ti: (bi, p, 0, 0))])
    return pl.pallas_call(
        functools.partial(_rwkv_body, tc=tc, lc=lc, nt=nt, pp=pp),
        out_shape=(jax.ShapeDtypeStruct((b, t, d_model), BF16),
                   jax.ShapeDtypeStruct((b, npairs, LANE, LANE), F32)),
        grid=(b, ngroups, nt),
        in_specs=in_specs,
        out_specs=(pl.BlockSpec((1, tc, wp), lambda bi, p, ti: (bi, ti, p)),
                   pl.BlockSpec((1, pp, LANE, LANE), lambda bi, p, ti: (bi, p, 0, 0))),
        scratch_shapes=[pltpu.VMEM((pp, LANE, LANE), F32),
                        pltpu.VMEM((8, wp), F32),
                        pltpu.VMEM((8, LANE), F32),
                        pltpu.VMEM((8, 2 * LANE), F32),
                        pltpu.VMEM((tc, wp), F32)],
        compiler_params=_cparams(("parallel", "parallel", "arbitrary")),
        name="rwkv7",
    )(*([z] * 6), *([shift_prev] * 6), *([mu] * 6), vecs, w2, a2, g2, s0_pairs)


def _pad_cols(x, segments):
    parts = []
    for start, width, padded in segments:
        seg = x[..., start:start + width]
        if padded > width:
            seg = jnp.concatenate([seg, jnp.zeros(seg.shape[:-1] + (padded - width,), seg.dtype)], axis=-1)
        parts.append(seg)
    return jnp.concatenate(parts, axis=-1)


def _pad_rows(x, padded):
    return jnp.concatenate([x, jnp.zeros((padded - x.shape[0],) + x.shape[1:], x.dtype)], axis=0)


def _pairs_from_heads(s):
    b, h = s.shape[0], s.shape[1]
    s = s.reshape(b, h // 2, 2, HEAD_DIM, HEAD_DIM)
    z = jnp.zeros_like(s[:, :, 0])
    top = jnp.concatenate([s[:, :, 0], z], axis=-1)
    bot = jnp.concatenate([z, s[:, :, 1]], axis=-1)
    return jnp.concatenate([top, bot], axis=-2)


def _heads_from_pairs(sp):
    b, npairs = sp.shape[0], sp.shape[1]
    s = jnp.stack([sp[:, :, :HEAD_DIM, :HEAD_DIM], sp[:, :, HEAD_DIM:, HEAD_DIM:]], axis=2)
    return s.reshape(b, 2 * npairs, HEAD_DIM, HEAD_DIM)


def _tile_gain(g, width):
    return jnp.tile(g.astype(F32), width // HEAD_DIM).reshape(1, width)


def _mem_attend(z, memq_cb, km, vm, q_gain):
    t_q = z.shape[1]
    wq = MEM_HEADS * HEAD_DIM
    tq = min(t_q, 512)
    no_sink = jnp.full((1, wq), -jnp.inf, F32)
    return _flash(z, memq_cb, wq, km, 0, vm, 0, _tile_gain(q_gain, LANE), no_sink,
                  mode="all", tq=tq, kb=km.shape[1], group=1, n_valid=km.shape[1])


def _conv_ffn(x2d, b, t, gain, w_up, conv_w, w_down, prev):
    act, u_last = _up_conv(x2d, gain, w_up, conv_w, prev, t)
    return _mm_res([act], [w_down], x2d), u_last


def kernel(x_prompt, x_sample, state_rwkv_wkv, state_rwkv_shift, cache_swa_k, cache_swa_v, cache_dsa_k, cache_dsa_v, cache_dsa_idx_k, cache_mem_k, cache_mem_v, state_ffn_conv, mem_prompt, attn_norm, ffn_norm, mem_norm, mem_w_kv, mem_q_norm, mem_k_norm, a_w_in, a_mu, a_w0, a_w2, a_a0, a_a2, a_g2, a_k_k, a_k_a, a_r_k, a_ln_w, a_ln_b, a_w_out, b_w_in, b_q_norm, b_k_norm, b_sink, b_w_out, c_w_in, c_q_norm, c_k_norm, c_idx_k_norm, c_w_out, ffn_w_up, ffn_conv, ffn_w_down):
    bp, t, d = x_prompt.shape
    bd, s_len = x_sample.shape[:2]
    depth = attn_norm.shape[0]
    win_rows = cache_swa_k.shape[2]
    past = cache_dsa_k.shape[2] if cache_dsa_k.shape[0] else 0
    d_ff = ffn_w_down.shape[1]
    mem_tokens = mem_prompt.shape[1]
    q_cols = d
    kv_cols = ATT_KV_HEADS * HEAD_DIM
    memq_cols = MEM_HEADS * HEAD_DIM
    att_group = (d // HEAD_DIM) // ATT_KV_HEADS
    dec_lora = a_w2.shape[1]
    a_lora = a_a2.shape[1]
    g_lora = a_g2.shape[1]
    rw_cols = 3 * d + dec_lora + a_lora + g_lora
    k_sel_p = min(TOPK_MAX, t // 4)
    k_sel_s = min(TOPK_MAX, (past + s_len) // 4)
    assert g_lora == 2 * LANE and dec_lora <= LANE and a_lora <= LANE

    xp = x_prompt.reshape(bp * t, d)
    xs = x_sample.reshape(bd * s_len, d)

    rw_segments = [(0, 3 * d, 3 * d), (3 * d, dec_lora, LANE), (3 * d + dec_lora, a_lora, LANE),
                   (3 * d + dec_lora + a_lora, g_lora, g_lora)]
    rw_padded = 3 * d + 2 * LANE + g_lora
    o_qi = q_cols + 2 * kv_cols
    o_ki = o_qi + IDX_HEADS * IDX_DIM
    c_cols = o_ki + IDX_DIM + IDX_HEADS

    outs = {k: [] for k in ("p_rw_wkv", "p_rw_sh", "p_sw_k", "p_sw_v", "p_ds_k", "p_ds_v", "p_ds_i", "p_mk",
                            "p_mv", "p_cv", "s_rw_wkv", "s_rw_sh", "s_sw_k", "s_sw_v", "s_ds_k", "s_ds_v",
                            "s_ds_i", "s_cv")}

    def unpad_rw(row):
        return jnp.concatenate([row[..., :3 * d], row[..., 3 * d:3 * d + dec_lora],
                                row[..., 3 * d + LANE:3 * d + LANE + a_lora],
                                row[..., 3 * d + 2 * LANE:3 * d + 2 * LANE + g_lora]], axis=-1)

    for i in range(depth):
        kind, j = i % 3, i // 3
        if kind == 0:
            w_in = jnp.concatenate([_pad_cols(a_w_in[j], rw_segments), a_w_in[j][:, rw_cols:]], axis=1).astype(BF16)
            memq_cb = rw_padded // memq_cols
            zp = _mm_norm(xp, attn_norm[i], w_in).reshape(bp, t, -1)
            zs = _mm_norm(xs, attn_norm[i], w_in).reshape(bd, s_len, -1)
            mu = _pad_cols(a_mu[j].reshape(1, -1), rw_segments)
            mu = jnp.concatenate([mu, jnp.zeros((1, memq_cols), F32)], axis=1)
            vecs = jnp.stack([a_w0[j], a_a0[j], a_k_k[j], a_k_a[j], a_r_k[j].reshape(-1), a_ln_w[j], a_ln_b[j],
                              jnp.zeros((d,), F32)], axis=0)
            w2 = _pad_rows(a_w2[j], LANE).astype(BF16)
            a2 = _pad_rows(a_a2[j], LANE).astype(BF16)
            g2 = a_g2[j].astype(BF16)
            sh_p = jnp.zeros((bp, 1, zp.shape[-1]), F32)
            st_p = jnp.zeros((bp, d // LANE, LANE, LANE), F32)
            sh_s = _pad_cols(state_rwkv_shift[j], rw_segments)
            sh_s = jnp.concatenate([sh_s, jnp.zeros((bd, memq_cols), F32)], axis=1).reshape(bd, 1, -1)
            st_s = _pairs_from_heads(state_rwkv_wkv[j])
            mp, stp = _rwkv(zp, sh_p, st_p, mu, vecs, w2, a2, g2, d)
            ms, sts = _rwkv(zs, sh_s, st_s, mu, vecs, w2, a2, g2, d)
            outs["p_rw_sh"].append(unpad_rw(zp[:, -1]))
            outs["p_rw_wkv"].append(_heads_from_pairs(stp))
            outs["s_rw_sh"].append(unpad_rw(zs[:, -1]))
            outs["s_rw_wkv"].append(_heads_from_pairs(sts))
            w_out = a_w_out[j]
        elif kind == 1:
            w_in = b_w_in[j].astype(BF16)
            memq_cb = (q_cols + 2 * kv_cols) // memq_cols
            k_cb, v_cb = q_cols // kv_cols, q_cols // kv_cols + 1
            zp = _mm_norm(xp, attn_norm[i], w_in).reshape(bp, t, -1)
            zs = _mm_norm(xs, attn_norm[i], w_in).reshape(bd, s_len, -1)
            kgain = _tile_gain(b_k_norm[j], kv_cols)
            qgain = _tile_gain(b_q_norm[j], LANE)
            sink = jnp.repeat(b_sink[j].astype(F32), HEAD_DIM).reshape(1, q_cols)
            knp = _headnorm(zp.reshape(bp * t, -1), k_cb, kv_cols, kgain).reshape(bp, t, kv_cols)
            mp = _flash(zp, 0, q_cols, knp, 0, zp, v_cb, qgain, sink, mode="band", tq=WINDOW, kb=WINDOW,
                        group=att_group)
            outs["p_sw_k"].append(knp[:, t - win_rows:].reshape(bp, win_rows, ATT_KV_HEADS, HEAD_DIM))
            outs["p_sw_v"].append(zp[:, t - win_rows:, q_cols + kv_cols:q_cols + 2 * kv_cols]
                                  .reshape(bp, win_rows, ATT_KV_HEADS, HEAD_DIM))
            kns = _headnorm(zs.reshape(bd * s_len, -1), k_cb, kv_cols, kgain).reshape(bd, s_len, kv_cols)
            vs_new = zs[:, :, q_cols + kv_cols:q_cols + 2 * kv_cols]
            k_all = jnp.concatenate([cache_swa_k[j].reshape(bd, win_rows, kv_cols), kns], axis=1)
            v_all = jnp.concatenate([cache_swa_v[j].reshape(bd, win_rows, kv_cols), vs_new], axis=1)
            n_keys = win_rows + s_len
            n_pad = -(-n_keys // LANE) * LANE
            pad = jnp.zeros((bd, n_pad - n_keys, kv_cols), F32)
            ms = _flash(zs, 0, q_cols, jnp.concatenate([k_all, pad], axis=1), 0,
                        jnp.concatenate([v_all, pad], axis=1), 0, qgain, sink, mode="all", tq=s_len, kb=n_pad,
                        group=att_group, n_valid=n_keys)
            outs["s_sw_k"].append(k_all[:, n_keys - win_rows:].reshape(bd, win_rows, ATT_KV_HEADS, HEAD_DIM))
            outs["s_sw_v"].append(v_all[:, n_keys - win_rows:].reshape(bd, win_rows, ATT_KV_HEADS, HEAD_DIM))
            w_out = b_w_out[j]
        else:
            wc = c_w_in[j]
            w_in = jnp.concatenate([wc[:, :o_ki], wc[:, c_cols:],
                                    _pad_cols(wc, [(o_ki, IDX_DIM + IDX_HEADS, LANE)])], axis=1).astype(BF16)
            memq_cb = o_ki // memq_cols
            kw_cb = (o_ki + memq_cols) // LANE
            k_cb, v_cb = q_cols // kv_cols, q_cols // kv_cols + 1
            qi_cb = o_qi // (IDX_HEADS * IDX_DIM)
            zp = _mm_norm(xp, attn_norm[i], w_in).reshape(bp, t, -1)
            zs = _mm_norm(xs, attn_norm[i], w_in).reshape(bd, s_len, -1)
            kgain = _tile_gain(c_k_norm[j], kv_cols)
            qgain = _tile_gain(c_q_norm[j], LANE)
            igain = jnp.concatenate([c_idx_k_norm[j].astype(F32), jnp.zeros((LANE - IDX_DIM,), F32)]).reshape(1, LANE)
            no_sink = jnp.full((1, q_cols), -jnp.inf, F32)
            knp = _headnorm(zp.reshape(bp * t, -1), k_cb, kv_cols, kgain).reshape(bp, t, kv_cols)
            kip = _headnorm(zp.reshape(bp * t, -1), kw_cb, LANE, igain).reshape(bp, t, LANE)
            tq = min(t, DSA_TQ)
            kb = min(t, DSA_KB)
            sc, tau, cut = _dsa_select(zp, qi_cb, kw_cb, kip, tq=tq, kb=kb, k_sel=k_sel_p, n_valid=t, causal=True)
            ftq, fkb = min(t, DSA_FLASH_TQ), min(t, DSA_FLASH_KB)
            assert all(_causal_blocks(r // tq, tq, kb) * kb >= _causal_blocks(r // ftq, ftq, fkb) * fkb
                       for r in range(0, t, CHUNK))
            mp = _flash(zp, 0, q_cols, knp, 0, zp, v_cb, qgain, no_sink, mode="dsa", tq=ftq, kb=fkb,
                        group=att_group, n_valid=t, causal=True, scores=sc, tau=tau, jb=cut)
            outs["p_ds_k"].append(knp.reshape(bp, t, ATT_KV_HEADS, HEAD_DIM))
            outs["p_ds_v"].append(zp[:, :, q_cols + kv_cols:q_cols + 2 * kv_cols].reshape(bp, t, ATT_KV_HEADS, HEAD_DIM))
            outs["p_ds_i"].append(kip[:, :, :IDX_DIM])
            kns = _headnorm(zs.reshape(bd * s_len, -1), k_cb, kv_cols, kgain).reshape(bd, s_len, kv_cols)
            kis = _headnorm(zs.reshape(bd * s_len, -1), kw_cb, LANE, igain).reshape(bd, s_len, LANE)
            vs_new = zs[:, :, q_cols + kv_cols:q_cols + 2 * kv_cols]
            n_keys = past + s_len
            n_pad = -(-n_keys // LANE) * LANE
            zpad = lambda w: jnp.zeros((bd, n_pad - n_keys, w), F32)
            k_all = jnp.concatenate([cache_dsa_k[j].reshape(bd, past, kv_cols), kns, zpad(kv_cols)], axis=1)
            v_all = jnp.concatenate([cache_dsa_v[j].reshape(bd, past, kv_cols), vs_new, zpad(kv_cols)], axis=1)
            ki_cache = jnp.concatenate([cache_dsa_idx_k[j], jnp.zeros((bd, past, LANE - IDX_DIM), F32)], axis=-1)
            ki_all = jnp.concatenate([ki_cache, kis, zpad(LANE)], axis=1)
            sc, tau, cut = _dsa_select(zs, qi_cb, kw_cb, ki_all, tq=s_len, kb=n_pad, k_sel=k_sel_s, n_valid=n_keys,
                                       causal=False)
            ms = _flash(zs, 0, q_cols, k_all, 0, v_all, 0, qgain, no_sink, mode="dsa", tq=s_len, kb=n_pad,
                        group=att_group, n_valid=n_keys, causal=False, scores=sc, tau=tau, jb=cut)
            outs["s_ds_k"].append(kns.reshape(bd, s_len, ATT_KV_HEADS, HEAD_DIM))
            outs["s_ds_v"].append(vs_new.reshape(bd, s_len, ATT_KV_HEADS, HEAD_DIM))
            outs["s_ds_i"].append(kis[:, :, :IDX_DIM])
            w_out = c_w_out[j]

        kv_mem = _mm_norm(mem_prompt.reshape(bp * mem_tokens, d), mem_norm[i], mem_w_kv[i].astype(BF16))
        km_p = _headnorm(kv_mem, 0, memq_cols, _tile_gain(mem_k_norm[i], memq_cols)).reshape(bp, mem_tokens, memq_cols)
        vm_p = kv_mem[:, memq_cols:].reshape(bp, mem_tokens, memq_cols)
        outs["p_mk"].append(km_p.reshape(bp, mem_tokens, MEM_HEADS, HEAD_DIM))
        outs["p_mv"].append(vm_p.reshape(bp, mem_tokens, MEM_HEADS, HEAD_DIM))
        mo_p = _mem_attend(zp, memq_cb, km_p, vm_p, mem_q_norm[i])
        mo_s = _mem_attend(zs, memq_cb, cache_mem_k[i].reshape(bd, mem_tokens, memq_cols),
                           cache_mem_v[i].reshape(bd, mem_tokens, memq_cols), mem_q_norm[i])
        w_mix, w_mem = w_out[:d].astype(BF16), w_out[d:].astype(BF16)
        xp = _mm_res([mp.reshape(bp * t, d), mo_p.reshape(bp * t, memq_cols)], [w_mix, w_mem], xp)
        xs = _mm_res([ms.reshape(bd * s_len, d), mo_s.reshape(bd * s_len, memq_cols)], [w_mix, w_mem], xs)

        w_up, w_down = ffn_w_up[i].astype(BF16), ffn_w_down[i].astype(BF16)
        xp, cp = _conv_ffn(xp, bp, t, ffn_norm[i], w_up, ffn_conv[i], w_down,
                           jnp.zeros((bp, CONV_W - 1, 2 * d_ff), F32))
        xs, cs = _conv_ffn(xs, bd, s_len, ffn_norm[i], w_up, ffn_conv[i], w_down, state_ffn_conv[i])
        outs["p_cv"].append(cp)
        outs["s_cv"].append(cs)

    st = jnp.stack
    order = ("p_rw_wkv", "p_rw_sh", "p_sw_k", "p_sw_v", "p_ds_k", "p_ds_v", "p_ds_i", "p_mk", "p_mv", "p_cv",
             "s_rw_wkv", "s_rw_sh", "s_sw_k", "s_sw_v", "s_ds_k", "s_ds_v", "s_ds_i", "s_cv")
    return (xp.reshape(bp, t, d), xs.reshape(bd, s_len, d)) + tuple(st(outs[k]) for k in order)
```

```python
import functools
import math

import jax
import jax.numpy as jnp
from jax import lax
from jax.experimental import pallas as pl
from jax.experimental.pallas import tpu as pltpu

F32 = jnp.float32
BF16 = jnp.bfloat16

HEAD_DIM = 64
CHUNK = 64
NORM_EPS = 1e-6
RW_GN_EPS = HEAD_DIM * 1e-5
ATT_KV_HEADS = 4
WINDOW = 128
IDX_HEADS = 8
IDX_DIM = 64
TOPK_MAX = 256
MEM_HEADS = 4
CONV_W = 3

LANE = 128
VMEM_LIMIT = 52 * 1024 * 1024
NEG_BIG = -(2.0 ** 100)
INT_MIN = -2147483648
MM_ROWS = 1024
DSA_TQ = 256
DSA_KB = 512
DSA_FLASH_TQ = 256
DSA_FLASH_KB = 512


def _cparams(sem, vmem=VMEM_LIMIT):
    return pltpu.CompilerParams(dimension_semantics=sem, vmem_limit_bytes=vmem)


def _split3(a):
    a1 = a.astype(BF16)
    r1 = a - a1.astype(F32)
    a2 = r1.astype(BF16)
    r2 = r1 - a2.astype(F32)
    return a1, a2, r2.astype(BF16)


def _dot_exact_rhs(a, e):
    a1, a2, a3 = _split3(a)
    d = lambda x: jnp.dot(x, e, preferred_element_type=F32)
    return (d(a3) + d(a2)) + d(a1)


def _dot_exact_lhs(e, a):
    a1, a2, a3 = _split3(a)
    d = lambda x: jnp.dot(e, x, preferred_element_type=F32)
    return (d(a3) + d(a2)) + d(a1)


def _head_block_matrix(width, value):
    r = lax.broadcasted_iota(jnp.int32, (width, width), 0) // HEAD_DIM
    c = lax.broadcasted_iota(jnp.int32, (width, width), 1) // HEAD_DIM
    return jnp.where(r == c, value, 0.0).astype(BF16)


def _dot_rw(a, b, dims):
    return lax.dot_general(a.astype(BF16), b.astype(BF16), (dims, ((), ())), preferred_element_type=F32)


def _pick_tile(n, cap):
    best = None
    for t in range(LANE, min(n, cap) + 1, LANE):
        if n % t == 0:
            best = t
    assert best is not None, n
    return best


def _mm_norm_body(x_ref, g_ref, w_ref, o_ref, xn_ref):
    @pl.when(pl.program_id(1) == 0)
    def _():
        x = x_ref[...]
        ms = jnp.mean(x * x, axis=-1, keepdims=True)
        xn_ref[...] = ((x * lax.rsqrt(ms + NORM_EPS)) * g_ref[...]).astype(BF16)

    o_ref[...] = jnp.dot(xn_ref[...], w_ref[...], preferred_element_type=F32)


def _mm_norm(x, gain, w):
    m, k = x.shape
    n = w.shape[1]
    tm = min(m, MM_ROWS)
    tn = _pick_tile(n, 1536)
    return pl.pallas_call(
        _mm_norm_body,
        out_shape=jax.ShapeDtypeStruct((m, n), F32),
        grid=(m // tm, n // tn),
        in_specs=[pl.BlockSpec((tm, k), lambda i, j: (i, 0)),
                  pl.BlockSpec((1, k), lambda i, j: (0, 0)),
                  pl.BlockSpec((k, tn), lambda i, j: (0, j))],
        out_specs=pl.BlockSpec((tm, tn), lambda i, j: (i, j)),
        scratch_shapes=[pltpu.VMEM((tm, k), BF16)],
        compiler_params=_cparams(("parallel", "arbitrary")),
        name="mm_norm",
    )(x, gain.reshape(1, k), w)


def _mm_res_body(*refs, n_lhs):
    lhs = refs[:n_lhs]
    ws = refs[n_lhs:2 * n_lhs]
    r_ref, o_ref = refs[2 * n_lhs], refs[2 * n_lhs + 1]
    acc = jnp.dot(lhs[0][...], ws[0][...], preferred_element_type=F32)
    for a, w in zip(lhs[1:], ws[1:]):
        acc = acc + jnp.dot(a[...], w[...], preferred_element_type=F32)
    o_ref[...] = r_ref[...] + acc


def _mm_res(lhs_list, w_list, res):
    m, n = res.shape
    ktot = sum(a.shape[1] for a in lhs_list)
    tm = min(m, MM_ROWS)
    tn = _pick_tile(n, 1024 if ktot <= 3072 else 512)
    n_lhs = len(lhs_list)
    in_specs = [pl.BlockSpec((tm, a.shape[1]), lambda i, j: (i, 0)) for a in lhs_list]
    in_specs += [pl.BlockSpec((w.shape[0], tn), lambda i, j: (0, j)) for w in w_list]
    in_specs += [pl.BlockSpec((tm, tn), lambda i, j: (i, j))]
    return pl.pallas_call(
        functools.partial(_mm_res_body, n_lhs=n_lhs),
        out_shape=jax.ShapeDtypeStruct((m, n), F32),
        grid=(m // tm, n // tn),
        in_specs=in_specs,
        out_specs=pl.BlockSpec((tm, tn), lambda i, j: (i, j)),
        compiler_params=_cparams(("parallel", "arbitrary")),
        name="mm_res",
    )(*lhs_list, *w_list, res)


def _headnorm_body(x_ref, g_ref, o_ref, *, width):
    avg = _head_block_matrix(LANE, 1.0 / HEAD_DIM)
    for c in range(width // LANE):
        x = x_ref[:, c * LANE:(c + 1) * LANE]
        ms = _dot_exact_rhs(x * x, avg)
        o_ref[:, c * LANE:(c + 1) * LANE] = (x * lax.rsqrt(ms + NORM_EPS)) * g_ref[:, c * LANE:(c + 1) * LANE]


def _headnorm(x, col_block, width, gain_row):
    m = x.shape[0]
    tm = min(m, 1024)
    return pl.pallas_call(
        functools.partial(_headnorm_body, width=width),
        out_shape=jax.ShapeDtypeStruct((m, width), F32),
        grid=(m // tm,),
        in_specs=[pl.BlockSpec((tm, width), lambda i: (i, col_block)),
                  pl.BlockSpec((1, width), lambda i: (0, 0))],
        out_specs=pl.BlockSpec((tm, width), lambda i: (i, 0)),
        compiler_params=_cparams(("parallel",)),
        name="headnorm",
    )(x, gain_row)


def _up_conv_body(x_ref, g_ref, wa_ref, wb_ref, pa_ref, pb_ref, cwa_ref, cwb_ref, o_ref, la_ref, lb_ref,
                  xn_ref, ca_ref, cb_ref, *, tm, tiles_per_batch, bpt):
    i = pl.program_id(0)
    j = pl.program_id(1)
    rpb = tm // bpt

    @pl.when(j == 0)
    def _():
        x = x_ref[...]
        ms = jnp.mean(x * x, axis=-1, keepdims=True)
        xn_ref[...] = ((x * lax.rsqrt(ms + NORM_EPS)) * g_ref[...]).astype(BF16)

    if bpt == 1:
        @pl.when(i % tiles_per_batch == 0)
        def _():
            ca_ref[j] = pa_ref[0]
            cb_ref[j] = pb_ref[0]

    xn = xn_ref[...]
    row = lax.broadcasted_iota(jnp.int32, o_ref.shape, 0)
    off = row if bpt == 1 else row % rpb
    if bpt > 1:
        pick = (lax.broadcasted_iota(jnp.int32, (tm, bpt), 0) // rpb
                == lax.broadcasted_iota(jnp.int32, (tm, bpt), 1)).astype(BF16)

    def conv(w_ref, p_ref, c_ref, cw_ref, last_ref):
        u = jnp.dot(xn, w_ref[...], preferred_element_type=F32)
        if bpt == 1:
            car = c_ref[j]
            c0, c1 = car[0:1, :], car[1:2, :]
            c_ref[j] = u[tm - 2:tm, :]
            last_ref[0] = u[tm - 2:tm, :]
        else:
            c0 = _dot_exact_lhs(pick, p_ref[:, 0, :])
            c1 = _dot_exact_lhs(pick, p_ref[:, 1, :])
            for bi in range(bpt):
                last_ref[bi] = u[(bi + 1) * rpb - 2:(bi + 1) * rpb, :]
        u1 = jnp.where(off == 0, c1, pltpu.roll(u, 1, axis=0))
        u2 = jnp.where(off == 0, c0, jnp.where(off == 1, c1, pltpu.roll(u, 2, axis=0)))
        return (u2 * cw_ref[0:1, :] + u1 * cw_ref[1:2, :]) + u * cw_ref[2:3, :]

    a = conv(wa_ref, pa_ref, ca_ref, cwa_ref, la_ref)
    b = conv(wb_ref, pb_ref, cb_ref, cwb_ref, lb_ref)
    o_ref[...] = ((a * jax.nn.sigmoid(a)) * b).astype(o_ref.dtype)


def _up_conv(x, gain, w_up, conv_w, prev, t):
    m, k = x.shape
    f = w_up.shape[1] // 2
    b = m // t
    tm = min(m, MM_ROWS)
    bpt = max(1, tm // t)
    assert tm % t == 0 or t % tm == 0
    tn = _pick_tile(f, 512)
    nj = f // tn
    tpb = max(1, t // tm)
    act, la, lb = pl.pallas_call(
        functools.partial(_up_conv_body, tm=tm, tiles_per_batch=tpb, bpt=bpt),
        out_shape=(jax.ShapeDtypeStruct((m, f), BF16),
                   jax.ShapeDtypeStruct((b, CONV_W - 1, f), F32),
                   jax.ShapeDtypeStruct((b, CONV_W - 1, f), F32)),
        grid=(m // tm, nj),
        in_specs=[pl.BlockSpec((tm, k), lambda i, j: (i, 0)),
                  pl.BlockSpec((1, k), lambda i, j: (0, 0)),
                  pl.BlockSpec((k, tn), lambda i, j: (0, j)),
                  pl.BlockSpec((k, tn), lambda i, j: (0, nj + j)),
                  pl.BlockSpec((bpt, CONV_W - 1, tn), lambda i, j: (i // tpb, 0, j)),
                  pl.BlockSpec((bpt, CONV_W - 1, tn), lambda i, j: (i // tpb, 0, nj + j)),
                  pl.BlockSpec((CONV_W, tn), lambda i, j: (0, j)),
                  pl.BlockSpec((CONV_W, tn), lambda i, j: (0, nj + j))],
        out_specs=(pl.BlockSpec((tm, tn), lambda i, j: (i, j)),
                   pl.BlockSpec((bpt, CONV_W - 1, tn), lambda i, j: (i // tpb, 0, j)),
                   pl.BlockSpec((bpt, CONV_W - 1, tn), lambda i, j: (i // tpb, 0, j))),
        scratch_shapes=[pltpu.VMEM((tm, k), BF16),
                        pltpu.VMEM((nj, CONV_W - 1, tn), F32),
                        pltpu.VMEM((nj, CONV_W - 1, tn), F32)],
        compiler_params=_cparams(("arbitrary", "arbitrary")),
        name="up_conv_gate",
    )(x, gain.reshape(1, k), w_up, w_up, prev, prev, conv_w, conv_w)
    return act, jnp.concatenate([la, lb], axis=-1)


def _causal_blocks(i, tq, kb):
    return ((i + 1) * tq + kb - 1) // kb


def _fold_steps(nq, tq, kb):
    if nq % 2:
        return None
    return max(_causal_blocks(r, tq, kb) + _causal_blocks(nq - 1 - r, tq, kb) for r in range(nq // 2))


def _fold_step(r, step, nq, tq, kb):
    n_lo = _causal_blocks(r, tq, kb)
    hi = nq - 1 - r
    in_lo = step < n_lo
    i = jnp.where(in_lo, r, hi)
    last_blk = _causal_blocks(i, tq, kb) - 1
    j = jnp.minimum(jnp.where(in_lo, step, step - n_lo), last_blk)
    active = step < n_lo + _causal_blocks(hi, tq, kb)
    return i, j, active & (jnp.where(in_lo, step, step - n_lo) == 0), active & (j == last_blk), active


def _flash_body(*refs, mode, tq, kb, wq, group, n_valid, nk, causal, fold_nq):
    q_ref, k_ref, v_ref, qg_ref, sink_ref = refs[:5]
    pos = 5
    if mode == "dsa":
        sc_ref, tau_ref, jb_ref = refs[pos:pos + 3]
        pos += 3
    o_ref = refs[pos]
    qn_scr, acc_scr, m_scr, l_scr, bias_scr = refs[pos + 1:]

    if fold_nq:
        i, j, is_first, is_last, fold_active = _fold_step(pl.program_id(1), pl.program_id(2), fold_nq, tq, kb)
    else:
        i, j = pl.program_id(1), pl.program_id(2)
        is_first, is_last = j == 0, j == nk - 1
    npairs = wq // LANE
    pairs_per_unit = group // 2 if group > 1 else 1
    lane = lax.broadcasted_iota(jnp.int32, (1, LANE), 1)
    lo_half = lane < HEAD_DIM

    @pl.when(is_first)
    def _():
        avg = _head_block_matrix(LANE, 1.0 / HEAD_DIM)
        for p in range(npairs):
            x = q_ref[0, :, p * LANE:(p + 1) * LANE]
            ms = _dot_exact_rhs(x * x, avg)
            qn = ((x * lax.rsqrt(ms + NORM_EPS)) * qg_ref[...]) * (HEAD_DIM ** -0.5)
            qn_scr[p] = qn.astype(BF16)
        acc_scr[...] = jnp.zeros(acc_scr.shape, F32)
        l_scr[...] = jnp.zeros(l_scr.shape, F32)
        m_scr[...] = jnp.full(m_scr.shape, NEG_BIG, F32)

    if mode == "band":
        kblk = i - WINDOW // kb + j
        active = kblk >= 0
    elif mode == "dsa" and causal:
        kblk = j
        active = fold_active if fold_nq else j * kb < (i + 1) * tq
    else:
        kblk = j
        active = None

    def step():
        kidx = kblk * kb + lax.broadcasted_iota(jnp.int32, (tq, kb), 1)
        qpos = i * tq + lax.broadcasted_iota(jnp.int32, (tq, kb), 0)
        if mode == "band":
            qchunk = qpos // CHUNK
            sel = (kidx >= (qchunk - WINDOW // CHUNK) * CHUNK) & (kidx < (qchunk + 1) * CHUNK) & (kidx >= 0)
        elif mode == "dsa":
            x = sc_ref[0]
            tau = tau_ref[0][:, 0:1]
            jb = jb_ref[0][:, 0:1]
            adm = kidx < ((qpos // CHUNK + 1) * CHUNK if causal else n_valid)
            sel = adm & ((x > tau) | ((x == tau) & (kidx <= jb)))
        else:
            sel = kidx < n_valid
        bias_scr[...] = jnp.where(sel, 0.0, NEG_BIG).astype(BF16)

        ppu = pairs_per_unit
        for u in range(npairs // ppu):
            ps = slice(u * ppu, (u + 1) * ppu)
            s = lax.dot_general(qn_scr[ps].reshape(ppu * tq, LANE), k_ref[0, 0, u], (((1,), (1,)), ((), ())),
                                preferred_element_type=F32).reshape(ppu, tq, 2 * kb)
            bias = bias_scr[...][None]
            m_old = m_scr[ps]
            s0 = s[:, :, 0:kb].astype(BF16) + bias
            s1 = s[:, :, kb:2 * kb].astype(BF16) + bias
            mn0 = jnp.maximum(m_old[:, :, 0:1], jnp.max(s0, axis=2, keepdims=True).astype(F32))
            mn1 = jnp.maximum(m_old[:, :, HEAD_DIM:HEAD_DIM + 1], jnp.max(s1, axis=2, keepdims=True).astype(F32))
            p0 = jnp.exp(s0 - mn0.astype(BF16))
            p1 = jnp.exp(s1 - mn1.astype(BF16))
            pcat = jnp.concatenate([p0, p1], axis=2).reshape(ppu * tq, 2 * kb)
            pv = jnp.dot(pcat, v_ref[0, 0, u], preferred_element_type=F32)
            pv = pv.reshape(ppu, tq, 2 * LANE)
            mn = jnp.where(lo_half, mn0, mn1)
            alpha = jnp.exp(m_old - mn)
            acc_scr[ps] = acc_scr[ps] * alpha + pv[:, :, 0:LANE]
            l_scr[ps] = l_scr[ps] * alpha + pv[:, :, LANE:2 * LANE]
            m_scr[ps] = mn

    if active is None:
        step()
    else:
        pl.when(active)(step)

    @pl.when(is_last)
    def _():
        for p in range(npairs):
            den = l_scr[p] + jnp.exp(sink_ref[:, p * LANE:(p + 1) * LANE] - m_scr[p])
            o_ref[0, :, p * LANE:(p + 1) * LANE] = (acc_scr[p] / den).astype(o_ref.dtype)


def _pair_operands_body(k_ref, v_ref, kab_ref, vab_ref, *, kb, group, n_units):
    lane = lax.broadcasted_iota(jnp.int32, (1, LANE), 1)
    lo_half = lane < HEAD_DIM
    ones_lo = jnp.broadcast_to(jnp.where(lo_half, 1.0, 0.0), (kb, LANE)).astype(BF16)
    ones_hi = jnp.broadcast_to(jnp.where(lo_half, 0.0, 1.0), (kb, LANE)).astype(BF16)
    for u in range(n_units):
        if group > 1:
            tile_idx, half = u // 2, u % 2
        else:
            tile_idx, half = u, None
        for src, dst in ((k_ref, kab_ref), (v_ref, vab_ref)):
            tile = src[0, :, tile_idx * LANE:(tile_idx + 1) * LANE]
            if half is None:
                a_part = jnp.where(lo_half, tile, 0.0)
                b_part = jnp.where(lo_half, 0.0, tile)
            elif half == 0:
                a_part = jnp.where(lo_half, tile, 0.0)
                b_part = pltpu.roll(a_part, HEAD_DIM, axis=1)
            else:
                b_part = jnp.where(lo_half, 0.0, tile)
                a_part = pltpu.roll(b_part, HEAD_DIM, axis=1)
            dst[0, 0, u, 0:kb, 0:LANE] = a_part.astype(BF16)
            dst[0, 0, u, kb:2 * kb, 0:LANE] = b_part.astype(BF16)
        vab_ref[0, 0, u, 0:kb, LANE:2 * LANE] = ones_lo
        vab_ref[0, 0, u, kb:2 * kb, LANE:2 * LANE] = ones_hi


def _pair_operands(k_arr, k_cb, v_arr, v_cb, kb, group, npairs):
    b, t_k = k_arr.shape[0], k_arr.shape[1]
    wk = ATT_KV_HEADS * HEAD_DIM
    n_units = npairs // (group // 2) if group > 1 else npairs
    nkb = t_k // kb
    return pl.pallas_call(
        functools.partial(_pair_operands_body, kb=kb, group=group, n_units=n_units),
        out_shape=(jax.ShapeDtypeStruct((b, nkb, n_units, 2 * kb, LANE), BF16),
                   jax.ShapeDtypeStruct((b, nkb, n_units, 2 * kb, 2 * LANE), BF16)),
        grid=(b, nkb),
        in_specs=[pl.BlockSpec((1, kb, wk), lambda bi, j: (bi, j, k_cb)),
                  pl.BlockSpec((1, kb, wk), lambda bi, j: (bi, j, v_cb))],
        out_specs=(pl.BlockSpec((1, 1, n_units, 2 * kb, LANE), lambda bi, j: (bi, j, 0, 0, 0)),
                   pl.BlockSpec((1, 1, n_units, 2 * kb, 2 * LANE), lambda bi, j: (bi, j, 0, 0, 0))),
        compiler_params=_cparams(("parallel", "parallel")),
        name="pair_operands",
    )(k_arr, v_arr)


def _flash(q_arr, q_cb, wq, k_arr, k_cb, v_arr, v_cb, q_gain, sink_row, *, mode, tq, kb, group,
           n_valid=None, causal=False, scores=None, tau=None, jb=None):
    b, t_q = q_arr.shape[0], q_arr.shape[1]
    t_k = k_arr.shape[1]
    nq = t_q // tq
    if mode == "band":
        assert tq == kb == WINDOW
        nk = 2
        kmap = lambda i, j: jnp.maximum(i - 1 + j, 0)
    elif mode == "dsa" and causal:
        nk = t_k // kb
        kmap = lambda i, j: jnp.minimum(j, ((i + 1) * tq - 1) // kb)
    else:
        nk = t_k // kb
        kmap = lambda i, j: j
    grid = (b, nq, nk)
    qmap = lambda i, j: i
    fold = _fold_steps(nq, tq, kb) if (mode == "dsa" and causal) else None
    if fold:
        grid = (b, nq // 2, fold)
        qmap = lambda r, s: _fold_step(r, s, nq, tq, kb)[0]
        kmap = lambda r, s: _fold_step(r, s, nq, tq, kb)[1]
    npairs = wq // LANE
    kab, vab = _pair_operands(k_arr, k_cb, v_arr, v_cb, kb, group, npairs)
    n_units = kab.shape[2]
    in_specs = [pl.BlockSpec((1, tq, wq), lambda bi, i, j: (bi, qmap(i, j), q_cb)),
                pl.BlockSpec((1, 1, n_units, 2 * kb, LANE), lambda bi, i, j: (bi, kmap(i, j), 0, 0, 0)),
                pl.BlockSpec((1, 1, n_units, 2 * kb, 2 * LANE), lambda bi, i, j: (bi, kmap(i, j), 0, 0, 0)),
                pl.BlockSpec((1, LANE), lambda bi, i, j: (0, 0)),
                pl.BlockSpec((1, wq), lambda bi, i, j: (0, 0))]
    args = [q_arr, kab, vab, q_gain, sink_row]
    if mode == "dsa":
        in_specs += [pl.BlockSpec((1, tq, kb), lambda bi, i, j: (bi, qmap(i, j), kmap(i, j))),
                     pl.BlockSpec((1, tq, LANE), lambda bi, i, j: (bi, qmap(i, j), 0)),
                     pl.BlockSpec((1, tq, LANE), lambda bi, i, j: (bi, qmap(i, j), 0))]
        args += [scores, tau, jb]
    body = functools.partial(_flash_body, mode=mode, tq=tq, kb=kb, wq=wq, group=group,
                             n_valid=n_valid, nk=nk, causal=causal, fold_nq=nq if fold else 0)
    return pl.pallas_call(
        body,
        out_shape=jax.ShapeDtypeStruct((b, t_q, wq), BF16),
        grid=grid,
        in_specs=in_specs,
        out_specs=pl.BlockSpec((1, tq, wq), lambda bi, i, j: (bi, qmap(i, j), 0)),
        scratch_shapes=[pltpu.VMEM((npairs, tq, LANE), BF16),
                        pltpu.VMEM((npairs, tq, LANE), F32),
                        pltpu.VMEM((npairs, tq, LANE), F32),
                        pltpu.VMEM((npairs, tq, LANE), F32),
                        pltpu.VMEM((tq, kb), BF16)],
        compiler_params=_cparams(("parallel", "parallel", "arbitrary")),
        name="flash_" + mode,
    )(*args)


def _dsa_select_body(qi_ref, kw_ref, ki_ref, sc_ref, tau_ref, jb_ref, sc_scr, cnt_scr, ebuf_scr, *, tq, kb, nk, k_sel,
                     n_valid, causal, fold_nq):
    if fold_nq:
        i, j, _, is_last, fold_active = _fold_step(pl.program_id(1), pl.program_id(2), fold_nq, tq, kb)
    else:
        i, j = pl.program_id(1), pl.program_id(2)
        is_last = j == nk - 1
    qpos = i * tq + lax.broadcasted_iota(jnp.int32, (tq, 1), 0)
    if causal:
        n_adm = (qpos // CHUNK + 1) * CHUNK
        active = fold_active if fold_nq else j * kb < (i + 1) * tq
        n_blocks = _causal_blocks(i, tq, kb)
    else:
        n_adm = jnp.full((tq, 1), n_valid, jnp.int32)
        active = None
        n_blocks = nk

    def compute():
        ka = ki_ref[0]
        kab = jnp.concatenate([ka, pltpu.roll(ka, HEAD_DIM, axis=1)], axis=0).astype(BF16)
        kw = kw_ref[0]
        acc = jnp.zeros((tq, kb), F32)
        q4 = jnp.concatenate([qi_ref[0, :, p * LANE:(p + 1) * LANE].astype(BF16) for p in range(IDX_HEADS // 2)],
                             axis=0)
        s = lax.dot_general(q4, kab, (((1,), (1,)), ((), ())), preferred_element_type=F32)
        for p in range(IDX_HEADS // 2):
            for h in range(2):
                col = HEAD_DIM + 2 * p + h
                w = (kw[:, col:col + 1] * IDX_HEADS ** -0.5) * IDX_DIM ** -0.5
                acc = acc + jnp.maximum(s[p * tq:(p + 1) * tq, h * kb:(h + 1) * kb], 0.0) * w
        acc = jnp.where(acc == 0.0, 0.0, acc)
        kidx = j * kb + lax.broadcasted_iota(jnp.int32, (tq, kb), 1)
        x = jnp.where(kidx < n_adm, acc, -jnp.inf)
        sc_scr[j] = x
        sc_ref[0] = x

    if active is None:
        compute()
    else:
        pl.when(active)(compute)

        if not fold_nq:
            @pl.when(jnp.logical_not(active))
            def _():
                sc_ref[0] = jnp.full((tq, kb), -jnp.inf, F32)

    @pl.when(is_last)
    def _():
        kf = float(k_sel)

        rsz = min(tq, LANE)
        lane_tiles = range(kb // LANE)

        def count(make_pred):
            starts = range(0, tq, rsz)
            preds = [make_pred(lambda col, r0=r0: jnp.broadcast_to(col[r0:r0 + rsz], (rsz, LANE))) for r0 in starts]
            accs = []
            for r0, pred in zip(starts, preds):
                def blk(jj, acc, r0=r0, pred=pred):
                    for a in lane_tiles:
                        x = sc_scr[jj, r0:r0 + rsz, a * LANE:(a + 1) * LANE]
                        acc = acc + jnp.where(pred(x), 1.0, 0.0)
                    return acc
                accs.append(lax.fori_loop(0, n_blocks, blk, jnp.zeros((rsz, LANE), F32)))
            return jnp.sum(jnp.concatenate(accs, axis=0), axis=1, keepdims=True)

        def key_to_float(tu):
            cs = tu ^ INT_MIN
            fb = jnp.where(cs >= 0, cs, cs ^ 0x7FFFFFFF)
            return lax.bitcast_convert_type(fb, F32)

        def bit_body(it, carry):
            tu, c_at = carry
            cand = tu | jnp.left_shift(jnp.int32(1), 31 - it)
            thr = key_to_float(cand)
            c = count(lambda widen: (lambda x, t=widen(thr): x >= t))
            keep = c >= kf
            return jnp.where(keep, cand, tu), jnp.where(keep, c, c_at)

        tu, c_ge = lax.fori_loop(0, 32, bit_body, (jnp.zeros((tq, 1), jnp.int32),
                                                   jnp.full((tq, 1), float(nk * kb), F32)))
        full = n_adm >= k_sel
        tau = jnp.where(full, key_to_float(tu), -jnp.inf)
        tau = jnp.where(jnp.abs(tau) < 1.1754944e-38, 0.0, tau)
        c_gt = count(lambda widen: (lambda x, t=widen(tau): x > t))
        need = kf - c_gt
        tau_ref[0] = jnp.broadcast_to(tau, (tq, LANE))
        jb_ref[0] = jnp.full((tq, LANE), 2 ** 30, jnp.int32)
        surplus = jnp.max(jnp.where(full, (c_ge - c_gt) - need, 0.0))

        @pl.when(surplus > 0.5)
        def _():
            tau_b = jnp.broadcast_to(tau, (tq, LANE))
            need_b = jnp.broadcast_to(need, (tq, LANE))
            ones_mat = jnp.ones((LANE, LANE), BF16)
            row_sum = lambda part: jnp.dot(part.astype(BF16), ones_mat, preferred_element_type=F32)

            def ties(jj, a):
                return jnp.where(sc_scr[jj, :, a * LANE:(a + 1) * LANE] == tau_b, 1.0, 0.0)

            def blk_count(jj, carry):
                part = ties(jj, 0)
                for a in lane_tiles[1:]:
                    part = part + ties(jj, a)
                cnt_scr[jj] = row_sum(part)
                return carry
            lax.fori_loop(0, n_blocks, blk_count, 0)

            def locate(jj, carry):
                run, blk_idx, before = carry
                run = run + cnt_scr[jj]
                below = run < need_b
                return run, blk_idx + jnp.where(below, 1, 0), jnp.where(below, run, before)
            zero = jnp.zeros((tq, LANE), F32)
            _, blk_idx, before = lax.fori_loop(0, n_blocks, locate, (zero, jnp.zeros((tq, LANE), jnp.int32), zero))
            need_in = need_b - before

            ebuf_scr[...] = jnp.zeros((tq, kb), F32)

            def extract(jj, carry):
                here = blk_idx == jj
                for a in lane_tiles:
                    cols = slice(a * LANE, (a + 1) * LANE)
                    ebuf_scr[:, cols] = jnp.where(here, ties(jj, a), ebuf_scr[:, cols])
                return carry
            lax.fori_loop(0, n_blocks, extract, 0)

            lane_idx = lax.broadcasted_iota(jnp.int32, (tq, LANE), 1)
            cut_bits = max(1, (kb - 1).bit_length())

            def cut_body(it, cut):
                cand = cut | jnp.left_shift(jnp.int32(1), cut_bits - 1 - it)
                part = zero
                for a in lane_tiles:
                    part = part + jnp.where(a * LANE + lane_idx < cand, ebuf_scr[:, a * LANE:(a + 1) * LANE], 0.0)
                return jnp.where(row_sum(part) < need_in, cand, cut)
            cut = lax.fori_loop(0, cut_bits, cut_body, jnp.zeros((tq, LANE), jnp.int32))
            jb_ref[0] = blk_idx * kb + cut


def _dsa_select(z, qi_cb, kw_cb, ki_n, *, tq, kb, k_sel, n_valid, causal):
    b, t_q = z.shape[0], z.shape[1]
    t_k = ki_n.shape[1]
    nq, nk = t_q // tq, t_k // kb
    assert kb // LANE <= 256, "per-lane tie counts of one key block go through a bf16 matmul, exact up to 256"
    kmap = (lambda i, j: jnp.minimum(j, ((i + 1) * tq - 1) // kb)) if causal else (lambda i, j: j)
    omap = lambda i, j: j
    qmap = lambda i, j: i
    grid = (b, nq, nk)
    fold = _fold_steps(nq, tq, kb) if causal else None
    if fold:
        grid = (b, nq // 2, fold)
        qmap = lambda r, s: _fold_step(r, s, nq, tq, kb)[0]
        kmap = omap = lambda r, s: _fold_step(r, s, nq, tq, kb)[1]
    body = functools.partial(_dsa_select_body, tq=tq, kb=kb, nk=nk, k_sel=k_sel, n_valid=n_valid,
                             causal=causal, fold_nq=nq if fold else 0)
    return pl.pallas_call(
        body,
        out_shape=(jax.ShapeDtypeStruct((b, t_q, t_k), F32),
                   jax.ShapeDtypeStruct((b, t_q, LANE), F32),
                   jax.ShapeDtypeStruct((b, t_q, LANE), jnp.int32)),
        grid=grid,
        in_specs=[pl.BlockSpec((1, tq, IDX_HEADS * IDX_DIM), lambda bi, i, j: (bi, qmap(i, j), qi_cb)),
                  pl.BlockSpec((1, tq, LANE), lambda bi, i, j: (bi, qmap(i, j), kw_cb)),
                  pl.BlockSpec((1, kb, LANE), lambda bi, i, j: (bi, kmap(i, j), 0))],
        out_specs=(pl.BlockSpec((1, tq, kb), lambda bi, i, j: (bi, qmap(i, j), omap(i, j))),
                   pl.BlockSpec((1, tq, LANE), lambda bi, i, j: (bi, qmap(i, j), 0)),
                   pl.BlockSpec((1, tq, LANE), lambda bi, i, j: (bi, qmap(i, j), 0))),
        scratch_shapes=[pltpu.VMEM((nk, tq, kb), F32), pltpu.VMEM((nk, tq, LANE), F32), pltpu.VMEM((tq, kb), F32)],
        compiler_params=_cparams(("parallel", "parallel", "arbitrary")),
        name="dsa_select",
    )(z, z, ki_n)


def _rwkv_body(zr_ref, zk_ref, zv_ref, zw_ref, za_ref, zg_ref,
               sr_ref, sk_ref, sv_ref, sw_ref, sa_ref, sg_ref,
               mr_ref, mk_ref, mv_ref, mw_ref, ma_ref, mg_ref,
               vec_ref, w2_ref, a2_ref, g2_ref, s0_ref,
               mix_ref, sfin_ref,
               s_scr, prev_scr, prevl_scr, prevg_scr, y_scr, *, tc, lc, nt, pp):
    t = pl.program_id(2)

    @pl.when(t == 0)
    def _():
        s_scr[...] = s0_ref[0]
        prev_scr[0:1, :] = sr_ref[0]
        prev_scr[1:2, :] = sk_ref[0]
        prev_scr[2:3, :] = sv_ref[0]
        prevl_scr[0:1, :] = sw_ref[0]
        prevl_scr[1:2, :] = sa_ref[0]
        prevg_scr[0:1, :] = sg_ref[0]

    row = lax.broadcasted_iota(jnp.int32, (tc, 1), 0)

    def shifted(z_ref, prow, mu_ref):
        z = z_ref[0]
        zp = jnp.where(row == 0, prow, pltpu.roll(z, 1, axis=0))
        return z + (zp - z) * mu_ref[...], z[tc - 1:tc, :]

    r, last_r = shifted(zr_ref, prev_scr[0:1, :], mr_ref)
    k, last_k = shifted(zk_ref, prev_scr[1:2, :], mk_ref)
    v, last_v = shifted(zv_ref, prev_scr[2:3, :], mv_ref)
    zw, last_w = shifted(zw_ref, prevl_scr[0:1, :], mw_ref)
    za, last_a = shifted(za_ref, prevl_scr[1:2, :], ma_ref)
    zg, last_g = shifted(zg_ref, prevg_scr[0:1, :], mg_ref)
    prev_scr[0:1, :] = last_r
    prev_scr[1:2, :] = last_k
    prev_scr[2:3, :] = last_v
    prevl_scr[0:1, :] = last_w
    prevl_scr[1:2, :] = last_a
    prevg_scr[0:1, :] = last_g

    w0, a0 = vec_ref[0:1, :], vec_ref[1:2, :]
    k_k, k_a, r_k = vec_ref[2:3, :], vec_ref[3:4, :], vec_ref[4:5, :]
    ln_w, ln_b = vec_ref[5:6, :], vec_ref[6:7, :]

    ones_blk = _head_block_matrix(LANE, 1.0)
    avg_blk = _head_block_matrix(LANE, 1.0 / HEAD_DIM)

    def per_head(x, blk):
        return jnp.concatenate([_dot_exact_rhs(x[:, i * LANE:(i + 1) * LANE], blk) for i in range(pp)], axis=1)

    xw = w0 + jnp.dot(jnp.tanh(zw).astype(BF16), w2_ref[...], preferred_element_type=F32)
    lw = -math.exp(-0.5) * jax.nn.sigmoid(xw)
    a = jax.nn.sigmoid(a0 + jnp.dot(za.astype(BF16), a2_ref[...], preferred_element_type=F32))
    g = jnp.dot(jax.nn.sigmoid(zg).astype(BF16), g2_ref[...], preferred_element_type=F32)
    kk = k * k_k
    kk = kk * jnp.minimum(lax.rsqrt(per_head(kk * kk, ones_blk)), 1e12)
    k2 = k * (1.0 + (a - 1.0) * k_a)

    a_step = -kk
    b_step = kk * a

    lane = lax.broadcasted_iota(jnp.int32, (1, LANE), 1)
    m0 = lane < HEAD_DIM
    rr = lax.broadcasted_iota(jnp.int32, (lc, 2 * lc), 0)
    cc = lax.broadcasted_iota(jnp.int32, (lc, 2 * lc), 1)
    incl = jnp.where(cc < lc, cc, cc - lc) <= rr
    strict = (lax.broadcasted_iota(jnp.int32, (2 * lc, 2 * lc), 1)
              < lax.broadcasted_iota(jnp.int32, (2 * lc, 2 * lc), 0))
    tri = (lax.broadcasted_iota(jnp.int32, (lc, lc), 1)
           <= lax.broadcasted_iota(jnp.int32, (lc, lc), 0)).astype(BF16)
    nsteps = lc.bit_length() - 1
    nt_dims = ((1,), (1,))
    nn_dims = ((1,), (0,))
    tn_dims = ((0,), (0,))

    chunks = range(tc // lc)
    st = []
    for c in chunks:
        rows = slice(c * lc, (c + 1) * lc)
        lwc = lw[rows]
        cs = _dot_exact_lhs(tri, lwc)
        p_in = jnp.exp(-cs)
        at = a_step[rows] * jnp.exp(cs - lwc)
        bt = b_step[rows] * p_in
        kt = k2[rows] * p_in
        rt = r[rows] * jnp.exp(cs)
        vc = v[rows]
        p_last = jnp.exp(cs[lc - 1:lc, :])
        for pi in range(pp):
            cols = slice(pi * LANE, (pi + 1) * LANE)
            stack = lambda x: jnp.concatenate([jnp.where(m0, x[:, cols], 0.0),
                                               jnp.where(m0, 0.0, x[:, cols])], axis=0).astype(BF16)
            st.append(dict(c=c, pi=pi,
                           ar=jnp.concatenate([stack(at), rt[:, cols].astype(BF16)], axis=0),
                           bk=jnp.concatenate([stack(bt), stack(kt)], axis=0),
                           v_s=stack(vc), p_last=p_last[:, cols]))
    for d in st:
        gram = _dot_rw(d["ar"], d["bk"], nt_dims)
        d["t_p"] = jnp.where(strict, gram[0:2 * lc, 0:2 * lc], 0.0)
        d["w_ak"] = jnp.where(strict, gram[0:2 * lc, 2 * lc:4 * lc], 0.0).astype(BF16)
        d["w_rb"] = jnp.where(incl, gram[2 * lc:3 * lc, 0:2 * lc], 0.0).astype(BF16)
        d["w_rk"] = jnp.where(incl, gram[2 * lc:3 * lc, 2 * lc:4 * lc], 0.0).astype(BF16)
    gw = 4 * lc
    lane_head = lax.broadcasted_iota(jnp.int32, (1, gw), 1) // lc
    eye_sbs = (lax.broadcasted_iota(jnp.int32, (lc, gw), 0)
               == lax.broadcasted_iota(jnp.int32, (lc, gw), 1) % lc).astype(F32)
    lane_pair = lax.broadcasted_iota(jnp.int32, (1, 2 * lc), 1)

    def blockdiag(y):
        yb = y.astype(BF16)
        return jnp.concatenate([jnp.where(lane_head == h, yb, jnp.zeros_like(yb)) for h in range(4)], axis=0)

    groups = []
    for gi in range(len(st) // 2):
        members = (st[2 * gi], st[2 * gi + 1])
        t_sbs = jnp.concatenate([m["t_p"][0:lc] + m["t_p"][lc:2 * lc] for m in members], axis=1)
        groups.append(dict(members=members, t=t_sbs, minv=eye_sbs + t_sbs))
    for grp in groups:
        grp["t"] = _dot_rw(grp["t"], blockdiag(grp["t"]), nn_dims)
    for step in range(nsteps - 1):
        for grp in groups:
            if step == nsteps - 2:
                grp["minv"] = grp["minv"] + _dot_rw(grp["minv"], blockdiag(grp["t"]), nn_dims)
            else:
                both = _dot_rw(jnp.concatenate([grp["minv"], grp["t"]], axis=0), blockdiag(grp["t"]), nn_dims)
                grp["minv"] = grp["minv"] + both[0:lc]
                grp["t"] = both[lc:2 * lc]
    for grp in groups:
        for idx, m in enumerate(grp["members"]):
            tile = grp["minv"][:, idx * 2 * lc:(idx + 1) * 2 * lc]
            m["minv"] = jnp.concatenate([jnp.where(lane_pair < lc, tile, 0.0),
                                         jnp.where(lane_pair < lc, 0.0, tile)], axis=0).astype(BF16)
    for d in st:
        both = _dot_rw(jnp.concatenate([d["w_ak"], d["w_rk"]], axis=0), d["v_s"], nn_dims)
        d["wv"] = both[0:2 * lc]
        d["y_c"] = both[2 * lc:3 * lc]
    for d in st:
        both = _dot_rw(d["minv"], jnp.concatenate([d["ar"][0:2 * lc], d["wv"].astype(BF16)], axis=1), nn_dims)
        d["ma"] = both[:, 0:LANE]
        d["mwv"] = both[:, LANE:2 * LANE]
    for d in st:
        d["g"] = _dot_rw(d["ma"], d["bk"][0:2 * lc], tn_dims).astype(BF16)
        d["d"] = _dot_rw(jnp.concatenate([d["mwv"].astype(BF16), d["v_s"]], axis=0), d["bk"], tn_dims)
    s_cur = [s_scr[pi] for pi in range(pp)]
    for d in st:
        s_in = s_cur[d["pi"]]
        d["s0"] = s_in.astype(BF16)
        s_cur[d["pi"]] = ((s_in + _dot_rw(d["s0"], d["g"], nn_dims)) + d["d"]) * d["p_last"]
    for pi in range(pp):
        s_scr[pi] = s_cur[pi]
    for d in st:
        d["xr"] = _dot_rw(d["ar"], d["s0"], nt_dims)
    for d in st:
        d["u"] = (_dot_rw(d["minv"], d["xr"][0:2 * lc], nn_dims) + d["mwv"]).astype(BF16)
    for d in st:
        c, pi = d["c"], d["pi"]
        y_scr[c * lc:(c + 1) * lc, pi * LANE:(pi + 1) * LANE] = (
            (d["xr"][2 * lc:3 * lc] + _dot_rw(d["w_rb"], d["u"], nn_dims)) + d["y_c"])

    y = y_scr[...]
    mean = per_head(y, avg_blk)
    dev = y - mean
    var = per_head(dev * dev, avg_blk)
    yn = (dev * lax.rsqrt(var + RW_GN_EPS)) * ln_w + ln_b
    bonus = per_head((r * k2) * r_k, ones_blk) * v
    mix_ref[0] = ((yn + bonus) * g).astype(mix_ref.dtype)

    @pl.when(t == nt - 1)
    def _():
        sfin_ref[0] = s_scr[...]


def _rwkv(z, shift_prev, s0_pairs, mu, vecs, w2, a2, g2, d_model):
    b, t = z.shape[0], z.shape[1]
    npairs = d_model // LANE
    tc = min(t, 512)
    lc = min(CHUNK, t)
    nt = t // tc
    pp = 2 if tc // lc >= 4 else 8
    wp = pp * LANE
    ngroups = npairs // pp
    cb_w, cb_a, cb_g = 3 * npairs, 3 * npairs + 1, (3 * npairs + 2) // 2

    def zspec(width, cbf):
        return pl.BlockSpec((1, tc, width), lambda bi, p, ti: (bi, ti, cbf(p)))

    def sspec(width, cbf):
        return pl.BlockSpec((1, 1, width), lambda bi, p, ti: (bi, 0, cbf(p)))

    def mspec(width, cbf):
        return pl.BlockSpec((1, width), lambda bi, p, ti: (0, cbf(p)))

    cbfs = [(wp, lambda p: p), (wp, lambda p: ngroups + p), (wp, lambda p: 2 * ngroups + p),
            (LANE, lambda p: cb_w), (LANE, lambda p: cb_a), (2 * LANE, lambda p: cb_g)]
    in_specs = ([zspec(w, f) for w, f in cbfs] + [sspec(w, f) for w, f in cbfs] + [mspec(w, f) for w, f in cbfs]
                + [pl.BlockSpec((8, wp), lambda bi, p, ti: (0, p)),
                   pl.BlockSpec((LANE, wp), lambda bi, p, ti: (0, p)),
                   pl.BlockSpec((LANE, wp), lambda bi, p, ti: (0, p)),
                   pl.BlockSpec((2 * LANE, wp), lambda bi, p, ti: (0, p)),
                   pl.BlockSpec((1, pp, LANE, LANE), lambda bi, p, ti: (bi, p, 0, 0))])
    return pl.pallas_call(
        functools.partial(_rwkv_body, tc=tc, lc=lc, nt=nt, pp=pp),
        out_shape=(jax.ShapeDtypeStruct((b, t, d_model), BF16),
                   jax.ShapeDtypeStruct((b, npairs, LANE, LANE), F32)),
        grid=(b, ngroups, nt),
        in_specs=in_specs,
        out_specs=(pl.BlockSpec((1, tc, wp), lambda bi, p, ti: (bi, ti, p)),
                   pl.BlockSpec((1, pp, LANE, LANE), lambda bi, p, ti: (bi, p, 0, 0))),
        scratch_shapes=[pltpu.VMEM((pp, LANE, LANE), F32),
                        pltpu.VMEM((8, wp), F32),
                        pltpu.VMEM((8, LANE), F32),
                        pltpu.VMEM((8, 2 * LANE), F32),
                        pltpu.VMEM((tc, wp), F32)],
        compiler_params=_cparams(("parallel", "parallel", "arbitrary")),
        name="rwkv7",
    )(*([z] * 6), *([shift_prev] * 6), *([mu] * 6), vecs, w2, a2, g2, s0_pairs)


def _pad_cols(x, segments):
    parts = []
    for start, width, padded in segments:
        seg = x[..., start:start + width]
        if padded > width:
            seg = jnp.concatenate([seg, jnp.zeros(seg.shape[:-1] + (padded - width,), seg.dtype)], axis=-1)
        parts.append(seg)
    return jnp.concatenate(parts, axis=-1)


def _pad_rows(x, padded):
    return jnp.concatenate([x, jnp.zeros((padded - x.shape[0],) + x.shape[1:], x.dtype)], axis=0)


def _pairs_from_heads(s):
    b, h = s.shape[0], s.shape[1]
    s = s.reshape(b, h // 2, 2, HEAD_DIM, HEAD_DIM)
    z = jnp.zeros_like(s[:, :, 0])
    top = jnp.concatenate([s[:, :, 0], z], axis=-1)
    bot = jnp.concatenate([z, s[:, :, 1]], axis=-1)
    return jnp.concatenate([top, bot], axis=-2)


def _heads_from_pairs(sp):
    b, npairs = sp.shape[0], sp.shape[1]
    s = jnp.stack([sp[:, :, :HEAD_DIM, :HEAD_DIM], sp[:, :, HEAD_DIM:, HEAD_DIM:]], axis=2)
    return s.reshape(b, 2 * npairs, HEAD_DIM, HEAD_DIM)


def _tile_gain(g, width):
    return jnp.tile(g.astype(F32), width // HEAD_DIM).reshape(1, width)


def _mem_attend(z, memq_cb, km, vm, q_gain):
    t_q = z.shape[1]
    wq = MEM_HEADS * HEAD_DIM
    tq = min(t_q, 512)
    no_sink = jnp.full((1, wq), -jnp.inf, F32)
    return _flash(z, memq_cb, wq, km, 0, vm, 0, _tile_gain(q_gain, LANE), no_sink,
                  mode="all", tq=tq, kb=km.shape[1], group=1, n_valid=km.shape[1])


def _conv_ffn(x2d, b, t, gain, w_up, conv_w, w_down, prev):
    act, u_last = _up_conv(x2d, gain, w_up, conv_w, prev, t)
    return _mm_res([act], [w_down], x2d), u_last


def kernel(x_prompt, x_sample, state_rwkv_wkv, state_rwkv_shift, cache_swa_k, cache_swa_v, cache_dsa_k, cache_dsa_v, cache_dsa_idx_k, cache_mem_k, cache_mem_v, state_ffn_conv, mem_prompt, attn_norm, ffn_norm, mem_norm, mem_w_kv, mem_q_norm, mem_k_norm, a_w_in, a_mu, a_w0, a_w2, a_a0, a_a2, a_g2, a_k_k, a_k_a, a_r_k, a_ln_w, a_ln_b, a_w_out, b_w_in, b_q_norm, b_k_norm, b_sink, b_w_out, c_w_in, c_q_norm, c_k_norm, c_idx_k_norm, c_w_out, ffn_w_up, ffn_conv, ffn_w_down):
    bp, t, d = x_prompt.shape
    bd, s_len = x_sample.shape[:2]
    depth = attn_norm.shape[0]
    win_rows = cache_swa_k.shape[2]
    past = cache_dsa_k.shape[2] if cache_dsa_k.shape[0] else 0
    d_ff = ffn_w_down.shape[1]
    mem_tokens = mem_prompt.shape[1]
    q_cols = d
    kv_cols = ATT_KV_HEADS * HEAD_DIM
    memq_cols = MEM_HEADS * HEAD_DIM
    att_group = (d // HEAD_DIM) // ATT_KV_HEADS
    dec_lora = a_w2.shape[1]
    a_lora = a_a2.shape[1]
    g_lora = a_g2.shape[1]
    rw_cols = 3 * d + dec_lora + a_lora + g_lora
    k_sel_p = min(TOPK_MAX, t // 4)
    k_sel_s = min(TOPK_MAX, (past + s_len) // 4)
    assert g_lora == 2 * LANE and dec_lora <= LANE and a_lora <= LANE

    xp = x_prompt.reshape(bp * t, d)
    xs = x_sample.reshape(bd * s_len, d)

    rw_segments = [(0, 3 * d, 3 * d), (3 * d, dec_lora, LANE), (3 * d + dec_lora, a_lora, LANE),
                   (3 * d + dec_lora + a_lora, g_lora, g_lora)]
    rw_padded = 3 * d + 2 * LANE + g_lora
    o_qi = q_cols + 2 * kv_cols
    o_ki = o_qi + IDX_HEADS * IDX_DIM
    c_cols = o_ki + IDX_DIM + IDX_HEADS

    outs = {k: [] for k in ("p_rw_wkv", "p_rw_sh", "p_sw_k", "p_sw_v", "p_ds_k", "p_ds_v", "p_ds_i", "p_mk",
                            "p_mv", "p_cv", "s_rw_wkv", "s_rw_sh", "s_sw_k", "s_sw_v", "s_ds_k", "s_ds_v",
                            "s_ds_i", "s_cv")}

    def unpad_rw(row):
        return jnp.concatenate([row[..., :3 * d], row[..., 3 * d:3 * d + dec_lora],
                                row[..., 3 * d + LANE:3 * d + LANE + a_lora],
                                row[..., 3 * d + 2 * LANE:3 * d + 2 * LANE + g_lora]], axis=-1)

    for i in range(depth):
        kind, j = i % 3, i // 3
        if kind == 0:
            w_in = jnp.concatenate([_pad_cols(a_w_in[j], rw_segments), a_w_in[j][:, rw_cols:]], axis=1).astype(BF16)
            memq_cb = rw_padded // memq_cols
            zp = _mm_norm(xp, attn_norm[i], w_in).reshape(bp, t, -1)
            zs = _mm_norm(xs, attn_norm[i], w_in).reshape(bd, s_len, -1)
            mu = _pad_cols(a_mu[j].reshape(1, -1), rw_segments)
            mu = jnp.concatenate([mu, jnp.zeros((1, memq_cols), F32)], axis=1)
            vecs = jnp.stack([a_w0[j], a_a0[j], a_k_k[j], a_k_a[j], a_r_k[j].reshape(-1), a_ln_w[j], a_ln_b[j],
                              jnp.zeros((d,), F32)], axis=0)
            w2 = _pad_rows(a_w2[j], LANE).astype(BF16)
            a2 = _pad_rows(a_a2[j], LANE).astype(BF16)
            g2 = a_g2[j].astype(BF16)
            sh_p = jnp.zeros((bp, 1, zp.shape[-1]), F32)
            st_p = jnp.zeros((bp, d // LANE, LANE, LANE), F32)
            sh_s = _pad_cols(state_rwkv_shift[j], rw_segments)
            sh_s = jnp.concatenate([sh_s, jnp.zeros((bd, memq_cols), F32)], axis=1).reshape(bd, 1, -1)
            st_s = _pairs_from_heads(state_rwkv_wkv[j])
            mp, stp = _rwkv(zp, sh_p, st_p, mu, vecs, w2, a2, g2, d)
            ms, sts = _rwkv(zs, sh_s, st_s, mu, vecs, w2, a2, g2, d)
            outs["p_rw_sh"].append(unpad_rw(zp[:, -1]))
            outs["p_rw_wkv"].append(_heads_from_pairs(stp))
            outs["s_rw_sh"].append(unpad_rw(zs[:, -1]))
            outs["s_rw_wkv"].append(_heads_from_pairs(sts))
            w_out = a_w_out[j]
        elif kind == 1:
            w_in = b_w_in[j].astype(BF16)
            memq_cb = (q_cols + 2 * kv_cols) // memq_cols
            k_cb, v_cb = q_cols // kv_cols, q_cols // kv_cols + 1
            zp = _mm_norm(xp, attn_norm[i], w_in).reshape(bp, t, -1)
            zs = _mm_norm(xs, attn_norm[i], w_in).reshape(bd, s_len, -1)
            kgain = _tile_gain(b_k_norm[j], kv_cols)
            qgain = _tile_gain(b_q_norm[j], LANE)
            sink = jnp.repeat(b_sink[j].astype(F32), HEAD_DIM).reshape(1, q_cols)
            knp = _headnorm(zp.reshape(bp * t, -1), k_cb, kv_cols, kgain).reshape(bp, t, kv_cols)
            mp = _flash(zp, 0, q_cols, knp, 0, zp, v_cb, qgain, sink, mode="band", tq=WINDOW, kb=WINDOW,
                        group=att_group)
            outs["p_sw_k"].append(knp[:, t - win_rows:].reshape(bp, win_rows, ATT_KV_HEADS, HEAD_DIM))
            outs["p_sw_v"].append(zp[:, t - win_rows:, q_cols + kv_cols:q_cols + 2 * kv_cols]
                                  .reshape(bp, win_rows, ATT_KV_HEADS, HEAD_DIM))
            kns = _headnorm(zs.reshape(bd * s_len, -1), k_cb, kv_cols, kgain).reshape(bd, s_len, kv_cols)
            vs_new = zs[:, :, q_cols + kv_cols:q_cols + 2 * kv_cols]
            k_all = jnp.concatenate([cache_swa_k[j].reshape(bd, win_rows, kv_cols), kns], axis=1)
            v_all = jnp.concatenate([cache_swa_v[j].reshape(bd, win_rows, kv_cols), vs_new], axis=1)
            n_keys = win_rows + s_len
            n_pad = -(-n_keys // LANE) * LANE
            pad = jnp.zeros((bd, n_pad - n_keys, kv_cols), F32)
            ms = _flash(zs, 0, q_cols, jnp.concatenate([k_all, pad], axis=1), 0,
                        jnp.concatenate([v_all, pad], axis=1), 0, qgain, sink, mode="all", tq=s_len, kb=n_pad,
                        group=att_group, n_valid=n_keys)
            outs["s_sw_k"].append(k_all[:, n_keys - win_rows:].reshape(bd, win_rows, ATT_KV_HEADS, HEAD_DIM))
            outs["s_sw_v"].append(v_all[:, n_keys - win_rows:].reshape(bd, win_rows, ATT_KV_HEADS, HEAD_DIM))
            w_out = b_w_out[j]
        else:
            wc = c_w_in[j]
            w_in = jnp.concatenate([wc[:, :o_ki], wc[:, c_cols:],
                                    _pad_cols(wc, [(o_ki, IDX_DIM + IDX_HEADS, LANE)])], axis=1).astype(BF16)
            memq_cb = o_ki // memq_cols
            kw_cb = (o_ki + memq_cols) // LANE
            k_cb, v_cb = q_cols // kv_cols, q_cols // kv_cols + 1
            qi_cb = o_qi // (IDX_HEADS * IDX_DIM)
            zp = _mm_norm(xp, attn_norm[i], w_in).reshape(bp, t, -1)
            zs = _mm_norm(xs, attn_norm[i], w_in).reshape(bd, s_len, -1)
            kgain = _tile_gain(c_k_norm[j], kv_cols)
            qgain = _tile_gain(c_q_norm[j], LANE)
            igain = jnp.concatenate([c_idx_k_norm[j].astype(F32), jnp.zeros((LANE - IDX_DIM,), F32)]).reshape(1, LANE)
            no_sink = jnp.full((1, q_cols), -jnp.inf, F32)
            knp = _headnorm(zp.reshape(bp * t, -1), k_cb, kv_cols, kgain).reshape(bp, t, kv_cols)
            kip = _headnorm(zp.reshape(bp * t, -1), kw_cb, LANE, igain).reshape(bp, t, LANE)
            tq = min(t, DSA_TQ)
            kb = min(t, DSA_KB)
            sc, tau, cut = _dsa_select(zp, qi_cb, kw_cb, kip, tq=tq, kb=kb, k_sel=k_sel_p, n_valid=t, causal=True)
            ftq, fkb = min(t, DSA_FLASH_TQ), min(t, DSA_FLASH_KB)
            assert all(_causal_blocks(r // tq, tq, kb) * kb >= _causal_blocks(r // ftq, ftq, fkb) * fkb
                       for r in range(0, t, CHUNK))
            mp = _flash(zp, 0, q_cols, knp, 0, zp, v_cb, qgain, no_sink, mode="dsa", tq=ftq, kb=fkb,
                        group=att_group, n_valid=t, causal=True, scores=sc, tau=tau, jb=cut)
            outs["p_ds_k"].append(knp.reshape(bp, t, ATT_KV_HEADS, HEAD_DIM))
            outs["p_ds_v"].append(zp[:, :, q_cols + kv_cols:q_cols + 2 * kv_cols].reshape(bp, t, ATT_KV_HEADS, HEAD_DIM))
            outs["p_ds_i"].append(kip[:, :, :IDX_DIM])
            kns = _headnorm(zs.reshape(bd * s_len, -1), k_cb, kv_cols, kgain).reshape(bd, s_len, kv_cols)
            kis = _headnorm(zs.reshape(bd * s_len, -1), kw_cb, LANE, igain).reshape(bd, s_len, LANE)
            vs_new = zs[:, :, q_cols + kv_cols:q_cols + 2 * kv_cols]
            n_keys = past + s_len
            n_pad = -(-n_keys // LANE) * LANE
            zpad = lambda w: jnp.zeros((bd, n_pad - n_keys, w), F32)
            k_all = jnp.concatenate([cache_dsa_k[j].reshape(bd, past, kv_cols), kns, zpad(kv_cols)], axis=1)
            v_all = jnp.concatenate([cache_dsa_v[j].reshape(bd, past, kv_cols), vs_new, zpad(kv_cols)], axis=1)
            ki_cache = jnp.concatenate([cache_dsa_idx_k[j], jnp.zeros((bd, past, LANE - IDX_DIM), F32)], axis=-1)
            ki_all = jnp.concatenate([ki_cache, kis, zpad(LANE)], axis=1)
            sc, tau, cut = _dsa_select(zs, qi_cb, kw_cb, ki_all, tq=s_len, kb=n_pad, k_sel=k_sel_s, n_valid=n_keys,
                                       causal=False)
            ms = _flash(zs, 0, q_cols, k_all, 0, v_all, 0, qgain, no_sink, mode="dsa", tq=s_len, kb=n_pad,
                        group=att_group, n_valid=n_keys, causal=False, scores=sc, tau=tau, jb=cut)
            outs["s_ds_k"].append(kns.reshape(bd, s_len, ATT_KV_HEADS, HEAD_DIM))
            outs["s_ds_v"].append(vs_new.reshape(bd, s_len, ATT_KV_HEADS, HEAD_DIM))
            outs["s_ds_i"].append(kis[:, :, :IDX_DIM])
            w_out = c_w_out[j]

        kv_mem = _mm_norm(mem_prompt.reshape(bp * mem_tokens, d), mem_norm[i], mem_w_kv[i].astype(BF16))
        km_p = _headnorm(kv_mem, 0, memq_cols, _tile_gain(mem_k_norm[i], memq_cols)).reshape(bp, mem_tokens, memq_cols)
        vm_p = kv_mem[:, memq_cols:].reshape(bp, mem_tokens, memq_cols)
        outs["p_mk"].append(km_p.reshape(bp, mem_tokens, MEM_HEADS, HEAD_DIM))
        outs["p_mv"].append(vm_p.reshape(bp, mem_tokens, MEM_HEADS, HEAD_DIM))
        mo_p = _mem_attend(zp, memq_cb, km_p, vm_p, mem_q_norm[i])
        mo_s = _mem_attend(zs, memq_cb, cache_mem_k[i].reshape(bd, mem_tokens, memq_cols),
                           cache_mem_v[i].reshape(bd, mem_tokens, memq_cols), mem_q_norm[i])
        w_mix, w_mem = w_out[:d].astype(BF16), w_out[d:].astype(BF16)
        xp = _mm_res([mp.reshape(bp * t, d), mo_p.reshape(bp * t, memq_cols)], [w_mix, w_mem], xp)
        xs = _mm_res([ms.reshape(bd * s_len, d), mo_s.reshape(bd * s_len, memq_cols)], [w_mix, w_mem], xs)

        w_up, w_down = ffn_w_up[i].astype(BF16), ffn_w_down[i].astype(BF16)
        xp, cp = _conv_ffn(xp, bp, t, ffn_norm[i], w_up, ffn_conv[i], w_down,
                           jnp.zeros((bp, CONV_W - 1, 2 * d_ff), F32))
        xs, cs = _conv_ffn(xs, bd, s_len, ffn_norm[i], w_up, ffn_conv[i], w_down, state_ffn_conv[i])
        outs["p_cv"].append(cp)
        outs["s_cv"].append(cs)

    st = jnp.stack
    order = ("p_rw_wkv", "p_rw_sh", "p_sw_k", "p_sw_v", "p_ds_k", "p_ds_v", "p_ds_i", "p_mk", "p_mv", "p_cv",
             "s_rw_wkv", "s_rw_sh", "s_sw_k", "s_sw_v", "s_ds_k", "s_ds_v", "s_ds_i", "s_cv")
    return (xp.reshape(bp, t, d), xs.reshape(bd, s_len, d)) + tuple(st(outs[k]) for k in order)
```

```python
import functools
import math

import jax
import jax.numpy as jnp
from jax import lax
from jax.experimental import pallas as pl
from jax.experimental.pallas import tpu as pltpu

F32 = jnp.float32
BF16 = jnp.bfloat16

HEAD_DIM = 64
CHUNK = 64
NORM_EPS = 1e-6
RW_GN_EPS = HEAD_DIM * 1e-5
ATT_KV_HEADS = 4
WINDOW = 128
IDX_HEADS = 8
IDX_DIM = 64
TOPK_MAX = 256
MEM_HEADS = 4
CONV_W = 3

LANE = 128
VMEM_LIMIT = 52 * 1024 * 1024
NEG_BIG = -(2.0 ** 100)
INT_MIN = -2147483648
MM_ROWS = 1024
DSA_TQ = 256
DSA_KB = 512
DSA_FLASH_TQ = 256
DSA_FLASH_KB = 512


def _cparams(sem, vmem=VMEM_LIMIT):
    return pltpu.CompilerParams(dimension_semantics=sem, vmem_limit_bytes=vmem)


def _split3(a):
    a1 = a.astype(BF16)
    r1 = a - a1.astype(F32)
    a2 = r1.astype(BF16)
    r2 = r1 - a2.astype(F32)
    return a1, a2, r2.astype(BF16)


def _dot_exact_rhs(a, e):
    a1, a2, a3 = _split3(a)
    d = lambda x: jnp.dot(x, e, preferred_element_type=F32)
    return (d(a3) + d(a2)) + d(a1)


def _dot_exact_lhs(e, a):
    a1, a2, a3 = _split3(a)
    d = lambda x: jnp.dot(e, x, preferred_element_type=F32)
    return (d(a3) + d(a2)) + d(a1)


def _head_block_matrix(width, value):
    r = lax.broadcasted_iota(jnp.int32, (width, width), 0) // HEAD_DIM
    c = lax.broadcasted_iota(jnp.int32, (width, width), 1) // HEAD_DIM
    return jnp.where(r == c, value, 0.0).astype(BF16)


def _dot_rw(a, b, dims):
    return lax.dot_general(a.astype(BF16), b.astype(BF16), (dims, ((), ())), preferred_element_type=F32)


def _pick_tile(n, cap):
    best = None
    for t in range(LANE, min(n, cap) + 1, LANE):
        if n % t == 0:
            best = t
    assert best is not None, n
    return best


def _mm_norm_body(x_ref, g_ref, w_ref, o_ref, xn_ref):
    @pl.when(pl.program_id(1) == 0)
    def _():
        x = x_ref[...]
        ms = jnp.mean(x * x, axis=-1, keepdims=True)
        xn_ref[...] = ((x * lax.rsqrt(ms + NORM_EPS)) * g_ref[...]).astype(BF16)

    o_ref[...] = jnp.dot(xn_ref[...], w_ref[...], preferred_element_type=F32)


def _mm_norm(x, gain, w):
    m, k = x.shape
    n = w.shape[1]
    tm = min(m, MM_ROWS)
    tn = _pick_tile(n, 1536)
    return pl.pallas_call(
        _mm_norm_body,
        out_shape=jax.ShapeDtypeStruct((m, n), F32),
        grid=(m // tm, n // tn),
        in_specs=[pl.BlockSpec((tm, k), lambda i, j: (i, 0)),
                  pl.BlockSpec((1, k), lambda i, j: (0, 0)),
                  pl.BlockSpec((k, tn), lambda i, j: (0, j))],
        out_specs=pl.BlockSpec((tm, tn), lambda i, j: (i, j)),
        scratch_shapes=[pltpu.VMEM((tm, k), BF16)],
        compiler_params=_cparams(("parallel", "arbitrary")),
        name="mm_norm",
    )(x, gain.reshape(1, k), w)


def _mm_res_body(*refs, n_lhs):
    lhs = refs[:n_lhs]
    ws = refs[n_lhs:2 * n_lhs]
    r_ref, o_ref = refs[2 * n_lhs], refs[2 * n_lhs + 1]
    acc = jnp.dot(lhs[0][...], ws[0][...], preferred_element_type=F32)
    for a, w in zip(lhs[1:], ws[1:]):
        acc = acc + jnp.dot(a[...], w[...], preferred_element_type=F32)
    o_ref[...] = r_ref[...] + acc


def _mm_res(lhs_list, w_list, res):
    m, n = res.shape
    ktot = sum(a.shape[1] for a in lhs_list)
    tm = min(m, MM_ROWS)
    tn = _pick_tile(n, 1024 if ktot <= 3072 else 512)
    n_lhs = len(lhs_list)
    in_specs = [pl.BlockSpec((tm, a.shape[1]), lambda i, j: (i, 0)) for a in lhs_list]
    in_specs += [pl.BlockSpec((w.shape[0], tn), lambda i, j: (0, j)) for w in w_list]
    in_specs += [pl.BlockSpec((tm, tn), lambda i, j: (i, j))]
    return pl.pallas_call(
        functools.partial(_mm_res_body, n_lhs=n_lhs),
        out_shape=jax.ShapeDtypeStruct((m, n), F32),
        grid=(m // tm, n // tn),
        in_specs=in_specs,
        out_specs=pl.BlockSpec((tm, tn), lambda i, j: (i, j)),
        compiler_params=_cparams(("parallel", "arbitrary")),
        name="mm_res",
    )(*lhs_list, *w_list, res)


def _headnorm_body(x_ref, g_ref, o_ref, *, width):
    avg = _head_block_matrix(LANE, 1.0 / HEAD_DIM)
    for c in range(width // LANE):
        x = x_ref[:, c * LANE:(c + 1) * LANE]
        ms = _dot_exact_rhs(x * x, avg)
        o_ref[:, c * LANE:(c + 1) * LANE] = (x * lax.rsqrt(ms + NORM_EPS)) * g_ref[:, c * LANE:(c + 1) * LANE]


def _headnorm(x, col_block, width, gain_row):
    m = x.shape[0]
    tm = min(m, 1024)
    return pl.pallas_call(
        functools.partial(_headnorm_body, width=width),
        out_shape=jax.ShapeDtypeStruct((m, width), F32),
        grid=(m // tm,),
        in_specs=[pl.BlockSpec((tm, width), lambda i: (i, col_block)),
                  pl.BlockSpec((1, width), lambda i: (0, 0))],
        out_specs=pl.BlockSpec((tm, width), lambda i: (i, 0)),
        compiler_params=_cparams(("parallel",)),
        name="headnorm",
    )(x, gain_row)


def _up_conv_body(x_ref, g_ref, wa_ref, wb_ref, pa_ref, pb_ref, cwa_ref, cwb_ref, o_ref, la_ref, lb_ref,
                  xn_ref, ca_ref, cb_ref, *, tm, tiles_per_batch, bpt):
    i = pl.program_id(0)
    j = pl.program_id(1)
    rpb = tm // bpt

    @pl.when(j == 0)
    def _():
        x = x_ref[...]
        ms = jnp.mean(x * x, axis=-1, keepdims=True)
        xn_ref[...] = ((x * lax.rsqrt(ms + NORM_EPS)) * g_ref[...]).astype(BF16)

    if bpt == 1:
        @pl.when(i % tiles_per_batch == 0)
        def _():
            ca_ref[j] = pa_ref[0]
            cb_ref[j] = pb_ref[0]

    xn = xn_ref[...]
    row = lax.broadcasted_iota(jnp.int32, o_ref.shape, 0)
    off = row if bpt == 1 else row % rpb
    if bpt > 1:
        pick = (lax.broadcasted_iota(jnp.int32, (tm, bpt), 0) // rpb
                == lax.broadcasted_iota(jnp.int32, (tm, bpt), 1)).astype(BF16)

    def conv(w_ref, p_ref, c_ref, cw_ref, last_ref):
        u = jnp.dot(xn, w_ref[...], preferred_element_type=F32)
        if bpt == 1:
            car = c_ref[j]
            c0, c1 = car[0:1, :], car[1:2, :]
            c_ref[j] = u[tm - 2:tm, :]
            last_ref[0] = u[tm - 2:tm, :]
        else:
            c0 = _dot_exact_lhs(pick, p_ref[:, 0, :])
            c1 = _dot_exact_lhs(pick, p_ref[:, 1, :])
            for bi in range(bpt):
                last_ref[bi] = u[(bi + 1) * rpb - 2:(bi + 1) * rpb, :]
        u1 = jnp.where(off == 0, c1, pltpu.roll(u, 1, axis=0))
        u2 = jnp.where(off == 0, c0, jnp.where(off == 1, c1, pltpu.roll(u, 2, axis=0)))
        return (u2 * cw_ref[0:1, :] + u1 * cw_ref[1:2, :]) + u * cw_ref[2:3, :]

    a = conv(wa_ref, pa_ref, ca_ref, cwa_ref, la_ref)
    b = conv(wb_ref, pb_ref, cb_ref, cwb_ref, lb_ref)
    o_ref[...] = ((a * jax.nn.sigmoid(a)) * b).astype(o_ref.dtype)


def _up_conv(x, gain, w_up, conv_w, prev, t):
    m, k = x.shape
    f = w_up.shape[1] // 2
    b = m // t
    tm = min(m, MM_ROWS)
    bpt = max(1, tm // t)
    assert tm % t == 0 or t % tm == 0
    tn = _pick_tile(f, 512)
    nj = f // tn
    tpb = max(1, t // tm)
    act, la, lb = pl.pallas_call(
        functools.partial(_up_conv_body, tm=tm, tiles_per_batch=tpb, bpt=bpt),
        out_shape=(jax.ShapeDtypeStruct((m, f), BF16),
                   jax.ShapeDtypeStruct((b, CONV_W - 1, f), F32),
                   jax.ShapeDtypeStruct((b, CONV_W - 1, f), F32)),
        grid=(m // tm, nj),
        in_specs=[pl.BlockSpec((tm, k), lambda i, j: (i, 0)),
                  pl.BlockSpec((1, k), lambda i, j: (0, 0)),
                  pl.BlockSpec((k, tn), lambda i, j: (0, j)),
                  pl.BlockSpec((k, tn), lambda i, j: (0, nj + j)),
                  pl.BlockSpec((bpt, CONV_W - 1, tn), lambda i, j: (i // tpb, 0, j)),
                  pl.BlockSpec((bpt, CONV_W - 1, tn), lambda i, j: (i // tpb, 0, nj + j)),
                  pl.BlockSpec((CONV_W, tn), lambda i, j: (0, j)),
                  pl.BlockSpec((CONV_W, tn), lambda i, j: (0, nj + j))],
        out_specs=(pl.BlockSpec((tm, tn), lambda i, j: (i, j)),
                   pl.BlockSpec((bpt, CONV_W - 1, tn), lambda i, j: (i // tpb, 0, j)),
                   pl.BlockSpec((bpt, CONV_W - 1, tn), lambda i, j: (i // tpb, 0, j))),
        scratch_shapes=[pltpu.VMEM((tm, k), BF16),
                        pltpu.VMEM((nj, CONV_W - 1, tn), F32),
                        pltpu.VMEM((nj, CONV_W - 1, tn), F32)],
        compiler_params=_cparams(("arbitrary", "arbitrary")),
        name="up_conv_gate",
    )(x, gain.reshape(1, k), w_up, w_up, prev, prev, conv_w, conv_w)
    return act, jnp.concatenate([la, lb], axis=-1)


def _causal_blocks(i, tq, kb):
    return ((i + 1) * tq + kb - 1) // kb


def _fold_steps(nq, tq, kb):
    if nq % 2:
        return None
    return max(_causal_blocks(r, tq, kb) + _causal_blocks(nq - 1 - r, tq, kb) for r in range(nq // 2))


def _fold_step(r, step, nq, tq, kb):
    n_lo = _causal_blocks(r, tq, kb)
    hi = nq - 1 - r
    in_lo = step < n_lo
    i = jnp.where(in_lo, r, hi)
    last_blk = _causal_blocks(i, tq, kb) - 1
    j = jnp.minimum(jnp.where(in_lo, step, step - n_lo), last_blk)
    active = step < n_lo + _causal_blocks(hi, tq, kb)
    return i, j, active & (jnp.where(in_lo, step, step - n_lo) == 0), active & (j == last_blk), active


def _flash_body(*refs, mode, tq, kb, wq, group, n_valid, nk, causal, fold_nq):
    q_ref, k_ref, v_ref, qg_ref, sink_ref = refs[:5]
    pos = 5
    if mode == "dsa":
        sc_ref, tau_ref, jb_ref = refs[pos:pos + 3]
        pos += 3
    o_ref = refs[pos]
    qn_scr, acc_scr, m_scr, l_scr, bias_scr = refs[pos + 1:]

    if fold_nq:
        i, j, is_first, is_last, fold_active = _fold_step(pl.program_id(1), pl.program_id(2), fold_nq, tq, kb)
    else:
        i, j = pl.program_id(1), pl.program_id(2)
        is_first, is_last = j == 0, j == nk - 1
    npairs = wq // LANE
    pairs_per_unit = group // 2 if group > 1 else 1
    lane = lax.broadcasted_iota(jnp.int32, (1, LANE), 1)
    lo_half = lane < HEAD_DIM

    @pl.when(is_first)
    def _():
        avg = _head_block_matrix(LANE, 1.0 / HEAD_DIM)
        for p in range(npairs):
            x = q_ref[0, :, p * LANE:(p + 1) * LANE]
            ms = jnp.dot((x * x).astype(BF16), avg, preferred_element_type=F32)
            qn = ((x * lax.rsqrt(ms + NORM_EPS)) * qg_ref[...]) * (HEAD_DIM ** -0.5)
            qn_scr[p] = qn.astype(BF16)
        acc_scr[...] = jnp.zeros(acc_scr.shape, F32)
        l_scr[...] = jnp.zeros(l_scr.shape, F32)
        m_scr[...] = jnp.full(m_scr.shape, NEG_BIG, F32)

    if mode == "band":
        kblk = i - WINDOW // kb + j
        active = kblk >= 0
    elif mode == "dsa" and causal:
        kblk = j
        active = fold_active if fold_nq else j * kb < (i + 1) * tq
    else:
        kblk = j
        active = None

    def step():
        kidx = kblk * kb + lax.broadcasted_iota(jnp.int32, (tq, kb), 1)
        qpos = i * tq + lax.broadcasted_iota(jnp.int32, (tq, kb), 0)
        if mode == "band":
            qchunk = qpos // CHUNK
            sel = (kidx >= (qchunk - WINDOW // CHUNK) * CHUNK) & (kidx < (qchunk + 1) * CHUNK) & (kidx >= 0)
        elif mode == "dsa":
            x = sc_ref[0]
            tau = tau_ref[0][:, 0:1]
            jb = jb_ref[0][:, 0:1]
            adm = kidx < ((qpos // CHUNK + 1) * CHUNK if causal else n_valid)
            sel = adm & ((x > tau) | ((x == tau) & (kidx <= jb)))
        else:
            sel = kidx < n_valid
        bias_scr[...] = jnp.where(sel, 0.0, NEG_BIG).astype(BF16)

        ppu = pairs_per_unit
        for u in range(npairs // ppu):
            ps = slice(u * ppu, (u + 1) * ppu)
            s = lax.dot_general(qn_scr[ps].reshape(ppu * tq, LANE), k_ref[0, 0, u], (((1,), (1,)), ((), ())),
                                preferred_element_type=F32).reshape(ppu, tq, 2 * kb)
            bias = bias_scr[...][None]
            m_old = m_scr[ps]
            s0 = s[:, :, 0:kb].astype(BF16) + bias
            s1 = s[:, :, kb:2 * kb].astype(BF16) + bias
            mn0 = jnp.maximum(m_old[:, :, 0:1], jnp.max(s0, axis=2, keepdims=True).astype(F32))
            mn1 = jnp.maximum(m_old[:, :, HEAD_DIM:HEAD_DIM + 1], jnp.max(s1, axis=2, keepdims=True).astype(F32))
            p0 = jnp.exp(s0 - mn0.astype(BF16))
            p1 = jnp.exp(s1 - mn1.astype(BF16))
            pcat = jnp.concatenate([p0, p1], axis=2).reshape(ppu * tq, 2 * kb)
            pv = jnp.dot(pcat, v_ref[0, 0, u], preferred_element_type=F32)
            pv = pv.reshape(ppu, tq, 2 * LANE)
            mn = jnp.where(lo_half, mn0, mn1)
            alpha = jnp.exp(m_old - mn)
            acc_scr[ps] = acc_scr[ps] * alpha + pv[:, :, 0:LANE]
            l_scr[ps] = l_scr[ps] * alpha + pv[:, :, LANE:2 * LANE]
            m_scr[ps] = mn

    if active is None:
        step()
    else:
        pl.when(active)(step)

    @pl.when(is_last)
    def _():
        for p in range(npairs):
            den = l_scr[p] + jnp.exp(sink_ref[:, p * LANE:(p + 1) * LANE] - m_scr[p])
            o_ref[0, :, p * LANE:(p + 1) * LANE] = (acc_scr[p] / den).astype(o_ref.dtype)


def _pair_operands_body(k_ref, v_ref, kab_ref, vab_ref, *, kb, group, n_units):
    lane = lax.broadcasted_iota(jnp.int32, (1, LANE), 1)
    lo_half = lane < HEAD_DIM
    ones_lo = jnp.broadcast_to(jnp.where(lo_half, 1.0, 0.0), (kb, LANE)).astype(BF16)
    ones_hi = jnp.broadcast_to(jnp.where(lo_half, 0.0, 1.0), (kb, LANE)).astype(BF16)
    for u in range(n_units):
        if group > 1:
            tile_idx, half = u // 2, u % 2
        else:
            tile_idx, half = u, None
        for src, dst in ((k_ref, kab_ref), (v_ref, vab_ref)):
            tile = src[0, :, tile_idx * LANE:(tile_idx + 1) * LANE]
            if half is None:
                a_part = jnp.where(lo_half, tile, 0.0)
                b_part = jnp.where(lo_half, 0.0, tile)
            elif half == 0:
                a_part = jnp.where(lo_half, tile, 0.0)
                b_part = pltpu.roll(a_part, HEAD_DIM, axis=1)
            else:
                b_part = jnp.where(lo_half, 0.0, tile)
                a_part = pltpu.roll(b_part, HEAD_DIM, axis=1)
            dst[0, 0, u, 0:kb, 0:LANE] = a_part.astype(BF16)
            dst[0, 0, u, kb:2 * kb, 0:LANE] = b_part.astype(BF16)
        vab_ref[0, 0, u, 0:kb, LANE:2 * LANE] = ones_lo
        vab_ref[0, 0, u, kb:2 * kb, LANE:2 * LANE] = ones_hi


def _pair_operands(k_arr, k_cb, v_arr, v_cb, kb, group, npairs):
    b, t_k = k_arr.shape[0], k_arr.shape[1]
    wk = ATT_KV_HEADS * HEAD_DIM
    n_units = npairs // (group // 2) if group > 1 else npairs
    nkb = t_k // kb
    return pl.pallas_call(
        functools.partial(_pair_operands_body, kb=kb, group=group, n_units=n_units),
        out_shape=(jax.ShapeDtypeStruct((b, nkb, n_units, 2 * kb, LANE), BF16),
                   jax.ShapeDtypeStruct((b, nkb, n_units, 2 * kb, 2 * LANE), BF16)),
        grid=(b, nkb),
        in_specs=[pl.BlockSpec((1, kb, wk), lambda bi, j: (bi, j, k_cb)),
                  pl.BlockSpec((1, kb, wk), lambda bi, j: (bi, j, v_cb))],
        out_specs=(pl.BlockSpec((1, 1, n_units, 2 * kb, LANE), lambda bi, j: (bi, j, 0, 0, 0)),
                   pl.BlockSpec((1, 1, n_units, 2 * kb, 2 * LANE), lambda bi, j: (bi, j, 0, 0, 0))),
        compiler_params=_cparams(("parallel", "parallel")),
        name="pair_operands",
    )(k_arr, v_arr)


def _flash(q_arr, q_cb, wq, k_arr, k_cb, v_arr, v_cb, q_gain, sink_row, *, mode, tq, kb, group,
           n_valid=None, causal=False, scores=None, tau=None, jb=None):
    b, t_q = q_arr.shape[0], q_arr.shape[1]
    t_k = k_arr.shape[1]
    nq = t_q // tq
    if mode == "band":
        assert tq == kb == WINDOW
        nk = 2
        kmap = lambda i, j: jnp.maximum(i - 1 + j, 0)
    elif mode == "dsa" and causal:
        nk = t_k // kb
        kmap = lambda i, j: jnp.minimum(j, ((i + 1) * tq - 1) // kb)
    else:
        nk = t_k // kb
        kmap = lambda i, j: j
    grid = (b, nq, nk)
    qmap = lambda i, j: i
    fold = _fold_steps(nq, tq, kb) if (mode == "dsa" and causal) else None
    if fold:
        grid = (b, nq // 2, fold)
        qmap = lambda r, s: _fold_step(r, s, nq, tq, kb)[0]
        kmap = lambda r, s: _fold_step(r, s, nq, tq, kb)[1]
    npairs = wq // LANE
    kab, vab = _pair_operands(k_arr, k_cb, v_arr, v_cb, kb, group, npairs)
    n_units = kab.shape[2]
    in_specs = [pl.BlockSpec((1, tq, wq), lambda bi, i, j: (bi, qmap(i, j), q_cb)),
                pl.BlockSpec((1, 1, n_units, 2 * kb, LANE), lambda bi, i, j: (bi, kmap(i, j), 0, 0, 0)),
                pl.BlockSpec((1, 1, n_units, 2 * kb, 2 * LANE), lambda bi, i, j: (bi, kmap(i, j), 0, 0, 0)),
                pl.BlockSpec((1, LANE), lambda bi, i, j: (0, 0)),
                pl.BlockSpec((1, wq), lambda bi, i, j: (0, 0))]
    args = [q_arr, kab, vab, q_gain, sink_row]
    if mode == "dsa":
        in_specs += [pl.BlockSpec((1, tq, kb), lambda bi, i, j: (bi, qmap(i, j), kmap(i, j))),
                     pl.BlockSpec((1, tq, LANE), lambda bi, i, j: (bi, qmap(i, j), 0)),
                     pl.BlockSpec((1, tq, LANE), lambda bi, i, j: (bi, qmap(i, j), 0))]
        args += [scores, tau, jb]
    body = functools.partial(_flash_body, mode=mode, tq=tq, kb=kb, wq=wq, group=group,
                             n_valid=n_valid, nk=nk, causal=causal, fold_nq=nq if fold else 0)
    return pl.pallas_call(
        body,
        out_shape=jax.ShapeDtypeStruct((b, t_q, wq), BF16),
        grid=grid,
        in_specs=in_specs,
        out_specs=pl.BlockSpec((1, tq, wq), lambda bi, i, j: (bi, qmap(i, j), 0)),
        scratch_shapes=[pltpu.VMEM((npairs, tq, LANE), BF16),
                        pltpu.VMEM((npairs, tq, LANE), F32),
                        pltpu.VMEM((npairs, tq, LANE), F32),
                        pltpu.VMEM((npairs, tq, LANE), F32),
                        pltpu.VMEM((tq, kb), BF16)],
        compiler_params=_cparams(("parallel", "parallel", "arbitrary")),
        name="flash_" + mode,
    )(*args)


def _dsa_select_body(qi_ref, kw_ref, ki_ref, sc_ref, tau_ref, jb_ref, sc_scr, cnt_scr, ebuf_scr, *, tq, kb, nk, k_sel,
                     n_valid, causal, fold_nq):
    if fold_nq:
        i, j, _, is_last, fold_active = _fold_step(pl.program_id(1), pl.program_id(2), fold_nq, tq, kb)
    else:
        i, j = pl.program_id(1), pl.program_id(2)
        is_last = j == nk - 1
    qpos = i * tq + lax.broadcasted_iota(jnp.int32, (tq, 1), 0)
    if causal:
        n_adm = (qpos // CHUNK + 1) * CHUNK
        active = fold_active if fold_nq else j * kb < (i + 1) * tq
        n_blocks = _causal_blocks(i, tq, kb)
    else:
        n_adm = jnp.full((tq, 1), n_valid, jnp.int32)
        active = None
        n_blocks = nk

    def compute():
        ka = ki_ref[0]
        kab = jnp.concatenate([ka, pltpu.roll(ka, HEAD_DIM, axis=1)], axis=0).astype(BF16)
        kw = kw_ref[0]
        acc = jnp.zeros((tq, kb), F32)
        q4 = jnp.concatenate([qi_ref[0, :, p * LANE:(p + 1) * LANE].astype(BF16) for p in range(IDX_HEADS // 2)],
                             axis=0)
        s = lax.dot_general(q4, kab, (((1,), (1,)), ((), ())), preferred_element_type=F32)
        for p in range(IDX_HEADS // 2):
            for h in range(2):
                col = HEAD_DIM + 2 * p + h
                w = (kw[:, col:col + 1] * IDX_HEADS ** -0.5) * IDX_DIM ** -0.5
                acc = acc + jnp.maximum(s[p * tq:(p + 1) * tq, h * kb:(h + 1) * kb], 0.0) * w
        acc = jnp.where(acc == 0.0, 0.0, acc)
        kidx = j * kb + lax.broadcasted_iota(jnp.int32, (tq, kb), 1)
        x = jnp.where(kidx < n_adm, acc, -jnp.inf)
        sc_scr[j] = x
        sc_ref[0] = x

    if active is None:
        compute()
    else:
        pl.when(active)(compute)

        if not fold_nq:
            @pl.when(jnp.logical_not(active))
            def _():
                sc_ref[0] = jnp.full((tq, kb), -jnp.inf, F32)

    @pl.when(is_last)
    def _():
        kf = float(k_sel)

        rsz = min(tq, LANE)
        lane_tiles = range(kb // LANE)

        def count(make_pred):
            starts = range(0, tq, rsz)
            preds = [make_pred(lambda col, r0=r0: jnp.broadcast_to(col[r0:r0 + rsz], (rsz, LANE))) for r0 in starts]
            accs = []
            for r0, pred in zip(starts, preds):
                def blk(jj, acc, r0=r0, pred=pred):
                    for a in lane_tiles:
                        x = sc_scr[jj, r0:r0 + rsz, a * LANE:(a + 1) * LANE]
                        acc = acc + jnp.where(pred(x), 1.0, 0.0)
                    return acc
                accs.append(lax.fori_loop(0, n_blocks, blk, jnp.zeros((rsz, LANE), F32)))
            return jnp.sum(jnp.concatenate(accs, axis=0), axis=1, keepdims=True)

        def key_to_float(tu):
            cs = tu ^ INT_MIN
            fb = jnp.where(cs >= 0, cs, cs ^ 0x7FFFFFFF)
            return lax.bitcast_convert_type(fb, F32)

        def bit_body(it, carry):
            tu, c_at = carry
            cand = tu | jnp.left_shift(jnp.int32(1), 31 - it)
            thr = key_to_float(cand)
            c = count(lambda widen: (lambda x, t=widen(thr): x >= t))
            keep = c >= kf
            return jnp.where(keep, cand, tu), jnp.where(keep, c, c_at)

        tu, c_ge = lax.fori_loop(0, 32, bit_body, (jnp.zeros((tq, 1), jnp.int32),
                                                   jnp.full((tq, 1), float(nk * kb), F32)))
        full = n_adm >= k_sel
        tau = jnp.where(full, key_to_float(tu), -jnp.inf)
        tau = jnp.where(jnp.abs(tau) < 1.1754944e-38, 0.0, tau)
        c_gt = count(lambda widen: (lambda x, t=widen(tau): x > t))
        need = kf - c_gt
        tau_ref[0] = jnp.broadcast_to(tau, (tq, LANE))
        jb_ref[0] = jnp.full((tq, LANE), 2 ** 30, jnp.int32)
        surplus = jnp.max(jnp.where(full, (c_ge - c_gt) - need, 0.0))

        @pl.when(surplus > 0.5)
        def _():
            tau_b = jnp.broadcast_to(tau, (tq, LANE))
            need_b = jnp.broadcast_to(need, (tq, LANE))
            ones_mat = jnp.ones((LANE, LANE), BF16)
            row_sum = lambda part: jnp.dot(part.astype(BF16), ones_mat, preferred_element_type=F32)

            def ties(jj, a):
                return jnp.where(sc_scr[jj, :, a * LANE:(a + 1) * LANE] == tau_b, 1.0, 0.0)

            def blk_count(jj, carry):
                part = ties(jj, 0)
                for a in lane_tiles[1:]:
                    part = part + ties(jj, a)
                cnt_scr[jj] = row_sum(part)
                return carry
            lax.fori_loop(0, n_blocks, blk_count, 0)

            def locate(jj, carry):
                run, blk_idx, before = carry
                run = run + cnt_scr[jj]
                below = run < need_b
                return run, blk_idx + jnp.where(below, 1, 0), jnp.where(below, run, before)
            zero = jnp.zeros((tq, LANE), F32)
            _, blk_idx, before = lax.fori_loop(0, n_blocks, locate, (zero, jnp.zeros((tq, LANE), jnp.int32), zero))
            need_in = need_b - before

            ebuf_scr[...] = jnp.zeros((tq, kb), F32)

            def extract(jj, carry):
                here = blk_idx == jj
                for a in lane_tiles:
                    cols = slice(a * LANE, (a + 1) * LANE)
                    ebuf_scr[:, cols] = jnp.where(here, ties(jj, a), ebuf_scr[:, cols])
                return carry
            lax.fori_loop(0, n_blocks, extract, 0)

            lane_idx = lax.broadcasted_iota(jnp.int32, (tq, LANE), 1)
            cut_bits = max(1, (kb - 1).bit_length())

            def cut_body(it, cut):
                cand = cut | jnp.left_shift(jnp.int32(1), cut_bits - 1 - it)
                part = zero
                for a in lane_tiles:
                    part = part + jnp.where(a * LANE + lane_idx < cand, ebuf_scr[:, a * LANE:(a + 1) * LANE], 0.0)
                return jnp.where(row_sum(part) < need_in, cand, cut)
            cut = lax.fori_loop(0, cut_bits, cut_body, jnp.zeros((tq, LANE), jnp.int32))
            jb_ref[0] = blk_idx * kb + cut


def _dsa_select(z, qi_cb, kw_cb, ki_n, *, tq, kb, k_sel, n_valid, causal):
    b, t_q = z.shape[0], z.shape[1]
    t_k = ki_n.shape[1]
    nq, nk = t_q // tq, t_k // kb
    assert kb // LANE <= 256, "per-lane tie counts of one key block go through a bf16 matmul, exact up to 256"
    kmap = (lambda i, j: jnp.minimum(j, ((i + 1) * tq - 1) // kb)) if causal else (lambda i, j: j)
    omap = lambda i, j: j
    qmap = lambda i, j: i
    grid = (b, nq, nk)
    fold = _fold_steps(nq, tq, kb) if causal else None
    if fold:
        grid = (b, nq // 2, fold)
        qmap = lambda r, s: _fold_step(r, s, nq, tq, kb)[0]
        kmap = omap = lambda r, s: _fold_step(r, s, nq, tq, kb)[1]
    body = functools.partial(_dsa_select_body, tq=tq, kb=kb, nk=nk, k_sel=k_sel, n_valid=n_valid,
                             causal=causal, fold_nq=nq if fold else 0)
    return pl.pallas_call(
        body,
        out_shape=(jax.ShapeDtypeStruct((b, t_q, t_k), F32),
                   jax.ShapeDtypeStruct((b, t_q, LANE), F32),
                   jax.ShapeDtypeStruct((b, t_q, LANE), jnp.int32)),
        grid=grid,
        in_specs=[pl.BlockSpec((1, tq, IDX_HEADS * IDX_DIM), lambda bi, i, j: (bi, qmap(i, j), qi_cb)),
                  pl.BlockSpec((1, tq, LANE), lambda bi, i, j: (bi, qmap(i, j), kw_cb)),
                  pl.BlockSpec((1, kb, LANE), lambda bi, i, j: (bi, kmap(i, j), 0))],
        out_specs=(pl.BlockSpec((1, tq, kb), lambda bi, i, j: (bi, qmap(i, j), omap(i, j))),
                   pl.BlockSpec((1, tq, LANE), lambda bi, i, j: (bi, qmap(i, j), 0)),
                   pl.BlockSpec((1, tq, LANE), lambda bi, i, j: (bi, qmap(i, j), 0))),
        scratch_shapes=[pltpu.VMEM((nk, tq, kb), F32), pltpu.VMEM((nk, tq, LANE), F32), pltpu.VMEM((tq, kb), F32)],
        compiler_params=_cparams(("parallel", "parallel", "arbitrary")),
        name="dsa_select",
    )(z, z, ki_n)


def _rwkv_body(zr_ref, zk_ref, zv_ref, zw_ref, za_ref, zg_ref,
               sr_ref, sk_ref, sv_ref, sw_ref, sa_ref, sg_ref,
               mr_ref, mk_ref, mv_ref, mw_ref, ma_ref, mg_ref,
               vec_ref, w2_ref, a2_ref, g2_ref, s0_ref,
               mix_ref, sfin_ref,
               s_scr, prev_scr, prevl_scr, prevg_scr, y_scr, *, tc, lc, nt, pp):
    t = pl.program_id(2)

    @pl.when(t == 0)
    def _():
        s_scr[...] = s0_ref[0]
        prev_scr[0:1, :] = sr_ref[0]
        prev_scr[1:2, :] = sk_ref[0]
        prev_scr[2:3, :] = sv_ref[0]
        prevl_scr[0:1, :] = sw_ref[0]
        prevl_scr[1:2, :] = sa_ref[0]
        prevg_scr[0:1, :] = sg_ref[0]

    row = lax.broadcasted_iota(jnp.int32, (tc, 1), 0)

    def shifted(z_ref, prow, mu_ref):
        z = z_ref[0]
        zp = jnp.where(row == 0, prow, pltpu.roll(z, 1, axis=0))
        return z + (zp - z) * mu_ref[...], z[tc - 1:tc, :]

    r, last_r = shifted(zr_ref, prev_scr[0:1, :], mr_ref)
    k, last_k = shifted(zk_ref, prev_scr[1:2, :], mk_ref)
    v, last_v = shifted(zv_ref, prev_scr[2:3, :], mv_ref)
    zw, last_w = shifted(zw_ref, prevl_scr[0:1, :], mw_ref)
    za, last_a = shifted(za_ref, prevl_scr[1:2, :], ma_ref)
    zg, last_g = shifted(zg_ref, prevg_scr[0:1, :], mg_ref)
    prev_scr[0:1, :] = last_r
    prev_scr[1:2, :] = last_k
    prev_scr[2:3, :] = last_v
    prevl_scr[0:1, :] = last_w
    prevl_scr[1:2, :] = last_a
    prevg_scr[0:1, :] = last_g

    w0, a0 = vec_ref[0:1, :], vec_ref[1:2, :]
    k_k, k_a, r_k = vec_ref[2:3, :], vec_ref[3:4, :], vec_ref[4:5, :]
    ln_w, ln_b = vec_ref[5:6, :], vec_ref[6:7, :]

    ones_blk = _head_block_matrix(LANE, 1.0)
    avg_blk = _head_block_matrix(LANE, 1.0 / HEAD_DIM)

    def per_head(x, blk):
        return jnp.concatenate([jnp.dot(x[:, i * LANE:(i + 1) * LANE].astype(BF16), blk, preferred_element_type=F32)
                                for i in range(pp)], axis=1)

    xw = w0 + jnp.dot(jnp.tanh(zw).astype(BF16), w2_ref[...], preferred_element_type=F32)
    lw = -math.exp(-0.5) * jax.nn.sigmoid(xw)
    a = jax.nn.sigmoid(a0 + jnp.dot(za.astype(BF16), a2_ref[...], preferred_element_type=F32))
    g = jnp.dot(jax.nn.sigmoid(zg).astype(BF16), g2_ref[...], preferred_element_type=F32)
    kk = k * k_k
    kk = kk * jnp.minimum(lax.rsqrt(per_head(kk * kk, ones_blk)), 1e12)
    k2 = k * (1.0 + (a - 1.0) * k_a)

    a_step = -kk
    b_step = kk * a

    lane = lax.broadcasted_iota(jnp.int32, (1, LANE), 1)
    m0 = lane < HEAD_DIM
    rr = lax.broadcasted_iota(jnp.int32, (lc, 2 * lc), 0)
    cc = lax.broadcasted_iota(jnp.int32, (lc, 2 * lc), 1)
    incl = jnp.where(cc < lc, cc, cc - lc) <= rr
    strict = (lax.broadcasted_iota(jnp.int32, (2 * lc, 2 * lc), 1)
              < lax.broadcasted_iota(jnp.int32, (2 * lc, 2 * lc), 0))
    tri = (lax.broadcasted_iota(jnp.int32, (lc, lc), 1)
           <= lax.broadcasted_iota(jnp.int32, (lc, lc), 0)).astype(BF16)
    nsteps = lc.bit_length() - 1
    nt_dims = ((1,), (1,))
    nn_dims = ((1,), (0,))
    tn_dims = ((0,), (0,))

    chunks = range(tc // lc)
    st = []
    for c in chunks:
        rows = slice(c * lc, (c + 1) * lc)
        lwc = lw[rows]
        cs = _dot_exact_lhs(tri, lwc)
        p_in = jnp.exp(-cs)
        at = a_step[rows] * jnp.exp(cs - lwc)
        bt = b_step[rows] * p_in
        kt = k2[rows] * p_in
        rt = r[rows] * jnp.exp(cs)
        vc = v[rows]
        p_last = jnp.exp(cs[lc - 1:lc, :])
        for pi in range(pp):
            cols = slice(pi * LANE, (pi + 1) * LANE)
            stack = lambda x: jnp.concatenate([jnp.where(m0, x[:, cols], 0.0),
                                               jnp.where(m0, 0.0, x[:, cols])], axis=0).astype(BF16)
            st.append(dict(c=c, pi=pi,
                           ar=jnp.concatenate([stack(at), rt[:, cols].astype(BF16)], axis=0),
                           bk=jnp.concatenate([stack(bt), stack(kt)], axis=0),
                           v_s=stack(vc), p_last=p_last[:, cols]))
    for d in st:
        gram = _dot_rw(d["ar"], d["bk"], nt_dims)
        d["t_p"] = jnp.where(strict, gram[0:2 * lc, 0:2 * lc], 0.0)
        d["w_ak"] = jnp.where(strict, gram[0:2 * lc, 2 * lc:4 * lc], 0.0).astype(BF16)
        d["w_rb"] = jnp.where(incl, gram[2 * lc:3 * lc, 0:2 * lc], 0.0).astype(BF16)
        d["w_rk"] = jnp.where(incl, gram[2 * lc:3 * lc, 2 * lc:4 * lc], 0.0).astype(BF16)
    gw = 4 * lc
    lane_head = lax.broadcasted_iota(jnp.int32, (1, gw), 1) // lc
    eye_sbs = (lax.broadcasted_iota(jnp.int32, (lc, gw), 0)
               == lax.broadcasted_iota(jnp.int32, (lc, gw), 1) % lc).astype(F32)
    lane_pair = lax.broadcasted_iota(jnp.int32, (1, 2 * lc), 1)

    def blockdiag(y):
        yb = y.astype(BF16)
        return jnp.concatenate([jnp.where(lane_head == h, yb, jnp.zeros_like(yb)) for h in range(4)], axis=0)

    groups = []
    for gi in range(len(st) // 2):
        members = (st[2 * gi], st[2 * gi + 1])
        t_sbs = jnp.concatenate([m["t_p"][0:lc] + m["t_p"][lc:2 * lc] for m in members], axis=1)
        groups.append(dict(members=members, t=t_sbs, minv=eye_sbs + t_sbs))
    for grp in groups:
        grp["t"] = _dot_rw(grp["t"], blockdiag(grp["t"]), nn_dims)
    for step in range(nsteps - 1):
        for grp in groups:
            if step == nsteps - 2:
                grp["minv"] = grp["minv"] + _dot_rw(grp["minv"], blockdiag(grp["t"]), nn_dims)
            else:
                both = _dot_rw(jnp.concatenate([grp["minv"], grp["t"]], axis=0), blockdiag(grp["t"]), nn_dims)
                grp["minv"] = grp["minv"] + both[0:lc]
                grp["t"] = both[lc:2 * lc]
    for grp in groups:
        for idx, m in enumerate(grp["members"]):
            tile = grp["minv"][:, idx * 2 * lc:(idx + 1) * 2 * lc]
            m["minv"] = jnp.concatenate([jnp.where(lane_pair < lc, tile, 0.0),
                                         jnp.where(lane_pair < lc, 0.0, tile)], axis=0).astype(BF16)
    for d in st:
        both = _dot_rw(jnp.concatenate([d["w_ak"], d["w_rk"]], axis=0), d["v_s"], nn_dims)
        d["wv"] = both[0:2 * lc]
        d["y_c"] = both[2 * lc:3 * lc]
    for d in st:
        both = _dot_rw(d["minv"], jnp.concatenate([d["ar"][0:2 * lc], d["wv"].astype(BF16)], axis=1), nn_dims)
        d["ma"] = both[:, 0:LANE]
        d["mwv"] = both[:, LANE:2 * LANE]
    for d in st:
        d["g"] = _dot_rw(d["ma"], d["bk"][0:2 * lc], tn_dims).astype(BF16)
        d["d"] = _dot_rw(jnp.concatenate([d["mwv"].astype(BF16), d["v_s"]], axis=0), d["bk"], tn_dims)
    s_cur = [s_scr[pi] for pi in range(pp)]
    for d in st:
        s_in = s_cur[d["pi"]]
        d["s0"] = s_in.astype(BF16)
        s_cur[d["pi"]] = ((s_in + _dot_rw(d["s0"], d["g"], nn_dims)) + d["d"]) * d["p_last"]
    for pi in range(pp):
        s_scr[pi] = s_cur[pi]
    for d in st:
        d["xr"] = _dot_rw(d["ar"], d["s0"], nt_dims)
    for d in st:
        d["u"] = (_dot_rw(d["minv"], d["xr"][0:2 * lc], nn_dims) + d["mwv"]).astype(BF16)
    for d in st:
        c, pi = d["c"], d["pi"]
        y_scr[c * lc:(c + 1) * lc, pi * LANE:(pi + 1) * LANE] = (
            (d["xr"][2 * lc:3 * lc] + _dot_rw(d["w_rb"], d["u"], nn_dims)) + d["y_c"])

    y = y_scr[...]
    mean = per_head(y, avg_blk)
    dev = y - mean
    var = per_head(dev * dev, avg_blk)
    yn = (dev * lax.rsqrt(var + RW_GN_EPS)) * ln_w + ln_b
    bonus = per_head((r * k2) * r_k, ones_blk) * v
    mix_ref[0] = ((yn + bonus) * g).astype(mix_ref.dtype)

    @pl.when(t == nt - 1)
    def _():
        sfin_ref[0] = s_scr[...]


def _rwkv(z, shift_prev, s0_pairs, mu, vecs, w2, a2, g2, d_model):
    b, t = z.shape[0], z.shape[1]
    npairs = d_model // LANE
    tc = min(t, 512)
    lc = min(CHUNK, t)
    nt = t // tc
    pp = 2 if tc // lc >= 4 else 8
    wp = pp * LANE
    ngroups = npairs // pp
    cb_w, cb_a, cb_g = 3 * npairs, 3 * npairs + 1, (3 * npairs + 2) // 2

    def zspec(width, cbf):
        return pl.BlockSpec((1, tc, width), lambda bi, p, ti: (bi, ti, cbf(p)))

    def sspec(width, cbf):
        return pl.BlockSpec((1, 1, width), lambda bi, p, ti: (bi, 0, cbf(p)))

    def mspec(width, cbf):
        return pl.BlockSpec((1, width), lambda bi, p, ti: (0, cbf(p)))

    cbfs = [(wp, lambda p: p), (wp, lambda p: ngroups + p), (wp, lambda p: 2 * ngroups + p),
            (LANE, lambda p: cb_w), (LANE, lambda p: cb_a), (2 * LANE, lambda p: cb_g)]
    in_specs = ([zspec(w, f) for w, f in cbfs] + [sspec(w, f) for w, f in cbfs] + [mspec(w, f) for w, f in cbfs]
                + [pl.BlockSpec((8, wp), lambda bi, p, ti: (0, p)),
                   pl.BlockSpec((LANE, wp), lambda bi, p, ti: (0, p)),
                   pl.BlockSpec((LANE, wp), lambda bi, p, ti: (0, p)),
                   pl.BlockSpec((2 * LANE, wp), lambda bi, p, ti: (0, p)),
                   pl.BlockSpec((1, pp, LANE, LANE), lambda bi, p, ti: (bi, p, 0, 0))])
    return pl.pallas_call(
        functools.partial(_rwkv_body, tc=tc, lc=lc, nt=nt, pp=pp),
        out_shape=(jax.ShapeDtypeStruct((b, t, d_model), BF16),
                   jax.ShapeDtypeStruct((b, npairs, LANE, LANE), F32)),
        grid=(b, ngroups, nt),
        in_specs=in_specs,
        out_specs=(pl.BlockSpec((1, tc, wp), lambda bi, p, ti: (bi, ti, p)),
                   pl.BlockSpec((1, pp, LANE, LANE), lambda bi, p, ti: (bi, p, 0, 0))),
        scratch_shapes=[pltpu.VMEM((pp, LANE, LANE), F32),
                        pltpu.VMEM((8, wp), F32),
                        pltpu.VMEM((8, LANE), F32),
                        pltpu.VMEM((8, 2 * LANE), F32),
                        pltpu.VMEM((tc, wp), F32)],
        compiler_params=_cparams(("parallel", "parallel", "arbitrary")),
        name="rwkv7",
    )(*([z] * 6), *([shift_prev] * 6), *([mu] * 6), vecs, w2, a2, g2, s0_pairs)


def _pad_cols(x, segments):
    parts = []
    for start, width, padded in segments:
        seg = x[..., start:start + width]
        if padded > width:
            seg = jnp.concatenate([seg, jnp.zeros(seg.shape[:-1] + (padded - width,), seg.dtype)], axis=-1)
        parts.append(seg)
    return jnp.concatenate(parts, axis=-1)


def _pad_rows(x, padded):
    return jnp.concatenate([x, jnp.zeros((padded - x.shape[0],) + x.shape[1:], x.dtype)], axis=0)


def _pairs_from_heads(s):
    b, h = s.shape[0], s.shape[1]
    s = s.reshape(b, h // 2, 2, HEAD_DIM, HEAD_DIM)
    z = jnp.zeros_like(s[:, :, 0])
    top = jnp.concatenate([s[:, :, 0], z], axis=-1)
    bot = jnp.concatenate([z, s[:, :, 1]], axis=-1)
    return jnp.concatenate([top, bot], axis=-2)


def _heads_from_pairs(sp):
    b, npairs = sp.shape[0], sp.shape[1]
    s = jnp.stack([sp[:, :, :HEAD_DIM, :HEAD_DIM], sp[:, :, HEAD_DIM:, HEAD_DIM:]], axis=2)
    return s.reshape(b, 2 * npairs, HEAD_DIM, HEAD_DIM)


def _tile_gain(g, width):
    return jnp.tile(g.astype(F32), width // HEAD_DIM).reshape(1, width)


def _mem_attend(z, memq_cb, km, vm, q_gain):
    t_q = z.shape[1]
    wq = MEM_HEADS * HEAD_DIM
    tq = min(t_q, 512)
    no_sink = jnp.full((1, wq), -jnp.inf, F32)
    return _flash(z, memq_cb, wq, km, 0, vm, 0, _tile_gain(q_gain, LANE), no_sink,
                  mode="all", tq=tq, kb=km.shape[1], group=1, n_valid=km.shape[1])


def _conv_ffn(x2d, b, t, gain, w_up, conv_w, w_down, prev):
    act, u_last = _up_conv(x2d, gain, w_up, conv_w, prev, t)
    return _mm_res([act], [w_down], x2d), u_last


def kernel(x_prompt, x_sample, state_rwkv_wkv, state_rwkv_shift, cache_swa_k, cache_swa_v, cache_dsa_k, cache_dsa_v, cache_dsa_idx_k, cache_mem_k, cache_mem_v, state_ffn_conv, mem_prompt, attn_norm, ffn_norm, mem_norm, mem_w_kv, mem_q_norm, mem_k_norm, a_w_in, a_mu, a_w0, a_w2, a_a0, a_a2, a_g2, a_k_k, a_k_a, a_r_k, a_ln_w, a_ln_b, a_w_out, b_w_in, b_q_norm, b_k_norm, b_sink, b_w_out, c_w_in, c_q_norm, c_k_norm, c_idx_k_norm, c_w_out, ffn_w_up, ffn_conv, ffn_w_down):
    bp, t, d = x_prompt.shape
    bd, s_len = x_sample.shape[:2]
    depth = attn_norm.shape[0]
    win_rows = cache_swa_k.shape[2]
    past = cache_dsa_k.shape[2] if cache_dsa_k.shape[0] else 0
    d_ff = ffn_w_down.shape[1]
    mem_tokens = mem_prompt.shape[1]
    q_cols = d
    kv_cols = ATT_KV_HEADS * HEAD_DIM
    memq_cols = MEM_HEADS * HEAD_DIM
    att_group = (d // HEAD_DIM) // ATT_KV_HEADS
    dec_lora = a_w2.shape[1]
    a_lora = a_a2.shape[1]
    g_lora = a_g2.shape[1]
    rw_cols = 3 * d + dec_lora + a_lora + g_lora
    k_sel_p = min(TOPK_MAX, t // 4)
    k_sel_s = min(TOPK_MAX, (past + s_len) // 4)
    assert g_lora == 2 * LANE and dec_lora <= LANE and a_lora <= LANE

    xp = x_prompt.reshape(bp * t, d)
    xs = x_sample.reshape(bd * s_len, d)

    rw_segments = [(0, 3 * d, 3 * d), (3 * d, dec_lora, LANE), (3 * d + dec_lora, a_lora, LANE),
                   (3 * d + dec_lora + a_lora, g_lora, g_lora)]
    rw_padded = 3 * d + 2 * LANE + g_lora
    o_qi = q_cols + 2 * kv_cols
    o_ki = o_qi + IDX_HEADS * IDX_DIM
    c_cols = o_ki + IDX_DIM + IDX_HEADS

    outs = {k: [] for k in ("p_rw_wkv", "p_rw_sh", "p_sw_k", "p_sw_v", "p_ds_k", "p_ds_v", "p_ds_i", "p_mk",
                            "p_mv", "p_cv", "s_rw_wkv", "s_rw_sh", "s_sw_k", "s_sw_v", "s_ds_k", "s_ds_v",
                            "s_ds_i", "s_cv")}

    def unpad_rw(row):
        return jnp.concatenate([row[..., :3 * d], row[..., 3 * d:3 * d + dec_lora],
                                row[..., 3 * d + LANE:3 * d + LANE + a_lora],
                                row[..., 3 * d + 2 * LANE:3 * d + 2 * LANE + g_lora]], axis=-1)

    for i in range(depth):
        kind, j = i % 3, i // 3
        if kind == 0:
            w_in = jnp.concatenate([_pad_cols(a_w_in[j], rw_segments), a_w_in[j][:, rw_cols:]], axis=1).astype(BF16)
            memq_cb = rw_padded // memq_cols
            zp = _mm_norm(xp, attn_norm[i], w_in).reshape(bp, t, -1)
            zs = _mm_norm(xs, attn_norm[i], w_in).reshape(bd, s_len, -1)
            mu = _pad_cols(a_mu[j].reshape(1, -1), rw_segments)
            mu = jnp.concatenate([mu, jnp.zeros((1, memq_cols), F32)], axis=1)
            vecs = jnp.stack([a_w0[j], a_a0[j], a_k_k[j], a_k_a[j], a_r_k[j].reshape(-1), a_ln_w[j], a_ln_b[j],
                              jnp.zeros((d,), F32)], axis=0)
            w2 = _pad_rows(a_w2[j], LANE).astype(BF16)
            a2 = _pad_rows(a_a2[j], LANE).astype(BF16)
            g2 = a_g2[j].astype(BF16)
            sh_p = jnp.zeros((bp, 1, zp.shape[-1]), F32)
            st_p = jnp.zeros((bp, d // LANE, LANE, LANE), F32)
            sh_s = _pad_cols(state_rwkv_shift[j], rw_segments)
            sh_s = jnp.concatenate([sh_s, jnp.zeros((bd, memq_cols), F32)], axis=1).reshape(bd, 1, -1)
            st_s = _pairs_from_heads(state_rwkv_wkv[j])
            mp, stp = _rwkv(zp, sh_p, st_p, mu, vecs, w2, a2, g2, d)
            ms, sts = _rwkv(zs, sh_s, st_s, mu, vecs, w2, a2, g2, d)
            outs["p_rw_sh"].append(unpad_rw(zp[:, -1]))
            outs["p_rw_wkv"].append(_heads_from_pairs(stp))
            outs["s_rw_sh"].append(unpad_rw(zs[:, -1]))
            outs["s_rw_wkv"].append(_heads_from_pairs(sts))
            w_out = a_w_out[j]
        elif kind == 1:
            w_in = b_w_in[j].astype(BF16)
            memq_cb = (q_cols + 2 * kv_cols) // memq_cols
            k_cb, v_cb = q_cols // kv_cols, q_cols // kv_cols + 1
            zp = _mm_norm(xp, attn_norm[i], w_in).reshape(bp, t, -1)
            zs = _mm_norm(xs, attn_norm[i], w_in).reshape(bd, s_len, -1)
            kgain = _tile_gain(b_k_norm[j], kv_cols)
            qgain = _tile_gain(b_q_norm[j], LANE)
            sink = jnp.repeat(b_sink[j].astype(F32), HEAD_DIM).reshape(1, q_cols)
            knp = _headnorm(zp.reshape(bp * t, -1), k_cb, kv_cols, kgain).reshape(bp, t, kv_cols)
            mp = _flash(zp, 0, q_cols, knp, 0, zp, v_cb, qgain, sink, mode="band", tq=WINDOW, kb=WINDOW,
                        group=att_group)
            outs["p_sw_k"].append(knp[:, t - win_rows:].reshape(bp, win_rows, ATT_KV_HEADS, HEAD_DIM))
            outs["p_sw_v"].append(zp[:, t - win_rows:, q_cols + kv_cols:q_cols + 2 * kv_cols]
                                  .reshape(bp, win_rows, ATT_KV_HEADS, HEAD_DIM))
            kns = _headnorm(zs.reshape(bd * s_len, -1), k_cb, kv_cols, kgain).reshape(bd, s_len, kv_cols)
            vs_new = zs[:, :, q_cols + kv_cols:q_cols + 2 * kv_cols]
            k_all = jnp.concatenate([cache_swa_k[j].reshape(bd, win_rows, kv_cols), kns], axis=1)
            v_all = jnp.concatenate([cache_swa_v[j].reshape(bd, win_rows, kv_cols), vs_new], axis=1)
            n_keys = win_rows + s_len
            n_pad = -(-n_keys // LANE) * LANE
            pad = jnp.zeros((bd, n_pad - n_keys, kv_cols), F32)
            ms = _flash(zs, 0, q_cols, jnp.concatenate([k_all, pad], axis=1), 0,
                        jnp.concatenate([v_all, pad], axis=1), 0, qgain, sink, mode="all", tq=s_len, kb=n_pad,
                        group=att_group, n_valid=n_keys)
            outs["s_sw_k"].append(k_all[:, n_keys - win_rows:].reshape(bd, win_rows, ATT_KV_HEADS, HEAD_DIM))
            outs["s_sw_v"].append(v_all[:, n_keys - win_rows:].reshape(bd, win_rows, ATT_KV_HEADS, HEAD_DIM))
            w_out = b_w_out[j]
        else:
            wc = c_w_in[j]
            w_in = jnp.concatenate([wc[:, :o_ki], wc[:, c_cols:],
                                    _pad_cols(wc, [(o_ki, IDX_DIM + IDX_HEADS, LANE)])], axis=1).astype(BF16)
            memq_cb = o_ki // memq_cols
            kw_cb = (o_ki + memq_cols) // LANE
            k_cb, v_cb = q_cols // kv_cols, q_cols // kv_cols + 1
            qi_cb = o_qi // (IDX_HEADS * IDX_DIM)
            zp = _mm_norm(xp, attn_norm[i], w_in).reshape(bp, t, -1)
            zs = _mm_norm(xs, attn_norm[i], w_in).reshape(bd, s_len, -1)
            kgain = _tile_gain(c_k_norm[j], kv_cols)
            qgain = _tile_gain(c_q_norm[j], LANE)
            igain = jnp.concatenate([c_idx_k_norm[j].astype(F32), jnp.zeros((LANE - IDX_DIM,), F32)]).reshape(1, LANE)
            no_sink = jnp.full((1, q_cols), -jnp.inf, F32)
            knp = _headnorm(zp.reshape(bp * t, -1), k_cb, kv_cols, kgain).reshape(bp, t, kv_cols)
            kip = _headnorm(zp.reshape(bp * t, -1), kw_cb, LANE, igain).reshape(bp, t, LANE)
            tq = min(t, DSA_TQ)
            kb = min(t, DSA_KB)
            sc, tau, cut = _dsa_select(zp, qi_cb, kw_cb, kip, tq=tq, kb=kb, k_sel=k_sel_p, n_valid=t, causal=True)
            ftq, fkb = min(t, DSA_FLASH_TQ), min(t, DSA_FLASH_KB)
            assert all(_causal_blocks(r // tq, tq, kb) * kb >= _causal_blocks(r // ftq, ftq, fkb) * fkb
                       for r in range(0, t, CHUNK))
            mp = _flash(zp, 0, q_cols, knp, 0, zp, v_cb, qgain, no_sink, mode="dsa", tq=ftq, kb=fkb,
                        group=att_group, n_valid=t, causal=True, scores=sc, tau=tau, jb=cut)
            outs["p_ds_k"].append(knp.reshape(bp, t, ATT_KV_HEADS, HEAD_DIM))
            outs["p_ds_v"].append(zp[:, :, q_cols + kv_cols:q_cols + 2 * kv_cols].reshape(bp, t, ATT_KV_HEADS, HEAD_DIM))
            outs["p_ds_i"].append(kip[:, :, :IDX_DIM])
            kns = _headnorm(zs.reshape(bd * s_len, -1), k_cb, kv_cols, kgain).reshape(bd, s_len, kv_cols)
            kis = _headnorm(zs.reshape(bd * s_len, -1), kw_cb, LANE, igain).reshape(bd, s_len, LANE)
            vs_new = zs[:, :, q_cols + kv_cols:q_cols + 2 * kv_cols]
            n_keys = past + s_len
            n_pad = -(-n_keys // LANE) * LANE
            zpad = lambda w: jnp.zeros((bd, n_pad - n_keys, w), F32)
            k_all = jnp.concatenate([cache_dsa_k[j].reshape(bd, past, kv_cols), kns, zpad(kv_cols)], axis=1)
            v_all = jnp.concatenate([cache_dsa_v[j].reshape(bd, past, kv_cols), vs_new, zpad(kv_cols)], axis=1)
            ki_cache = jnp.concatenate([cache_dsa_idx_k[j], jnp.zeros((bd, past, LANE - IDX_DIM), F32)], axis=-1)
            ki_all = jnp.concatenate([ki_cache, kis, zpad(LANE)], axis=1)
            sc, tau, cut = _dsa_select(zs, qi_cb, kw_cb, ki_all, tq=s_len, kb=n_pad, k_sel=k_sel_s, n_valid=n_keys,
                                       causal=False)
            ms = _flash(zs, 0, q_cols, k_all, 0, v_all, 0, qgain, no_sink, mode="dsa", tq=s_len, kb=n_pad,
                        group=att_group, n_valid=n_keys, causal=False, scores=sc, tau=tau, jb=cut)
            outs["s_ds_k"].append(kns.reshape(bd, s_len, ATT_KV_HEADS, HEAD_DIM))
            outs["s_ds_v"].append(vs_new.reshape(bd, s_len, ATT_KV_HEADS, HEAD_DIM))
            outs["s_ds_i"].append(kis[:, :, :IDX_DIM])
            w_out = c_w_out[j]

        kv_mem = _mm_norm(mem_prompt.reshape(bp * mem_tokens, d), mem_norm[i], mem_w_kv[i].astype(BF16))
        km_p = _headnorm(kv_mem, 0, memq_cols, _tile_gain(mem_k_norm[i], memq_cols)).reshape(bp, mem_tokens, memq_cols)
        vm_p = kv_mem[:, memq_cols:].reshape(bp, mem_tokens, memq_cols)
        outs["p_mk"].append(km_p.reshape(bp, mem_tokens, MEM_HEADS, HEAD_DIM))
        outs["p_mv"].append(vm_p.reshape(bp, mem_tokens, MEM_HEADS, HEAD_DIM))
        mo_p = _mem_attend(zp, memq_cb, km_p, vm_p, mem_q_norm[i])
        mo_s = _mem_attend(zs, memq_cb, cache_mem_k[i].reshape(bd, mem_tokens, memq_cols),
                           cache_mem_v[i].reshape(bd, mem_tokens, memq_cols), mem_q_norm[i])
        w_mix, w_mem = w_out[:d].astype(BF16), w_out[d:].astype(BF16)
        xp = _mm_res([mp.reshape(bp * t, d), mo_p.reshape(bp * t, memq_cols)], [w_mix, w_mem], xp)
        xs = _mm_res([ms.reshape(bd * s_len, d), mo_s.reshape(bd * s_len, memq_cols)], [w_mix, w_mem], xs)

        w_up, w_down = ffn_w_up[i].astype(BF16), ffn_w_down[i].astype(BF16)
        xp, cp = _conv_ffn(xp, bp, t, ffn_norm[i], w_up, ffn_conv[i], w_down,
                           jnp.zeros((bp, CONV_W - 1, 2 * d_ff), F32))
        xs, cs = _conv_ffn(xs, bd, s_len, ffn_norm[i], w_up, ffn_conv[i], w_down, state_ffn_conv[i])
        outs["p_cv"].append(cp)
        outs["s_cv"].append(cs)

    st = jnp.stack
    order = ("p_rw_wkv", "p_rw_sh", "p_sw_k", "p_sw_v", "p_ds_k", "p_ds_v", "p_ds_i", "p_mk", "p_mv", "p_cv",
             "s_rw_wkv", "s_rw_sh", "s_sw_k", "s_sw_v", "s_ds_k", "s_ds_v", "s_ds_i", "s_cv")
    return (xp.reshape(bp, t, d), xs.reshape(bd, s_len, d)) + tuple(st(outs[k]) for k in order)
```

```python
import functools
import math

import jax
import jax.numpy as jnp
from jax import lax
from jax.experimental import pallas as pl
from jax.experimental.pallas import tpu as pltpu

F32 = jnp.float32
BF16 = jnp.bfloat16

HEAD_DIM = 64
CHUNK = 64
NORM_EPS = 1e-6
RW_GN_EPS = HEAD_DIM * 1e-5
ATT_KV_HEADS = 4
WINDOW = 128
IDX_HEADS = 8
IDX_DIM = 64
TOPK_MAX = 256
MEM_HEADS = 4
CONV_W = 3

LANE = 128
VMEM_LIMIT = 52 * 1024 * 1024
NEG_BIG = -(2.0 ** 100)
INT_MIN = -2147483648
MM_ROWS = 1024
DSA_TQ = 256
DSA_KB = 512
DSA_FLASH_TQ = 256
DSA_FLASH_KB = 512


def _cparams(sem, vmem=VMEM_LIMIT):
    return pltpu.CompilerParams(dimension_semantics=sem, vmem_limit_bytes=vmem)


def _split3(a):
    a1 = a.astype(BF16)
    r1 = a - a1.astype(F32)
    a2 = r1.astype(BF16)
    r2 = r1 - a2.astype(F32)
    return a1, a2, r2.astype(BF16)


def _dot_exact_rhs(a, e):
    a1, a2, a3 = _split3(a)
    d = lambda x: jnp.dot(x, e, preferred_element_type=F32)
    return (d(a3) + d(a2)) + d(a1)


def _dot_exact_lhs(e, a):
    a1, a2, a3 = _split3(a)
    d = lambda x: jnp.dot(e, x, preferred_element_type=F32)
    return (d(a3) + d(a2)) + d(a1)


def _head_block_matrix(width, value):
    r = lax.broadcasted_iota(jnp.int32, (width, width), 0) // HEAD_DIM
    c = lax.broadcasted_iota(jnp.int32, (width, width), 1) // HEAD_DIM
    return jnp.where(r == c, value, 0.0).astype(BF16)


def _dot_rw(a, b, dims):
    return lax.dot_general(a.astype(BF16), b.astype(BF16), (dims, ((), ())), preferred_element_type=F32)


def _pick_tile(n, cap):
    best = None
    for t in range(LANE, min(n, cap) + 1, LANE):
        if n % t == 0:
            best = t
    assert best is not None, n
    return best


def _mm_norm_body(x_ref, g_ref, w_ref, o_ref, xn_ref):
    @pl.when(pl.program_id(1) == 0)
    def _():
        x = x_ref[...]
        ms = jnp.mean(x * x, axis=-1, keepdims=True)
        xn_ref[...] = ((x * lax.rsqrt(ms + NORM_EPS)) * g_ref[...]).astype(BF16)

    o_ref[...] = jnp.dot(xn_ref[...], w_ref[...], preferred_element_type=F32)


def _mm_norm(x, gain, w):
    m, k = x.shape
    n = w.shape[1]
    tm = min(m, MM_ROWS)
    tn = _pick_tile(n, 1536)
    return pl.pallas_call(
        _mm_norm_body,
        out_shape=jax.ShapeDtypeStruct((m, n), F32),
        grid=(m // tm, n // tn),
        in_specs=[pl.BlockSpec((tm, k), lambda i, j: (i, 0)),
                  pl.BlockSpec((1, k), lambda i, j: (0, 0)),
                  pl.BlockSpec((k, tn), lambda i, j: (0, j))],
        out_specs=pl.BlockSpec((tm, tn), lambda i, j: (i, j)),
        scratch_shapes=[pltpu.VMEM((tm, k), BF16)],
        compiler_params=_cparams(("parallel", "arbitrary")),
        name="mm_norm",
    )(x, gain.reshape(1, k), w)


def _mm_res_body(*refs, n_lhs):
    lhs = refs[:n_lhs]
    ws = refs[n_lhs:2 * n_lhs]
    r_ref, o_ref = refs[2 * n_lhs], refs[2 * n_lhs + 1]
    acc = jnp.dot(lhs[0][...], ws[0][...], preferred_element_type=F32)
    for a, w in zip(lhs[1:], ws[1:]):
        acc = acc + jnp.dot(a[...], w[...], preferred_element_type=F32)
    o_ref[...] = r_ref[...] + acc


def _mm_res(lhs_list, w_list, res):
    m, n = res.shape
    ktot = sum(a.shape[1] for a in lhs_list)
    tm = min(m, MM_ROWS)
    tn = _pick_tile(n, 1024 if ktot <= 3072 else 512)
    n_lhs = len(lhs_list)
    in_specs = [pl.BlockSpec((tm, a.shape[1]), lambda i, j: (i, 0)) for a in lhs_list]
    in_specs += [pl.BlockSpec((w.shape[0], tn), lambda i, j: (0, j)) for w in w_list]
    in_specs += [pl.BlockSpec((tm, tn), lambda i, j: (i, j))]
    return pl.pallas_call(
        functools.partial(_mm_res_body, n_lhs=n_lhs),
        out_shape=jax.ShapeDtypeStruct((m, n), F32),
        grid=(m // tm, n // tn),
        in_specs=in_specs,
        out_specs=pl.BlockSpec((tm, tn), lambda i, j: (i, j)),
        compiler_params=_cparams(("parallel", "arbitrary")),
        name="mm_res",
    )(*lhs_list, *w_list, res)


def _headnorm_body(x_ref, g_ref, o_ref, *, width):
    avg = _head_block_matrix(LANE, 1.0 / HEAD_DIM)
    for c in range(width // LANE):
        x = x_ref[:, c * LANE:(c + 1) * LANE]
        ms = _dot_exact_rhs(x * x, avg)
        o_ref[:, c * LANE:(c + 1) * LANE] = (x * lax.rsqrt(ms + NORM_EPS)) * g_ref[:, c * LANE:(c + 1) * LANE]


def _headnorm(x, col_block, width, gain_row):
    m = x.shape[0]
    tm = min(m, 1024)
    return pl.pallas_call(
        functools.partial(_headnorm_body, width=width),
        out_shape=jax.ShapeDtypeStruct((m, width), F32),
        grid=(m // tm,),
        in_specs=[pl.BlockSpec((tm, width), lambda i: (i, col_block)),
                  pl.BlockSpec((1, width), lambda i: (0, 0))],
        out_specs=pl.BlockSpec((tm, width), lambda i: (i, 0)),
        compiler_params=_cparams(("parallel",)),
        name="headnorm",
    )(x, gain_row)


def _up_conv_body(x_ref, g_ref, wa_ref, wb_ref, pa_ref, pb_ref, cwa_ref, cwb_ref, o_ref, la_ref, lb_ref,
                  xn_ref, ca_ref, cb_ref, *, tm, tiles_per_batch, bpt):
    i = pl.program_id(0)
    j = pl.program_id(1)
    rpb = tm // bpt

    @pl.when(j == 0)
    def _():
        x = x_ref[...]
        ms = jnp.mean(x * x, axis=-1, keepdims=True)
        xn_ref[...] = ((x * lax.rsqrt(ms + NORM_EPS)) * g_ref[...]).astype(BF16)

    if bpt == 1:
        @pl.when(i % tiles_per_batch == 0)
        def _():
            ca_ref[j] = pa_ref[0]
            cb_ref[j] = pb_ref[0]

    xn = xn_ref[...]
    row = lax.broadcasted_iota(jnp.int32, o_ref.shape, 0)
    off = row if bpt == 1 else row % rpb
    if bpt > 1:
        pick = (lax.broadcasted_iota(jnp.int32, (tm, bpt), 0) // rpb
                == lax.broadcasted_iota(jnp.int32, (tm, bpt), 1)).astype(BF16)

    def conv(w_ref, p_ref, c_ref, cw_ref, last_ref):
        u = jnp.dot(xn, w_ref[...], preferred_element_type=F32)
        if bpt == 1:
            car = c_ref[j]
            c0, c1 = car[0:1, :], car[1:2, :]
            c_ref[j] = u[tm - 2:tm, :]
            last_ref[0] = u[tm - 2:tm, :]
        else:
            c0 = _dot_exact_lhs(pick, p_ref[:, 0, :])
            c1 = _dot_exact_lhs(pick, p_ref[:, 1, :])
            for bi in range(bpt):
                last_ref[bi] = u[(bi + 1) * rpb - 2:(bi + 1) * rpb, :]
        u1 = jnp.where(off == 0, c1, pltpu.roll(u, 1, axis=0))
        u2 = jnp.where(off == 0, c0, jnp.where(off == 1, c1, pltpu.roll(u, 2, axis=0)))
        return (u2 * cw_ref[0:1, :] + u1 * cw_ref[1:2, :]) + u * cw_ref[2:3, :]

    a = conv(wa_ref, pa_ref, ca_ref, cwa_ref, la_ref)
    b = conv(wb_ref, pb_ref, cb_ref, cwb_ref, lb_ref)
    o_ref[...] = ((a * jax.nn.sigmoid(a)) * b).astype(o_ref.dtype)


def _up_conv(x, gain, w_up, conv_w, prev, t):
    m, k = x.shape
    f = w_up.shape[1] // 2
    b = m // t
    tm = min(m, MM_ROWS)
    bpt = max(1, tm // t)
    assert tm % t == 0 or t % tm == 0
    tn = _pick_tile(f, 512)
    nj = f // tn
    tpb = max(1, t // tm)
    act, la, lb = pl.pallas_call(
        functools.partial(_up_conv_body, tm=tm, tiles_per_batch=tpb, bpt=bpt),
        out_shape=(jax.ShapeDtypeStruct((m, f), BF16),
                   jax.ShapeDtypeStruct((b, CONV_W - 1, f), F32),
                   jax.ShapeDtypeStruct((b, CONV_W - 1, f), F32)),
        grid=(m // tm, nj),
        in_specs=[pl.BlockSpec((tm, k), lambda i, j: (i, 0)),
                  pl.BlockSpec((1, k), lambda i, j: (0, 0)),
                  pl.BlockSpec((k, tn), lambda i, j: (0, j)),
                  pl.BlockSpec((k, tn), lambda i, j: (0, nj + j)),
                  pl.BlockSpec((bpt, CONV_W - 1, tn), lambda i, j: (i // tpb, 0, j)),
                  pl.BlockSpec((bpt, CONV_W - 1, tn), lambda i, j: (i // tpb, 0, nj + j)),
                  pl.BlockSpec((CONV_W, tn), lambda i, j: (0, j)),
                  pl.BlockSpec((CONV_W, tn), lambda i, j: (0, nj + j))],
        out_specs=(pl.BlockSpec((tm, tn), lambda i, j: (i, j)),
                   pl.BlockSpec((bpt, CONV_W - 1, tn), lambda i, j: (i // tpb, 0, j)),
                   pl.BlockSpec((bpt, CONV_W - 1, tn), lambda i, j: (i // tpb, 0, j))),
        scratch_shapes=[pltpu.VMEM((tm, k), BF16),
                        pltpu.VMEM((nj, CONV_W - 1, tn), F32),
                        pltpu.VMEM((nj, CONV_W - 1, tn), F32)],
        compiler_params=_cparams(("arbitrary", "arbitrary")),
        name="up_conv_gate",
    )(x, gain.reshape(1, k), w_up, w_up, prev, prev, conv_w, conv_w)
    return act, jnp.concatenate([la, lb], axis=-1)


def _causal_blocks(i, tq, kb):
    return ((i + 1) * tq + kb - 1) // kb


def _fold_steps(nq, tq, kb):
    if nq % 2:
        return None
    return max(_causal_blocks(r, tq, kb) + _causal_blocks(nq - 1 - r, tq, kb) for r in range(nq // 2))


def _fold_step(r, step, nq, tq, kb):
    n_lo = _causal_blocks(r, tq, kb)
    hi = nq - 1 - r
    in_lo = step < n_lo
    i = jnp.where(in_lo, r, hi)
    last_blk = _causal_blocks(i, tq, kb) - 1
    j = jnp.minimum(jnp.where(in_lo, step, step - n_lo), last_blk)
    active = step < n_lo + _causal_blocks(hi, tq, kb)
    return i, j, active & (jnp.where(in_lo, step, step - n_lo) == 0), active & (j == last_blk), active


def _flash_body(*refs, mode, tq, kb, wq, group, n_valid, nk, causal, fold_nq):
    q_ref, k_ref, v_ref, qg_ref, sink_ref = refs[:5]
    pos = 5
    if mode == "dsa":
        sc_ref, tau_ref, jb_ref = refs[pos:pos + 3]
        pos += 3
    o_ref = refs[pos]
    qn_scr, acc_scr, m_scr, l_scr, bias_scr = refs[pos + 1:]

    if fold_nq:
        i, j, is_first, is_last, fold_active = _fold_step(pl.program_id(1), pl.program_id(2), fold_nq, tq, kb)
    else:
        i, j = pl.program_id(1), pl.program_id(2)
        is_first, is_last = j == 0, j == nk - 1
    npairs = wq // LANE
    pairs_per_unit = group // 2 if group > 1 else 1
    lane = lax.broadcasted_iota(jnp.int32, (1, LANE), 1)
    lo_half = lane < HEAD_DIM

    @pl.when(is_first)
    def _():
        avg = _head_block_matrix(LANE, 1.0 / HEAD_DIM)
        for p in range(npairs):
            x = q_ref[0, :, p * LANE:(p + 1) * LANE]
            ms = jnp.dot((x * x).astype(BF16), avg, preferred_element_type=F32)
            qn = ((x * lax.rsqrt(ms + NORM_EPS)) * qg_ref[...]) * (HEAD_DIM ** -0.5)
            qn_scr[p] = qn.astype(BF16)
        acc_scr[...] = jnp.zeros(acc_scr.shape, F32)
        l_scr[...] = jnp.zeros(l_scr.shape, F32)
        m_scr[...] = jnp.full(m_scr.shape, NEG_BIG, F32)

    if mode == "band":
        kblk = i - WINDOW // kb + j
        active = kblk >= 0
    elif mode == "dsa" and causal:
        kblk = j
        active = fold_active if fold_nq else j * kb < (i + 1) * tq
    else:
        kblk = j
        active = None

    def step():
        kidx = kblk * kb + lax.broadcasted_iota(jnp.int32, (tq, kb), 1)
        qpos = i * tq + lax.broadcasted_iota(jnp.int32, (tq, kb), 0)
        if mode == "band":
            qchunk = qpos // CHUNK
            sel = (kidx >= (qchunk - WINDOW // CHUNK) * CHUNK) & (kidx < (qchunk + 1) * CHUNK) & (kidx >= 0)
        elif mode == "dsa":
            x = sc_ref[0]
            tau = tau_ref[0][:, 0:1]
            jb = jb_ref[0][:, 0:1]
            adm = kidx < ((qpos // CHUNK + 1) * CHUNK if causal else n_valid)
            sel = adm & ((x > tau) | ((x == tau) & (kidx <= jb)))
        else:
            sel = kidx < n_valid
        bias_scr[...] = jnp.where(sel, 0.0, NEG_BIG).astype(BF16)

        ppu = pairs_per_unit
        for u in range(npairs // ppu):
            ps = slice(u * ppu, (u + 1) * ppu)
            s = lax.dot_general(qn_scr[ps].reshape(ppu * tq, LANE), k_ref[0, 0, u], (((1,), (1,)), ((), ())),
                                preferred_element_type=F32).reshape(ppu, tq, 2 * kb)
            bias = bias_scr[...][None]
            m_old = m_scr[ps]
            s0 = s[:, :, 0:kb].astype(BF16) + bias
            s1 = s[:, :, kb:2 * kb].astype(BF16) + bias
            mn0 = jnp.maximum(m_old[:, :, 0:1], jnp.max(s0, axis=2, keepdims=True).astype(F32))
            mn1 = jnp.maximum(m_old[:, :, HEAD_DIM:HEAD_DIM + 1], jnp.max(s1, axis=2, keepdims=True).astype(F32))
            p0 = jnp.exp(s0 - mn0.astype(BF16))
            p1 = jnp.exp(s1 - mn1.astype(BF16))
            pcat = jnp.concatenate([p0, p1], axis=2).reshape(ppu * tq, 2 * kb)
            pv = jnp.dot(pcat, v_ref[0, 0, u], preferred_element_type=F32)
            pv = pv.reshape(ppu, tq, 2 * LANE)
            mn = jnp.where(lo_half, mn0, mn1)
            alpha = jnp.exp(m_old - mn)
            acc_scr[ps] = acc_scr[ps] * alpha + pv[:, :, 0:LANE]
            l_scr[ps] = l_scr[ps] * alpha + pv[:, :, LANE:2 * LANE]
            m_scr[ps] = mn

    if active is None:
        step()
    else:
        pl.when(active)(step)

    @pl.when(is_last)
    def _():
        for p in range(npairs):
            den = l_scr[p] + jnp.exp(sink_ref[:, p * LANE:(p + 1) * LANE] - m_scr[p])
            o_ref[0, :, p * LANE:(p + 1) * LANE] = (acc_scr[p] / den).astype(o_ref.dtype)


def _pair_operands_body(k_ref, v_ref, kab_ref, vab_ref, *, kb, group, n_units):
    lane = lax.broadcasted_iota(jnp.int32, (1, LANE), 1)
    lo_half = lane < HEAD_DIM
    ones_lo = jnp.broadcast_to(jnp.where(lo_half, 1.0, 0.0), (kb, LANE)).astype(BF16)
    ones_hi = jnp.broadcast_to(jnp.where(lo_half, 0.0, 1.0), (kb, LANE)).astype(BF16)
    for u in range(n_units):
        if group > 1:
            tile_idx, half = u // 2, u % 2
        else:
            tile_idx, half = u, None
        for src, dst in ((k_ref, kab_ref), (v_ref, vab_ref)):
            tile = src[0, :, tile_idx * LANE:(tile_idx + 1) * LANE]
            if half is None:
                a_part = jnp.where(lo_half, tile, 0.0)
                b_part = jnp.where(lo_half, 0.0, tile)
            elif half == 0:
                a_part = jnp.where(lo_half, tile, 0.0)
                b_part = pltpu.roll(a_part, HEAD_DIM, axis=1)
            else:
                b_part = jnp.where(lo_half, 0.0, tile)
                a_part = pltpu.roll(b_part, HEAD_DIM, axis=1)
            dst[0, 0, u, 0:kb, 0:LANE] = a_part.astype(BF16)
            dst[0, 0, u, kb:2 * kb, 0:LANE] = b_part.astype(BF16)
        vab_ref[0, 0, u, 0:kb, LANE:2 * LANE] = ones_lo
        vab_ref[0, 0, u, kb:2 * kb, LANE:2 * LANE] = ones_hi


def _pair_operands(k_arr, k_cb, v_arr, v_cb, kb, group, npairs):
    b, t_k = k_arr.shape[0], k_arr.shape[1]
    wk = ATT_KV_HEADS * HEAD_DIM
    n_units = npairs // (group // 2) if group > 1 else npairs
    nkb = t_k // kb
    return pl.pallas_call(
        functools.partial(_pair_operands_body, kb=kb, group=group, n_units=n_units),
        out_shape=(jax.ShapeDtypeStruct((b, nkb, n_units, 2 * kb, LANE), BF16),
                   jax.ShapeDtypeStruct((b, nkb, n_units, 2 * kb, 2 * LANE), BF16)),
        grid=(b, nkb),
        in_specs=[pl.BlockSpec((1, kb, wk), lambda bi, j: (bi, j, k_cb)),
                  pl.BlockSpec((1, kb, wk), lambda bi, j: (bi, j, v_cb))],
        out_specs=(pl.BlockSpec((1, 1, n_units, 2 * kb, LANE), lambda bi, j: (bi, j, 0, 0, 0)),
                   pl.BlockSpec((1, 1, n_units, 2 * kb, 2 * LANE), lambda bi, j: (bi, j, 0, 0, 0))),
        compiler_params=_cparams(("parallel", "parallel")),
        name="pair_operands",
    )(k_arr, v_arr)


def _flash(q_arr, q_cb, wq, k_arr, k_cb, v_arr, v_cb, q_gain, sink_row, *, mode, tq, kb, group,
           n_valid=None, causal=False, scores=None, tau=None, jb=None):
    b, t_q = q_arr.shape[0], q_arr.shape[1]
    t_k = k_arr.shape[1]
    nq = t_q // tq
    if mode == "band":
        assert tq == kb == WINDOW
        nk = 2
        kmap = lambda i, j: jnp.maximum(i - 1 + j, 0)
    elif mode == "dsa" and causal:
        nk = t_k // kb
        kmap = lambda i, j: jnp.minimum(j, ((i + 1) * tq - 1) // kb)
    else:
        nk = t_k // kb
        kmap = lambda i, j: j
    grid = (b, nq, nk)
    qmap = lambda i, j: i
    fold = _fold_steps(nq, tq, kb) if (mode == "dsa" and causal) else None
    if fold:
        grid = (b, nq // 2, fold)
        qmap = lambda r, s: _fold_step(r, s, nq, tq, kb)[0]
        kmap = lambda r, s: _fold_step(r, s, nq, tq, kb)[1]
    npairs = wq // LANE
    kab, vab = _pair_operands(k_arr, k_cb, v_arr, v_cb, kb, group, npairs)
    n_units = kab.shape[2]
    in_specs = [pl.BlockSpec((1, tq, wq), lambda bi, i, j: (bi, qmap(i, j), q_cb)),
                pl.BlockSpec((1, 1, n_units, 2 * kb, LANE), lambda bi, i, j: (bi, kmap(i, j), 0, 0, 0)),
                pl.BlockSpec((1, 1, n_units, 2 * kb, 2 * LANE), lambda bi, i, j: (bi, kmap(i, j), 0, 0, 0)),
                pl.BlockSpec((1, LANE), lambda bi, i, j: (0, 0)),
                pl.BlockSpec((1, wq), lambda bi, i, j: (0, 0))]
    args = [q_arr, kab, vab, q_gain, sink_row]
    if mode == "dsa":
        in_specs += [pl.BlockSpec((1, tq, kb), lambda bi, i, j: (bi, qmap(i, j), kmap(i, j))),
                     pl.BlockSpec((1, tq, LANE), lambda bi, i, j: (bi, qmap(i, j), 0)),
                     pl.BlockSpec((1, tq, LANE), lambda bi, i, j: (bi, qmap(i, j), 0))]
        args += [scores, tau, jb]
    body = functools.partial(_flash_body, mode=mode, tq=tq, kb=kb, wq=wq, group=group,
                             n_valid=n_valid, nk=nk, causal=causal, fold_nq=nq if fold else 0)
    return pl.pallas_call(
        body,
        out_shape=jax.ShapeDtypeStruct((b, t_q, wq), BF16),
        grid=grid,
        in_specs=in_specs,
        out_specs=pl.BlockSpec((1, tq, wq), lambda bi, i, j: (bi, qmap(i, j), 0)),
        scratch_shapes=[pltpu.VMEM((npairs, tq, LANE), BF16),
                        pltpu.VMEM((npairs, tq, LANE), F32),
                        pltpu.VMEM((npairs, tq, LANE), F32),
                        pltpu.VMEM((npairs, tq, LANE), F32),
                        pltpu.VMEM((tq, kb), BF16)],
        compiler_params=_cparams(("parallel", "parallel", "arbitrary")),
        name="flash_" + mode,
    )(*args)


def _dsa_select_body(qi_ref, kw_ref, ki_ref, sc_ref, tau_ref, jb_ref, sc_scr, cnt_scr, ebuf_scr, *, tq, kb, nk, k_sel,
                     n_valid, causal, fold_nq):
    if fold_nq:
        i, j, _, is_last, fold_active = _fold_step(pl.program_id(1), pl.program_id(2), fold_nq, tq, kb)
    else:
        i, j = pl.program_id(1), pl.program_id(2)
        is_last = j == nk - 1
    qpos = i * tq + lax.broadcasted_iota(jnp.int32, (tq, 1), 0)
    if causal:
        n_adm = (qpos // CHUNK + 1) * CHUNK
        active = fold_active if fold_nq else j * kb < (i + 1) * tq
        n_blocks = _causal_blocks(i, tq, kb)
    else:
        n_adm = jnp.full((tq, 1), n_valid, jnp.int32)
        active = None
        n_blocks = nk

    def compute():
        ka = ki_ref[0]
        kab = jnp.concatenate([ka, pltpu.roll(ka, HEAD_DIM, axis=1)], axis=0).astype(BF16)
        kw = kw_ref[0]
        acc = jnp.zeros((tq, kb), F32)
        q4 = jnp.concatenate([qi_ref[0, :, p * LANE:(p + 1) * LANE].astype(BF16) for p in range(IDX_HEADS // 2)],
                             axis=0)
        s = lax.dot_general(q4, kab, (((1,), (1,)), ((), ())), preferred_element_type=F32)
        for p in range(IDX_HEADS // 2):
            for h in range(2):
                col = HEAD_DIM + 2 * p + h
                w = (kw[:, col:col + 1] * IDX_HEADS ** -0.5) * IDX_DIM ** -0.5
                acc = acc + jnp.maximum(s[p * tq:(p + 1) * tq, h * kb:(h + 1) * kb], 0.0) * w
        acc = jnp.where(acc == 0.0, 0.0, acc)
        kidx = j * kb + lax.broadcasted_iota(jnp.int32, (tq, kb), 1)
        x = jnp.where(kidx < n_adm, acc, -jnp.inf)
        sc_scr[j] = x
        sc_ref[0] = x

    if active is None:
        compute()
    else:
        pl.when(active)(compute)

        if not fold_nq:
            @pl.when(jnp.logical_not(active))
            def _():
                sc_ref[0] = jnp.full((tq, kb), -jnp.inf, F32)

    @pl.when(is_last)
    def _():
        kf = float(k_sel)

        rsz = min(tq, LANE)
        lane_tiles = range(kb // LANE)

        def count(make_pred):
            starts = range(0, tq, rsz)
            preds = [make_pred(lambda col, r0=r0: jnp.broadcast_to(col[r0:r0 + rsz], (rsz, LANE))) for r0 in starts]
            accs = []
            for r0, pred in zip(starts, preds):
                def blk(jj, acc, r0=r0, pred=pred):
                    for a in lane_tiles:
                        x = sc_scr[jj, r0:r0 + rsz, a * LANE:(a + 1) * LANE]
                        acc = acc + jnp.where(pred(x), 1.0, 0.0)
                    return acc
                accs.append(lax.fori_loop(0, n_blocks, blk, jnp.zeros((rsz, LANE), F32)))
            return jnp.sum(jnp.concatenate(accs, axis=0), axis=1, keepdims=True)

        def key_to_float(tu):
            cs = tu ^ INT_MIN
            fb = jnp.where(cs >= 0, cs, cs ^ 0x7FFFFFFF)
            return lax.bitcast_convert_type(fb, F32)

        def bit_body(it, carry):
            tu, c_at = carry
            cand = tu | jnp.left_shift(jnp.int32(1), 31 - it)
            thr = key_to_float(cand)
            c = count(lambda widen: (lambda x, t=widen(thr): x >= t))
            keep = c >= kf
            return jnp.where(keep, cand, tu), jnp.where(keep, c, c_at)

        tu, c_ge = lax.fori_loop(0, 32, bit_body, (jnp.zeros((tq, 1), jnp.int32),
                                                   jnp.full((tq, 1), float(nk * kb), F32)))
        full = n_adm >= k_sel
        tau = jnp.where(full, key_to_float(tu), -jnp.inf)
        tau = jnp.where(jnp.abs(tau) < 1.1754944e-38, 0.0, tau)
        c_gt = count(lambda widen: (lambda x, t=widen(tau): x > t))
        need = kf - c_gt
        tau_ref[0] = jnp.broadcast_to(tau, (tq, LANE))
        jb_ref[0] = jnp.full((tq, LANE), 2 ** 30, jnp.int32)
        surplus = jnp.max(jnp.where(full, (c_ge - c_gt) - need, 0.0))

        @pl.when(surplus > 0.5)
        def _():
            tau_b = jnp.broadcast_to(tau, (tq, LANE))
            need_b = jnp.broadcast_to(need, (tq, LANE))
            ones_mat = jnp.ones((LANE, LANE), BF16)
            row_sum = lambda part: jnp.dot(part.astype(BF16), ones_mat, preferred_element_type=F32)

            def ties(jj, a):
                return jnp.where(sc_scr[jj, :, a * LANE:(a + 1) * LANE] == tau_b, 1.0, 0.0)

            def blk_count(jj, carry):
                part = ties(jj, 0)
                for a in lane_tiles[1:]:
                    part = part + ties(jj, a)
                cnt_scr[jj] = row_sum(part)
                return carry
            lax.fori_loop(0, n_blocks, blk_count, 0)

            def locate(jj, carry):
                run, blk_idx, before = carry
                run = run + cnt_scr[jj]
                below = run < need_b
                return run, blk_idx + jnp.where(below, 1, 0), jnp.where(below, run, before)
            zero = jnp.zeros((tq, LANE), F32)
            _, blk_idx, before = lax.fori_loop(0, n_blocks, locate, (zero, jnp.zeros((tq, LANE), jnp.int32), zero))
            need_in = need_b - before

            ebuf_scr[...] = jnp.zeros((tq, kb), F32)

            def extract(jj, carry):
                here = blk_idx == jj
                for a in lane_tiles:
                    cols = slice(a * LANE, (a + 1) * LANE)
                    ebuf_scr[:, cols] = jnp.where(here, ties(jj, a), ebuf_scr[:, cols])
                return carry
            lax.fori_loop(0, n_blocks, extract, 0)

            lane_idx = lax.broadcasted_iota(jnp.int32, (tq, LANE), 1)
            cut_bits = max(1, (kb - 1).bit_length())

            def cut_body(it, cut):
                cand = cut | jnp.left_shift(jnp.int32(1), cut_bits - 1 - it)
                part = zero
                for a in lane_tiles:
                    part = part + jnp.where(a * LANE + lane_idx < cand, ebuf_scr[:, a * LANE:(a + 1) * LANE], 0.0)
                return jnp.where(row_sum(part) < need_in, cand, cut)
            cut = lax.fori_loop(0, cut_bits, cut_body, jnp.zeros((tq, LANE), jnp.int32))
            jb_ref[0] = blk_idx * kb + cut


def _dsa_select(z, qi_cb, kw_cb, ki_n, *, tq, kb, k_sel, n_valid, causal):
    b, t_q = z.shape[0], z.shape[1]
    t_k = ki_n.shape[1]
    nq, nk = t_q // tq, t_k // kb
    assert kb // LANE <= 256, "per-lane tie counts of one key block go through a bf16 matmul, exact up to 256"
    kmap = (lambda i, j: jnp.minimum(j, ((i + 1) * tq - 1) // kb)) if causal else (lambda i, j: j)
    omap = lambda i, j: j
    qmap = lambda i, j: i
    grid = (b, nq, nk)
    fold = _fold_steps(nq, tq, kb) if causal else None
    if fold:
        grid = (b, nq // 2, fold)
        qmap = lambda r, s: _fold_step(r, s, nq, tq, kb)[0]
        kmap = omap = lambda r, s: _fold_step(r, s, nq, tq, kb)[1]
    body = functools.partial(_dsa_select_body, tq=tq, kb=kb, nk=nk, k_sel=k_sel, n_valid=n_valid,
                             causal=causal, fold_nq=nq if fold else 0)
    return pl.pallas_call(
        body,
        out_shape=(jax.ShapeDtypeStruct((b, t_q, t_k), F32),
                   jax.ShapeDtypeStruct((b, t_q, LANE), F32),
                   jax.ShapeDtypeStruct((b, t_q, LANE), jnp.int32)),
        grid=grid,
        in_specs=[pl.BlockSpec((1, tq, IDX_HEADS * IDX_DIM), lambda bi, i, j: (bi, qmap(i, j), qi_cb)),
                  pl.BlockSpec((1, tq, LANE), lambda bi, i, j: (bi, qmap(i, j), kw_cb)),
                  pl.BlockSpec((1, kb, LANE), lambda bi, i, j: (bi, kmap(i, j), 0))],
        out_specs=(pl.BlockSpec((1, tq, kb), lambda bi, i, j: (bi, qmap(i, j), omap(i, j))),
                   pl.BlockSpec((1, tq, LANE), lambda bi, i, j: (bi, qmap(i, j), 0)),
                   pl.BlockSpec((1, tq, LANE), lambda bi, i, j: (bi, qmap(i, j), 0))),
        scratch_shapes=[pltpu.VMEM((nk, tq, kb), F32), pltpu.VMEM((nk, tq, LANE), F32), pltpu.VMEM((tq, kb), F32)],
        compiler_params=_cparams(("parallel", "parallel", "arbitrary")),
        name="dsa_select",
    )(z, z, ki_n)


def _rwkv_body(zr_ref, zk_ref, zv_ref, zw_ref, za_ref, zg_ref,
               sr_ref, sk_ref, sv_ref, sw_ref, sa_ref, sg_ref,
               mr_ref, mk_ref, mv_ref, mw_ref, ma_ref, mg_ref,
               vec_ref, w2_ref, a2_ref, g2_ref, s0_ref,
               mix_ref, sfin_ref,
               s_scr, prev_scr, prevl_scr, prevg_scr, y_scr, *, tc, lc, nt, pp):
    t = pl.program_id(2)

    @pl.when(t == 0)
    def _():
        s_scr[...] = s0_ref[0]
        prev_scr[0:1, :] = sr_ref[0]
        prev_scr[1:2, :] = sk_ref[0]
        prev_scr[2:3, :] = sv_ref[0]
        prevl_scr[0:1, :] = sw_ref[0]
        prevl_scr[1:2, :] = sa_ref[0]
        prevg_scr[0:1, :] = sg_ref[0]

    row = lax.broadcasted_iota(jnp.int32, (tc, 1), 0)

    def shifted(z_ref, prow, mu_ref):
        z = z_ref[0]
        zp = jnp.where(row == 0, prow, pltpu.roll(z, 1, axis=0))
        return z + (zp - z) * mu_ref[...], z[tc - 1:tc, :]

    r, last_r = shifted(zr_ref, prev_scr[0:1, :], mr_ref)
    k, last_k = shifted(zk_ref, prev_scr[1:2, :], mk_ref)
    v, last_v = shifted(zv_ref, prev_scr[2:3, :], mv_ref)
    zw, last_w = shifted(zw_ref, prevl_scr[0:1, :], mw_ref)
    za, last_a = shifted(za_ref, prevl_scr[1:2, :], ma_ref)
    zg, last_g = shifted(zg_ref, prevg_scr[0:1, :], mg_ref)
    prev_scr[0:1, :] = last_r
    prev_scr[1:2, :] = last_k
    prev_scr[2:3, :] = last_v
    prevl_scr[0:1, :] = last_w
    prevl_scr[1:2, :] = last_a
    prevg_scr[0:1, :] = last_g

    w0, a0 = vec_ref[0:1, :], vec_ref[1:2, :]
    k_k, k_a, r_k = vec_ref[2:3, :], vec_ref[3:4, :], vec_ref[4:5, :]
    ln_w, ln_b = vec_ref[5:6, :], vec_ref[6:7, :]

    ones_blk = _head_block_matrix(LANE, 1.0)
    avg_blk = _head_block_matrix(LANE, 1.0 / HEAD_DIM)

    def per_head(x, blk):
        return jnp.concatenate([jnp.dot(x[:, i * LANE:(i + 1) * LANE].astype(BF16), blk, preferred_element_type=F32)
                                for i in range(pp)], axis=1)

    xw = w0 + jnp.dot(jnp.tanh(zw).astype(BF16), w2_ref[...], preferred_element_type=F32)
    lw = -math.exp(-0.5) * jax.nn.sigmoid(xw)
    a = jax.nn.sigmoid(a0 + jnp.dot(za.astype(BF16), a2_ref[...], preferred_element_type=F32))
    g = jnp.dot(jax.nn.sigmoid(zg).astype(BF16), g2_ref[...], preferred_element_type=F32)
    kk = k * k_k
    kk = kk * jnp.minimum(lax.rsqrt(per_head(kk * kk, ones_blk)), 1e12)
    k2 = k * (1.0 + (a - 1.0) * k_a)

    a_step = -kk
    b_step = kk * a

    lane = lax.broadcasted_iota(jnp.int32, (1, LANE), 1)
    m0 = lane < HEAD_DIM
    rr = lax.broadcasted_iota(jnp.int32, (lc, 2 * lc), 0)
    cc = lax.broadcasted_iota(jnp.int32, (lc, 2 * lc), 1)
    incl = jnp.where(cc < lc, cc, cc - lc) <= rr
    strict = (lax.broadcasted_iota(jnp.int32, (2 * lc, 2 * lc), 1)
              < lax.broadcasted_iota(jnp.int32, (2 * lc, 2 * lc), 0))
    tri = (lax.broadcasted_iota(jnp.int32, (lc, lc), 1)
           <= lax.broadcasted_iota(jnp.int32, (lc, lc), 0)).astype(BF16)
    nsteps = lc.bit_length() - 1
    nt_dims = ((1,), (1,))
    nn_dims = ((1,), (0,))
    tn_dims = ((0,), (0,))

    chunks = range(tc // lc)
    st = []
    for c in chunks:
        rows = slice(c * lc, (c + 1) * lc)
        lwc = lw[rows]
        cs = _dot_exact_lhs(tri, lwc)
        p_in = jnp.exp(-cs)
        at = a_step[rows] * jnp.exp(cs - lwc)
        bt = b_step[rows] * p_in
        kt = k2[rows] * p_in
        rt = r[rows] * jnp.exp(cs)
        vc = v[rows]
        p_last = jnp.exp(cs[lc - 1:lc, :])
        for pi in range(pp):
            cols = slice(pi * LANE, (pi + 1) * LANE)
            stack = lambda x: jnp.concatenate([jnp.where(m0, x[:, cols], 0.0),
                                               jnp.where(m0, 0.0, x[:, cols])], axis=0).astype(BF16)
            st.append(dict(c=c, pi=pi,
                           ar=jnp.concatenate([stack(at), rt[:, cols].astype(BF16)], axis=0),
                           bk=jnp.concatenate([stack(bt), stack(kt)], axis=0),
                           v_s=stack(vc), p_last=p_last[:, cols]))
    for d in st:
        gram = _dot_rw(d["ar"], d["bk"], nt_dims)
        d["t_p"] = jnp.where(strict, gram[0:2 * lc, 0:2 * lc], 0.0)
        d["w_ak"] = jnp.where(strict, gram[0:2 * lc, 2 * lc:4 * lc], 0.0).astype(BF16)
        d["w_rb"] = jnp.where(incl, gram[2 * lc:3 * lc, 0:2 * lc], 0.0).astype(BF16)
        d["w_rk"] = jnp.where(incl, gram[2 * lc:3 * lc, 2 * lc:4 * lc], 0.0).astype(BF16)
    gw = 4 * lc
    lane_head = lax.broadcasted_iota(jnp.int32, (1, gw), 1) // lc
    eye_sbs = (lax.broadcasted_iota(jnp.int32, (lc, gw), 0)
               == lax.broadcasted_iota(jnp.int32, (lc, gw), 1) % lc).astype(F32)
    lane_pair = lax.broadcasted_iota(jnp.int32, (1, 2 * lc), 1)

    def blockdiag(y):
        yb = y.astype(BF16)
        return jnp.concatenate([jnp.where(lane_head == h, yb, jnp.zeros_like(yb)) for h in range(4)], axis=0)

    groups = []
    for gi in range(len(st) // 2):
        members = (st[2 * gi], st[2 * gi + 1])
        t_sbs = jnp.concatenate([m["t_p"][0:lc] + m["t_p"][lc:2 * lc] for m in members], axis=1)
        groups.append(dict(members=members, t=t_sbs, minv=eye_sbs + t_sbs))
    for grp in groups:
        grp["t"] = _dot_rw(grp["t"], blockdiag(grp["t"]), nn_dims)
    for step in range(nsteps - 1):
        for grp in groups:
            if step == nsteps - 2:
                grp["minv"] = grp["minv"] + _dot_rw(grp["minv"], blockdiag(grp["t"]), nn_dims)
            else:
                both = _dot_rw(jnp.concatenate([grp["minv"], grp["t"]], axis=0), blockdiag(grp["t"]), nn_dims)
                grp["minv"] = grp["minv"] + both[0:lc]
                grp["t"] = both[lc:2 * lc]
    for grp in groups:
        for idx, m in enumerate(grp["members"]):
            tile = grp["minv"][:, idx * 2 * lc:(idx + 1) * 2 * lc]
            m["minv"] = jnp.concatenate([jnp.where(lane_pair < lc, tile, 0.0),
                                         jnp.where(lane_pair < lc, 0.0, tile)], axis=0).astype(BF16)
    for d in st:
        both = _dot_rw(jnp.concatenate([d["w_ak"], d["w_rk"]], axis=0), d["v_s"], nn_dims)
        d["wv"] = both[0:2 * lc]
        d["y_c"] = both[2 * lc:3 * lc]
    for d in st:
        both = _dot_rw(d["minv"], jnp.concatenate([d["ar"][0:2 * lc], d["wv"].astype(BF16)], axis=1), nn_dims)
        d["ma"] = both[:, 0:LANE]
        d["mwv"] = both[:, LANE:2 * LANE]
    for d in st:
        d["g"] = _dot_rw(d["ma"], d["bk"][0:2 * lc], tn_dims).astype(BF16)
        d["d"] = _dot_rw(jnp.concatenate([d["mwv"].astype(BF16), d["v_s"]], axis=0), d["bk"], tn_dims)
    s_cur = [s_scr[pi] for pi in range(pp)]
    for d in st:
        s_in = s_cur[d["pi"]]
        d["s0"] = s_in.astype(BF16)
        s_cur[d["pi"]] = ((s_in + _dot_rw(d["s0"], d["g"], nn_dims)) + d["d"]) * d["p_last"]
    for pi in range(pp):
        s_scr[pi] = s_cur[pi]
    for d in st:
        d["xr"] = _dot_rw(d["ar"], d["s0"], nt_dims)
    for d in st:
        d["u"] = (_dot_rw(d["minv"], d["xr"][0:2 * lc], nn_dims) + d["mwv"]).astype(BF16)
    for d in st:
        c, pi = d["c"], d["pi"]
        y_scr[c * lc:(c + 1) * lc, pi * LANE:(pi + 1) * LANE] = (
            (d["xr"][2 * lc:3 * lc] + _dot_rw(d["w_rb"], d["u"], nn_dims)) + d["y_c"])

    y = y_scr[...]
    mean = per_head(y, avg_blk)
    dev = y - mean
    var = per_head(dev * dev, avg_blk)
    yn = (dev * lax.rsqrt(var + RW_GN_EPS)) * ln_w + ln_b
    bonus = per_head((r * k2) * r_k, ones_blk) * v
    mix_ref[0] = ((yn + bonus) * g).astype(mix_ref.dtype)

    @pl.when(t == nt - 1)
    def _():
        sfin_ref[0] = s_scr[...]


def _rwkv(z, shift_prev, s0_pairs, mu, vecs, w2, a2, g2, d_model):
    b, t = z.shape[0], z.shape[1]
    npairs = d_model // LANE
    tc = min(t, 512)
    lc = min(CHUNK, t)
    nt = t // tc
    pp = 4 if tc // lc >= 4 else 8
    wp = pp * LANE
    ngroups = npairs // pp
    cb_w, cb_a, cb_g = 3 * npairs, 3 * npairs + 1, (3 * npairs + 2) // 2

    def zspec(width, cbf):
        return pl.BlockSpec((1, tc, width), lambda bi, p, ti: (bi, ti, cbf(p)))

    def sspec(width, cbf):
        return pl.BlockSpec((1, 1, width), lambda bi, p, ti: (bi, 0, cbf(p)))

    def mspec(width, cbf):
        return pl.BlockSpec((1, width), lambda bi, p, ti: (0, cbf(p)))

    cbfs = [(wp, lambda p: p), (wp, lambda p: ngroups + p), (wp, lambda p: 2 * ngroups + p),
            (LANE, lambda p: cb_w), (LANE, lambda p: cb_a), (2 * LANE, lambda p: cb_g)]
    in_specs = ([zspec(w, f) for w, f in cbfs] + [sspec(w, f) for w, f in cbfs] + [mspec(w, f) for w, f in cbfs]
                + [pl.BlockSpec((8, wp), lambda bi, p, ti: (0, p)),
                   pl.BlockSpec((LANE, wp), lambda bi, p, ti: (0, p)),
                   pl.BlockSpec((LANE, wp), lambda bi, p, ti: (0, p)),
                   pl.BlockSpec((2 * LANE, wp), lambda bi, p, ti: (0, p)),
                   pl.BlockSpec((1, pp, LANE, LANE), lambda bi, p, ti: (bi, p, 0, 0))])
    return pl.pallas_call(
        functools.partial(_rwkv_body, tc=tc, lc=lc, nt=nt, pp=pp),
        out_shape=(jax.ShapeDtypeStruct((b, t, d_model), BF16),
                   jax.ShapeDtypeStruct((b, npairs, LANE, LANE), F32)),
        grid=(b, ngroups, nt),
        in_specs=in_specs,
        out_specs=(pl.BlockSpec((1, tc, wp), lambda bi, p, ti: (bi, ti, p)),
                   pl.BlockSpec((1, pp, LANE, LANE), lambda bi, p, ti: (bi, p, 0, 0))),
        scratch_shapes=[pltpu.VMEM((pp, LANE, LANE), F32),
                        pltpu.VMEM((8, wp), F32),
                        pltpu.VMEM((8, LANE), F32),
                        pltpu.VMEM((8, 2 * LANE), F32),
                        pltpu.VMEM((tc, wp), F32)],
        compiler_params=_cparams(("parallel", "parallel", "arbitrary")),
        name="rwkv7",
    )(*([z] * 6), *([shift_prev] * 6), *([mu] * 6), vecs, w2, a2, g2, s0_pairs)


def _pad_cols(x, segments):
    parts = []
    for start, width, padded in segments:
        seg = x[..., start:start + width]
        if padded > width:
            seg = jnp.concatenate([seg, jnp.zeros(seg.shape[:-1] + (padded - width,), seg.dtype)], axis=-1)
        parts.append(seg)
    return jnp.concatenate(parts, axis=-1)


def _pad_rows(x, padded):
    return jnp.concatenate([x, jnp.zeros((padded - x.shape[0],) + x.shape[1:], x.dtype)], axis=0)


def _pairs_from_heads(s):
    b, h = s.shape[0], s.shape[1]
    s = s.reshape(b, h // 2, 2, HEAD_DIM, HEAD_DIM)
    z = jnp.zeros_like(s[:, :, 0])
    top = jnp.concatenate([s[:, :, 0], z], axis=-1)
    bot = jnp.concatenate([z, s[:, :, 1]], axis=-1)
    return jnp.concatenate([top, bot], axis=-2)


def _heads_from_pairs(sp):
    b, npairs = sp.shape[0], sp.shape[1]
    s = jnp.stack([sp[:, :, :HEAD_DIM, :HEAD_DIM], sp[:, :, HEAD_DIM:, HEAD_DIM:]], axis=2)
    return s.reshape(b, 2 * npairs, HEAD_DIM, HEAD_DIM)


def _tile_gain(g, width):
    return jnp.tile(g.astype(F32), width // HEAD_DIM).reshape(1, width)


def _mem_attend(z, memq_cb, km, vm, q_gain):
    t_q = z.shape[1]
    wq = MEM_HEADS * HEAD_DIM
    tq = min(t_q, 512)
    no_sink = jnp.full((1, wq), -jnp.inf, F32)
    return _flash(z, memq_cb, wq, km, 0, vm, 0, _tile_gain(q_gain, LANE), no_sink,
                  mode="all", tq=tq, kb=km.shape[1], group=1, n_valid=km.shape[1])


def _conv_ffn(x2d, b, t, gain, w_up, conv_w, w_down, prev):
    act, u_last = _up_conv(x2d, gain, w_up, conv_w, prev, t)
    return _mm_res([act], [w_down], x2d), u_last


def kernel(x_prompt, x_sample, state_rwkv_wkv, state_rwkv_shift, cache_swa_k, cache_swa_v, cache_dsa_k, cache_dsa_v, cache_dsa_idx_k, cache_mem_k, cache_mem_v, state_ffn_conv, mem_prompt, attn_norm, ffn_norm, mem_norm, mem_w_kv, mem_q_norm, mem_k_norm, a_w_in, a_mu, a_w0, a_w2, a_a0, a_a2, a_g2, a_k_k, a_k_a, a_r_k, a_ln_w, a_ln_b, a_w_out, b_w_in, b_q_norm, b_k_norm, b_sink, b_w_out, c_w_in, c_q_norm, c_k_norm, c_idx_k_norm, c_w_out, ffn_w_up, ffn_conv, ffn_w_down):
    bp, t, d = x_prompt.shape
    bd, s_len = x_sample.shape[:2]
    depth = attn_norm.shape[0]
    win_rows = cache_swa_k.shape[2]
    past = cache_dsa_k.shape[2] if cache_dsa_k.shape[0] else 0
    d_ff = ffn_w_down.shape[1]
    mem_tokens = mem_prompt.shape[1]
    q_cols = d
    kv_cols = ATT_KV_HEADS * HEAD_DIM
    memq_cols = MEM_HEADS * HEAD_DIM
    att_group = (d // HEAD_DIM) // ATT_KV_HEADS
    dec_lora = a_w2.shape[1]
    a_lora = a_a2.shape[1]
    g_lora = a_g2.shape[1]
    rw_cols = 3 * d + dec_lora + a_lora + g_lora
    k_sel_p = min(TOPK_MAX, t // 4)
    k_sel_s = min(TOPK_MAX, (past + s_len) // 4)
    assert g_lora == 2 * LANE and dec_lora <= LANE and a_lora <= LANE

    xp = x_prompt.reshape(bp * t, d)
    xs = x_sample.reshape(bd * s_len, d)

    rw_segments = [(0, 3 * d, 3 * d), (3 * d, dec_lora, LANE), (3 * d + dec_lora, a_lora, LANE),
                   (3 * d + dec_lora + a_lora, g_lora, g_lora)]
    rw_padded = 3 * d + 2 * LANE + g_lora
    o_qi = q_cols + 2 * kv_cols
    o_ki = o_qi + IDX_HEADS * IDX_DIM
    c_cols = o_ki + IDX_DIM + IDX_HEADS

    outs = {k: [] for k in ("p_rw_wkv", "p_rw_sh", "p_sw_k", "p_sw_v", "p_ds_k", "p_ds_v", "p_ds_i", "p_mk",
                            "p_mv", "p_cv", "s_rw_wkv", "s_rw_sh", "s_sw_k", "s_sw_v", "s_ds_k", "s_ds_v",
                            "s_ds_i", "s_cv")}

    def unpad_rw(row):
        return jnp.concatenate([row[..., :3 * d], row[..., 3 * d:3 * d + dec_lora],
                                row[..., 3 * d + LANE:3 * d + LANE + a_lora],
                                row[..., 3 * d + 2 * LANE:3 * d + 2 * LANE + g_lora]], axis=-1)

    for i in range(depth):
        kind, j = i % 3, i // 3
        if kind == 0:
            w_in = jnp.concatenate([_pad_cols(a_w_in[j], rw_segments), a_w_in[j][:, rw_cols:]], axis=1).astype(BF16)
            memq_cb = rw_padded // memq_cols
            zp = _mm_norm(xp, attn_norm[i], w_in).reshape(bp, t, -1)
            zs = _mm_norm(xs, attn_norm[i], w_in).reshape(bd, s_len, -1)
            mu = _pad_cols(a_mu[j].reshape(1, -1), rw_segments)
            mu = jnp.concatenate([mu, jnp.zeros((1, memq_cols), F32)], axis=1)
            vecs = jnp.stack([a_w0[j], a_a0[j], a_k_k[j], a_k_a[j], a_r_k[j].reshape(-1), a_ln_w[j], a_ln_b[j],
                              jnp.zeros((d,), F32)], axis=0)
            w2 = _pad_rows(a_w2[j], LANE).astype(BF16)
            a2 = _pad_rows(a_a2[j], LANE).astype(BF16)
            g2 = a_g2[j].astype(BF16)
            sh_p = jnp.zeros((bp, 1, zp.shape[-1]), F32)
            st_p = jnp.zeros((bp, d // LANE, LANE, LANE), F32)
            sh_s = _pad_cols(state_rwkv_shift[j], rw_segments)
            sh_s = jnp.concatenate([sh_s, jnp.zeros((bd, memq_cols), F32)], axis=1).reshape(bd, 1, -1)
            st_s = _pairs_from_heads(state_rwkv_wkv[j])
            mp, stp = _rwkv(zp, sh_p, st_p, mu, vecs, w2, a2, g2, d)
            ms, sts = _rwkv(zs, sh_s, st_s, mu, vecs, w2, a2, g2, d)
            outs["p_rw_sh"].append(unpad_rw(zp[:, -1]))
            outs["p_rw_wkv"].append(_heads_from_pairs(stp))
            outs["s_rw_sh"].append(unpad_rw(zs[:, -1]))
            outs["s_rw_wkv"].append(_heads_from_pairs(sts))
            w_out = a_w_out[j]
        elif kind == 1:
            w_in = b_w_in[j].astype(BF16)
            memq_cb = (q_cols + 2 * kv_cols) // memq_cols
            k_cb, v_cb = q_cols // kv_cols, q_cols // kv_cols + 1
            zp = _mm_norm(xp, attn_norm[i], w_in).reshape(bp, t, -1)
            zs = _mm_norm(xs, attn_norm[i], w_in).reshape(bd, s_len, -1)
            kgain = _tile_gain(b_k_norm[j], kv_cols)
            qgain = _tile_gain(b_q_norm[j], LANE)
            sink = jnp.repeat(b_sink[j].astype(F32), HEAD_DIM).reshape(1, q_cols)
            knp = _headnorm(zp.reshape(bp * t, -1), k_cb, kv_cols, kgain).reshape(bp, t, kv_cols)
            mp = _flash(zp, 0, q_cols, knp, 0, zp, v_cb, qgain, sink, mode="band", tq=WINDOW, kb=WINDOW,
                        group=att_group)
            outs["p_sw_k"].append(knp[:, t - win_rows:].reshape(bp, win_rows, ATT_KV_HEADS, HEAD_DIM))
            outs["p_sw_v"].append(zp[:, t - win_rows:, q_cols + kv_cols:q_cols + 2 * kv_cols]
                                  .reshape(bp, win_rows, ATT_KV_HEADS, HEAD_DIM))
            kns = _headnorm(zs.reshape(bd * s_len, -1), k_cb, kv_cols, kgain).reshape(bd, s_len, kv_cols)
            vs_new = zs[:, :, q_cols + kv_cols:q_cols + 2 * kv_cols]
            k_all = jnp.concatenate([cache_swa_k[j].reshape(bd, win_rows, kv_cols), kns], axis=1)
            v_all = jnp.concatenate([cache_swa_v[j].reshape(bd, win_rows, kv_cols), vs_new], axis=1)
            n_keys = win_rows + s_len
            n_pad = -(-n_keys // LANE) * LANE
            pad = jnp.zeros((bd, n_pad - n_keys, kv_cols), F32)
            ms = _flash(zs, 0, q_cols, jnp.concatenate([k_all, pad], axis=1), 0,
                        jnp.concatenate([v_all, pad], axis=1), 0, qgain, sink, mode="all", tq=s_len, kb=n_pad,
                        group=att_group, n_valid=n_keys)
            outs["s_sw_k"].append(k_all[:, n_keys - win_rows:].reshape(bd, win_rows, ATT_KV_HEADS, HEAD_DIM))
            outs["s_sw_v"].append(v_all[:, n_keys - win_rows:].reshape(bd, win_rows, ATT_KV_HEADS, HEAD_DIM))
            w_out = b_w_out[j]
        else:
            wc = c_w_in[j]
            w_in = jnp.concatenate([wc[:, :o_ki], wc[:, c_cols:],
                                    _pad_cols(wc, [(o_ki, IDX_DIM + IDX_HEADS, LANE)])], axis=1).astype(BF16)
            memq_cb = o_ki // memq_cols
            kw_cb = (o_ki + memq_cols) // LANE
            k_cb, v_cb = q_cols // kv_cols, q_cols // kv_cols + 1
            qi_cb = o_qi // (IDX_HEADS * IDX_DIM)
            zp = _mm_norm(xp, attn_norm[i], w_in).reshape(bp, t, -1)
            zs = _mm_norm(xs, attn_norm[i], w_in).reshape(bd, s_len, -1)
            kgain = _tile_gain(c_k_norm[j], kv_cols)
            qgain = _tile_gain(c_q_norm[j], LANE)
            igain = jnp.concatenate([c_idx_k_norm[j].astype(F32), jnp.zeros((LANE - IDX_DIM,), F32)]).reshape(1, LANE)
            no_sink = jnp.full((1, q_cols), -jnp.inf, F32)
            knp = _headnorm(zp.reshape(bp * t, -1), k_cb, kv_cols, kgain).reshape(bp, t, kv_cols)
            kip = _headnorm(zp.reshape(bp * t, -1), kw_cb, LANE, igain).reshape(bp, t, LANE)
            tq = min(t, DSA_TQ)
            kb = min(t, DSA_KB)
            sc, tau, cut = _dsa_select(zp, qi_cb, kw_cb, kip, tq=tq, kb=kb, k_sel=k_sel_p, n_valid=t, causal=True)
            ftq, fkb = min(t, DSA_FLASH_TQ), min(t, DSA_FLASH_KB)
            assert all(_causal_blocks(r // tq, tq, kb) * kb >= _causal_blocks(r // ftq, ftq, fkb) * fkb
                       for r in range(0, t, CHUNK))
            mp = _flash(zp, 0, q_cols, knp, 0, zp, v_cb, qgain, no_sink, mode="dsa", tq=ftq, kb=fkb,
                        group=att_group, n_valid=t, causal=True, scores=sc, tau=tau, jb=cut)
            outs["p_ds_k"].append(knp.reshape(bp, t, ATT_KV_HEADS, HEAD_DIM))
            outs["p_ds_v"].append(zp[:, :, q_cols + kv_cols:q_cols + 2 * kv_cols].reshape(bp, t, ATT_KV_HEADS, HEAD_DIM))
            outs["p_ds_i"].append(kip[:, :, :IDX_DIM])
            kns = _headnorm(zs.reshape(bd * s_len, -1), k_cb, kv_cols, kgain).reshape(bd, s_len, kv_cols)
            kis = _headnorm(zs.reshape(bd * s_len, -1), kw_cb, LANE, igain).reshape(bd, s_len, LANE)
            vs_new = zs[:, :, q_cols + kv_cols:q_cols + 2 * kv_cols]
            n_keys = past + s_len
            n_pad = -(-n_keys // LANE) * LANE
            zpad = lambda w: jnp.zeros((bd, n_pad - n_keys, w), F32)
            k_all = jnp.concatenate([cache_dsa_k[j].reshape(bd, past, kv_cols), kns, zpad(kv_cols)], axis=1)
            v_all = jnp.concatenate([cache_dsa_v[j].reshape(bd, past, kv_cols), vs_new, zpad(kv_cols)], axis=1)
            ki_cache = jnp.concatenate([cache_dsa_idx_k[j], jnp.zeros((bd, past, LANE - IDX_DIM), F32)], axis=-1)
            ki_all = jnp.concatenate([ki_cache, kis, zpad(LANE)], axis=1)
            sc, tau, cut = _dsa_select(zs, qi_cb, kw_cb, ki_all, tq=s_len, kb=n_pad, k_sel=k_sel_s, n_valid=n_keys,
                                       causal=False)
            ms = _flash(zs, 0, q_cols, k_all, 0, v_all, 0, qgain, no_sink, mode="dsa", tq=s_len, kb=n_pad,
                        group=att_group, n_valid=n_keys, causal=False, scores=sc, tau=tau, jb=cut)
            outs["s_ds_k"].append(kns.reshape(bd, s_len, ATT_KV_HEADS, HEAD_DIM))
            outs["s_ds_v"].append(vs_new.reshape(bd, s_len, ATT_KV_HEADS, HEAD_DIM))
            outs["s_ds_i"].append(kis[:, :, :IDX_DIM])
            w_out = c_w_out[j]

        kv_mem = _mm_norm(mem_prompt.reshape(bp * mem_tokens, d), mem_norm[i], mem_w_kv[i].astype(BF16))
        km_p = _headnorm(kv_mem, 0, memq_cols, _tile_gain(mem_k_norm[i], memq_cols)).reshape(bp, mem_tokens, memq_cols)
        vm_p = kv_mem[:, memq_cols:].reshape(bp, mem_tokens, memq_cols)
        outs["p_mk"].append(km_p.reshape(bp, mem_tokens, MEM_HEADS, HEAD_DIM))
        outs["p_mv"].append(vm_p.reshape(bp, mem_tokens, MEM_HEADS, HEAD_DIM))
        mo_p = _mem_attend(zp, memq_cb, km_p, vm_p, mem_q_norm[i])
        mo_s = _mem_attend(zs, memq_cb, cache_mem_k[i].reshape(bd, mem_tokens, memq_cols),
                           cache_mem_v[i].reshape(bd, mem_tokens, memq_cols), mem_q_norm[i])
        w_mix, w_mem = w_out[:d].astype(BF16), w_out[d:].astype(BF16)
        xp = _mm_res([mp.reshape(bp * t, d), mo_p.reshape(bp * t, memq_cols)], [w_mix, w_mem], xp)
        xs = _mm_res([ms.reshape(bd * s_len, d), mo_s.reshape(bd * s_len, memq_cols)], [w_mix, w_mem], xs)

        w_up, w_down = ffn_w_up[i].astype(BF16), ffn_w_down[i].astype(BF16)
        xp, cp = _conv_ffn(xp, bp, t, ffn_norm[i], w_up, ffn_conv[i], w_down,
                           jnp.zeros((bp, CONV_W - 1, 2 * d_ff), F32))
        xs, cs = _conv_ffn(xs, bd, s_len, ffn_norm[i], w_up, ffn_conv[i], w_down, state_ffn_conv[i])
        outs["p_cv"].append(cp)
        outs["s_cv"].append(cs)

    st = jnp.stack
    order = ("p_rw_wkv", "p_rw_sh", "p_sw_k", "p_sw_v", "p_ds_k", "p_ds_v", "p_ds_i", "p_mk", "p_mv", "p_cv",
             "s_rw_wkv", "s_rw_sh", "s_sw_k", "s_sw_v", "s_ds_k", "s_ds_v", "s_ds_i", "s_cv")
    return (xp.reshape(bp, t, d), xs.reshape(bd, s_len, d)) + tuple(st(outs[k]) for k in order)
```

```python
import functools
import math

import jax
import jax.numpy as jnp
from jax import lax
from jax.experimental import pallas as pl
from jax.experimental.pallas import tpu as pltpu

F32 = jnp.float32
BF16 = jnp.bfloat16

HEAD_DIM = 64
CHUNK = 64
NORM_EPS = 1e-6
RW_GN_EPS = HEAD_DIM * 1e-5
ATT_KV_HEADS = 4
WINDOW = 128
IDX_HEADS = 8
IDX_DIM = 64
TOPK_MAX = 256
MEM_HEADS = 4
CONV_W = 3

LANE = 128
VMEM_LIMIT = 52 * 1024 * 1024
NEG_BIG = -(2.0 ** 100)
INT_MIN = -2147483648
MM_ROWS = 1024
DSA_TQ = 256
DSA_KB = 512
DSA_FLASH_TQ = 256
DSA_FLASH_KB = 512


def _cparams(sem, vmem=VMEM_LIMIT):
    return pltpu.CompilerParams(dimension_semantics=sem, vmem_limit_bytes=vmem)


def _split3(a):
    a1 = a.astype(BF16)
    r1 = a - a1.astype(F32)
    a2 = r1.astype(BF16)
    r2 = r1 - a2.astype(F32)
    return a1, a2, r2.astype(BF16)


def _dot_exact_rhs(a, e):
    a1, a2, a3 = _split3(a)
    d = lambda x: jnp.dot(x, e, preferred_element_type=F32)
    return (d(a3) + d(a2)) + d(a1)


def _dot_exact_lhs(e, a):
    a1, a2, a3 = _split3(a)
    d = lambda x: jnp.dot(e, x, preferred_element_type=F32)
    return (d(a3) + d(a2)) + d(a1)


def _head_block_matrix(width, value):
    r = lax.broadcasted_iota(jnp.int32, (width, width), 0) // HEAD_DIM
    c = lax.broadcasted_iota(jnp.int32, (width, width), 1) // HEAD_DIM
    return jnp.where(r == c, value, 0.0).astype(BF16)


def _dot_rw(a, b, dims):
    return lax.dot_general(a.astype(BF16), b.astype(BF16), (dims, ((), ())), preferred_element_type=F32)


def _pick_tile(n, cap):
    best = None
    for t in range(LANE, min(n, cap) + 1, LANE):
        if n % t == 0:
            best = t
    assert best is not None, n
    return best


def _mm_norm_body(x_ref, g_ref, w_ref, o_ref, xn_ref):
    @pl.when(pl.program_id(1) == 0)
    def _():
        x = x_ref[...]
        ms = jnp.mean(x * x, axis=-1, keepdims=True)
        xn_ref[...] = ((x * lax.rsqrt(ms + NORM_EPS)) * g_ref[...]).astype(BF16)

    o_ref[...] = jnp.dot(xn_ref[...], w_ref[...], preferred_element_type=F32)


def _mm_norm(x, gain, w):
    m, k = x.shape
    n = w.shape[1]
    tm = min(m, MM_ROWS)
    tn = _pick_tile(n, 1536)
    return pl.pallas_call(
        _mm_norm_body,
        out_shape=jax.ShapeDtypeStruct((m, n), F32),
        grid=(m // tm, n // tn),
        in_specs=[pl.BlockSpec((tm, k), lambda i, j: (i, 0)),
                  pl.BlockSpec((1, k), lambda i, j: (0, 0)),
                  pl.BlockSpec((k, tn), lambda i, j: (0, j))],
        out_specs=pl.BlockSpec((tm, tn), lambda i, j: (i, j)),
        scratch_shapes=[pltpu.VMEM((tm, k), BF16)],
        compiler_params=_cparams(("parallel", "arbitrary")),
        name="mm_norm",
    )(x, gain.reshape(1, k), w)


def _mm_res_body(*refs, n_lhs):
    lhs = refs[:n_lhs]
    ws = refs[n_lhs:2 * n_lhs]
    r_ref, o_ref = refs[2 * n_lhs], refs[2 * n_lhs + 1]
    acc = jnp.dot(lhs[0][...], ws[0][...], preferred_element_type=F32)
    for a, w in zip(lhs[1:], ws[1:]):
        acc = acc + jnp.dot(a[...], w[...], preferred_element_type=F32)
    o_ref[...] = r_ref[...] + acc


def _mm_res(lhs_list, w_list, res):
    m, n = res.shape
    ktot = sum(a.shape[1] for a in lhs_list)
    tm = min(m, MM_ROWS)
    tn = _pick_tile(n, 1024 if ktot <= 3072 else 512)
    n_lhs = len(lhs_list)
    in_specs = [pl.BlockSpec((tm, a.shape[1]), lambda i, j: (i, 0)) for a in lhs_list]
    in_specs += [pl.BlockSpec((w.shape[0], tn), lambda i, j: (0, j)) for w in w_list]
    in_specs += [pl.BlockSpec((tm, tn), lambda i, j: (i, j))]
    return pl.pallas_call(
        functools.partial(_mm_res_body, n_lhs=n_lhs),
        out_shape=jax.ShapeDtypeStruct((m, n), F32),
        grid=(m // tm, n // tn),
        in_specs=in_specs,
        out_specs=pl.BlockSpec((tm, tn), lambda i, j: (i, j)),
        compiler_params=_cparams(("parallel", "arbitrary")),
        name="mm_res",
    )(*lhs_list, *w_list, res)


def _headnorm_body(x_ref, g_ref, o_ref, *, width):
    avg = _head_block_matrix(LANE, 1.0 / HEAD_DIM)
    for c in range(width // LANE):
        x = x_ref[:, c * LANE:(c + 1) * LANE]
        ms = _dot_exact_rhs(x * x, avg)
        o_ref[:, c * LANE:(c + 1) * LANE] = (x * lax.rsqrt(ms + NORM_EPS)) * g_ref[:, c * LANE:(c + 1) * LANE]


def _headnorm(x, col_block, width, gain_row):
    m = x.shape[0]
    tm = min(m, 1024)
    return pl.pallas_call(
        functools.partial(_headnorm_body, width=width),
        out_shape=jax.ShapeDtypeStruct((m, width), F32),
        grid=(m // tm,),
        in_specs=[pl.BlockSpec((tm, width), lambda i: (i, col_block)),
                  pl.BlockSpec((1, width), lambda i: (0, 0))],
        out_specs=pl.BlockSpec((tm, width), lambda i: (i, 0)),
        compiler_params=_cparams(("parallel",)),
        name="headnorm",
    )(x, gain_row)


def _up_conv_body(x_ref, g_ref, wa_ref, wb_ref, pa_ref, pb_ref, cwa_ref, cwb_ref, o_ref, la_ref, lb_ref,
                  xn_ref, ca_ref, cb_ref, *, tm, tiles_per_batch, bpt):
    i = pl.program_id(0)
    j = pl.program_id(1)
    rpb = tm // bpt

    @pl.when(j == 0)
    def _():
        x = x_ref[...]
        ms = jnp.mean(x * x, axis=-1, keepdims=True)
        xn_ref[...] = ((x * lax.rsqrt(ms + NORM_EPS)) * g_ref[...]).astype(BF16)

    if bpt == 1:
        @pl.when(i % tiles_per_batch == 0)
        def _():
            ca_ref[j] = pa_ref[0]
            cb_ref[j] = pb_ref[0]

    xn = xn_ref[...]
    row = lax.broadcasted_iota(jnp.int32, o_ref.shape, 0)
    off = row if bpt == 1 else row % rpb
    if bpt > 1:
        pick = (lax.broadcasted_iota(jnp.int32, (tm, bpt), 0) // rpb
                == lax.broadcasted_iota(jnp.int32, (tm, bpt), 1)).astype(BF16)

    def conv(w_ref, p_ref, c_ref, cw_ref, last_ref):
        u = jnp.dot(xn, w_ref[...], preferred_element_type=F32)
        if bpt == 1:
            car = c_ref[j]
            c0, c1 = car[0:1, :], car[1:2, :]
            c_ref[j] = u[tm - 2:tm, :]
            last_ref[0] = u[tm - 2:tm, :]
        else:
            c0 = _dot_exact_lhs(pick, p_ref[:, 0, :])
            c1 = _dot_exact_lhs(pick, p_ref[:, 1, :])
            for bi in range(bpt):
                last_ref[bi] = u[(bi + 1) * rpb - 2:(bi + 1) * rpb, :]
        u1 = jnp.where(off == 0, c1, pltpu.roll(u, 1, axis=0))
        u2 = jnp.where(off == 0, c0, jnp.where(off == 1, c1, pltpu.roll(u, 2, axis=0)))
        return (u2 * cw_ref[0:1, :] + u1 * cw_ref[1:2, :]) + u * cw_ref[2:3, :]

    a = conv(wa_ref, pa_ref, ca_ref, cwa_ref, la_ref)
    b = conv(wb_ref, pb_ref, cb_ref, cwb_ref, lb_ref)
    o_ref[...] = ((a * jax.nn.sigmoid(a)) * b).astype(o_ref.dtype)


def _up_conv(x, gain, w_up, conv_w, prev, t):
    m, k = x.shape
    f = w_up.shape[1] // 2
    b = m // t
    tm = min(m, MM_ROWS)
    bpt = max(1, tm // t)
    assert tm % t == 0 or t % tm == 0
    tn = _pick_tile(f, 512)
    nj = f // tn
    tpb = max(1, t // tm)
    act, la, lb = pl.pallas_call(
        functools.partial(_up_conv_body, tm=tm, tiles_per_batch=tpb, bpt=bpt),
        out_shape=(jax.ShapeDtypeStruct((m, f), BF16),
                   jax.ShapeDtypeStruct((b, CONV_W - 1, f), F32),
                   jax.ShapeDtypeStruct((b, CONV_W - 1, f), F32)),
        grid=(m // tm, nj),
        in_specs=[pl.BlockSpec((tm, k), lambda i, j: (i, 0)),
                  pl.BlockSpec((1, k), lambda i, j: (0, 0)),
                  pl.BlockSpec((k, tn), lambda i, j: (0, j)),
                  pl.BlockSpec((k, tn), lambda i, j: (0, nj + j)),
                  pl.BlockSpec((bpt, CONV_W - 1, tn), lambda i, j: (i // tpb, 0, j)),
                  pl.BlockSpec((bpt, CONV_W - 1, tn), lambda i, j: (i // tpb, 0, nj + j)),
                  pl.BlockSpec((CONV_W, tn), lambda i, j: (0, j)),
                  pl.BlockSpec((CONV_W, tn), lambda i, j: (0, nj + j))],
        out_specs=(pl.BlockSpec((tm, tn), lambda i, j: (i, j)),
                   pl.BlockSpec((bpt, CONV_W - 1, tn), lambda i, j: (i // tpb, 0, j)),
                   pl.BlockSpec((bpt, CONV_W - 1, tn), lambda i, j: (i // tpb, 0, j))),
        scratch_shapes=[pltpu.VMEM((tm, k), BF16),
                        pltpu.VMEM((nj, CONV_W - 1, tn), F32),
                        pltpu.VMEM((nj, CONV_W - 1, tn), F32)],
        compiler_params=_cparams(("arbitrary", "arbitrary")),
        name="up_conv_gate",
    )(x, gain.reshape(1, k), w_up, w_up, prev, prev, conv_w, conv_w)
    return act, jnp.concatenate([la, lb], axis=-1)


def _causal_blocks(i, tq, kb):
    return ((i + 1) * tq + kb - 1) // kb


def _fold_steps(nq, tq, kb):
    if nq % 2:
        return None
    return max(_causal_blocks(r, tq, kb) + _causal_blocks(nq - 1 - r, tq, kb) for r in range(nq // 2))


def _fold_step(r, step, nq, tq, kb):
    n_lo = _causal_blocks(r, tq, kb)
    hi = nq - 1 - r
    in_lo = step < n_lo
    i = jnp.where(in_lo, r, hi)
    last_blk = _causal_blocks(i, tq, kb) - 1
    j = jnp.minimum(jnp.where(in_lo, step, step - n_lo), last_blk)
    active = step < n_lo + _causal_blocks(hi, tq, kb)
    return i, j, active & (jnp.where(in_lo, step, step - n_lo) == 0), active & (j == last_blk), active


def _flash_body(*refs, mode, tq, kb, wq, group, n_valid, nk, causal, fold_nq):
    q_ref, k_ref, v_ref, qg_ref, sink_ref = refs[:5]
    pos = 5
    if mode == "dsa":
        sc_ref, tau_ref, jb_ref = refs[pos:pos + 3]
        pos += 3
    o_ref = refs[pos]
    qn_scr, acc_scr, m_scr, l_scr, bias_scr = refs[pos + 1:]

    if fold_nq:
        i, j, is_first, is_last, fold_active = _fold_step(pl.program_id(1), pl.program_id(2), fold_nq, tq, kb)
    else:
        i, j = pl.program_id(1), pl.program_id(2)
        is_first, is_last = j == 0, j == nk - 1
    npairs = wq // LANE
    pairs_per_unit = group // 2 if group > 1 else 1
    lane = lax.broadcasted_iota(jnp.int32, (1, LANE), 1)
    lo_half = lane < HEAD_DIM

    @pl.when(is_first)
    def _():
        avg = _head_block_matrix(LANE, 1.0 / HEAD_DIM)
        for p in range(npairs):
            x = q_ref[0, :, p * LANE:(p + 1) * LANE]
            ms = jnp.dot((x * x).astype(BF16), avg, preferred_element_type=F32)
            qn = ((x * lax.rsqrt(ms + NORM_EPS)) * qg_ref[...]) * (HEAD_DIM ** -0.5)
            qn_scr[p] = qn.astype(BF16)
        acc_scr[...] = jnp.zeros(acc_scr.shape, F32)
        l_scr[...] = jnp.zeros(l_scr.shape, F32)
        m_scr[...] = jnp.full(m_scr.shape, NEG_BIG, F32)

    if mode == "band":
        kblk = i - WINDOW // kb + j
        active = kblk >= 0
    elif mode == "dsa" and causal:
        kblk = j
        active = fold_active if fold_nq else j * kb < (i + 1) * tq
    else:
        kblk = j
        active = None

    def step():
        kidx = kblk * kb + lax.broadcasted_iota(jnp.int32, (tq, kb), 1)
        qpos = i * tq + lax.broadcasted_iota(jnp.int32, (tq, kb), 0)
        if mode == "band":
            qchunk = qpos // CHUNK
            sel = (kidx >= (qchunk - WINDOW // CHUNK) * CHUNK) & (kidx < (qchunk + 1) * CHUNK) & (kidx >= 0)
        elif mode == "dsa":
            x = sc_ref[0]
            tau = tau_ref[0][:, 0:1]
            jb = jb_ref[0][:, 0:1]
            adm = kidx < ((qpos // CHUNK + 1) * CHUNK if causal else n_valid)
            sel = adm & ((x > tau) | ((x == tau) & (kidx <= jb)))
        else:
            sel = kidx < n_valid
        bias_scr[...] = jnp.where(sel, 0.0, NEG_BIG).astype(BF16)

        ppu = pairs_per_unit
        for u in range(npairs // ppu):
            ps = slice(u * ppu, (u + 1) * ppu)
            s = lax.dot_general(qn_scr[ps].reshape(ppu * tq, LANE), k_ref[0, 0, u], (((1,), (1,)), ((), ())),
                                preferred_element_type=F32).reshape(ppu, tq, 2 * kb)
            bias = bias_scr[...][None]
            m_old = m_scr[ps]
            s0 = s[:, :, 0:kb].astype(BF16) + bias
            s1 = s[:, :, kb:2 * kb].astype(BF16) + bias
            mn0 = jnp.maximum(m_old[:, :, 0:1], jnp.max(s0, axis=2, keepdims=True).astype(F32))
            mn1 = jnp.maximum(m_old[:, :, HEAD_DIM:HEAD_DIM + 1], jnp.max(s1, axis=2, keepdims=True).astype(F32))
            p0 = jnp.exp(s0 - mn0.astype(BF16))
            p1 = jnp.exp(s1 - mn1.astype(BF16))
            pcat = jnp.concatenate([p0, p1], axis=2).reshape(ppu * tq, 2 * kb)
            pv = jnp.dot(pcat, v_ref[0, 0, u], preferred_element_type=F32)
            pv = pv.reshape(ppu, tq, 2 * LANE)
            mn = jnp.where(lo_half, mn0, mn1)
            alpha = jnp.exp(m_old - mn)
            acc_scr[ps] = acc_scr[ps] * alpha + pv[:, :, 0:LANE]
            l_scr[ps] = l_scr[ps] * alpha + pv[:, :, LANE:2 * LANE]
            m_scr[ps] = mn

    if active is None:
        step()
    else:
        pl.when(active)(step)

    @pl.when(is_last)
    def _():
        for p in range(npairs):
            den = l_scr[p] + jnp.exp(sink_ref[:, p * LANE:(p + 1) * LANE] - m_scr[p])
            o_ref[0, :, p * LANE:(p + 1) * LANE] = (acc_scr[p] / den).astype(o_ref.dtype)


def _pair_operands_body(k_ref, v_ref, kab_ref, vab_ref, *, kb, group, n_units):
    lane = lax.broadcasted_iota(jnp.int32, (1, LANE), 1)
    lo_half = lane < HEAD_DIM
    ones_lo = jnp.broadcast_to(jnp.where(lo_half, 1.0, 0.0), (kb, LANE)).astype(BF16)
    ones_hi = jnp.broadcast_to(jnp.where(lo_half, 0.0, 1.0), (kb, LANE)).astype(BF16)
    for u in range(n_units):
        if group > 1:
            tile_idx, half = u // 2, u % 2
        else:
            tile_idx, half = u, None
        for src, dst in ((k_ref, kab_ref), (v_ref, vab_ref)):
            tile = src[0, :, tile_idx * LANE:(tile_idx + 1) * LANE]
            if half is None:
                a_part = jnp.where(lo_half, tile, 0.0)
                b_part = jnp.where(lo_half, 0.0, tile)
            elif half == 0:
                a_part = jnp.where(lo_half, tile, 0.0)
                b_part = pltpu.roll(a_part, HEAD_DIM, axis=1)
            else:
                b_part = jnp.where(lo_half, 0.0, tile)
                a_part = pltpu.roll(b_part, HEAD_DIM, axis=1)
            dst[0, 0, u, 0:kb, 0:LANE] = a_part.astype(BF16)
            dst[0, 0, u, kb:2 * kb, 0:LANE] = b_part.astype(BF16)
        vab_ref[0, 0, u, 0:kb, LANE:2 * LANE] = ones_lo
        vab_ref[0, 0, u, kb:2 * kb, LANE:2 * LANE] = ones_hi


def _pair_operands(k_arr, k_cb, v_arr, v_cb, kb, group, npairs):
    b, t_k = k_arr.shape[0], k_arr.shape[1]
    wk = ATT_KV_HEADS * HEAD_DIM
    n_units = npairs // (group // 2) if group > 1 else npairs
    nkb = t_k // kb
    return pl.pallas_call(
        functools.partial(_pair_operands_body, kb=kb, group=group, n_units=n_units),
        out_shape=(jax.ShapeDtypeStruct((b, nkb, n_units, 2 * kb, LANE), BF16),
                   jax.ShapeDtypeStruct((b, nkb, n_units, 2 * kb, 2 * LANE), BF16)),
        grid=(b, nkb),
        in_specs=[pl.BlockSpec((1, kb, wk), lambda bi, j: (bi, j, k_cb)),
                  pl.BlockSpec((1, kb, wk), lambda bi, j: (bi, j, v_cb))],
        out_specs=(pl.BlockSpec((1, 1, n_units, 2 * kb, LANE), lambda bi, j: (bi, j, 0, 0, 0)),
                   pl.BlockSpec((1, 1, n_units, 2 * kb, 2 * LANE), lambda bi, j: (bi, j, 0, 0, 0))),
        compiler_params=_cparams(("parallel", "parallel")),
        name="pair_operands",
    )(k_arr, v_arr)


def _flash(q_arr, q_cb, wq, k_arr, k_cb, v_arr, v_cb, q_gain, sink_row, *, mode, tq, kb, group,
           n_valid=None, causal=False, scores=None, tau=None, jb=None):
    b, t_q = q_arr.shape[0], q_arr.shape[1]
    t_k = k_arr.shape[1]
    nq = t_q // tq
    if mode == "band":
        assert tq == kb == WINDOW
        nk = 2
        kmap = lambda i, j: jnp.maximum(i - 1 + j, 0)
    elif mode == "dsa" and causal:
        nk = t_k // kb
        kmap = lambda i, j: jnp.minimum(j, ((i + 1) * tq - 1) // kb)
    else:
        nk = t_k // kb
        kmap = lambda i, j: j
    grid = (b, nq, nk)
    qmap = lambda i, j: i
    fold = _fold_steps(nq, tq, kb) if (mode == "dsa" and causal) else None
    if fold:
        grid = (b, nq // 2, fold)
        qmap = lambda r, s: _fold_step(r, s, nq, tq, kb)[0]
        kmap = lambda r, s: _fold_step(r, s, nq, tq, kb)[1]
    npairs = wq // LANE
    kab, vab = _pair_operands(k_arr, k_cb, v_arr, v_cb, kb, group, npairs)
    n_units = kab.shape[2]
    in_specs = [pl.BlockSpec((1, tq, wq), lambda bi, i, j: (bi, qmap(i, j), q_cb)),
                pl.BlockSpec((1, 1, n_units, 2 * kb, LANE), lambda bi, i, j: (bi, kmap(i, j), 0, 0, 0)),
                pl.BlockSpec((1, 1, n_units, 2 * kb, 2 * LANE), lambda bi, i, j: (bi, kmap(i, j), 0, 0, 0)),
                pl.BlockSpec((1, LANE), lambda bi, i, j: (0, 0)),
                pl.BlockSpec((1, wq), lambda bi, i, j: (0, 0))]
    args = [q_arr, kab, vab, q_gain, sink_row]
    if mode == "dsa":
        in_specs += [pl.BlockSpec((1, tq, kb), lambda bi, i, j: (bi, qmap(i, j), kmap(i, j))),
                     pl.BlockSpec((1, tq, LANE), lambda bi, i, j: (bi, qmap(i, j), 0)),
                     pl.BlockSpec((1, tq, LANE), lambda bi, i, j: (bi, qmap(i, j), 0))]
        args += [scores, tau, jb]
    body = functools.partial(_flash_body, mode=mode, tq=tq, kb=kb, wq=wq, group=group,
                             n_valid=n_valid, nk=nk, causal=causal, fold_nq=nq if fold else 0)
    return pl.pallas_call(
        body,
        out_shape=jax.ShapeDtypeStruct((b, t_q, wq), BF16),
        grid=grid,
        in_specs=in_specs,
        out_specs=pl.BlockSpec((1, tq, wq), lambda bi, i, j: (bi, qmap(i, j), 0)),
        scratch_shapes=[pltpu.VMEM((npairs, tq, LANE), BF16),
                        pltpu.VMEM((npairs, tq, LANE), F32),
                        pltpu.VMEM((npairs, tq, LANE), F32),
                        pltpu.VMEM((npairs, tq, LANE), F32),
                        pltpu.VMEM((tq, kb), BF16)],
        compiler_params=_cparams(("parallel", "parallel", "arbitrary")),
        name="flash_" + mode,
    )(*args)


def _dsa_select_body(qi_ref, kw_ref, ki_ref, sc_ref, tau_ref, jb_ref, sc_scr, cnt_scr, ebuf_scr, *, tq, kb, nk, k_sel,
                     n_valid, causal, fold_nq):
    if fold_nq:
        i, j, _, is_last, fold_active = _fold_step(pl.program_id(1), pl.program_id(2), fold_nq, tq, kb)
    else:
        i, j = pl.program_id(1), pl.program_id(2)
        is_last = j == nk - 1
    qpos = i * tq + lax.broadcasted_iota(jnp.int32, (tq, 1), 0)
    if causal:
        n_adm = (qpos // CHUNK + 1) * CHUNK
        active = fold_active if fold_nq else j * kb < (i + 1) * tq
        n_blocks = _causal_blocks(i, tq, kb)
    else:
        n_adm = jnp.full((tq, 1), n_valid, jnp.int32)
        active = None
        n_blocks = nk

    def compute():
        ka = ki_ref[0]
        kab = jnp.concatenate([ka, pltpu.roll(ka, HEAD_DIM, axis=1)], axis=0).astype(BF16)
        kw = kw_ref[0]
        acc = jnp.zeros((tq, kb), F32)
        q4 = jnp.concatenate([qi_ref[0, :, p * LANE:(p + 1) * LANE].astype(BF16) for p in range(IDX_HEADS // 2)],
                             axis=0)
        s = lax.dot_general(q4, kab, (((1,), (1,)), ((), ())), preferred_element_type=F32)
        for p in range(IDX_HEADS // 2):
            for h in range(2):
                col = HEAD_DIM + 2 * p + h
                w = (kw[:, col:col + 1] * IDX_HEADS ** -0.5) * IDX_DIM ** -0.5
                acc = acc + jnp.maximum(s[p * tq:(p + 1) * tq, h * kb:(h + 1) * kb], 0.0) * w
        acc = jnp.where(acc == 0.0, 0.0, acc)
        kidx = j * kb + lax.broadcasted_iota(jnp.int32, (tq, kb), 1)
        x = jnp.where(kidx < n_adm, acc, -jnp.inf)
        sc_scr[j] = x
        sc_ref[0] = x

    if active is None:
        compute()
    else:
        pl.when(active)(compute)

        if not fold_nq:
            @pl.when(jnp.logical_not(active))
            def _():
                sc_ref[0] = jnp.full((tq, kb), -jnp.inf, F32)

    @pl.when(is_last)
    def _():
        kf = float(k_sel)

        rsz = min(tq, LANE)
        lane_tiles = range(kb // LANE)

        def count(make_pred):
            starts = range(0, tq, rsz)
            preds = [make_pred(lambda col, r0=r0: jnp.broadcast_to(col[r0:r0 + rsz], (rsz, LANE))) for r0 in starts]
            accs = []
            for r0, pred in zip(starts, preds):
                def blk(jj, acc, r0=r0, pred=pred):
                    for a in lane_tiles:
                        x = sc_scr[jj, r0:r0 + rsz, a * LANE:(a + 1) * LANE]
                        acc = acc + jnp.where(pred(x), 1.0, 0.0)
                    return acc
                accs.append(lax.fori_loop(0, n_blocks, blk, jnp.zeros((rsz, LANE), F32)))
            return jnp.sum(jnp.concatenate(accs, axis=0), axis=1, keepdims=True)

        def key_to_float(tu):
            cs = tu ^ INT_MIN
            fb = jnp.where(cs >= 0, cs, cs ^ 0x7FFFFFFF)
            return lax.bitcast_convert_type(fb, F32)

        def bit_body(it, carry):
            tu, c_at = carry
            cand = tu | jnp.left_shift(jnp.int32(1), 31 - it)
            thr = key_to_float(cand)
            c = count(lambda widen: (lambda x, t=widen(thr): x >= t))
            keep = c >= kf
            return jnp.where(keep, cand, tu), jnp.where(keep, c, c_at)

        tu, c_ge = lax.fori_loop(0, 32, bit_body, (jnp.zeros((tq, 1), jnp.int32),
                                                   jnp.full((tq, 1), float(nk * kb), F32)))
        full = n_adm >= k_sel
        tau = jnp.where(full, key_to_float(tu), -jnp.inf)
        tau = jnp.where(jnp.abs(tau) < 1.1754944e-38, 0.0, tau)
        c_gt = count(lambda widen: (lambda x, t=widen(tau): x > t))
        need = kf - c_gt
        tau_ref[0] = jnp.broadcast_to(tau, (tq, LANE))
        jb_ref[0] = jnp.full((tq, LANE), 2 ** 30, jnp.int32)
        surplus = jnp.max(jnp.where(full, (c_ge - c_gt) - need, 0.0))

        @pl.when(surplus > 0.5)
        def _():
            tau_b = jnp.broadcast_to(tau, (tq, LANE))
            need_b = jnp.broadcast_to(need, (tq, LANE))
            ones_mat = jnp.ones((LANE, LANE), BF16)
            row_sum = lambda part: jnp.dot(part.astype(BF16), ones_mat, preferred_element_type=F32)

            def ties(jj, a):
                return jnp.where(sc_scr[jj, :, a * LANE:(a + 1) * LANE] == tau_b, 1.0, 0.0)

            def blk_count(jj, carry):
                part = ties(jj, 0)
                for a in lane_tiles[1:]:
                    part = part + ties(jj, a)
                cnt_scr[jj] = row_sum(part)
                return carry
            lax.fori_loop(0, n_blocks, blk_count, 0)

            def locate(jj, carry):
                run, blk_idx, before = carry
                run = run + cnt_scr[jj]
                below = run < need_b
                return run, blk_idx + jnp.where(below, 1, 0), jnp.where(below, run, before)
            zero = jnp.zeros((tq, LANE), F32)
            _, blk_idx, before = lax.fori_loop(0, n_blocks, locate, (zero, jnp.zeros((tq, LANE), jnp.int32), zero))
            need_in = need_b - before

            ebuf_scr[...] = jnp.zeros((tq, kb), F32)

            def extract(jj, carry):
                here = blk_idx == jj
                for a in lane_tiles:
                    cols = slice(a * LANE, (a + 1) * LANE)
                    ebuf_scr[:, cols] = jnp.where(here, ties(jj, a), ebuf_scr[:, cols])
                return carry
            lax.fori_loop(0, n_blocks, extract, 0)

            lane_idx = lax.broadcasted_iota(jnp.int32, (tq, LANE), 1)
            cut_bits = max(1, (kb - 1).bit_length())

            def cut_body(it, cut):
                cand = cut | jnp.left_shift(jnp.int32(1), cut_bits - 1 - it)
                part = zero
                for a in lane_tiles:
                    part = part + jnp.where(a * LANE + lane_idx < cand, ebuf_scr[:, a * LANE:(a + 1) * LANE], 0.0)
                return jnp.where(row_sum(part) < need_in, cand, cut)
            cut = lax.fori_loop(0, cut_bits, cut_body, jnp.zeros((tq, LANE), jnp.int32))
            jb_ref[0] = blk_idx * kb + cut


def _dsa_select(z, qi_cb, kw_cb, ki_n, *, tq, kb, k_sel, n_valid, causal):
    b, t_q = z.shape[0], z.shape[1]
    t_k = ki_n.shape[1]
    nq, nk = t_q // tq, t_k // kb
    assert kb // LANE <= 256, "per-lane tie counts of one key block go through a bf16 matmul, exact up to 256"
    kmap = (lambda i, j: jnp.minimum(j, ((i + 1) * tq - 1) // kb)) if causal else (lambda i, j: j)
    omap = lambda i, j: j
    qmap = lambda i, j: i
    grid = (b, nq, nk)
    fold = _fold_steps(nq, tq, kb) if causal else None
    if fold:
        grid = (b, nq // 2, fold)
        qmap = lambda r, s: _fold_step(r, s, nq, tq, kb)[0]
        kmap = omap = lambda r, s: _fold_step(r, s, nq, tq, kb)[1]
    body = functools.partial(_dsa_select_body, tq=tq, kb=kb, nk=nk, k_sel=k_sel, n_valid=n_valid,
                             causal=causal, fold_nq=nq if fold else 0)
    return pl.pallas_call(
        body,
        out_shape=(jax.ShapeDtypeStruct((b, t_q, t_k), F32),
                   jax.ShapeDtypeStruct((b, t_q, LANE), F32),
                   jax.ShapeDtypeStruct((b, t_q, LANE), jnp.int32)),
        grid=grid,
        in_specs=[pl.BlockSpec((1, tq, IDX_HEADS * IDX_DIM), lambda bi, i, j: (bi, qmap(i, j), qi_cb)),
                  pl.BlockSpec((1, tq, LANE), lambda bi, i, j: (bi, qmap(i, j), kw_cb)),
                  pl.BlockSpec((1, kb, LANE), lambda bi, i, j: (bi, kmap(i, j), 0))],
        out_specs=(pl.BlockSpec((1, tq, kb), lambda bi, i, j: (bi, qmap(i, j), omap(i, j))),
                   pl.BlockSpec((1, tq, LANE), lambda bi, i, j: (bi, qmap(i, j), 0)),
                   pl.BlockSpec((1, tq, LANE), lambda bi, i, j: (bi, qmap(i, j), 0))),
        scratch_shapes=[pltpu.VMEM((nk, tq, kb), F32), pltpu.VMEM((nk, tq, LANE), F32), pltpu.VMEM((tq, kb), F32)],
        compiler_params=_cparams(("parallel", "parallel", "arbitrary")),
        name="dsa_select",
    )(z, z, ki_n)


def _rwkv_body(zr_ref, zk_ref, zv_ref, zw_ref, za_ref, zg_ref,
               sr_ref, sk_ref, sv_ref, sw_ref, sa_ref, sg_ref,
               mr_ref, mk_ref, mv_ref, mw_ref, ma_ref, mg_ref,
               vec_ref, w2_ref, a2_ref, g2_ref, s0_ref,
               mix_ref, sfin_ref,
               s_scr, prev_scr, prevl_scr, prevg_scr, y_scr, *, tc, lc, nt, pp):
    t = pl.program_id(2)

    @pl.when(t == 0)
    def _():
        s_scr[...] = s0_ref[0]
        prev_scr[0:1, :] = sr_ref[0]
        prev_scr[1:2, :] = sk_ref[0]
        prev_scr[2:3, :] = sv_ref[0]
        prevl_scr[0:1, :] = sw_ref[0]
        prevl_scr[1:2, :] = sa_ref[0]
        prevg_scr[0:1, :] = sg_ref[0]

    row = lax.broadcasted_iota(jnp.int32, (tc, 1), 0)

    def shifted(z_ref, prow, mu_ref):
        z = z_ref[0]
        zp = jnp.where(row == 0, prow, pltpu.roll(z, 1, axis=0))
        return z + (zp - z) * mu_ref[...], z[tc - 1:tc, :]

    r, last_r = shifted(zr_ref, prev_scr[0:1, :], mr_ref)
    k, last_k = shifted(zk_ref, prev_scr[1:2, :], mk_ref)
    v, last_v = shifted(zv_ref, prev_scr[2:3, :], mv_ref)
    zw, last_w = shifted(zw_ref, prevl_scr[0:1, :], mw_ref)
    za, last_a = shifted(za_ref, prevl_scr[1:2, :], ma_ref)
    zg, last_g = shifted(zg_ref, prevg_scr[0:1, :], mg_ref)
    prev_scr[0:1, :] = last_r
    prev_scr[1:2, :] = last_k
    prev_scr[2:3, :] = last_v
    prevl_scr[0:1, :] = last_w
    prevl_scr[1:2, :] = last_a
    prevg_scr[0:1, :] = last_g

    w0, a0 = vec_ref[0:1, :], vec_ref[1:2, :]
    k_k, k_a, r_k = vec_ref[2:3, :], vec_ref[3:4, :], vec_ref[4:5, :]
    ln_w, ln_b = vec_ref[5:6, :], vec_ref[6:7, :]

    ones_blk = _head_block_matrix(LANE, 1.0)
    avg_blk = _head_block_matrix(LANE, 1.0 / HEAD_DIM)

    def per_head(x, blk):
        return jnp.concatenate([jnp.dot(x[:, i * LANE:(i + 1) * LANE].astype(BF16), blk, preferred_element_type=F32)
                                for i in range(pp)], axis=1)

    xw = w0 + jnp.dot(jnp.tanh(zw).astype(BF16), w2_ref[...], preferred_element_type=F32)
    lw = -math.exp(-0.5) * jax.nn.sigmoid(xw)
    a = jax.nn.sigmoid(a0 + jnp.dot(za.astype(BF16), a2_ref[...], preferred_element_type=F32))
    g = jnp.dot(jax.nn.sigmoid(zg).astype(BF16), g2_ref[...], preferred_element_type=F32)
    kk = k * k_k
    kk = kk * jnp.minimum(lax.rsqrt(per_head(kk * kk, ones_blk)), 1e12)
    k2 = k * (1.0 + (a - 1.0) * k_a)

    a_step = -kk
    b_step = kk * a

    lane = lax.broadcasted_iota(jnp.int32, (1, LANE), 1)
    m0 = lane < HEAD_DIM
    rr = lax.broadcasted_iota(jnp.int32, (lc, 2 * lc), 0)
    cc = lax.broadcasted_iota(jnp.int32, (lc, 2 * lc), 1)
    incl = jnp.where(cc < lc, cc, cc - lc) <= rr
    strict = (lax.broadcasted_iota(jnp.int32, (2 * lc, 2 * lc), 1)
              < lax.broadcasted_iota(jnp.int32, (2 * lc, 2 * lc), 0))
    tri = (lax.broadcasted_iota(jnp.int32, (lc, lc), 1)
           <= lax.broadcasted_iota(jnp.int32, (lc, lc), 0)).astype(BF16)
    nsteps = lc.bit_length() - 1
    nt_dims = ((1,), (1,))
    nn_dims = ((1,), (0,))
    tn_dims = ((0,), (0,))

    chunks = range(tc // lc)
    st = []
    for c in chunks:
        rows = slice(c * lc, (c + 1) * lc)
        lwc = lw[rows]
        cs = _dot_exact_lhs(tri, lwc)
        p_in = jnp.exp(-cs)
        at = a_step[rows] * jnp.exp(cs - lwc)
        bt = b_step[rows] * p_in
        kt = k2[rows] * p_in
        rt = r[rows] * jnp.exp(cs)
        vc = v[rows]
        p_last = jnp.exp(cs[lc - 1:lc, :])
        for pi in range(pp):
            cols = slice(pi * LANE, (pi + 1) * LANE)
            stack = lambda x: jnp.concatenate([jnp.where(m0, x[:, cols], 0.0),
                                               jnp.where(m0, 0.0, x[:, cols])], axis=0).astype(BF16)
            st.append(dict(c=c, pi=pi,
                           ar=jnp.concatenate([stack(at), rt[:, cols].astype(BF16)], axis=0),
                           bk=jnp.concatenate([stack(bt), stack(kt)], axis=0),
                           v_s=stack(vc), p_last=p_last[:, cols]))
    for d in st:
        gram = _dot_rw(d["ar"], d["bk"], nt_dims)
        d["t_p"] = jnp.where(strict, gram[0:2 * lc, 0:2 * lc], 0.0)
        d["w_ak"] = jnp.where(strict, gram[0:2 * lc, 2 * lc:4 * lc], 0.0).astype(BF16)
        d["w_rb"] = jnp.where(incl, gram[2 * lc:3 * lc, 0:2 * lc], 0.0).astype(BF16)
        d["w_rk"] = jnp.where(incl, gram[2 * lc:3 * lc, 2 * lc:4 * lc], 0.0).astype(BF16)
    gw = 4 * lc
    lane_head = lax.broadcasted_iota(jnp.int32, (1, gw), 1) // lc
    eye_sbs = (lax.broadcasted_iota(jnp.int32, (lc, gw), 0)
               == lax.broadcasted_iota(jnp.int32, (lc, gw), 1) % lc).astype(F32)
    lane_pair = lax.broadcasted_iota(jnp.int32, (1, 2 * lc), 1)

    def blockdiag(y):
        yb = y.astype(BF16)
        return jnp.concatenate([jnp.where(lane_head == h, yb, jnp.zeros_like(yb)) for h in range(4)], axis=0)

    groups = []
    for gi in range(len(st) // 2):
        members = (st[2 * gi], st[2 * gi + 1])
        t_sbs = jnp.concatenate([m["t_p"][0:lc] + m["t_p"][lc:2 * lc] for m in members], axis=1)
        groups.append(dict(members=members, t=t_sbs, minv=eye_sbs + t_sbs))
    for grp in groups:
        grp["t"] = _dot_rw(grp["t"], blockdiag(grp["t"]), nn_dims)
    for step in range(nsteps - 1):
        for grp in groups:
            if step == nsteps - 2:
                grp["minv"] = grp["minv"] + _dot_rw(grp["minv"], blockdiag(grp["t"]), nn_dims)
            else:
                both = _dot_rw(jnp.concatenate([grp["minv"], grp["t"]], axis=0), blockdiag(grp["t"]), nn_dims)
                grp["minv"] = grp["minv"] + both[0:lc]
                grp["t"] = both[lc:2 * lc]
    for grp in groups:
        for idx, m in enumerate(grp["members"]):
            tile = grp["minv"][:, idx * 2 * lc:(idx + 1) * 2 * lc]
            m["minv"] = jnp.concatenate([jnp.where(lane_pair < lc, tile, 0.0),
                                         jnp.where(lane_pair < lc, 0.0, tile)], axis=0).astype(BF16)
    for d in st:
        both = _dot_rw(jnp.concatenate([d["w_ak"], d["w_rk"]], axis=0), d["v_s"], nn_dims)
        d["wv"] = both[0:2 * lc]
        d["y_c"] = both[2 * lc:3 * lc]
    for d in st:
        both = _dot_rw(d["minv"], jnp.concatenate([d["ar"][0:2 * lc], d["wv"].astype(BF16)], axis=1), nn_dims)
        d["ma"] = both[:, 0:LANE]
        d["mwv"] = both[:, LANE:2 * LANE]
    for d in st:
        d["g"] = _dot_rw(d["ma"], d["bk"][0:2 * lc], tn_dims).astype(BF16)
        d["d"] = _dot_rw(jnp.concatenate([d["mwv"].astype(BF16), d["v_s"]], axis=0), d["bk"], tn_dims)
    s_cur = [s_scr[pi] for pi in range(pp)]
    for d in st:
        s_in = s_cur[d["pi"]]
        d["s0"] = s_in.astype(BF16)
        s_cur[d["pi"]] = ((s_in + _dot_rw(d["s0"], d["g"], nn_dims)) + d["d"]) * d["p_last"]
    for pi in range(pp):
        s_scr[pi] = s_cur[pi]
    for d in st:
        d["xr"] = _dot_rw(d["ar"], d["s0"], nt_dims)
    for d in st:
        d["u"] = (_dot_rw(d["minv"], d["xr"][0:2 * lc], nn_dims) + d["mwv"]).astype(BF16)
    for d in st:
        c, pi = d["c"], d["pi"]
        y_scr[c * lc:(c + 1) * lc, pi * LANE:(pi + 1) * LANE] = (
            (d["xr"][2 * lc:3 * lc] + _dot_rw(d["w_rb"], d["u"], nn_dims)) + d["y_c"])

    y = y_scr[...]
    mean = per_head(y, avg_blk)
    dev = y - mean
    var = per_head(dev * dev, avg_blk)
    yn = (dev * lax.rsqrt(var + RW_GN_EPS)) * ln_w + ln_b
    bonus = per_head((r * k2) * r_k, ones_blk) * v
    mix_ref[0] = ((yn + bonus) * g).astype(mix_ref.dtype)

    @pl.when(t == nt - 1)
    def _():
        sfin_ref[0] = s_scr[...]


def _rwkv(z, shift_prev, s0_pairs, mu, vecs, w2, a2, g2, d_model):
    b, t = z.shape[0], z.shape[1]
    npairs = d_model // LANE
    tc = min(t, 512)
    lc = min(CHUNK, t)
    nt = t // tc
    pp = 8
    wp = pp * LANE
    ngroups = npairs // pp
    cb_w, cb_a, cb_g = 3 * npairs, 3 * npairs + 1, (3 * npairs + 2) // 2

    def zspec(width, cbf):
        return pl.BlockSpec((1, tc, width), lambda bi, p, ti: (bi, ti, cbf(p)))

    def sspec(width, cbf):
        return pl.BlockSpec((1, 1, width), lambda bi, p, ti: (bi, 0, cbf(p)))

    def mspec(width, cbf):
        return pl.BlockSpec((1, width), lambda bi, p, ti: (0, cbf(p)))

    cbfs = [(wp, lambda p: p), (wp, lambda p: ngroups + p), (wp, lambda p: 2 * ngroups + p),
            (LANE, lambda p: cb_w), (LANE, lambda p: cb_a), (2 * LANE, lambda p: cb_g)]
    in_specs = ([zspec(w, f) for w, f in cbfs] + [sspec(w, f) for w, f in cbfs] + [mspec(w, f) for w, f in cbfs]
                + [pl.BlockSpec((8, wp), lambda bi, p, ti: (0, p)),
                   pl.BlockSpec((LANE, wp), lambda bi, p, ti: (0, p)),
                   pl.BlockSpec((LANE, wp), lambda bi, p, ti: (0, p)),
                   pl.BlockSpec((2 * LANE, wp), lambda bi, p, ti: (0, p)),
                   pl.BlockSpec((1, pp, LANE, LANE), lambda bi, p, ti: (bi, p, 0, 0))])
    return pl.pallas_call(
        functools.partial(_rwkv_body, tc=tc, lc=lc, nt=nt, pp=pp),
        out_shape=(jax.ShapeDtypeStruct((b, t, d_model), BF16),
                   jax.ShapeDtypeStruct((b, npairs, LANE, LANE), F32)),
        grid=(b, ngroups, nt),
        in_specs=in_specs,
        out_specs=(pl.BlockSpec((1, tc, wp), lambda bi, p, ti: (bi, ti, p)),
                   pl.BlockSpec((1, pp, LANE, LANE), lambda bi, p, ti: (bi, p, 0, 0))),
        scratch_shapes=[pltpu.VMEM((pp, LANE, LANE), F32),
                        pltpu.VMEM((8, wp), F32),
                        pltpu.VMEM((8, LANE), F32),
                        pltpu.VMEM((8, 2 * LANE), F32),
                        pltpu.VMEM((tc, wp), F32)],
        compiler_params=_cparams(("parallel", "parallel", "arbitrary")),
        name="rwkv7",
    )(*([z] * 6), *([shift_prev] * 6), *([mu] * 6), vecs, w2, a2, g2, s0_pairs)


def _pad_cols(x, segments):
    parts = []
    for start, width, padded in segments:
        seg = x[..., start:start + width]
        if padded > width:
            seg = jnp.concatenate([seg, jnp.zeros(seg.shape[:-1] + (padded - width,), seg.dtype)], axis=-1)
        parts.append(seg)
    return jnp.concatenate(parts, axis=-1)


def _pad_rows(x, padded):
    return jnp.concatenate([x, jnp.zeros((padded - x.shape[0],) + x.shape[1:], x.dtype)], axis=0)


def _pairs_from_heads(s):
    b, h = s.shape[0], s.shape[1]
    s = s.reshape(b, h // 2, 2, HEAD_DIM, HEAD_DIM)
    z = jnp.zeros_like(s[:, :, 0])
    top = jnp.concatenate([s[:, :, 0], z], axis=-1)
    bot = jnp.concatenate([z, s[:, :, 1]], axis=-1)
    return jnp.concatenate([top, bot], axis=-2)


def _heads_from_pairs(sp):
    b, npairs = sp.shape[0], sp.shape[1]
    s = jnp.stack([sp[:, :, :HEAD_DIM, :HEAD_DIM], sp[:, :, HEAD_DIM:, HEAD_DIM:]], axis=2)
    return s.reshape(b, 2 * npairs, HEAD_DIM, HEAD_DIM)


def _tile_gain(g, width):
    return jnp.tile(g.astype(F32), width // HEAD_DIM).reshape(1, width)


def _mem_attend(z, memq_cb, km, vm, q_gain):
    t_q = z.shape[1]
    wq = MEM_HEADS * HEAD_DIM
    tq = min(t_q, 512)
    no_sink = jnp.full((1, wq), -jnp.inf, F32)
    return _flash(z, memq_cb, wq, km, 0, vm, 0, _tile_gain(q_gain, LANE), no_sink,
                  mode="all", tq=tq, kb=km.shape[1], group=1, n_valid=km.shape[1])


def _conv_ffn(x2d, b, t, gain, w_up, conv_w, w_down, prev):
    act, u_last = _up_conv(x2d, gain, w_up, conv_w, prev, t)
    return _mm_res([act], [w_down], x2d), u_last


def kernel(x_prompt, x_sample, state_rwkv_wkv, state_rwkv_shift, cache_swa_k, cache_swa_v, cache_dsa_k, cache_dsa_v, cache_dsa_idx_k, cache_mem_k, cache_mem_v, state_ffn_conv, mem_prompt, attn_norm, ffn_norm, mem_norm, mem_w_kv, mem_q_norm, mem_k_norm, a_w_in, a_mu, a_w0, a_w2, a_a0, a_a2, a_g2, a_k_k, a_k_a, a_r_k, a_ln_w, a_ln_b, a_w_out, b_w_in, b_q_norm, b_k_norm, b_sink, b_w_out, c_w_in, c_q_norm, c_k_norm, c_idx_k_norm, c_w_out, ffn_w_up, ffn_conv, ffn_w_down):
    bp, t, d = x_prompt.shape
    bd, s_len = x_sample.shape[:2]
    depth = attn_norm.shape[0]
    win_rows = cache_swa_k.shape[2]
    past = cache_dsa_k.shape[2] if cache_dsa_k.shape[0] else 0
    d_ff = ffn_w_down.shape[1]
    mem_tokens = mem_prompt.shape[1]
    q_cols = d
    kv_cols = ATT_KV_HEADS * HEAD_DIM
    memq_cols = MEM_HEADS * HEAD_DIM
    att_group = (d // HEAD_DIM) // ATT_KV_HEADS
    dec_lora = a_w2.shape[1]
    a_lora = a_a2.shape[1]
    g_lora = a_g2.shape[1]
    rw_cols = 3 * d + dec_lora + a_lora + g_lora
    k_sel_p = min(TOPK_MAX, t // 4)
    k_sel_s = min(TOPK_MAX, (past + s_len) // 4)
    assert g_lora == 2 * LANE and dec_lora <= LANE and a_lora <= LANE

    xp = x_prompt.reshape(bp * t, d)
    xs = x_sample.reshape(bd * s_len, d)

    rw_segments = [(0, 3 * d, 3 * d), (3 * d, dec_lora, LANE), (3 * d + dec_lora, a_lora, LANE),
                   (3 * d + dec_lora + a_lora, g_lora, g_lora)]
    rw_padded = 3 * d + 2 * LANE + g_lora
    o_qi = q_cols + 2 * kv_cols
    o_ki = o_qi + IDX_HEADS * IDX_DIM
    c_cols = o_ki + IDX_DIM + IDX_HEADS

    outs = {k: [] for k in ("p_rw_wkv", "p_rw_sh", "p_sw_k", "p_sw_v", "p_ds_k", "p_ds_v", "p_ds_i", "p_mk",
                            "p_mv", "p_cv", "s_rw_wkv", "s_rw_sh", "s_sw_k", "s_sw_v", "s_ds_k", "s_ds_v",
                            "s_ds_i", "s_cv")}

    def unpad_rw(row):
        return jnp.concatenate([row[..., :3 * d], row[..., 3 * d:3 * d + dec_lora],
                                row[..., 3 * d + LANE:3 * d + LANE + a_lora],
                                row[..., 3 * d + 2 * LANE:3 * d + 2 * LANE + g_lora]], axis=-1)

    for i in range(depth):
        kind, j = i % 3, i // 3
        if kind == 0:
            w_in = jnp.concatenate([_pad_cols(a_w_in[j], rw_segments), a_w_in[j][:, rw_cols:]], axis=1).astype(BF16)
            memq_cb = rw_padded // memq_cols
            zp = _mm_norm(xp, attn_norm[i], w_in).reshape(bp, t, -1)
            zs = _mm_norm(xs, attn_norm[i], w_in).reshape(bd, s_len, -1)
            mu = _pad_cols(a_mu[j].reshape(1, -1), rw_segments)
            mu = jnp.concatenate([mu, jnp.zeros((1, memq_cols), F32)], axis=1)
            vecs = jnp.stack([a_w0[j], a_a0[j], a_k_k[j], a_k_a[j], a_r_k[j].reshape(-1), a_ln_w[j], a_ln_b[j],
                              jnp.zeros((d,), F32)], axis=0)
            w2 = _pad_rows(a_w2[j], LANE).astype(BF16)
            a2 = _pad_rows(a_a2[j], LANE).astype(BF16)
            g2 = a_g2[j].astype(BF16)
            sh_p = jnp.zeros((bp, 1, zp.shape[-1]), F32)
            st_p = jnp.zeros((bp, d // LANE, LANE, LANE), F32)
            sh_s = _pad_cols(state_rwkv_shift[j], rw_segments)
            sh_s = jnp.concatenate([sh_s, jnp.zeros((bd, memq_cols), F32)], axis=1).reshape(bd, 1, -1)
            st_s = _pairs_from_heads(state_rwkv_wkv[j])
            mp, stp = _rwkv(zp, sh_p, st_p, mu, vecs, w2, a2, g2, d)
            ms, sts = _rwkv(zs, sh_s, st_s, mu, vecs, w2, a2, g2, d)
            outs["p_rw_sh"].append(unpad_rw(zp[:, -1]))
            outs["p_rw_wkv"].append(_heads_from_pairs(stp))
            outs["s_rw_sh"].append(unpad_rw(zs[:, -1]))
            outs["s_rw_wkv"].append(_heads_from_pairs(sts))
            w_out = a_w_out[j]
        elif kind == 1:
            w_in = b_w_in[j].astype(BF16)
            memq_cb = (q_cols + 2 * kv_cols) // memq_cols
            k_cb, v_cb = q_cols // kv_cols, q_cols // kv_cols + 1
            zp = _mm_norm(xp, attn_norm[i], w_in).reshape(bp, t, -1)
            zs = _mm_norm(xs, attn_norm[i], w_in).reshape(bd, s_len, -1)
            kgain = _tile_gain(b_k_norm[j], kv_cols)
            qgain = _tile_gain(b_q_norm[j], LANE)
            sink = jnp.repeat(b_sink[j].astype(F32), HEAD_DIM).reshape(1, q_cols)
            knp = _headnorm(zp.reshape(bp * t, -1), k_cb, kv_cols, kgain).reshape(bp, t, kv_cols)
            mp = _flash(zp, 0, q_cols, knp, 0, zp, v_cb, qgain, sink, mode="band", tq=WINDOW, kb=WINDOW,
                        group=att_group)
            outs["p_sw_k"].append(knp[:, t - win_rows:].reshape(bp, win_rows, ATT_KV_HEADS, HEAD_DIM))
            outs["p_sw_v"].append(zp[:, t - win_rows:, q_cols + kv_cols:q_cols + 2 * kv_cols]
                                  .reshape(bp, win_rows, ATT_KV_HEADS, HEAD_DIM))
            kns = _headnorm(zs.reshape(bd * s_len, -1), k_cb, kv_cols, kgain).reshape(bd, s_len, kv_cols)
            vs_new = zs[:, :, q_cols + kv_cols:q_cols + 2 * kv_cols]
            k_all = jnp.concatenate([cache_swa_k[j].reshape(bd, win_rows, kv_cols), kns], axis=1)
            v_all = jnp.concatenate([cache_swa_v[j].reshape(bd, win_rows, kv_cols), vs_new], axis=1)
            n_keys = win_rows + s_len
            n_pad = -(-n_keys // LANE) * LANE
            pad = jnp.zeros((bd, n_pad - n_keys, kv_cols), F32)
            ms = _flash(zs, 0, q_cols, jnp.concatenate([k_all, pad], axis=1), 0,
                        jnp.concatenate([v_all, pad], axis=1), 0, qgain, sink, mode="all", tq=s_len, kb=n_pad,
                        group=att_group, n_valid=n_keys)
            outs["s_sw_k"].append(k_all[:, n_keys - win_rows:].reshape(bd, win_rows, ATT_KV_HEADS, HEAD_DIM))
            outs["s_sw_v"].append(v_all[:, n_keys - win_rows:].reshape(bd, win_rows, ATT_KV_HEADS, HEAD_DIM))
            w_out = b_w_out[j]
        else:
            wc = c_w_in[j]
            w_in = jnp.concatenate([wc[:, :o_ki], wc[:, c_cols:],
                                    _pad_cols(wc, [(o_ki, IDX_DIM + IDX_HEADS, LANE)])], axis=1).astype(BF16)
            memq_cb = o_ki // memq_cols
            kw_cb = (o_ki + memq_cols) // LANE
            k_cb, v_cb = q_cols // kv_cols, q_cols // kv_cols + 1
            qi_cb = o_qi // (IDX_HEADS * IDX_DIM)
            zp = _mm_norm(xp, attn_norm[i], w_in).reshape(bp, t, -1)
            zs = _mm_norm(xs, attn_norm[i], w_in).reshape(bd, s_len, -1)
            kgain = _tile_gain(c_k_norm[j], kv_cols)
            qgain = _tile_gain(c_q_norm[j], LANE)
            igain = jnp.concatenate([c_idx_k_norm[j].astype(F32), jnp.zeros((LANE - IDX_DIM,), F32)]).reshape(1, LANE)
            no_sink = jnp.full((1, q_cols), -jnp.inf, F32)
            knp = _headnorm(zp.reshape(bp * t, -1), k_cb, kv_cols, kgain).reshape(bp, t, kv_cols)
            kip = _headnorm(zp.reshape(bp * t, -1), kw_cb, LANE, igain).reshape(bp, t, LANE)
            tq = min(t, DSA_TQ)
            kb = min(t, DSA_KB)
            sc, tau, cut = _dsa_select(zp, qi_cb, kw_cb, kip, tq=tq, kb=kb, k_sel=k_sel_p, n_valid=t, causal=True)
            ftq, fkb = min(t, DSA_FLASH_TQ), min(t, DSA_FLASH_KB)
            assert all(_causal_blocks(r // tq, tq, kb) * kb >= _causal_blocks(r // ftq, ftq, fkb) * fkb
                       for r in range(0, t, CHUNK))
            mp = _flash(zp, 0, q_cols, knp, 0, zp, v_cb, qgain, no_sink, mode="dsa", tq=ftq, kb=fkb,
                        group=att_group, n_valid=t, causal=True, scores=sc, tau=tau, jb=cut)
            outs["p_ds_k"].append(knp.reshape(bp, t, ATT_KV_HEADS, HEAD_DIM))
            outs["p_ds_v"].append(zp[:, :, q_cols + kv_cols:q_cols + 2 * kv_cols].reshape(bp, t, ATT_KV_HEADS, HEAD_DIM))
            outs["p_ds_i"].append(kip[:, :, :IDX_DIM])
            kns = _headnorm(zs.reshape(bd * s_len, -1), k_cb, kv_cols, kgain).reshape(bd, s_len, kv_cols)
            kis = _headnorm(zs.reshape(bd * s_len, -1), kw_cb, LANE, igain).reshape(bd, s_len, LANE)
            vs_new = zs[:, :, q_cols + kv_cols:q_cols + 2 * kv_cols]
            n_keys = past + s_len
            n_pad = -(-n_keys // LANE) * LANE
            zpad = lambda w: jnp.zeros((bd, n_pad - n_keys, w), F32)
            k_all = jnp.concatenate([cache_dsa_k[j].reshape(bd, past, kv_cols), kns, zpad(kv_cols)], axis=1)
            v_all = jnp.concatenate([cache_dsa_v[j].reshape(bd, past, kv_cols), vs_new, zpad(kv_cols)], axis=1)
            ki_cache = jnp.concatenate([cache_dsa_idx_k[j], jnp.zeros((bd, past, LANE - IDX_DIM), F32)], axis=-1)
            ki_all = jnp.concatenate([ki_cache, kis, zpad(LANE)], axis=1)
            sc, tau, cut = _dsa_select(zs, qi_cb, kw_cb, ki_all, tq=s_len, kb=n_pad, k_sel=k_sel_s, n_valid=n_keys,
                                       causal=False)
            ms = _flash(zs, 0, q_cols, k_all, 0, v_all, 0, qgain, no_sink, mode="dsa", tq=s_len, kb=n_pad,
                        group=att_group, n_valid=n_keys, causal=False, scores=sc, tau=tau, jb=cut)
            outs["s_ds_k"].append(kns.reshape(bd, s_len, ATT_KV_HEADS, HEAD_DIM))
            outs["s_ds_v"].append(vs_new.reshape(bd, s_len, ATT_KV_HEADS, HEAD_DIM))
            outs["s_ds_i"].append(kis[:, :, :IDX_DIM])
            w_out = c_w_out[j]

        kv_mem = _mm_norm(mem_prompt.reshape(bp * mem_tokens, d), mem_norm[i], mem_w_kv[i].astype(BF16))
        km_p = _headnorm(kv_mem, 0, memq_cols, _tile_gain(mem_k_norm[i], memq_cols)).reshape(bp, mem_tokens, memq_cols)
        vm_p = kv_mem[:, memq_cols:].reshape(bp, mem_tokens, memq_cols)
        outs["p_mk"].append(km_p.reshape(bp, mem_tokens, MEM_HEADS, HEAD_DIM))
        outs["p_mv"].append(vm_p.reshape(bp, mem_tokens, MEM_HEADS, HEAD_DIM))
        mo_p = _mem_attend(zp, memq_cb, km_p, vm_p, mem_q_norm[i])
        mo_s = _mem_attend(zs, memq_cb, cache_mem_k[i].reshape(bd, mem_tokens, memq_cols),
                           cache_mem_v[i].reshape(bd, mem_tokens, memq_cols), mem_q_norm[i])
        w_mix, w_mem = w_out[:d].astype(BF16), w_out[d:].astype(BF16)
        xp = _mm_res([mp.reshape(bp * t, d), mo_p.reshape(bp * t, memq_cols)], [w_mix, w_mem], xp)
        xs = _mm_res([ms.reshape(bd * s_len, d), mo_s.reshape(bd * s_len, memq_cols)], [w_mix, w_mem], xs)

        w_up, w_down = ffn_w_up[i].astype(BF16), ffn_w_down[i].astype(BF16)
        xp, cp = _conv_ffn(xp, bp, t, ffn_norm[i], w_up, ffn_conv[i], w_down,
                           jnp.zeros((bp, CONV_W - 1, 2 * d_ff), F32))
        xs, cs = _conv_ffn(xs, bd, s_len, ffn_norm[i], w_up, ffn_conv[i], w_down, state_ffn_conv[i])
        outs["p_cv"].append(cp)
        outs["s_cv"].append(cs)

    st = jnp.stack
    order = ("p_rw_wkv", "p_rw_sh", "p_sw_k", "p_sw_v", "p_ds_k", "p_ds_v", "p_ds_i", "p_mk", "p_mv", "p_cv",
             "s_rw_wkv", "s_rw_sh", "s_sw_k", "s_sw_v", "s_ds_k", "s_ds_v", "s_ds_i", "s_cv")
    return (xp.reshape(bp, t, d), xs.reshape(bd, s_len, d)) + tuple(st(outs[k]) for k in order)
```
